```python
import math
import jax
import jax.numpy as jnp
from jax import lax
import numpy as np

D_MODEL = 1024
BATCH = 4
SEQ = 4096
DEPTH = 4
DEC_BATCH = 32
DEC_SEQ = 1
PAST_LEN = 8192
PAGE_SIZE = 128

N_MIXERS = 3
N_A_LAYERS = len(range(0, DEPTH, N_MIXERS))
N_B_LAYERS = len(range(1, DEPTH, N_MIXERS))
N_C_LAYERS = len(range(2, DEPTH, N_MIXERS))

A_WINDOWS = (128, 512, 2048)
A_DILATIONS = (1, 4, 16)
A_GROUPS = 3
A_SLOTS = 8
A_HEAD_DIM = 64
A_OUT = A_SLOTS * A_HEAD_DIM
A_QKV = 3 * A_GROUPS * A_SLOTS * A_HEAD_DIM
A_QBLOCK = 64

B_HEADS = 4
B_DK = D_MODEL // (2 * B_HEADS)
B_DV = D_MODEL // B_HEADS
B_GATE_RANK = 16
B_GATE_TAU = 16.0
B_CHUNK = 64
B_SIZES = (B_HEADS * B_DK, B_HEADS * B_DK, B_HEADS * B_DV, B_GATE_RANK, B_HEADS * B_DV)

C_HEADS = 16
C_KV_HEADS = 4
C_HEAD_DIM = 64
C_CMP_BLOCK = 32
C_CMP_STRIDE = 16
C_SEL_BLOCK = 64
C_TOPK = 16
C_WINDOW = 512
C_CMP_HIDDEN = 128
C_QBLOCK = 64
C_SIZES = (C_HEADS * C_HEAD_DIM,) + (C_KV_HEADS * C_HEAD_DIM,) * 6 + (3 * C_HEADS,)

MEM_LEN = 256
MEM_HEADS = 4
MEM_HEAD_DIM = D_MODEL // MEM_HEADS

FFN_HIDDEN = ((8 * D_MODEL + 3 * 256 - 1) // (3 * 256)) * 256

RMS_EPS = 1e-6
NEG_INF = -1e30
FORCE = 1e30
TINY = 1e-30

kernel_name = 'hybrid_dilated_gla_nsa_decoder_step'


def rmsnorm(x, g):
    xf = x.astype(jnp.float32)
    y = xf * lax.rsqrt(jnp.mean(xf * xf, axis=-1, keepdims=True) + RMS_EPS)
    return (y * g.astype(jnp.float32)).astype(x.dtype)


def split_sizes(x, sizes):
    return jnp.split(x, [int(c) for c in np.cumsum(sizes)[:-1]], axis=-1)


def alibi_slopes(n):
    return jnp.asarray(2.0 ** (-8.0 * np.arange(1, n + 1) / n), dtype=jnp.float32)


def masked_softmax(s, mask):
    s = jnp.where(mask, s, NEG_INF)
    m = jnp.max(s, axis=-1, keepdims=True)
    e = jnp.where(mask, jnp.exp(s - m), 0.0)
    den = jnp.maximum(jnp.sum(e, axis=-1, keepdims=True), TINY)
    return e / den, (m + jnp.log(den))[..., 0]


def swiglu(h, w_in, w_out):
    a, b = jnp.split(h @ w_in, 2, axis=-1)
    return (jax.nn.silu(a) * b) @ w_out


def cross_attention(h, mem_kv, w_q, w_o):
    B, L, _ = h.shape
    q = (h @ w_q).reshape(B, L, MEM_HEADS, MEM_HEAD_DIM)
    s = jnp.einsum('blhd,bmhd->bhlm', q, mem_kv[:, :, 0]).astype(jnp.float32) * MEM_HEAD_DIM ** -0.5
    p = jax.nn.softmax(s, axis=-1)
    o = jnp.einsum('bhlm,bmhd->blhd', p.astype(h.dtype), mem_kv[:, :, 1])
    return o.reshape(B, L, MEM_HEADS * MEM_HEAD_DIM) @ w_o


def dilated_attention(q, ks, vs, starts, slopes):
    B, Lq = q.shape[:2]
    qb = math.gcd(Lq, A_QBLOCK)
    scale = A_HEAD_DIM ** -0.5

    def block(i):
        q_blk = lax.dynamic_slice_in_dim(q, i * qb, qb, axis=1)
        outs, lses = [], []
        for g in range(A_GROUPS):
            d = A_DILATIONS[g]
            dist = d * jnp.arange(A_WINDOWS[g] // d + 1)
            t = starts[g] + i * qb + jnp.arange(qb)
            kpos = t[:, None] - dist[None, :]
            kidx = jnp.clip(kpos, 0, ks[g].shape[1] - 1)
            kg = jnp.take(ks[g], kidx, axis=1)
            vg = jnp.take(vs[g], kidx, axis=1)
            s = jnp.einsum('bqsd,bqksd->bsqk', q_blk[:, :, g], kg).astype(jnp.float32) * scale
            s = s - slopes[g][None, :, None, None] * dist.astype(jnp.float32)
            p, lse = masked_softmax(s, (kpos >= 0)[None, None])
            outs.append(jnp.einsum('bsqk,bqksd->bqsd', p.astype(vg.dtype), vg))
            lses.append(lse)
        w = jax.nn.softmax(jnp.stack(lses), axis=0)
        o = jnp.einsum('gbsq,gbqsd->bqsd', w.astype(q.dtype), jnp.stack(outs))
        return o.reshape(B, qb, A_OUT)

    o = lax.map(block, jnp.arange(Lq // qb))
    return jnp.moveaxis(o, 0, 1).reshape(B, Lq, A_OUT)


def mixer_dilated(h, w_qkv, w_o, slopes, bufs):
    B, L, _ = h.shape
    qkv = (h @ w_qkv).reshape(B, L, 3, A_GROUPS, A_SLOTS, A_HEAD_DIM)
    q, k, v = qkv[:, :, 0], qkv[:, :, 1], qkv[:, :, 2]
    ks, vs, starts, new_state = [], [], [], []
    for g in range(A_GROUPS):
        kg, vg = k[:, :, g], v[:, :, g]
        kv_new = jnp.stack([kg, vg], axis=2)
        if bufs is None:
            ks.append(kg)
            vs.append(vg)
            starts.append(0)
            new_state.append(kv_new[:, -min(A_WINDOWS[g], L):])
        else:
            buf = bufs[g]
            ks.append(jnp.concatenate([buf[:, :, 0], kg], axis=1))
            vs.append(jnp.concatenate([buf[:, :, 1], vg], axis=1))
            starts.append(buf.shape[1])
            new_state.append(kv_new)
    o = dilated_attention(q, ks, vs, starts, slopes)
    return o @ w_o, new_state


def gla_recurrence(q, k, v, log_a, s0):
    B, L, H, Dk = q.shape
    Dv = v.shape[-1]
    c = math.gcd(L, B_CHUNK)
    n = L // c
    f32 = jnp.float32

    def chunks(a):
        return jnp.moveaxis(a.astype(f32).reshape(B, n, c, H, a.shape[-1]), 1, 0)

    causal = jnp.tril(jnp.ones((c, c), dtype=bool))[None, :, :, None, None]

    def step(S, inp):
        qc, kc, vc, la = inp
        b = jnp.cumsum(la, axis=1)
        o_inter = jnp.einsum('bihk,bhkv->bihv', qc * jnp.exp(b), S)
        decay = jnp.exp(jnp.where(causal, b[:, :, None] - b[:, None, :], -jnp.inf))
        attn = jnp.einsum('bihk,bjhk,bijhk->bhij', qc, kc, decay)
        o = o_inter + jnp.einsum('bhij,bjhv->bihv', attn, vc)
        b_last = b[:, -1]
        S = jnp.exp(b_last)[..., None] * S + jnp.einsum('bjhk,bjhv->bhkv', kc * jnp.exp(b_last[:, None] - b), vc)
        return S, o

    S, o = lax.scan(step, s0.astype(f32), (chunks(q), chunks(k), chunks(v), chunks(log_a)))
    return jnp.moveaxis(o, 0, 1).reshape(B, L, H, Dv), S


def mixer_gla(h, w_in, w_gate2, b_gate, g_head, w_o, s0):
    B, L, _ = h.shape
    q, k, v, g_low, r = split_sizes(h @ w_in, B_SIZES)
    q = q.reshape(B, L, B_HEADS, B_DK) * B_DK ** -0.5
    k = k.reshape(B, L, B_HEADS, B_DK)
    v = v.reshape(B, L, B_HEADS, B_DV)
    log_a = jax.nn.log_sigmoid((g_low @ w_gate2 + b_gate).astype(jnp.float32)) / B_GATE_TAU
    o, S = gla_recurrence(q, k, v, log_a.reshape(B, L, B_HEADS, B_DK), s0)
    o = rmsnorm(o.astype(h.dtype), g_head).reshape(B, L, B_HEADS * B_DV) * jax.nn.silu(r)
    return o @ w_o, S


def compress(x, pos_emb, w1, w2):
    B, Lp, G, Dh = x.shape
    r = C_CMP_BLOCK // C_CMP_STRIDE
    n_chunks = Lp // C_CMP_STRIDE
    n_cmp = n_chunks - r + 1
    xs = x.reshape(B, n_chunks, C_CMP_STRIDE, G, Dh)
    blocks = jnp.concatenate([xs[:, j:j + n_cmp] for j in range(r)], axis=2)
    blocks = blocks + pos_emb[None, None, :, None, :]
    flat = jnp.moveaxis(blocks, 3, 2).reshape(B, n_cmp, G, C_CMP_BLOCK * Dh)
    return jax.nn.gelu(flat @ w1) @ w2


def nsa_attention(q, q_pos0, kc, vc, ks, vs, kw, vw, win_start, gates, slopes):
    B, Lq = q.shape[:2]
    G, hpg, Dh, SEL = C_KV_HEADS, C_HEADS // C_KV_HEADS, C_HEAD_DIM, C_SEL_BLOCK
    n_cmp, n_slc = kc.shape[1], ks.shape[1] // SEL
    topk = min(C_TOPK, n_slc)
    qb = math.gcd(Lq, C_QBLOCK)
    scale = Dh ** -0.5
    f32 = jnp.float32
    qg = q.reshape(B, Lq, G, hpg, Dh)
    gg = gates.reshape(B, Lq, G, hpg, 3).astype(q.dtype)
    sl = slopes.reshape(G, hpg)[None, :, :, None, None]
    cmp_end = jnp.arange(n_cmp) * C_CMP_STRIDE + (C_CMP_BLOCK - 1)
    blk = jnp.arange(n_slc)
    c_start = jnp.arange(n_cmp)[:, None] * C_CMP_STRIDE
    cover = ((c_start < (blk[None, :] + 1) * SEL) & (c_start + C_CMP_BLOCK > blk[None, :] * SEL)).astype(f32)
    ksb = jnp.moveaxis(ks.reshape(B, n_slc, SEL, G, Dh), 3, 1)
    vsb = jnp.moveaxis(vs.reshape(B, n_slc, SEL, G, Dh), 3, 1)
    kwp = jnp.pad(kw, ((0, 0), (C_WINDOW, 0), (0, 0), (0, 0)))
    vwp = jnp.pad(vw, ((0, 0), (C_WINDOW, 0), (0, 0), (0, 0)))
    bi = jnp.arange(B)[:, None, None]
    gi = jnp.arange(G)[None, :, None]

    def block(i):
        q0 = q_pos0 + i * qb
        qi = lax.dynamic_slice_in_dim(qg, i * qb, qb, axis=1)
        gt = lax.dynamic_slice_in_dim(gg, i * qb, qb, axis=1)
        t = q0 + jnp.arange(qb)
        tf = t.astype(f32)
        s = jnp.einsum('bqghd,bcgd->bghqc', qi, kc).astype(f32) * scale - sl * (tf[:, None] - cmp_end.astype(f32))
        p_c, _ = masked_softmax(s, cmp_end[None, :] <= t[:, None])
        o_c = jnp.einsum('bghqc,bcgd->bqghd', p_c.astype(vc.dtype), vc)
        imp = jnp.einsum('bghqc,cj->bgqj', p_c, cover)
        cur = (t // SEL)[:, None]
        forced = (blk == 0) | (blk == cur) | (blk == cur - 1)
        score = jnp.where(blk * SEL <= t[:, None], jnp.where(forced, FORCE, imp), NEG_INF)
        top_s, top_j = lax.top_k(score, topk)
        idx = top_j.reshape(B, G, qb * topk)
        kb = ksb[bi, gi, idx].reshape(B, G, qb, topk * SEL, Dh)
        vb = vsb[bi, gi, idx].reshape(B, G, qb, topk * SEL, Dh)
        kpos = (top_j[..., None] * SEL + jnp.arange(SEL)).reshape(B, G, qb, topk * SEL)
        ok = jnp.repeat(top_s > 0.5 * NEG_INF, SEL, axis=-1) & (kpos <= t[:, None])
        s = jnp.einsum('bqghd,bgqkd->bghqk', qi, kb).astype(f32) * scale - sl * (tf[:, None] - kpos.astype(f32))[:, :, None]
        p_s, _ = masked_softmax(s, ok[:, :, None])
        o_s = jnp.einsum('bghqk,bgqkd->bqghd', p_s.astype(vb.dtype), vb)
        kwin = lax.dynamic_slice_in_dim(kwp, q0 - win_start, C_WINDOW + qb, axis=1)
        vwin = lax.dynamic_slice_in_dim(vwp, q0 - win_start, C_WINDOW + qb, axis=1)
        wpos = q0 - C_WINDOW + jnp.arange(C_WINDOW + qb)
        okw = (wpos[None, :] <= t[:, None]) & (wpos[None, :] >= t[:, None] - C_WINDOW) & (wpos[None, :] >= win_start)
        s = jnp.einsum('bqghd,bkgd->bghqk', qi, kwin).astype(f32) * scale - sl * (tf[:, None] - wpos.astype(f32))
        p_w, _ = masked_softmax(s, okw)
        o_w = jnp.einsum('bghqk,bkgd->bqghd', p_w.astype(vwin.dtype), vwin)
        o = gt[..., 0:1] * o_c + gt[..., 1:2] * o_s + gt[..., 2:3] * o_w
        return o.reshape(B, qb, C_HEADS * Dh)

    o = lax.map(block, jnp.arange(Lq // qb))
    return jnp.moveaxis(o, 0, 1).reshape(B, Lq, C_HEADS * Dh)


def mixer_nsa(h, w_in, b_gate, pos_k, pos_v, wk1, wk2, wv1, wv2, w_o, slopes, past, win_buf, q_pos0):
    B, L, _ = h.shape
    q, kc, vc, ks, vs, kw, vw, g = split_sizes(h @ w_in, C_SIZES)
    rows = lambda a: a.reshape(B, L, C_KV_HEADS, C_HEAD_DIM)
    new_rows = jnp.stack([rows(kc), rows(vc), rows(ks), rows(vs)], axis=2)
    win_rows = jnp.stack([rows(kw), rows(vw)], axis=2)
    gates = jax.nn.sigmoid((g + b_gate).astype(jnp.float32)).reshape(B, L, C_HEADS, 3)
    if past is None:
        full, win, win_start = new_rows, win_rows, 0
        win_state = win_rows[:, -min(C_WINDOW, L):]
    else:
        full = jnp.concatenate([past, new_rows], axis=1)
        win = jnp.concatenate([win_buf, win_rows], axis=1)
        win_start = q_pos0 - win_buf.shape[1]
        win_state = win_rows
    n_tot = full.shape[1]
    n_pad = -(-n_tot // C_SEL_BLOCK) * C_SEL_BLOCK - n_tot
    full = jnp.pad(full, ((0, 0), (0, n_pad), (0, 0), (0, 0), (0, 0)))
    k_cmp = compress(full[:, :, 0], pos_k, wk1, wk2)
    v_cmp = compress(full[:, :, 1], pos_v, wv1, wv2)
    o = nsa_attention(q.reshape(B, L, C_HEADS, C_HEAD_DIM), q_pos0, k_cmp, v_cmp, full[:, :, 2], full[:, :, 3],
                      win[:, :, 0], win[:, :, 1], win_start, gates, slopes)
    return o @ w_o, new_rows, win_state


def setup_inputs(seed: int = 0) -> dict:
    key = jax.random.key(seed)
    keys = iter(jax.random.split(key, 64))

    def nrm(shape, scale=1.0):
        return scale * jax.random.normal(next(keys), shape, jnp.float32)

    def dense(shape):
        return nrm(shape, shape[-2] ** -0.5)

    def gain(shape):
        return 1.0 + nrm(shape, 0.05)

    n_pages = PAST_LEN // PAGE_SIZE
    n_used = DEC_BATCH * n_pages
    n_pool = n_used + max(1, n_used // 4)
    kvh, dh = C_KV_HEADS, C_HEAD_DIM
    return {
        'x_prompt': nrm((BATCH, SEQ, D_MODEL)),
        'x_sample': nrm((DEC_BATCH, DEC_SEQ, D_MODEL)),
        'cache_dil_w128': nrm((N_A_LAYERS, DEC_BATCH, min(A_WINDOWS[0], PAST_LEN), 2, A_SLOTS, A_HEAD_DIM)),
        'cache_dil_w512': nrm((N_A_LAYERS, DEC_BATCH, min(A_WINDOWS[1], PAST_LEN), 2, A_SLOTS, A_HEAD_DIM)),
        'cache_dil_w2048': nrm((N_A_LAYERS, DEC_BATCH, min(A_WINDOWS[2], PAST_LEN), 2, A_SLOTS, A_HEAD_DIM)),
        'state_gla': nrm((N_B_LAYERS, DEC_BATCH, B_HEADS, B_DK, B_DV)),
        'cache_nsa_win': nrm((N_C_LAYERS, DEC_BATCH, min(C_WINDOW, PAST_LEN), 2, kvh, dh)),
        'cache_nsa_kv': nrm((N_C_LAYERS, n_pool, PAGE_SIZE, 4, kvh, dh)),
        'cache_mem_kv': nrm((DEPTH, DEC_BATCH, MEM_LEN, 2, MEM_HEADS, MEM_HEAD_DIM)),
        'page_table': jax.random.permutation(next(keys), n_pool)[:n_used].reshape(DEC_BATCH, n_pages).astype(jnp.int32),
        'mem_prompt': nrm((BATCH, MEM_LEN, D_MODEL)),
        'g_mix': gain((DEPTH, D_MODEL)),
        'g_cross': gain((DEPTH, D_MODEL)),
        'g_mem': gain((DEPTH, D_MODEL)),
        'g_ffn': gain((DEPTH, D_MODEL)),
        'g_final': gain((D_MODEL,)),
        'w_a_qkv': dense((N_A_LAYERS, D_MODEL, A_QKV)),
        'w_a_o': dense((N_A_LAYERS, A_OUT, D_MODEL)),
        'w_b_in': dense((N_B_LAYERS, D_MODEL, sum(B_SIZES))),
        'w_b_gate2': dense((N_B_LAYERS, B_GATE_RANK, B_HEADS * B_DK)),
        'b_b_gate': nrm((N_B_LAYERS, B_HEADS * B_DK), 0.1),
        'g_b_head': gain((N_B_LAYERS, B_HEADS, B_DV)),
        'w_b_o': dense((N_B_LAYERS, B_HEADS * B_DV, D_MODEL)),
        'w_c_in': dense((N_C_LAYERS, D_MODEL, sum(C_SIZES))),
        'b_c_gate': nrm((N_C_LAYERS, 3 * C_HEADS), 0.1),
        'c_pos_k': nrm((N_C_LAYERS, C_CMP_BLOCK, dh), 0.5),
        'c_pos_v': nrm((N_C_LAYERS, C_CMP_BLOCK, dh), 0.5),
        'w_c_k1': dense((N_C_LAYERS, C_CMP_BLOCK * dh, C_CMP_HIDDEN)),
        'w_c_k2': dense((N_C_LAYERS, C_CMP_HIDDEN, dh)),
        'w_c_v1': dense((N_C_LAYERS, C_CMP_BLOCK * dh, C_CMP_HIDDEN)),
        'w_c_v2': dense((N_C_LAYERS, C_CMP_HIDDEN, dh)),
        'w_c_o': dense((N_C_LAYERS, C_HEADS * dh, D_MODEL)),
        'w_x_q': dense((DEPTH, D_MODEL, MEM_HEADS * MEM_HEAD_DIM)),
        'w_x_kv': dense((DEPTH, D_MODEL, 2 * MEM_HEADS * MEM_HEAD_DIM)),
        'w_x_o': dense((DEPTH, MEM_HEADS * MEM_HEAD_DIM, D_MODEL)),
        'w_ffn_in': dense((DEPTH, D_MODEL, 2 * FFN_HIDDEN)),
        'w_ffn_out': dense((DEPTH, FFN_HIDDEN, D_MODEL)),
    }


def reference(x_prompt, x_sample, cache_dil_w128, cache_dil_w512, cache_dil_w2048, state_gla, cache_nsa_win,
              cache_nsa_kv, cache_mem_kv, page_table, mem_prompt, g_mix, g_cross, g_mem, g_ffn, g_final,
              w_a_qkv, w_a_o, w_b_in, w_b_gate2, b_b_gate, g_b_head, w_b_o, w_c_in, b_c_gate, c_pos_k, c_pos_v,
              w_c_k1, w_c_k2, w_c_v1, w_c_v2, w_c_o, w_x_q, w_x_kv, w_x_o, w_ffn_in, w_ffn_out):
    slopes_a = alibi_slopes(A_GROUPS * A_SLOTS).reshape(A_GROUPS, A_SLOTS)
    slopes_c = alibi_slopes(C_HEADS)

    def trunk(x, q_pos0, mem_kvs, dil_bufs, gla_s0s, nsa_pasts, nsa_wins):
        new_dil, new_gla, new_rows, new_win = [], [], [], []
        for l in range(DEPTH):
            kind, j = l % N_MIXERS, l // N_MIXERS
            h = rmsnorm(x, g_mix[l])
            if kind == 0:
                o, st = mixer_dilated(h, w_a_qkv[j], w_a_o[j], slopes_a, dil_bufs[j])
                new_dil.append(st)
            elif kind == 1:
                o, st = mixer_gla(h, w_b_in[j], w_b_gate2[j], b_b_gate[j], g_b_head[j], w_b_o[j], gla_s0s[j])
                new_gla.append(st)
            else:
                o, rows, win = mixer_nsa(h, w_c_in[j], b_c_gate[j], c_pos_k[j], c_pos_v[j], w_c_k1[j], w_c_k2[j],
                                         w_c_v1[j], w_c_v2[j], w_c_o[j], slopes_c, nsa_pasts[j], nsa_wins[j], q_pos0)
                new_rows.append(rows)
                new_win.append(win)
            x = x + o
            x = x + cross_attention(rmsnorm(x, g_cross[l]), mem_kvs[l], w_x_q[l], w_x_o[l])
            x = x + swiglu(rmsnorm(x, g_ffn[l]), w_ffn_in[l], w_ffn_out[l])
        return rmsnorm(x, g_final), new_dil, new_gla, new_rows, new_win

    bp = x_prompt.shape[0]
    mem_kv_p = [(rmsnorm(mem_prompt, g_mem[l]) @ w_x_kv[l]).reshape(bp, mem_prompt.shape[1], 2, MEM_HEADS, MEM_HEAD_DIM)
                for l in range(DEPTH)]
    zero_state = jnp.zeros((bp, B_HEADS, B_DK, B_DV), jnp.float32)
    y_prompt, dil_p, gla_p, rows_p, win_p = trunk(
        x_prompt, 0, mem_kv_p, [None] * N_A_LAYERS, [zero_state] * N_B_LAYERS, [None] * N_C_LAYERS, [None] * N_C_LAYERS)

    bs = x_sample.shape[0]
    nsa_pasts = [cache_nsa_kv[j][page_table].reshape(bs, -1, 4, C_KV_HEADS, C_HEAD_DIM) for j in range(N_C_LAYERS)]
    y_sample, dil_s, gla_s, rows_s, win_s = trunk(
        x_sample, PAST_LEN, [cache_mem_kv[l] for l in range(DEPTH)],
        [[cache_dil_w128[j], cache_dil_w512[j], cache_dil_w2048[j]] for j in range(N_A_LAYERS)],
        [state_gla[j] for j in range(N_B_LAYERS)], nsa_pasts, [cache_nsa_win[j] for j in range(N_C_LAYERS)])

    dil128_prompt = jnp.stack([st[0] for st in dil_p])
    dil128_sample = jnp.stack([st[0] for st in dil_s])
    dil512_prompt = jnp.stack([st[1] for st in dil_p])
    dil512_sample = jnp.stack([st[1] for st in dil_s])
    dil2048_prompt = jnp.stack([st[2] for st in dil_p])
    dil2048_sample = jnp.stack([st[2] for st in dil_s])
    gla_prompt = jnp.stack(gla_p)
    gla_sample = jnp.stack(gla_s)
    nsa_win_prompt = jnp.stack(win_p)
    nsa_win_sample = jnp.stack(win_s)
    nsa_kv_prompt = jnp.stack(rows_p)
    nsa_kv_sample = jnp.stack(rows_s)
    mem_kv_prompt = jnp.stack(mem_kv_p)
    return (y_prompt, y_sample, dil128_prompt, dil128_sample, dil512_prompt, dil512_sample, dil2048_prompt,
            dil2048_sample, gla_prompt, gla_sample, nsa_win_prompt, nsa_win_sample, nsa_kv_prompt, nsa_kv_sample,
            mem_kv_prompt)
```

```python
import functools
import math

import jax
import jax.numpy as jnp
import numpy as np
from jax import lax
from jax.experimental import pallas as pl
from jax.experimental.pallas import tpu as pltpu

f32 = jnp.float32
bf16 = jnp.bfloat16

N_MIXERS = 3
A_WINDOWS = (128, 512, 2048)
A_DILATIONS = (1, 4, 16)
A_GROUPS = 3
A_SLOTS = 8
A_HEAD_DIM = 64
A_OUT = A_SLOTS * A_HEAD_DIM
A_WIN_STEPS = 128

B_HEADS = 4
B_DK = 128
B_DV = 256
B_GATE_RANK = 16
B_GATE_TAU = 16.0

C_HEADS = 16
C_KV_HEADS = 4
C_HPG = C_HEADS // C_KV_HEADS
C_HEAD_DIM = 64
C_CMP_BLOCK = 32
C_CMP_STRIDE = 16
C_SEL_BLOCK = 64
C_SEL_SHIFT = 6
C_TOPK = 16
C_WINDOW = 512
C_CMP_HIDDEN = 128

MEM_HEADS = 4
PAGE_SIZE = 128

RMS_EPS = 1e-6
NEG_INF = -1e30
FORCE = 1e30
TINY = 1e-30

LANES = 128
VMEM_LIMIT_BYTES = 56 * 1024 * 1024
HIGHEST = lax.Precision.HIGHEST


def _alibi_slopes(n):
    return np.asarray(2.0 ** (-8.0 * np.arange(1, n + 1) / n), dtype=np.float32)


SLOPES_A = _alibi_slopes(A_GROUPS * A_SLOTS).reshape(A_GROUPS, A_SLOTS)
SLOPES_C = _alibi_slopes(C_HEADS)


def _params(*sem):
    return pltpu.CompilerParams(dimension_semantics=sem, vmem_limit_bytes=VMEM_LIMIT_BYTES)


def _rms(x, g):
    return x * lax.rsqrt(jnp.mean(x * x, axis=-1, keepdims=True) + RMS_EPS) * g


def _dot_t(a, b):
    return lax.dot_general(a, b, (((1,), (1,)), ((), ())), preferred_element_type=f32)


def _pick_tile(n, target):
    best = LANES
    for t in range(LANES, min(n, target) + 1, LANES):
        if n % t == 0:
            best = t
    return best


def _mm_body(*refs, norm, res):
    it = iter(refs)
    x_ref = next(it)
    g_ref = next(it) if norm else None
    w_ref = next(it)
    r_ref = next(it) if res else None
    o_ref = next(it)
    xn_ref = next(it)

    @pl.when(pl.program_id(1) == 0)
    def _():
        x = x_ref[...].astype(f32)
        if norm:
            x = _rms(x, g_ref[...])
        xn_ref[...] = x.astype(bf16)

    acc = jnp.dot(xn_ref[...], w_ref[...], preferred_element_type=f32)
    if res:
        acc = acc + r_ref[...]
    o_ref[...] = acc.astype(o_ref.dtype)


def mm(x, w, *, g=None, res=None, out_dtype=f32, tm=512, tn=512):
    M, K = x.shape
    N = w.shape[1]
    tm = min(tm, M)
    tn = _pick_tile(N, tn)
    assert M % tm == 0 and N % tn == 0
    args = [x]
    specs = [pl.BlockSpec((tm, K), lambda i, j: (i, 0))]
    if g is not None:
        args.append(g.reshape(1, K))
        specs.append(pl.BlockSpec((1, K), lambda i, j: (0, 0)))
    args.append(w)
    specs.append(pl.BlockSpec((K, tn), lambda i, j: (0, j)))
    if res is not None:
        args.append(res)
        specs.append(pl.BlockSpec((tm, tn), lambda i, j: (i, j)))
    return pl.pallas_call(
        functools.partial(_mm_body, norm=g is not None, res=res is not None),
        out_shape=jax.ShapeDtypeStruct((M, N), out_dtype),
        grid=(M // tm, N // tn),
        in_specs=specs,
        out_specs=pl.BlockSpec((tm, tn), lambda i, j: (i, j)),
        scratch_shapes=[pltpu.VMEM((tm, K), bf16)],
        compiler_params=_params("parallel", "arbitrary"),
        name="mm",
    )(*args)


def _ffn_body(x_ref, g_ref, wa_ref, wb_ref, wo_ref, o_ref, xn_ref, acc_ref):
    j = pl.program_id(1)

    @pl.when(j == 0)
    def _():
        xn_ref[...] = _rms(x_ref[...], g_ref[...]).astype(bf16)
        acc_ref[...] = jnp.zeros_like(acc_ref)

    xn = xn_ref[...]
    a = jnp.dot(xn, wa_ref[...], preferred_element_type=f32)
    b = jnp.dot(xn, wb_ref[...], preferred_element_type=f32)
    h = (a * jax.nn.sigmoid(a) * b).astype(bf16)
    acc_ref[...] += jnp.dot(h, wo_ref[...], preferred_element_type=f32)

    @pl.when(j == pl.num_programs(1) - 1)
    def _():
        o_ref[...] = x_ref[...] + acc_ref[...]


def ffn(x, g, w_in, w_out, *, tm=512, th=704):
    M, D = x.shape
    F = w_out.shape[0]
    tm = min(tm, M)
    th = _pick_tile(F, th)
    nh = F // th
    return pl.pallas_call(
        _ffn_body,
        out_shape=jax.ShapeDtypeStruct((M, D), f32),
        grid=(M // tm, nh),
        in_specs=[
            pl.BlockSpec((tm, D), lambda i, j: (i, 0)),
            pl.BlockSpec((1, D), lambda i, j: (0, 0)),
            pl.BlockSpec((D, th), lambda i, j: (0, j)),
            pl.BlockSpec((D, th), lambda i, j: (0, j + nh)),
            pl.BlockSpec((th, D), lambda i, j: (j, 0)),
        ],
        out_specs=pl.BlockSpec((tm, D), lambda i, j: (i, 0)),
        scratch_shapes=[pltpu.VMEM((tm, D), bf16), pltpu.VMEM((tm, D), f32)],
        compiler_params=_params("parallel", "arbitrary"),
        name="ffn",
    )(x, g.reshape(1, D), w_in, w_in, w_out)


def _xattn_body(x_ref, g_ref, wq_ref, kv_ref, wo_ref, o_ref, *, heads):
    x = x_ref[...]
    D = x.shape[-1]
    dh = D // heads
    xn = _rms(x, g_ref[...]).astype(bf16)
    q = jnp.dot(xn, wq_ref[...], preferred_element_type=f32).astype(bf16)
    outs = []
    for h in range(heads):
        kh = kv_ref[:, h * dh:(h + 1) * dh]
        vh = kv_ref[:, D + h * dh:D + (h + 1) * dh]
        s = _dot_t(q[:, h * dh:(h + 1) * dh], kh) * dh ** -0.5
        m = jnp.max(s, axis=-1, keepdims=True)
        e = jnp.exp(s - m)
        p = e / jnp.sum(e, axis=-1, keepdims=True)
        outs.append(jnp.dot(p.astype(bf16), vh, preferred_element_type=f32))
    o = jnp.concatenate(outs, axis=-1).astype(bf16)
    o_ref[...] = x + jnp.dot(o, wo_ref[...], preferred_element_type=f32)


def xattn_prompt(x, g, wq, kv, wo, *, tq=512):
    B, L, D = x.shape
    Mem = kv.shape[1]
    tq = min(tq, L)
    return pl.pallas_call(
        functools.partial(_xattn_body, heads=MEM_HEADS),
        out_shape=jax.ShapeDtypeStruct((B, L, D), f32),
        grid=(B, L // tq),
        in_specs=[
            pl.BlockSpec((None, tq, D), lambda b, i: (b, i, 0)),
            pl.BlockSpec((1, D), lambda b, i: (0, 0)),
            pl.BlockSpec((D, D), lambda b, i: (0, 0)),
            pl.BlockSpec((None, Mem, 2 * D), lambda b, i: (b, 0, 0)),
            pl.BlockSpec((D, D), lambda b, i: (0, 0)),
        ],
        out_specs=pl.BlockSpec((None, tq, D), lambda b, i: (b, i, 0)),
        compiler_params=_params("parallel", "parallel"),
        name="xattn_prompt",
    )(x, g.reshape(1, D), wq, kv, wo)


def _dil_body(q_ref, kc_ref, kp_ref, vc_ref, vp_ref, o_ref, lse_ref, *, dil, slopes):
    i = pl.program_id(2)
    tq = q_ref.shape[0]
    nk = tq + A_WIN_STEPS
    scale = A_HEAD_DIM ** -0.5
    q = q_ref[...].astype(bf16)
    k = jnp.concatenate([kp_ref[...], kc_ref[...]], axis=0).astype(bf16)
    v = jnp.concatenate([vp_ref[...], vc_ref[...]], axis=0).astype(bf16)
    row = lax.broadcasted_iota(jnp.int32, (tq, nk), 0)
    col = lax.broadcasted_iota(jnp.int32, (tq, nk), 1)
    dist = row + A_WIN_STEPS - col
    valid = (dist >= 0) & (dist <= A_WIN_STEPS) & ((col >= A_WIN_STEPS) | (i > 0))
    distf = (dist * dil).astype(f32)
    lane = lax.broadcasted_iota(jnp.int32, (1, LANES), 1)
    o_tiles, l_tiles = [], []
    for c in range(A_OUT // LANES):
        qt = q[:, c * LANES:(c + 1) * LANES]
        kt = k[:, c * LANES:(c + 1) * LANES]
        vt = v[:, c * LANES:(c + 1) * LANES]
        halves = []
        for half in range(2):
            in_head = (lane >= A_HEAD_DIM) == bool(half)
            qm = jnp.where(in_head, qt, jnp.zeros_like(qt))
            s = _dot_t(qm, kt) * scale - float(slopes[2 * c + half]) * distf
            s = jnp.where(valid, s, NEG_INF)
            m = jnp.max(s, axis=-1, keepdims=True)
            e = jnp.exp(s - m)
            den = jnp.sum(e, axis=-1, keepdims=True)
            pv = jnp.dot((e / den).astype(bf16), vt, preferred_element_type=f32)
            halves.append((pv, m + jnp.log(den)))
        first = lane < A_HEAD_DIM
        o_tiles.append(jnp.where(first, halves[0][0], halves[1][0]))
        l_tiles.append(jnp.where(first, halves[0][1], halves[1][1]))
    o_ref[...] = jnp.concatenate(o_tiles, axis=-1)
    lse_ref[...] = jnp.concatenate(l_tiles, axis=-1)


def dilated_group(qkv, grp, *, tq=128):
    B, L, W3 = qkv.shape
    d = A_DILATIONS[grp]
    Lc = L // d
    tq = min(tq, Lc)
    assert L % d == 0 and Lc % tq == 0 and tq % A_WIN_STEPS == 0
    ncol = W3 // A_OUT
    view = qkv.reshape(B, Lc, d * W3)
    ratio = tq // A_WIN_STEPS

    def spec(t, prev=False):
        if prev:
            return pl.BlockSpec((None, A_WIN_STEPS, A_OUT),
                                lambda b, r, i: (b, jnp.maximum(i * ratio - 1, 0), r * ncol + t * A_GROUPS + grp))
        return pl.BlockSpec((None, tq, A_OUT), lambda b, r, i: (b, i, r * ncol + t * A_GROUPS + grp))

    out_spec = pl.BlockSpec((None, tq, A_OUT), lambda b, r, i: (b, i, r))
    o, lse = pl.pallas_call(
        functools.partial(_dil_body, dil=d, slopes=SLOPES_A[grp]),
        out_shape=[jax.ShapeDtypeStruct((B, Lc, d * A_OUT), f32)] * 2,
        grid=(B, d, Lc // tq),
        in_specs=[spec(0), spec(1), spec(1, True), spec(2), spec(2, True)],
        out_specs=[out_spec, out_spec],
        compiler_params=_params("parallel", "parallel", "parallel"),
        name=f"dilated_g{grp}",
    )(view, view, view, view, view)
    return o.reshape(B, L, A_OUT), lse.reshape(B, L, A_OUT)


def _dil_out_body(o0, o1, o2, l0, l1, l2, w_ref, r_ref, out_ref):
    ls = [l0[...], l1[...], l2[...]]
    m = jnp.maximum(jnp.maximum(ls[0], ls[1]), ls[2])
    es = [jnp.exp(l - m) for l in ls]
    den = es[0] + es[1] + es[2]
    o = (es[0] / den) * o0[...] + (es[1] / den) * o1[...] + (es[2] / den) * o2[...]
    out_ref[...] = r_ref[...] + jnp.dot(o.astype(bf16), w_ref[...], preferred_element_type=f32)


def dilated_merge_out(os, lses, w_o, res, *, tm=512):
    M, D = res.shape
    tm = min(tm, M)
    row = lambda n: pl.BlockSpec((tm, n), lambda i: (i, 0))
    return pl.pallas_call(
        _dil_out_body,
        out_shape=jax.ShapeDtypeStruct((M, D), f32),
        grid=(M // tm,),
        in_specs=[row(A_OUT)] * 6 + [pl.BlockSpec((A_OUT, D), lambda i: (0, 0)), row(D)],
        out_specs=row(D),
        compiler_params=_params("parallel"),
        name="dilated_merge_out",
    )(*os, *lses, w_o, res)


GLA_SUB = 16


def _log_sigmoid(z):
    return -(jnp.maximum(-z, 0.0) + jnp.log1p(jnp.exp(-jnp.abs(z))))


def _gla_body(q_ref, k_ref, v_ref, r_ref, gl_ref, w2_ref, bg_ref, gh_ref, s0_ref, y_ref, s_ref, a_ref):
    C = q_ref.shape[0]
    c = GLA_SUB

    @pl.when(pl.program_id(1) == 0)
    def _():
        s_ref[...] = s0_ref[...]

    z = jnp.dot(gl_ref[...], w2_ref[...], precision=HIGHEST, preferred_element_type=f32) + bg_ref[...]
    la = _log_sigmoid(z) / B_GATE_TAU
    rowC = lax.broadcasted_iota(jnp.int32, (C, C), 0)
    colC = lax.broadcasted_iota(jnp.int32, (C, C), 1)
    tri = (rowC >= colC).astype(f32)
    b_all = jnp.dot(tri, la, precision=HIGHEST, preferred_element_type=f32)
    row_k = lax.broadcasted_iota(jnp.int32, (C, B_DK), 0)
    col_c = lax.broadcasted_iota(jnp.int32, (c, C), 1)
    row_c = lax.broadcasted_iota(jnp.int32, (c, 1), 0)

    for h in range(B_HEADS):
        bh = b_all[:, h * B_DK:(h + 1) * B_DK]
        qh = q_ref[:, h * B_DK:(h + 1) * B_DK] * B_DK ** -0.5
        kh = k_ref[:, h * B_DK:(h + 1) * B_DK]
        vh = v_ref[:, h * B_DV:(h + 1) * B_DV].astype(bf16)
        S = s_ref[h]
        o = jnp.dot((qh * jnp.exp(bh)).astype(bf16), S.astype(bf16), preferred_element_type=f32)

        for I in range(C // c):
            r0 = I * c
            qI, kI, bI = qh[r0:r0 + c], kh[r0:r0 + c], bh[r0:r0 + c]
            if I == 0:
                A_I = jnp.zeros((c, C), f32)
            else:
                beta = bh[r0 - 1:r0]
                qt = qI * jnp.exp(bI - beta)
                kt = kh * jnp.exp(jnp.where(row_k < r0, beta - bh, 0.0))
                A_I = jnp.where(col_c < r0, _dot_t(qt.astype(bf16), kt.astype(bf16)), 0.0)
            for j in range(c):
                ex = jnp.exp(jnp.minimum(bI - bI[j:j + 1], 0.0))
                tj = jnp.sum(qI * kI[j:j + 1] * ex, axis=-1, keepdims=True)
                tj = jnp.where(row_c >= j, tj, 0.0)
                A_I = jnp.where(col_c == r0 + j, tj, A_I)
            a_ref[r0:r0 + c, :] = A_I
        o = o + jnp.dot(a_ref[...].astype(bf16), vh, preferred_element_type=f32)

        b_last = bh[C - 1:C]
        kdec = jnp.transpose(kh * jnp.exp(b_last - bh)).astype(bf16)
        decay = jnp.transpose(jnp.broadcast_to(jnp.exp(b_last), (8, B_DK)))[:, 0:1]
        s_ref[h] = decay * S + jnp.dot(kdec, vh, preferred_element_type=f32)

        on = _rms(o, gh_ref[:, h * B_DV:(h + 1) * B_DV])
        rh = r_ref[:, h * B_DV:(h + 1) * B_DV]
        y_ref[:, h * B_DV:(h + 1) * B_DV] = (on * (rh * jax.nn.sigmoid(rh))).astype(y_ref.dtype)


def gla_prompt(proj, w_gate2p, b_gate, g_head, s0, *, chunk=128):
    B, L, _ = proj.shape
    HK, HV = B_HEADS * B_DK, B_HEADS * B_DV
    C = min(chunk, L)
    assert L % C == 0 and C % GLA_SUB == 0
    return pl.pallas_call(
        _gla_body,
        out_shape=[jax.ShapeDtypeStruct((B, L, HV), bf16),
                   jax.ShapeDtypeStruct((B, B_HEADS, B_DK, B_DV), f32)],
        grid=(B, L // C),
        in_specs=[
            pl.BlockSpec((None, C, HK), lambda b, i: (b, i, 0)),
            pl.BlockSpec((None, C, HK), lambda b, i: (b, i, 1)),
            pl.BlockSpec((None, C, HV), lambda b, i: (b, i, 1)),
            pl.BlockSpec((None, C, HV), lambda b, i: (b, i, 2)),
            pl.BlockSpec((None, C, LANES), lambda b, i: (b, i, (2 * HK + 2 * HV) // LANES)),
            pl.BlockSpec((LANES, HK), lambda b, i: (0, 0)),
            pl.BlockSpec((1, HK), lambda b, i: (0, 0)),
            pl.BlockSpec((1, HV), lambda b, i: (0, 0)),
            pl.BlockSpec((None, B_HEADS, B_DK, B_DV), lambda b, i: (b, 0, 0, 0)),
        ],
        out_specs=[pl.BlockSpec((None, C, HV), lambda b, i: (b, i, 0)),
                   pl.BlockSpec((None, B_HEADS, B_DK, B_DV), lambda b, i: (b, 0, 0, 0))],
        scratch_shapes=[pltpu.VMEM((C, C), f32)],
        compiler_params=_params("parallel", "arbitrary"),
        name="gla_prompt",
    )(proj, proj, proj, proj, proj, w_gate2p, b_gate.reshape(1, HK), g_head.reshape(1, HV), s0)


def _cmp_mlp(x_ref, p_ref, w1_ref, w2_ref):
    n = x_ref.shape[0]
    H = C_CMP_HIDDEN
    z = jnp.dot(x_ref[...].astype(bf16), w1_ref[...], preferred_element_type=f32)
    pz = jnp.dot(p_ref[...].astype(bf16), w1_ref[...], preferred_element_type=f32)
    hid = z[:, :H] + pltpu.roll(z[:, H:], n - 1, 0) + pz[0:1, :H] + pz[1:2, H:]
    return jnp.dot(jax.nn.gelu(hid).astype(bf16), w2_ref[...], preferred_element_type=f32)


def _cmp_body(xk_ref, xv_ref, pk_ref, pv_ref, wk1_ref, wk2_ref, wv1_ref, wv2_ref, ok_ref, ov_ref):
    ok_ref[...] = _cmp_mlp(xk_ref, pk_ref, wk1_ref, wk2_ref).astype(ok_ref.dtype)
    ov_ref[...] = _cmp_mlp(xv_ref, pv_ref, wv1_ref, wv2_ref).astype(ov_ref.dtype)


def nsa_compress(xk, xv, pos_k, pos_v, wk1, wk2, wv1, wv2):
    B, G, n, W = xk.shape
    x_spec = pl.BlockSpec((None, None, n, W), lambda b, g: (b, g, 0, 0))
    full = lambda a: pl.BlockSpec(a.shape, lambda b, g: (0,) * a.ndim)
    o_spec = pl.BlockSpec((None, None, n, C_HEAD_DIM), lambda b, g: (b, g, 0, 0))
    consts = [pos_k, pos_v, wk1, wk2, wv1, wv2]
    return pl.pallas_call(
        _cmp_body,
        out_shape=[jax.ShapeDtypeStruct((B, G, n, C_HEAD_DIM), bf16)] * 2,
        grid=(B, G),
        in_specs=[x_spec, x_spec] + [full(a) for a in consts],
        out_specs=[o_spec, o_spec],
        compiler_params=_params("parallel", "parallel"),
        name="nsa_compress",
    )(xk, xv, *consts)


def _nsa_select_mask(imp, t1, n_slc):
    tq, NS = imp.shape
    blk = lax.broadcasted_iota(jnp.int32, (1, NS), 1)
    cur = t1 >> C_SEL_SHIFT
    forced = (blk == 0) | (blk == cur) | (blk == cur - 1)
    score = jnp.where(blk * C_SEL_BLOCK <= t1, jnp.where(forced, FORCE, imp), NEG_INF)
    sT = jnp.transpose(score)
    blk_col = lax.broadcasted_iota(jnp.int32, (NS, 1), 0)
    cnt = jnp.zeros((NS, tq), f32)
    for j in range(n_slc):
        rj = sT[j:j + 1, :]
        ge = jnp.where(rj >= sT, 1.0, 0.0)
        gt = jnp.where(rj > sT, 1.0, 0.0)
        cnt = cnt + jnp.where(blk_col > j, ge, gt)
    keep = jnp.where(cnt < C_TOPK, sT, NEG_INF) > 0.5 * NEG_INF
    return jnp.transpose(jnp.where(keep, 0.0, NEG_INF))


def _nsa_body(sl_ref, q_ref, gate_ref, bg_ref, kc_ref, vc_ref, ks_ref, vs_ref, kw_ref, vw_ref,
              o_ref, m_ref, l_ref, acc_ref, *, tk):
    g = pl.program_id(1)
    i = pl.program_id(2)
    tq = q_ref.shape[0]
    R = C_HPG * tq
    L = ks_ref.shape[0]
    n_cmp = kc_ref.shape[0]
    n_slc = L // C_SEL_BLOCK
    NS = -(-n_slc // LANES) * LANES
    dh = C_HEAD_DIM
    scale = dh ** -0.5
    q0 = i * tq

    q = q_ref[...]
    qs = jnp.concatenate([q[:, hh * dh:(hh + 1) * dh] for hh in range(C_HPG)], axis=0).astype(bf16)
    slope = jnp.concatenate([jnp.full((tq, 1), sl_ref[g * C_HPG + hh], f32) for hh in range(C_HPG)], axis=0)
    t1 = q0 + lax.broadcasted_iota(jnp.int32, (tq, 1), 0)
    t = jnp.concatenate([t1] * C_HPG, axis=0)
    tf = t.astype(f32)

    cpos = lax.broadcasted_iota(jnp.int32, (1, n_cmp), 1) * C_CMP_STRIDE + (C_CMP_BLOCK - 1)
    s = _dot_t(qs, kc_ref[...]) * scale - slope * (tf - cpos.astype(f32))
    mask = cpos <= t
    s = jnp.where(mask, s, NEG_INF)
    m = jnp.max(s, axis=-1, keepdims=True)
    e = jnp.where(mask, jnp.exp(s - m), 0.0)
    p = e / jnp.maximum(jnp.sum(e, axis=-1, keepdims=True), TINY)
    o_c = jnp.dot(p.astype(bf16), vc_ref[...], preferred_element_type=f32)

    psum = p[0:tq]
    for hh in range(1, C_HPG):
        psum = psum + p[hh * tq:(hh + 1) * tq]
    c_start = lax.broadcasted_iota(jnp.int32, (n_cmp, NS), 0) * C_CMP_STRIDE
    jb = lax.broadcasted_iota(jnp.int32, (n_cmp, NS), 1)
    cover = ((c_start < (jb + 1) * C_SEL_BLOCK) & (c_start + C_CMP_BLOCK > jb * C_SEL_BLOCK)).astype(f32)
    imp = jnp.dot(psum, cover, precision=HIGHEST, preferred_element_type=f32)
    neg1 = _nsa_select_mask(imp, t1, n_slc).astype(bf16)
    neg = jnp.concatenate([neg1] * C_HPG, axis=0)

    m_ref[...] = jnp.full_like(m_ref, NEG_INF)
    l_ref[...] = jnp.zeros_like(l_ref)
    acc_ref[...] = jnp.zeros_like(acc_ref)
    e_row = lax.broadcasted_iota(jnp.int32, (NS, tk), 0)
    e_col = lax.broadcasted_iota(jnp.int32, (NS, tk), 1)

    def sel_step(kt, carry):
        k0 = pl.multiple_of(kt * tk, tk)
        kk = ks_ref[pl.ds(k0, tk), :]
        vv = vs_ref[pl.ds(k0, tk), :]
        expand = (e_row == ((k0 + e_col) >> C_SEL_SHIFT)).astype(bf16)
        kpos = k0 + lax.broadcasted_iota(jnp.int32, (1, tk), 1)
        sc = _dot_t(qs, kk) * scale - slope * (tf - kpos.astype(f32))
        sc = sc + jnp.dot(neg, expand, preferred_element_type=f32)
        sc = jnp.where(kpos <= t, sc, NEG_INF)
        m_old = m_ref[...]
        m_new = jnp.maximum(m_old, jnp.max(sc, axis=-1, keepdims=True))
        alpha = jnp.exp(m_old - m_new)
        ee = jnp.exp(sc - m_new)
        l_ref[...] = alpha * l_ref[...] + jnp.sum(ee, axis=-1, keepdims=True)
        acc_ref[...] = alpha * acc_ref[...] + jnp.dot(ee.astype(bf16), vv, preferred_element_type=f32)
        m_ref[...] = m_new
        return carry

    lax.fori_loop(0, (q0 + tq - 1) // tk + 1, sel_step, 0)
    o_s = acc_ref[...] / l_ref[...]

    W = min(L, C_WINDOW + tq)
    w0 = pl.multiple_of(jnp.clip(q0 - C_WINDOW, 0, L - W), tq)
    wpos = w0 + lax.broadcasted_iota(jnp.int32, (1, W), 1)
    okw = (wpos <= t) & (wpos >= t - C_WINDOW)
    s = _dot_t(qs, kw_ref[pl.ds(w0, W), :]) * scale - slope * (tf - wpos.astype(f32))
    s = jnp.where(okw, s, NEG_INF)
    m = jnp.max(s, axis=-1, keepdims=True)
    e = jnp.exp(s - m)
    p = e / jnp.sum(e, axis=-1, keepdims=True)
    o_w = jnp.dot(p.astype(bf16), vw_ref[pl.ds(w0, W), :], preferred_element_type=f32)

    gs = jax.nn.sigmoid(gate_ref[...] + bg_ref[...])
    lane = lax.broadcasted_iota(jnp.int32, (1, LANES), 1)

    def gate(r):
        cols = [jnp.sum(jnp.where(lane == (g * C_HPG + hh) * 3 + r, gs, 0.0), axis=-1, keepdims=True)
                for hh in range(C_HPG)]
        return jnp.concatenate(cols, axis=0)

    o = gate(0) * o_c + gate(1) * o_s + gate(2) * o_w
    o_ref[...] = jnp.concatenate([o[hh * tq:(hh + 1) * tq] for hh in range(C_HPG)], axis=-1).astype(o_ref.dtype)


def nsa_attention_prompt(proj, b_gate_p, kc, vc, ks, vs, kw, vw, *, tq=128, tk=512):
    B, L, _ = proj.shape
    G, dh = C_KV_HEADS, C_HEAD_DIM
    tq = min(tq, L)
    tk = min(tk, L)
    R = C_HPG * tq
    n_cmp = kc.shape[2]
    gate_blk = (C_HEADS * dh + 6 * G * dh) // LANES
    seq = lambda n: pl.BlockSpec((None, None, n, dh), lambda b, g, i, sl: (b, g, 0, 0))
    return pl.pallas_call(
        functools.partial(_nsa_body, tk=tk),
        out_shape=jax.ShapeDtypeStruct((B, L, C_HEADS * dh), bf16),
        grid_spec=pltpu.PrefetchScalarGridSpec(
            num_scalar_prefetch=1,
            grid=(B, G, L // tq),
            in_specs=[
                pl.BlockSpec((None, tq, C_HPG * dh), lambda b, g, i, sl: (b, i, g)),
                pl.BlockSpec((None, tq, LANES), lambda b, g, i, sl: (b, i, gate_blk)),
                pl.BlockSpec((1, LANES), lambda b, g, i, sl: (0, 0)),
                seq(n_cmp), seq(n_cmp), seq(L), seq(L), seq(L), seq(L),
            ],
            out_specs=pl.BlockSpec((None, tq, C_HPG * dh), lambda b, g, i, sl: (b, i, g)),
            scratch_shapes=[pltpu.VMEM((R, 1), f32), pltpu.VMEM((R, 1), f32), pltpu.VMEM((R, dh), f32)],
        ),
        compiler_params=_params("parallel", "parallel", "parallel"),
        name="nsa_attention_prompt",
    )(jnp.asarray(SLOPES_C), proj, proj, b_gate_p, kc, vc, ks, vs, kw, vw)


def _head_maps(heads, dh):
    hd = heads * dh
    r = lax.broadcasted_iota(jnp.int32, (hd, LANES), 0)
    c = lax.broadcasted_iota(jnp.int32, (hd, LANES), 1)
    seg = jnp.where((r >= c * dh) & (r < (c + 1) * dh), 1.0, 0.0)
    r2 = lax.broadcasted_iota(jnp.int32, (LANES, hd), 0)
    c2 = lax.broadcasted_iota(jnp.int32, (LANES, hd), 1)
    exp_ = jnp.where((c2 >= r2 * dh) & (c2 < (r2 + 1) * dh), 1.0, 0.0)
    return seg, exp_


def _hdot(a, b):
    return jnp.dot(a, b, precision=HIGHEST, preferred_element_type=f32)


def _sq_body(*refs, ng, heads, dh, dils, has_new):
    it = iter(refs)
    sl_ref = next(it)
    q_refs = [next(it) for _ in range(ng)]
    c_refs = [next(it) for _ in range(ng)]
    kn_refs = [next(it) for _ in range(ng)] if has_new else []
    vn_refs = [next(it) for _ in range(ng)] if has_new else []
    o_ref = next(it)
    hd = heads * dh
    scale = dh ** -0.5
    seg, exp_ = _head_maps(heads, dh)
    scores, news = [], []
    for g in range(ng):
        q = q_refs[g][...]
        nk = c_refs[g].shape[0]
        s = _hdot(c_refs[g][:, :hd] * q, seg) * scale
        dist = (nk - lax.broadcasted_iota(jnp.int32, (nk, 1), 0)) * dils[g]
        scores.append(s - sl_ref[g:g + 1, :] * dist.astype(f32))
        if has_new:
            news.append(_hdot(kn_refs[g][...] * q, seg) * scale)
    m = functools.reduce(jnp.maximum, [jnp.max(s, axis=0, keepdims=True) for s in scores] + news)
    den = jnp.zeros((1, LANES), f32)
    acc = jnp.zeros((1, hd), f32)
    for g in range(ng):
        e = jnp.exp(scores[g] - m)
        den = den + jnp.sum(e, axis=0, keepdims=True)
        acc = acc + jnp.sum(_hdot(e, exp_) * c_refs[g][:, hd:], axis=0, keepdims=True)
        if has_new:
            en = jnp.exp(news[g] - m)
            den = den + en
            acc = acc + _hdot(en, exp_) * vn_refs[g][...]
    o_ref[...] = acc / _hdot(den, exp_)


def sq_attention(slopes, qs, caches, news, *, heads, dh, dils):
    ng = len(qs)
    hd = heads * dh
    Bs = qs[0][0].shape[0]
    row = lambda blk: pl.BlockSpec((None, 1, hd), lambda b, blk=blk: (b, 0, blk))
    args = [slopes] + [a for a, _ in qs] + list(caches)
    specs = [pl.BlockSpec(slopes.shape, lambda b: (0, 0))] + [row(blk) for _, blk in qs]
    specs += [pl.BlockSpec((None, c.shape[1], 2 * hd), lambda b: (b, 0, 0)) for c in caches]
    if news is not None:
        args += [k[0] for k, _ in news] + [v[0] for _, v in news]
        specs += [row(k[1]) for k, _ in news] + [row(v[1]) for _, v in news]
    return pl.pallas_call(
        functools.partial(_sq_body, ng=ng, heads=heads, dh=dh, dils=dils, has_new=news is not None),
        out_shape=jax.ShapeDtypeStruct((Bs, 1, hd), f32),
        grid=(Bs,),
        in_specs=specs,
        out_specs=pl.BlockSpec((None, 1, hd), lambda b: (b, 0, 0)),
        compiler_params=_params("parallel"),
        name="sq_attention",
    )(*args)


def _gla_step_body(q_ref, k_ref, v_ref, r_ref, gl_ref, w2_ref, bg_ref, gh_ref, s0_ref, y_ref, s_ref):
    z = _hdot(gl_ref[...], w2_ref[...]) + bg_ref[...]
    a = jnp.exp(_log_sigmoid(z) / B_GATE_TAU)
    pad = jnp.zeros((5, B_DK), f32)
    for h in range(B_HEADS):
        ks_ = slice(h * B_DK, (h + 1) * B_DK)
        vs_ = slice(h * B_DV, (h + 1) * B_DV)
        rows = jnp.concatenate([a[:, ks_], k_ref[:, ks_], q_ref[:, ks_] * B_DK ** -0.5, pad], axis=0)
        cols = jnp.transpose(rows)
        S = cols[:, 0:1] * s0_ref[h] + cols[:, 1:2] * v_ref[:, vs_]
        s_ref[h] = S
        o = jnp.sum(cols[:, 2:3] * S, axis=0, keepdims=True)
        rh = r_ref[:, vs_]
        y_ref[:, vs_] = (_rms(o, gh_ref[:, vs_]) * (rh * jax.nn.sigmoid(rh))).astype(y_ref.dtype)


def gla_step(proj, w_gate2p, b_gate, g_head, s0):
    Bs = proj.shape[0]
    HK, HV = B_HEADS * B_DK, B_HEADS * B_DV
    st = pl.BlockSpec((None, B_HEADS, B_DK, B_DV), lambda b: (b, 0, 0, 0))
    return pl.pallas_call(
        _gla_step_body,
        out_shape=[jax.ShapeDtypeStruct((Bs, 1, HV), bf16),
                   jax.ShapeDtypeStruct((Bs, B_HEADS, B_DK, B_DV), f32)],
        grid=(Bs,),
        in_specs=[
            pl.BlockSpec((None, 1, HK), lambda b: (b, 0, 0)),
            pl.BlockSpec((None, 1, HK), lambda b: (b, 0, 1)),
            pl.BlockSpec((None, 1, HV), lambda b: (b, 0, 1)),
            pl.BlockSpec((None, 1, HV), lambda b: (b, 0, 2)),
            pl.BlockSpec((None, 1, LANES), lambda b: (b, 0, (2 * HK + 2 * HV) // LANES)),
            pl.BlockSpec((LANES, HK), lambda b: (0, 0)),
            pl.BlockSpec((1, HK), lambda b: (0, 0)),
            pl.BlockSpec((1, HV), lambda b: (0, 0)),
            st,
        ],
        out_specs=[pl.BlockSpec((None, 1, HV), lambda b: (b, 0, 0)), st],
        compiler_params=_params("parallel"),
        name="gla_step",
    )(proj, proj, proj, proj, proj, w_gate2p, b_gate.reshape(1, HK), g_head.reshape(1, HV), s0)


PAGES_PER_STEP = 8


def _cmp_sample_body(pt_ref, *refs):
    pages = refs[:PAGES_PER_STEP]
    (pk_ref, pv_ref, wk1_ref, wk2_ref, wv1_ref, wv2_ref, wkp_ref, wvp_ref,
     ok_ref, ov_ref, x_ref) = refs[PAGES_PER_STEP:]
    i = pl.program_id(1)
    H = C_CMP_HIDDEN
    n_tiles = x_ref.shape[0]
    for k in range(PAGES_PER_STEP):
        r0 = pl.multiple_of((i * PAGES_PER_STEP + k) * PAGE_SIZE, PAGE_SIZE)
        for ct in range(n_tiles):
            x_ref[ct, pl.ds(r0, PAGE_SIZE), :] = pages[k][:, ct * LANES:(ct + 1) * LANES]

    @pl.when(i == pl.num_programs(1) - 1)
    def _():
        n = x_ref.shape[1] // C_CMP_STRIDE
        per_kind = n_tiles // 2
        for kind, (p_ref, w1_ref, w2_ref, wp_ref, o_ref) in enumerate(
                ((pk_ref, wk1_ref, wk2_ref, wkp_ref, ok_ref), (pv_ref, wv1_ref, wv2_ref, wvp_ref, ov_ref))):
            pz = jnp.dot(p_ref[...].astype(bf16), w1_ref[...], preferred_element_type=f32)
            posb = pz[0:1, :H] + pz[1:2, H:]
            outs = []
            for tile in range(per_kind):
                ct = kind * per_kind + tile
                z2 = jnp.zeros((n, 4 * H), f32)
                for p in range(C_CMP_STRIDE):
                    xp = x_ref[ct, pl.ds(p, n, stride=C_CMP_STRIDE), :].astype(bf16)
                    z2 = z2 + jnp.dot(xp, wp_ref[p], preferred_element_type=f32)
                for z in (z2[:, :2 * H], z2[:, 2 * H:]):
                    hid = z[:, :H] + pltpu.roll(z[:, H:], n - 1, 0) + posb
                    outs.append(jnp.dot(jax.nn.gelu(hid).astype(bf16), w2_ref[...], preferred_element_type=f32))
            o_ref[...] = jnp.concatenate(outs, axis=-1)


def _pair_block_diag(w1):
    dh = C_HEAD_DIM
    w = w1.reshape(C_CMP_STRIDE, dh, w1.shape[1])
    z = jnp.zeros_like(w)
    return jnp.concatenate([jnp.concatenate([w, z], axis=-1), jnp.concatenate([z, w], axis=-1)], axis=1)


def nsa_compress_sample(page_table, cache, pos_k, pos_v, wk1, wk2, wv1, wv2):
    Bs, n_pages = page_table.shape
    G, dh = C_KV_HEADS, C_HEAD_DIM
    assert n_pages % PAGES_PER_STEP == 0
    n_rows = n_pages * PAGE_SIZE
    n = n_rows // C_CMP_STRIDE
    page = lambda k: pl.BlockSpec((None, PAGE_SIZE, 2 * G * dh),
                                  lambda b, i, pt, k=k: (pt[b, i * PAGES_PER_STEP + k], 0, 0))
    consts = [pos_k, pos_v, wk1, wk2, wv1, wv2, _pair_block_diag(wk1), _pair_block_diag(wv1)]
    full = lambda a: pl.BlockSpec(a.shape, lambda b, i, pt: (0,) * a.ndim)
    o_spec = pl.BlockSpec((None, n, G * dh), lambda b, i, pt: (b, 0, 0))
    return pl.pallas_call(
        _cmp_sample_body,
        out_shape=[jax.ShapeDtypeStruct((Bs, n, G * dh), f32)] * 2,
        grid_spec=pltpu.PrefetchScalarGridSpec(
            num_scalar_prefetch=1,
            grid=(Bs, n_pages // PAGES_PER_STEP),
            in_specs=[page(k) for k in range(PAGES_PER_STEP)] + [full(a) for a in consts],
            out_specs=[o_spec, o_spec],
            scratch_shapes=[pltpu.VMEM((2 * G * dh // LANES, n_rows, LANES), f32)],
        ),
        compiler_params=_params("parallel", "arbitrary"),
        name="nsa_compress_sample",
    )(page_table, *([cache] * PAGES_PER_STEP), *consts)


def _cmp_topk_sample_body(sl_ref, q_ref, kc_ref, vc_ref, oc_ref, idx_ref, *, t):
    G, dh = C_KV_HEADS, C_HEAD_DIM
    n = kc_ref.shape[0]
    n_slc = t // C_SEL_BLOCK + 1
    NS = -(-n_slc // LANES) * LANES
    cpos = lax.broadcasted_iota(jnp.int32, (1, n), 1) * C_CMP_STRIDE + (C_CMP_BLOCK - 1)
    mask = cpos <= t
    c_start = lax.broadcasted_iota(jnp.int32, (n, NS), 0) * C_CMP_STRIDE
    jb = lax.broadcasted_iota(jnp.int32, (n, NS), 1)
    cover = ((c_start < (jb + 1) * C_SEL_BLOCK) & (c_start + C_CMP_BLOCK > jb * C_SEL_BLOCK)).astype(f32)
    blk = lax.broadcasted_iota(jnp.int32, (1, NS), 1)
    cur = t // C_SEL_BLOCK
    forced = (blk == 0) | (blk == cur) | (blk == cur - 1)
    lane = lax.broadcasted_iota(jnp.int32, (1, LANES), 1)
    o_parts, idx_rows = [], []
    for g in range(G):
        q8 = jnp.concatenate([q_ref[:, (g * C_HPG + hh) * dh:(g * C_HPG + hh + 1) * dh] for hh in range(C_HPG)]
                             + [jnp.zeros((8 - C_HPG, dh), f32)], axis=0)
        slope = jnp.concatenate([jnp.full((1, 1), sl_ref[g * C_HPG + hh], f32) for hh in range(C_HPG)]
                                + [jnp.zeros((8 - C_HPG, 1), f32)], axis=0)
        s = lax.dot_general(q8, kc_ref[:, g * dh:(g + 1) * dh], (((1,), (1,)), ((), ())),
                            precision=HIGHEST, preferred_element_type=f32) * dh ** -0.5
        s = s - slope * (float(t) - cpos.astype(f32))
        s = jnp.where(mask, s, NEG_INF)
        m = jnp.max(s, axis=-1, keepdims=True)
        e = jnp.where(mask, jnp.exp(s - m), 0.0)
        p = e / jnp.maximum(jnp.sum(e, axis=-1, keepdims=True), TINY)
        o = _hdot(p, vc_ref[:, g * dh:(g + 1) * dh])
        o_parts += [o[hh:hh + 1] for hh in range(C_HPG)]
        imp = _hdot(jnp.sum(p[0:C_HPG], axis=0, keepdims=True), cover)
        score = jnp.where(blk * C_SEL_BLOCK <= t, jnp.where(forced, FORCE, imp), NEG_INF)
        row = jnp.zeros((1, LANES), jnp.int32)
        for r in range(C_TOPK):
            mx = jnp.max(score, axis=-1, keepdims=True)
            pick = jnp.min(jnp.where(score == mx, blk, NS), axis=-1, keepdims=True)
            row = jnp.where(lane == r, pick, row)
            score = jnp.where(blk == pick, -3e38, score)
        idx_rows.append(row)
    oc_ref[...] = jnp.concatenate(o_parts, axis=-1)
    idx_ref[...] = jnp.concatenate(idx_rows + [jnp.zeros((8 - G, LANES), jnp.int32)], axis=0)


def nsa_cmp_topk_sample(proj, kc, vc, t):
    Bs = proj.shape[0]
    n, W = kc.shape[1:]
    HD = C_HEADS * C_HEAD_DIM
    return pl.pallas_call(
        functools.partial(_cmp_topk_sample_body, t=t),
        out_shape=[jax.ShapeDtypeStruct((Bs, 1, HD), f32), jax.ShapeDtypeStruct((Bs, 8, LANES), jnp.int32)],
        grid_spec=pltpu.PrefetchScalarGridSpec(
            num_scalar_prefetch=1,
            grid=(Bs,),
            in_specs=[pl.BlockSpec((None, 1, HD), lambda b, sl: (b, 0, 0)),
                      pl.BlockSpec((None, n, W), lambda b, sl: (b, 0, 0)),
                      pl.BlockSpec((None, n, W), lambda b, sl: (b, 0, 0))],
            out_specs=[pl.BlockSpec((None, 1, HD), lambda b, sl: (b, 0, 0)),
                       pl.BlockSpec((None, 8, LANES), lambda b, sl: (b, 0, 0))],
        ),
        compiler_params=_params("parallel"),
        name="nsa_cmp_topk_sample",
    )(jnp.asarray(SLOPES_C), proj, kc, vc)


def _sel_win_sample_body(pt_ref, ix_ref, sl_ref, *refs, t):
    blocks = refs[:C_TOPK]
    (win_ref, q_ref, ksn_ref, vsn_ref, kwn_ref, vwn_ref, gate_ref, bg_ref, oc_ref, o_ref) = refs[C_TOPK:]
    b = pl.program_id(0)
    g = pl.program_id(1)
    G, dh = C_KV_HEADS, C_HEAD_DIM
    GD = G * dh
    scale = dh ** -0.5
    n_past = t // C_SEL_BLOCK
    lane_g = lax.broadcasted_iota(jnp.int32, (1, GD), 1) // dh
    q8 = jnp.concatenate(
        [jnp.concatenate([q_ref[:, (gg * C_HPG + hh) * dh:(gg * C_HPG + hh + 1) * dh] for gg in range(G)], axis=-1)
         for hh in range(C_HPG)] + [jnp.zeros((8 - C_HPG, GD), f32)], axis=0)
    q8 = jnp.where(lane_g == g, q8, 0.0)
    slope = jnp.concatenate([jnp.full((1, 1), sl_ref[g * C_HPG + hh], f32) for hh in range(C_HPG)]
                            + [jnp.zeros((8 - C_HPG, 1), f32)], axis=0)

    def softmax_pv(scores, values, s_new, v_new):
        m = functools.reduce(jnp.maximum, [jnp.max(s, axis=-1, keepdims=True) for s in scores] + [s_new])
        e_new = jnp.exp(s_new - m)
        den = e_new
        acc = e_new * v_new
        for s, v in zip(scores, values):
            e = jnp.exp(s - m)
            den = den + jnp.sum(e, axis=-1, keepdims=True)
            acc = acc + _hdot(e, v)
        o = acc / den
        return functools.reduce(
            jnp.add, [jnp.where(g == gg, o[:, gg * dh:(gg + 1) * dh], 0.0) for gg in range(G)])

    pos64 = lax.broadcasted_iota(jnp.int32, (1, C_SEL_BLOCK), 1)
    scores, values = [], []
    for s_i in range(C_TOPK):
        j = ix_ref[b, g * C_TOPK + s_i]
        kpos = j * C_SEL_BLOCK + pos64
        sc = lax.dot_general(q8, blocks[s_i][:, :GD], (((1,), (1,)), ((), ())),
                             precision=HIGHEST, preferred_element_type=f32) * scale
        sc = sc - slope * (t - kpos).astype(f32)
        scores.append(jnp.where(j < n_past, sc, NEG_INF))
        values.append(blocks[s_i][:, GD:])
    s_new = jnp.sum(q8 * ksn_ref[...], axis=-1, keepdims=True) * scale
    o_s = softmax_pv(scores, values, s_new, vsn_ref[...])

    nw = win_ref.shape[0]
    wdist = (nw - lax.broadcasted_iota(jnp.int32, (1, nw), 1)).astype(f32)
    sw = lax.dot_general(q8, win_ref[:, :GD], (((1,), (1,)), ((), ())),
                         precision=HIGHEST, preferred_element_type=f32) * scale - slope * wdist
    s_new = jnp.sum(q8 * kwn_ref[...], axis=-1, keepdims=True) * scale
    o_w = softmax_pv([sw], [win_ref[:, GD:]], s_new, vwn_ref[...])

    gs = jax.nn.sigmoid(gate_ref[...] + bg_ref[...])
    lane = lax.broadcasted_iota(jnp.int32, (1, LANES), 1)
    outs = []
    for hh in range(C_HPG):
        gate = [jnp.sum(jnp.where(lane == (g * C_HPG + hh) * 3 + r, gs, 0.0), axis=-1, keepdims=True)
                for r in range(3)]
        outs.append(gate[0] * oc_ref[:, hh * dh:(hh + 1) * dh] + gate[1] * o_s[hh:hh + 1] + gate[2] * o_w[hh:hh + 1])
    o_ref[...] = jnp.concatenate(outs, axis=-1).astype(o_ref.dtype)


def nsa_sel_win_sample(page_table, sel_idx, cache, win, proj, b_gate_p, o_c, t):
    Bs = proj.shape[0]
    G, dh = C_KV_HEADS, C_HEAD_DIM
    GD = G * dh
    assert win.shape[1] == C_WINDOW and t % C_SEL_BLOCK == 0
    n_past = t // C_SEL_BLOCK
    per_page = PAGE_SIZE // C_SEL_BLOCK

    def blk_spec(s_i):
        def imap(b, g, pt, ix, sl):
            j = jnp.minimum(ix[b, g * C_TOPK + s_i], n_past - 1)
            return (pt[b, j // per_page], j % per_page, 1)
        return pl.BlockSpec((None, C_SEL_BLOCK, 2 * GD), imap)

    row = lambda w, blk: pl.BlockSpec((None, 1, w), lambda b, g, pt, ix, sl, blk=blk: (b, 0, blk))
    base = C_HEADS * dh // GD
    gate_blk = (C_HEADS * dh + 6 * GD) // LANES
    return pl.pallas_call(
        functools.partial(_sel_win_sample_body, t=t),
        out_shape=jax.ShapeDtypeStruct((Bs, 1, C_HEADS * dh), bf16),
        grid_spec=pltpu.PrefetchScalarGridSpec(
            num_scalar_prefetch=3,
            grid=(Bs, G),
            in_specs=[blk_spec(s_i) for s_i in range(C_TOPK)] + [
                pl.BlockSpec((None, C_WINDOW, 2 * GD), lambda b, g, pt, ix, sl: (b, 0, 0)),
                row(C_HEADS * dh, 0),
                row(GD, base + 2), row(GD, base + 3), row(GD, base + 4), row(GD, base + 5),
                row(LANES, gate_blk),
                pl.BlockSpec((1, LANES), lambda b, g, pt, ix, sl: (0, 0)),
                pl.BlockSpec((None, 1, C_HPG * dh), lambda b, g, pt, ix, sl: (b, 0, g)),
            ],
            out_specs=pl.BlockSpec((None, 1, C_HPG * dh), lambda b, g, pt, ix, sl: (b, 0, g)),
        ),
        compiler_params=_params("parallel", "parallel"),
        name="nsa_sel_win_sample",
    )(page_table, sel_idx, jnp.asarray(SLOPES_C), *([cache] * C_TOPK), win, proj, proj, proj, proj, proj,
      proj, b_gate_p, o_c)


def _norm_body(x_ref, g_ref, o_ref):
    o_ref[...] = _rms(x_ref[...], g_ref[...])


def rmsnorm_rows(x, g, *, tm=1024):
    M, D = x.shape
    tm = min(tm, M)
    return pl.pallas_call(
        _norm_body,
        out_shape=jax.ShapeDtypeStruct((M, D), f32),
        grid=(M // tm,),
        in_specs=[pl.BlockSpec((tm, D), lambda i: (i, 0)), pl.BlockSpec((1, D), lambda i: (0, 0))],
        out_specs=pl.BlockSpec((tm, D), lambda i: (i, 0)),
        compiler_params=_params("parallel"),
        name="rmsnorm",
    )(x, g.reshape(1, D))


def _pad_cols(a, n):
    return jnp.pad(a, ((0, 0), (0, n - a.shape[1])))


def kernel(x_prompt, x_sample, cache_dil_w128, cache_dil_w512, cache_dil_w2048, state_gla, cache_nsa_win,
           cache_nsa_kv, cache_mem_kv, page_table, mem_prompt, g_mix, g_cross, g_mem, g_ffn, g_final,
           w_a_qkv, w_a_o, w_b_in, w_b_gate2, b_b_gate, g_b_head, w_b_o, w_c_in, b_c_gate, c_pos_k, c_pos_v,
           w_c_k1, w_c_k2, w_c_v1, w_c_v2, w_c_o, w_x_q, w_x_kv, w_x_o, w_ffn_in, w_ffn_out):
    Bp, L, D = x_prompt.shape
    Bs = x_sample.shape[0]
    depth = g_mix.shape[0]
    n_pages = page_table.shape[1]
    t_s = n_pages * PAGE_SIZE
    G, dh = C_KV_HEADS, C_HEAD_DIM
    GD = G * dh
    HK, HV = B_HEADS * B_DK, B_HEADS * B_DV
    cast = lambda a: a.astype(bf16)

    wa_qkv, wa_o = cast(w_a_qkv), cast(w_a_o)
    nb = 2 * HK + HV
    wb_in = cast(jnp.concatenate(
        [w_b_in[..., :nb], w_b_in[..., nb + B_GATE_RANK:], w_b_in[..., nb:nb + B_GATE_RANK],
         jnp.zeros(w_b_in.shape[:2] + (LANES - B_GATE_RANK,), f32)], axis=-1))
    wb_gate2 = jnp.pad(w_b_gate2, ((0, 0), (0, LANES - B_GATE_RANK), (0, 0)))
    wb_o = cast(w_b_o)
    nc = -(-w_c_in.shape[-1] // LANES) * LANES
    wc_in = cast(jnp.pad(w_c_in, ((0, 0), (0, 0), (0, nc - w_c_in.shape[-1]))))
    bc_gate = jnp.pad(b_c_gate, ((0, 0), (0, LANES - b_c_gate.shape[-1])))
    half = C_CMP_STRIDE * dh
    two_chunk = lambda w: cast(jnp.concatenate([w[:, :half], w[:, half:]], axis=-1))
    wc_k1, wc_v1 = two_chunk(w_c_k1), two_chunk(w_c_v1)
    wc_k2, wc_v2, wc_o = cast(w_c_k2), cast(w_c_v2), cast(w_c_o)
    pos_k = c_pos_k.reshape(-1, 2, half)
    pos_v = c_pos_v.reshape(-1, 2, half)
    wx_q, wx_kv, wx_o = cast(w_x_q), cast(w_x_kv), cast(w_x_o)
    wf_in, wf_out = cast(w_ffn_in), cast(w_ffn_out)
    slopes_a = jnp.asarray(np.pad(SLOPES_A, ((0, 0), (0, LANES - A_SLOTS))))

    Mp = Bp * L
    x = x_prompt.reshape(Mp, D)
    mem2d = mem_prompt.reshape(-1, D)
    dil_p = [[] for _ in range(A_GROUPS)]
    gla_p, rows_p, win_p, mem_p = [], [], [], []
    for l in range(depth):
        kind, j = l % N_MIXERS, l // N_MIXERS
        if kind == 0:
            qkv = mm(x, wa_qkv[j], g=g_mix[l]).reshape(Bp, L, -1)
            outs = [dilated_group(qkv, grp) for grp in range(A_GROUPS)]
            x = dilated_merge_out([o.reshape(Mp, A_OUT) for o, _ in outs],
                                  [s.reshape(Mp, A_OUT) for _, s in outs], wa_o[j], x)
            kv = qkv.reshape(Bp, L, 3, A_GROUPS, A_SLOTS, A_HEAD_DIM)
            for grp in range(A_GROUPS):
                w = min(A_WINDOWS[grp], L)
                dil_p[grp].append(jnp.stack([kv[:, L - w:, 1, grp], kv[:, L - w:, 2, grp]], axis=2))
        elif kind == 1:
            proj = mm(x, wb_in[j], g=g_mix[l]).reshape(Bp, L, -1)
            y, S = gla_prompt(proj, wb_gate2[j], b_b_gate[j], g_b_head[j].reshape(-1),
                              jnp.zeros((Bp, B_HEADS, B_DK, B_DV), f32))
            gla_p.append(S)
            x = mm(y.reshape(Mp, HV), wb_o[j], res=x)
        else:
            proj = mm(x, wc_in[j], g=g_mix[l]).reshape(Bp, L, -1)
            q_w = C_HEADS * dh
            rows = proj[..., q_w:q_w + 6 * GD].reshape(Bp, L, 6, G, dh)
            chunks = lambda a: a.reshape(Bp, L // C_CMP_STRIDE, C_CMP_STRIDE, G, dh).transpose(
                0, 3, 1, 2, 4).reshape(Bp, G, L // C_CMP_STRIDE, half)
            kc, vc = nsa_compress(chunks(rows[:, :, 0]), chunks(rows[:, :, 1]), pos_k[j], pos_v[j],
                                  wc_k1[j], wc_k2[j], wc_v1[j], wc_v2[j])
            seqs = [cast(rows[:, :, r].transpose(0, 2, 1, 3)) for r in range(2, 6)]
            o = nsa_attention_prompt(proj, bc_gate[j:j + 1], kc, vc, *seqs)
            x = mm(o.reshape(Mp, q_w), wc_o[j], res=x)
            rows_p.append(rows[:, :, :4])
            win_p.append(rows[:, L - min(C_WINDOW, L):, 4:])
        mem_kv = mm(mem2d, wx_kv[l], g=g_mem[l]).reshape(Bp, -1, 2 * D)
        mem_p.append(mem_kv.reshape(Bp, -1, 2, MEM_HEADS, D // MEM_HEADS))
        x = xattn_prompt(x.reshape(Bp, L, D), g_cross[l], wx_q[l], cast(mem_kv), wx_o[l]).reshape(Mp, D)
        x = ffn(x, g_ffn[l], wf_in[l], wf_out[l])
    y_prompt = rmsnorm_rows(x, g_final).reshape(Bp, L, D)

    x = x_sample.reshape(Bs, D)
    dil_caches = (cache_dil_w128, cache_dil_w512, cache_dil_w2048)
    dil_s = [[] for _ in range(A_GROUPS)]
    gla_s, rows_s, win_s = [], [], []
    for l in range(depth):
        kind, j = l % N_MIXERS, l // N_MIXERS
        if kind == 0:
            qkv = mm(x, wa_qkv[j], g=g_mix[l]).reshape(Bs, 1, -1)
            caches = []
            for grp in range(A_GROUPS):
                c = dil_caches[grp][j]
                assert c.shape[1] == A_WINDOWS[grp]
                caches.append(c.reshape(Bs, A_WIN_STEPS, -1))
            o = sq_attention(slopes_a, [(qkv, grp) for grp in range(A_GROUPS)], caches,
                             [((qkv, A_GROUPS + grp), (qkv, 2 * A_GROUPS + grp)) for grp in range(A_GROUPS)],
                             heads=A_SLOTS, dh=A_HEAD_DIM, dils=A_DILATIONS)
            x = mm(o.reshape(Bs, A_OUT), wa_o[j], res=x)
            kv = qkv.reshape(Bs, 1, 3, A_GROUPS, A_SLOTS, A_HEAD_DIM)
            for grp in range(A_GROUPS):
                dil_s[grp].append(jnp.stack([kv[:, :, 1, grp], kv[:, :, 2, grp]], axis=2))
        elif kind == 1:
            proj = mm(x, wb_in[j], g=g_mix[l]).reshape(Bs, 1, -1)
            y, S = gla_step(proj, wb_gate2[j], b_b_gate[j], g_b_head[j].reshape(-1), state_gla[j])
            gla_s.append(S)
            x = mm(y.reshape(Bs, HV), wb_o[j], res=x)
        else:
            proj = mm(x, wc_in[j], g=g_mix[l]).reshape(Bs, 1, -1)
            q_w = C_HEADS * dh
            cache = cache_nsa_kv[j].reshape(cache_nsa_kv.shape[1], PAGE_SIZE, 4 * GD)
            kc, vc = nsa_compress_sample(page_table, cache, pos_k[j], pos_v[j],
                                         wc_k1[j], wc_k2[j], wc_v1[j], wc_v2[j])
            o_c, idx = nsa_cmp_topk_sample(proj, kc, vc, t_s)
            sel_idx = idx[:, :G, :C_TOPK].reshape(Bs, G * C_TOPK)
            o = nsa_sel_win_sample(page_table, sel_idx, cache, cache_nsa_win[j].reshape(Bs, C_WINDOW, 2 * GD),
                                   proj, bc_gate[j:j + 1], o_c, t_s)
            x = mm(o.reshape(Bs, q_w), wc_o[j], res=x)
            rows = proj[..., q_w:q_w + 6 * GD].reshape(Bs, 1, 6, G, dh)
            rows_s.append(rows[:, :, :4])
            win_s.append(rows[:, :, 4:])
        q = mm(x, wx_q[l], g=g_cross[l]).reshape(Bs, 1, D)
        o = sq_attention(jnp.zeros((1, LANES), f32), [(q, 0)], [cache_mem_kv[l].reshape(Bs, -1, 2 * D)], None,
                         heads=MEM_HEADS, dh=D // MEM_HEADS, dils=(1,))
        x = mm(o.reshape(Bs, D), wx_o[l], res=x)
        x = ffn(x, g_ffn[l], wf_in[l], wf_out[l])
    y_sample = rmsnorm_rows(x, g_final).reshape(Bs, 1, D)

    st = jnp.stack
    return (y_prompt, y_sample, st(dil_p[0]), st(dil_s[0]), st(dil_p[1]), st(dil_s[1]), st(dil_p[2]), st(dil_s[2]),
            st(gla_p), st(gla_s), st(win_p), st(win_s), st(rows_p), st(rows_s), st(mem_p))
```

```python
import functools

import jax
import jax.numpy as jnp
import numpy as np
from jax import lax
from jax.experimental import pallas as pl
from jax.experimental.pallas import tpu as pltpu

f32 = jnp.float32
bf16 = jnp.bfloat16

N_MIXERS = 3
A_WINDOWS = (128, 512, 2048)
A_DILATIONS = (1, 4, 16)
A_GROUPS = 3
A_SLOTS = 8
A_HEAD_DIM = 64
A_OUT = A_SLOTS * A_HEAD_DIM
A_WIN_STEPS = 128

B_HEADS = 4
B_DK = 128
B_DV = 256
B_GATE_RANK = 16
B_GATE_TAU = 16.0

C_HEADS = 16
C_KV_HEADS = 4
C_HPG = C_HEADS // C_KV_HEADS
C_HEAD_DIM = 64
C_CMP_BLOCK = 32
C_CMP_STRIDE = 16
C_SEL_BLOCK = 64
C_SEL_SHIFT = 6
C_TOPK = 16
C_WINDOW = 512
C_CMP_HIDDEN = 128

MEM_HEADS = 4
PAGE_SIZE = 128

RMS_EPS = 1e-6
NEG_INF = -1e30
FORCE = 1e30
TINY = 1e-30
REMOVED = -3e38

LANES = 128
VMEM_LIMIT_BYTES = 56 * 1024 * 1024
HIGHEST = lax.Precision.HIGHEST


def _alibi_slopes(n):
    return np.asarray(2.0 ** (-8.0 * np.arange(1, n + 1) / n), dtype=np.float32)


def _bf16_pieces(x, n):
    out, rest = [], np.asarray(x, np.float32)
    for _ in range(n):
        p = rest.astype(bf16).astype(np.float32)
        out.append(p)
        rest = (rest - p).astype(np.float32)
    return np.stack(out, axis=-1)


SLOPES_A = _alibi_slopes(A_GROUPS * A_SLOTS).reshape(A_GROUPS, A_SLOTS)
SLOPES_C = _alibi_slopes(C_HEADS)
N_PIECES = 3
SLOPES_C_PIECES = _bf16_pieces(SLOPES_C, N_PIECES).reshape(-1)
AUG = 2 * N_PIECES


def _params(*sem):
    return pltpu.CompilerParams(dimension_semantics=sem, vmem_limit_bytes=VMEM_LIMIT_BYTES)


def _rms(x, g):
    return x * lax.rsqrt(jnp.mean(x * x, axis=-1, keepdims=True) + RMS_EPS) * g


def _dot_t(a, b, precision=None):
    return lax.dot_general(a, b, (((1,), (1,)), ((), ())), precision=precision, preferred_element_type=f32)


def _hdot(a, b):
    return jnp.dot(a, b, precision=HIGHEST, preferred_element_type=f32)


def _pick_tile(n, target):
    best = LANES
    for t in range(LANES, min(n, target) + 1, LANES):
        if n % t == 0:
            best = t
    return best


def _mm_body(*refs, norm, res):
    it = iter(refs)
    x_ref = next(it)
    g_ref = next(it) if norm else None
    w_ref = next(it)
    r_ref = next(it) if res else None
    o_ref = next(it)
    xn_ref = next(it)

    @pl.when(pl.program_id(1) == 0)
    def _():
        x = x_ref[...].astype(f32)
        if norm:
            x = _rms(x, g_ref[...])
        xn_ref[...] = x.astype(bf16)

    acc = jnp.dot(xn_ref[...], w_ref[...], preferred_element_type=f32)
    if res:
        acc = acc + r_ref[...]
    o_ref[...] = acc.astype(o_ref.dtype)


def mm(x, w, *, g=None, res=None, out_dtype=f32, tm=512, tn=512):
    M, K = x.shape
    N = w.shape[1]
    tm = min(tm, M)
    tn = _pick_tile(N, tn)
    assert M % tm == 0 and N % tn == 0
    args = [x]
    specs = [pl.BlockSpec((tm, K), lambda i, j: (i, 0))]
    if g is not None:
        args.append(g.reshape(1, K))
        specs.append(pl.BlockSpec((1, K), lambda i, j: (0, 0)))
    args.append(w)
    specs.append(pl.BlockSpec((K, tn), lambda i, j: (0, j)))
    if res is not None:
        args.append(res)
        specs.append(pl.BlockSpec((tm, tn), lambda i, j: (i, j)))
    return pl.pallas_call(
        functools.partial(_mm_body, norm=g is not None, res=res is not None),
        out_shape=jax.ShapeDtypeStruct((M, N), out_dtype),
        grid=(M // tm, N // tn),
        in_specs=specs,
        out_specs=pl.BlockSpec((tm, tn), lambda i, j: (i, j)),
        scratch_shapes=[pltpu.VMEM((tm, K), bf16)],
        compiler_params=_params("parallel", "arbitrary"),
        name="mm",
    )(*args)


def _ffn_body(x_ref, g_ref, wa_ref, wb_ref, wo_ref, o_ref, xn_ref, acc_ref):
    j = pl.program_id(1)

    @pl.when(j == 0)
    def _():
        xn_ref[...] = _rms(x_ref[...], g_ref[...]).astype(bf16)
        acc_ref[...] = jnp.zeros_like(acc_ref)

    xn = xn_ref[...]
    a = jnp.dot(xn, wa_ref[...], preferred_element_type=f32)
    b = jnp.dot(xn, wb_ref[...], preferred_element_type=f32)
    h = (a * jax.nn.sigmoid(a) * b).astype(bf16)
    acc_ref[...] += jnp.dot(h, wo_ref[...], preferred_element_type=f32)

    @pl.when(j == pl.num_programs(1) - 1)
    def _():
        o_ref[...] = x_ref[...] + acc_ref[...]


def ffn(x, g, w_in, w_out, *, tm=512, th=704):
    M, D = x.shape
    F = w_out.shape[0]
    tm = min(tm, M)
    th = _pick_tile(F, th)
    nh = F // th
    return pl.pallas_call(
        _ffn_body,
        out_shape=jax.ShapeDtypeStruct((M, D), f32),
        grid=(M // tm, nh),
        in_specs=[
            pl.BlockSpec((tm, D), lambda i, j: (i, 0)),
            pl.BlockSpec((1, D), lambda i, j: (0, 0)),
            pl.BlockSpec((D, th), lambda i, j: (0, j)),
            pl.BlockSpec((D, th), lambda i, j: (0, j + nh)),
            pl.BlockSpec((th, D), lambda i, j: (j, 0)),
        ],
        out_specs=pl.BlockSpec((tm, D), lambda i, j: (i, 0)),
        scratch_shapes=[pltpu.VMEM((tm, D), bf16), pltpu.VMEM((tm, D), f32)],
        compiler_params=_params("parallel", "arbitrary"),
        name="ffn",
    )(x, g.reshape(1, D), w_in, w_in, w_out)


def _xattn_body(x_ref, g_ref, wq_ref, kv_ref, wo_ref, o_ref, *, heads):
    x = x_ref[...]
    D = x.shape[-1]
    dh = D // heads
    xn = _rms(x, g_ref[...]).astype(bf16)
    q = jnp.dot(xn, wq_ref[...], preferred_element_type=f32).astype(bf16)
    outs = []
    for h in range(heads):
        kh = kv_ref[:, h * dh:(h + 1) * dh]
        vh = kv_ref[:, D + h * dh:D + (h + 1) * dh]
        s = _dot_t(q[:, h * dh:(h + 1) * dh], kh) * dh ** -0.5
        m = jnp.max(s, axis=-1, keepdims=True)
        e = jnp.exp(s - m)
        p = e / jnp.sum(e, axis=-1, keepdims=True)
        outs.append(jnp.dot(p.astype(bf16), vh, preferred_element_type=f32))
    o = jnp.concatenate(outs, axis=-1).astype(bf16)
    o_ref[...] = x + jnp.dot(o, wo_ref[...], preferred_element_type=f32)


def xattn_prompt(x, g, wq, kv, wo, *, tq=512):
    B, L, D = x.shape
    Mem = kv.shape[1]
    tq = min(tq, L)
    return pl.pallas_call(
        functools.partial(_xattn_body, heads=MEM_HEADS),
        out_shape=jax.ShapeDtypeStruct((B, L, D), f32),
        grid=(B, L // tq),
        in_specs=[
            pl.BlockSpec((None, tq, D), lambda b, i: (b, i, 0)),
            pl.BlockSpec((1, D), lambda b, i: (0, 0)),
            pl.BlockSpec((D, D), lambda b, i: (0, 0)),
            pl.BlockSpec((None, Mem, 2 * D), lambda b, i: (b, 0, 0)),
            pl.BlockSpec((D, D), lambda b, i: (0, 0)),
        ],
        out_specs=pl.BlockSpec((None, tq, D), lambda b, i: (b, i, 0)),
        compiler_params=_params("parallel", "parallel"),
        name="xattn_prompt",
    )(x, g.reshape(1, D), wq, kv, wo)


def _dil_body(sl_ref, q_ref, kc_ref, kp_ref, vc_ref, vp_ref, o_ref, lse_ref, *, grp, dil, tq):
    i = pl.program_id(1)
    c = pl.program_id(2)
    nk = tq + A_WIN_STEPS
    scale = A_HEAD_DIM ** -0.5
    row = lax.broadcasted_iota(jnp.int32, (tq, nk), 0)
    col = lax.broadcasted_iota(jnp.int32, (tq, nk), 1)
    dist = row + A_WIN_STEPS - col
    valid = (dist >= 0) & (dist <= A_WIN_STEPS) & ((col >= A_WIN_STEPS) | (i > 0))
    distf = (dist * dil).astype(f32)
    lane = lax.broadcasted_iota(jnp.int32, (1, LANES), 1)
    first = lane < A_HEAD_DIM
    slopes = [sl_ref[grp * A_SLOTS + 2 * c + half] for half in range(2)]

    def rows(ref, r, n):
        return ref[pl.ds(r, n, stride=dil), :] if dil > 1 else ref[...]

    def residue(r, carry):
        q = rows(q_ref, r, tq).astype(bf16)
        k = jnp.concatenate([rows(kp_ref, r, A_WIN_STEPS), rows(kc_ref, r, tq)], axis=0).astype(bf16)
        v = jnp.concatenate([rows(vp_ref, r, A_WIN_STEPS), rows(vc_ref, r, tq)], axis=0).astype(bf16)
        halves = []
        for half in range(2):
            qm = jnp.where(first == (half == 0), q, jnp.zeros_like(q))
            s = _dot_t(qm, k) * scale - slopes[half] * distf
            s = jnp.where(valid, s, NEG_INF)
            m = jnp.max(s, axis=-1, keepdims=True)
            e = jnp.exp(s - m)
            den = jnp.sum(e, axis=-1, keepdims=True)
            pv = jnp.dot((e / den).astype(bf16), v, preferred_element_type=f32)
            halves.append((pv, m + jnp.log(den)))
        o_t = jnp.where(first, halves[0][0], halves[1][0])
        l_t = jnp.where(first, halves[0][1], halves[1][1])
        if dil > 1:
            o_ref[pl.ds(r, tq, stride=dil), :] = o_t
            lse_ref[pl.ds(r, tq, stride=dil), :] = l_t
        else:
            o_ref[...] = o_t
            lse_ref[...] = l_t
        return carry

    if dil > 1:
        lax.fori_loop(0, dil, residue, 0)
    else:
        residue(0, 0)


DIL_TQ = (512, 256, 128)


def dilated_group(qkv, grp):
    B, L, W3 = qkv.shape
    d = A_DILATIONS[grp]
    tq = min(DIL_TQ[grp], L // d)
    rows_blk = tq * d
    prev_blk = A_WIN_STEPS * d
    assert L % rows_blk == 0 and tq % A_WIN_STEPS == 0
    ratio = tq // A_WIN_STEPS
    tiles = A_OUT // LANES
    sect = A_GROUPS * tiles

    def spec(t, prev=False):
        col = lambda c: t * sect + grp * tiles + c
        if prev:
            return pl.BlockSpec((None, prev_blk, LANES),
                                lambda b, i, c, sl: (b, jnp.maximum(i * ratio - 1, 0), col(c)))
        return pl.BlockSpec((None, rows_blk, LANES), lambda b, i, c, sl: (b, i, col(c)))

    out_spec = pl.BlockSpec((None, rows_blk, LANES), lambda b, i, c, sl: (b, i, c))
    return pl.pallas_call(
        functools.partial(_dil_body, grp=grp, dil=d, tq=tq),
        out_shape=[jax.ShapeDtypeStruct((B, L, A_OUT), f32)] * 2,
        grid_spec=pltpu.PrefetchScalarGridSpec(
            num_scalar_prefetch=1,
            grid=(B, L // rows_blk, tiles),
            in_specs=[spec(0), spec(1), spec(1, True), spec(2), spec(2, True)],
            out_specs=[out_spec, out_spec],
        ),
        compiler_params=_params("parallel", "parallel", "parallel"),
        name=f"dilated_g{grp}",
    )(jnp.asarray(SLOPES_A.reshape(-1)), qkv, qkv, qkv, qkv, qkv)


def _dil_out_body(o0, o1, o2, l0, l1, l2, w_ref, r_ref, out_ref):
    ls = [l0[...], l1[...], l2[...]]
    m = jnp.maximum(jnp.maximum(ls[0], ls[1]), ls[2])
    es = [jnp.exp(l - m) for l in ls]
    den = es[0] + es[1] + es[2]
    o = (es[0] / den) * o0[...] + (es[1] / den) * o1[...] + (es[2] / den) * o2[...]
    out_ref[...] = r_ref[...] + jnp.dot(o.astype(bf16), w_ref[...], preferred_element_type=f32)


def dilated_merge_out(os, lses, w_o, res, *, tm=512):
    M, D = res.shape
    tm = min(tm, M)
    row = lambda n: pl.BlockSpec((tm, n), lambda i: (i, 0))
    return pl.pallas_call(
        _dil_out_body,
        out_shape=jax.ShapeDtypeStruct((M, D), f32),
        grid=(M // tm,),
        in_specs=[row(A_OUT)] * 6 + [pl.BlockSpec((A_OUT, D), lambda i: (0, 0)), row(D)],
        out_specs=row(D),
        compiler_params=_params("parallel"),
        name="dilated_merge_out",
    )(*os, *lses, w_o, res)


GLA_SUB = 16


def _log_sigmoid(z):
    return -(jnp.maximum(-z, 0.0) + jnp.log1p(jnp.exp(-jnp.abs(z))))


def _gla_body(q_ref, k_ref, v_ref, r_ref, gl_ref, w2_ref, bg_ref, gh_ref, s0_ref, y_ref, s_ref, a_ref):
    C = q_ref.shape[0]
    c = GLA_SUB

    @pl.when(pl.program_id(1) == 0)
    def _():
        s_ref[...] = s0_ref[...]

    z = _hdot(gl_ref[...], w2_ref[...]) + bg_ref[...]
    la = _log_sigmoid(z) / B_GATE_TAU
    rowC = lax.broadcasted_iota(jnp.int32, (C, C), 0)
    colC = lax.broadcasted_iota(jnp.int32, (C, C), 1)
    tri = (rowC >= colC).astype(f32)
    b_all = _hdot(tri, la)
    row_k = lax.broadcasted_iota(jnp.int32, (C, B_DK), 0)
    col_c = lax.broadcasted_iota(jnp.int32, (c, C), 1)
    row_c = lax.broadcasted_iota(jnp.int32, (c, 1), 0)

    for h in range(B_HEADS):
        bh = b_all[:, h * B_DK:(h + 1) * B_DK]
        qh = q_ref[:, h * B_DK:(h + 1) * B_DK] * B_DK ** -0.5
        kh = k_ref[:, h * B_DK:(h + 1) * B_DK]
        vh = v_ref[:, h * B_DV:(h + 1) * B_DV].astype(bf16)
        S = s_ref[h]
        o = jnp.dot((qh * jnp.exp(bh)).astype(bf16), S.astype(bf16), preferred_element_type=f32)

        for I in range(C // c):
            r0 = I * c
            qI, kI, bI = qh[r0:r0 + c], kh[r0:r0 + c], bh[r0:r0 + c]
            if I == 0:
                A_I = jnp.zeros((c, C), f32)
            else:
                beta = bh[r0 - 1:r0]
                qt = qI * jnp.exp(bI - beta)
                kt = kh * jnp.exp(jnp.where(row_k < r0, beta - bh, 0.0))
                A_I = jnp.where(col_c < r0, _dot_t(qt.astype(bf16), kt.astype(bf16)), 0.0)
            for j in range(c):
                ex = jnp.exp(jnp.minimum(bI - bI[j:j + 1], 0.0))
                tj = jnp.sum(qI * kI[j:j + 1] * ex, axis=-1, keepdims=True)
                tj = jnp.where(row_c >= j, tj, 0.0)
                A_I = jnp.where(col_c == r0 + j, tj, A_I)
            a_ref[r0:r0 + c, :] = A_I
        o = o + jnp.dot(a_ref[...].astype(bf16), vh, preferred_element_type=f32)

        b_last = bh[C - 1:C]
        kdec = jnp.transpose(kh * jnp.exp(b_last - bh)).astype(bf16)
        decay = jnp.transpose(jnp.broadcast_to(jnp.exp(b_last), (8, B_DK)))[:, 0:1]
        s_ref[h] = decay * S + jnp.dot(kdec, vh, preferred_element_type=f32)

        on = _rms(o, gh_ref[:, h * B_DV:(h + 1) * B_DV])
        rh = r_ref[:, h * B_DV:(h + 1) * B_DV]
        y_ref[:, h * B_DV:(h + 1) * B_DV] = (on * (rh * jax.nn.sigmoid(rh))).astype(y_ref.dtype)


def gla_prompt(proj, w_gate2p, b_gate, g_head, s0, *, chunk=128):
    B, L, _ = proj.shape
    HK, HV = B_HEADS * B_DK, B_HEADS * B_DV
    C = min(chunk, L)
    assert L % C == 0 and C % GLA_SUB == 0
    return pl.pallas_call(
        _gla_body,
        out_shape=[jax.ShapeDtypeStruct((B, L, HV), bf16),
                   jax.ShapeDtypeStruct((B, B_HEADS, B_DK, B_DV), f32)],
        grid=(B, L // C),
        in_specs=[
            pl.BlockSpec((None, C, HK), lambda b, i: (b, i, 0)),
            pl.BlockSpec((None, C, HK), lambda b, i: (b, i, 1)),
            pl.BlockSpec((None, C, HV), lambda b, i: (b, i, 1)),
            pl.BlockSpec((None, C, HV), lambda b, i: (b, i, 2)),
            pl.BlockSpec((None, C, LANES), lambda b, i: (b, i, (2 * HK + 2 * HV) // LANES)),
            pl.BlockSpec((LANES, HK), lambda b, i: (0, 0)),
            pl.BlockSpec((1, HK), lambda b, i: (0, 0)),
            pl.BlockSpec((1, HV), lambda b, i: (0, 0)),
            pl.BlockSpec((None, B_HEADS, B_DK, B_DV), lambda b, i: (b, 0, 0, 0)),
        ],
        out_specs=[pl.BlockSpec((None, C, HV), lambda b, i: (b, i, 0)),
                   pl.BlockSpec((None, B_HEADS, B_DK, B_DV), lambda b, i: (b, 0, 0, 0))],
        scratch_shapes=[pltpu.VMEM((C, C), f32)],
        compiler_params=_params("parallel", "arbitrary"),
        name="gla_prompt",
    )(proj, proj, proj, proj, proj, w_gate2p, b_gate.reshape(1, HK), g_head.reshape(1, HV), s0)


def _cmp_mlp_tail(z, posb, w2_ref):
    n = z.shape[0]
    H = C_CMP_HIDDEN
    hid = z[:, :H] + pltpu.roll(z[:, H:], n - 1, 0) + posb
    return jnp.dot(jax.nn.gelu(hid).astype(bf16), w2_ref[...], preferred_element_type=f32)


def _cmp_pos_bias(p_ref, w1_ref):
    H = C_CMP_HIDDEN
    pz = jnp.dot(p_ref[...].astype(bf16), w1_ref[...], preferred_element_type=f32)
    return pz[0:1, :H] + pz[1:2, H:]


def _cmp_body(xk_ref, xv_ref, pk_ref, pv_ref, wk1_ref, wk2_ref, wv1_ref, wv2_ref, ok_ref, ov_ref):
    for x_ref, p_ref, w1_ref, w2_ref, o_ref in ((xk_ref, pk_ref, wk1_ref, wk2_ref, ok_ref),
                                                (xv_ref, pv_ref, wv1_ref, wv2_ref, ov_ref)):
        z = jnp.dot(x_ref[...].astype(bf16), w1_ref[...], preferred_element_type=f32)
        o_ref[...] = _cmp_mlp_tail(z, _cmp_pos_bias(p_ref, w1_ref), w2_ref).astype(o_ref.dtype)


def nsa_compress(xk, xv, pos_k, pos_v, wk1, wk2, wv1, wv2):
    B, G, n, W = xk.shape
    x_spec = pl.BlockSpec((None, None, n, W), lambda b, g: (b, g, 0, 0))
    full = lambda a: pl.BlockSpec(a.shape, lambda b, g: (0,) * a.ndim)
    o_spec = pl.BlockSpec((None, None, n, C_HEAD_DIM), lambda b, g: (b, g, 0, 0))
    consts = [pos_k, pos_v, wk1, wk2, wv1, wv2]
    return pl.pallas_call(
        _cmp_body,
        out_shape=[jax.ShapeDtypeStruct((B, G, n, C_HEAD_DIM), bf16)] * 2,
        grid=(B, G),
        in_specs=[x_spec, x_spec] + [full(a) for a in consts],
        out_specs=[o_spec, o_spec],
        compiler_params=_params("parallel", "parallel"),
        name="nsa_compress",
    )(xk, xv, *consts)


def _topk_mask_t(score_t, blk_t):
    keep = jnp.zeros(score_t.shape, f32)
    big = float(score_t.shape[0])
    for _ in range(C_TOPK):
        mx = jnp.max(score_t, axis=0, keepdims=True)
        first = jnp.min(jnp.where(score_t == mx, blk_t, big), axis=0, keepdims=True)
        hit = blk_t == first
        keep = jnp.where(hit, jnp.where(mx > 0.5 * NEG_INF, 1.0, 0.0), keep)
        score_t = jnp.where(hit, REMOVED, score_t)
    return keep


def _aug_query(q_rows, pieces, lane):
    R, dh = q_rows.shape
    qa = jnp.concatenate([q_rows, jnp.zeros((R, LANES - dh), f32)], axis=-1)
    for n in reversed(range(N_PIECES)):
        qa = jnp.where((lane >= dh + 2 * n) & (lane < dh + 2 * n + 2), pieces[n], qa)
    return qa.astype(bf16)


def _nsa_body(sp_ref, q_ref, gate_ref, bg_ref, kc_ref, vc_ref, ks_ref, vs_ref, kw_ref, vw_ref,
              o_ref, m_ref, l_ref, acc_ref, *, tk):
    g = pl.program_id(1)
    i = pl.program_id(2)
    tq = q_ref.shape[0]
    R = C_HPG * tq
    L = ks_ref.shape[0]
    n_cmp = kc_ref.shape[0]
    NS = LANES
    dh = C_HEAD_DIM
    q0 = i * tq
    lane = lax.broadcasted_iota(jnp.int32, (1, LANES), 1)

    q = q_ref[...] * dh ** -0.5
    q_rows = jnp.concatenate([q[:, hh * dh:(hh + 1) * dh] for hh in range(C_HPG)], axis=0)
    pieces = [jnp.concatenate([jnp.full((tq, 1), sp_ref[(g * C_HPG + hh) * N_PIECES + n], f32)
                               for hh in range(C_HPG)], axis=0) for n in range(N_PIECES)]
    qa = _aug_query(q_rows, pieces, lane)
    t1 = q0 + lax.broadcasted_iota(jnp.int32, (tq, 1), 0)
    t = jnp.concatenate([t1] * C_HPG, axis=0)

    cpos = lax.broadcasted_iota(jnp.int32, (1, n_cmp), 1) * C_CMP_STRIDE + (C_CMP_BLOCK - 1)
    mask = cpos <= t
    s = jnp.where(mask, _dot_t(qa, kc_ref[...]), NEG_INF)
    m = jnp.max(s, axis=-1, keepdims=True)
    e = jnp.where(mask, jnp.exp(s - m), 0.0)
    p = e / jnp.maximum(jnp.sum(e, axis=-1, keepdims=True), TINY)
    o_c = jnp.dot(p.astype(bf16), vc_ref[...], preferred_element_type=f32)

    psum = p[0:tq]
    for hh in range(1, C_HPG):
        psum = psum + p[hh * tq:(hh + 1) * tq]
    c_start = lax.broadcasted_iota(jnp.int32, (n_cmp, NS), 0) * C_CMP_STRIDE
    jb = lax.broadcasted_iota(jnp.int32, (n_cmp, NS), 1)
    cover = jnp.where(c_start < (jb + 1) * C_SEL_BLOCK,
                      jnp.where(c_start + C_CMP_BLOCK > jb * C_SEL_BLOCK, 1.0, 0.0), 0.0)
    imp = _hdot(psum, cover)
    blk = lax.broadcasted_iota(jnp.int32, (1, NS), 1)
    cur = t1 >> C_SEL_SHIFT
    forced = (blk == 0) | (blk == cur) | (blk == cur - 1)
    score = jnp.where(blk * C_SEL_BLOCK <= t1, jnp.where(forced, FORCE, imp), NEG_INF)
    blk_t = lax.broadcasted_iota(jnp.int32, (NS, tq), 0).astype(f32)
    keep = _topk_mask_t(jnp.transpose(score), blk_t)
    neg1 = jnp.transpose(jnp.where(keep > 0.5, 0.0, NEG_INF)).astype(bf16)
    qa2 = jnp.concatenate([qa, jnp.concatenate([neg1] * C_HPG, axis=0)], axis=-1)

    m_ref[...] = jnp.full_like(m_ref, NEG_INF)
    l_ref[...] = jnp.zeros_like(l_ref)
    acc_ref[...] = jnp.zeros_like(acc_ref)

    def sel_tile(kt, causal):
        k0 = pl.multiple_of(kt * tk, tk)
        sc = _dot_t(qa2, ks_ref[pl.ds(k0, tk), :])
        if causal:
            kpos = k0 + lax.broadcasted_iota(jnp.int32, (1, tk), 1)
            sc = jnp.where(kpos <= t, sc, NEG_INF)
        m_old = m_ref[...]
        m_new = jnp.maximum(m_old, jnp.max(sc, axis=-1, keepdims=True))
        alpha = jnp.exp(m_old - m_new)
        ee = jnp.exp(sc - m_new)
        l_ref[...] = alpha * l_ref[...] + jnp.sum(ee, axis=-1, keepdims=True)
        acc_ref[...] = alpha * acc_ref[...] + jnp.dot(ee.astype(bf16), vs_ref[pl.ds(k0, tk), :],
                                                      preferred_element_type=f32)
        m_ref[...] = m_new

    kt_last = q0 // tk

    def body(kt, carry):
        sel_tile(kt, False)
        return carry

    lax.fori_loop(0, kt_last, body, 0)
    sel_tile(kt_last, True)
    o_s = acc_ref[...] / l_ref[...]

    W = min(L, C_WINDOW + tq)
    w0 = pl.multiple_of(jnp.clip(q0 - C_WINDOW, 0, L - W), tq)
    wpos = w0 + lax.broadcasted_iota(jnp.int32, (1, W), 1)
    s = _dot_t(qa, kw_ref[pl.ds(w0, W), :])
    s = jnp.where(wpos <= t, jnp.where(wpos >= t - C_WINDOW, s, NEG_INF), NEG_INF)
    m = jnp.max(s, axis=-1, keepdims=True)
    e = jnp.exp(s - m)
    p = e / jnp.sum(e, axis=-1, keepdims=True)
    o_w = jnp.dot(p.astype(bf16), vw_ref[pl.ds(w0, W), :], preferred_element_type=f32)

    gs = jax.nn.sigmoid(gate_ref[...] + bg_ref[...])

    def gate(r):
        cols = [jnp.sum(jnp.where(lane == (g * C_HPG + hh) * 3 + r, gs, 0.0), axis=-1, keepdims=True)
                for hh in range(C_HPG)]
        return jnp.concatenate(cols, axis=0)

    o = gate(0) * o_c + gate(1) * o_s + gate(2) * o_w
    o_ref[...] = jnp.concatenate([o[hh * tq:(hh + 1) * tq] for hh in range(C_HPG)], axis=-1).astype(o_ref.dtype)


def _aug_keys(k, pos, one_hot_blocks):
    B, G, n, dh = k.shape
    hi = (pos // C_SEL_BLOCK) * C_SEL_BLOCK
    cols = np.zeros((n, LANES - dh), np.float32)
    for j in range(N_PIECES):
        cols[:, 2 * j] = hi
        cols[:, 2 * j + 1] = pos - hi
    parts = [k, jnp.broadcast_to(jnp.asarray(cols, bf16), (B, G, n, LANES - dh))]
    if one_hot_blocks:
        oh = (pos[:, None] // C_SEL_BLOCK == np.arange(LANES)[None, :]).astype(np.float32)
        parts.append(jnp.broadcast_to(jnp.asarray(oh, bf16), (B, G, n, LANES)))
    return jnp.concatenate(parts, axis=-1)


def nsa_attention_prompt(proj, b_gate_p, kc, vc, ks, vs, kw, vw, *, tq=128, tk=512):
    B, L, _ = proj.shape
    G, dh = C_KV_HEADS, C_HEAD_DIM
    tq = min(tq, L)
    tk = min(tk, L)
    assert tk % tq == 0 and L % tk == 0 and L // C_SEL_BLOCK <= LANES
    R = C_HPG * tq
    n_cmp = kc.shape[2]
    kc_a = _aug_keys(kc, np.arange(n_cmp) * C_CMP_STRIDE + (C_CMP_BLOCK - 1), False)
    ks_a = _aug_keys(ks, np.arange(L), True)
    kw_a = _aug_keys(kw, np.arange(L), False)
    gate_blk = (C_HEADS * dh + 6 * G * dh) // LANES
    seq = lambda n, w: pl.BlockSpec((None, None, n, w), lambda b, g, i, sp: (b, g, 0, 0))
    return pl.pallas_call(
        functools.partial(_nsa_body, tk=tk),
        out_shape=jax.ShapeDtypeStruct((B, L, C_HEADS * dh), bf16),
        grid_spec=pltpu.PrefetchScalarGridSpec(
            num_scalar_prefetch=1,
            grid=(B, G, L // tq),
            in_specs=[
                pl.BlockSpec((None, tq, C_HPG * dh), lambda b, g, i, sp: (b, i, g)),
                pl.BlockSpec((None, tq, LANES), lambda b, g, i, sp: (b, i, gate_blk)),
                pl.BlockSpec((1, LANES), lambda b, g, i, sp: (0, 0)),
                seq(n_cmp, LANES), seq(n_cmp, dh), seq(L, 2 * LANES), seq(L, dh), seq(L, LANES), seq(L, dh),
            ],
            out_specs=pl.BlockSpec((None, tq, C_HPG * dh), lambda b, g, i, sp: (b, i, g)),
            scratch_shapes=[pltpu.VMEM((R, 1), f32), pltpu.VMEM((R, 1), f32), pltpu.VMEM((R, dh), f32)],
        ),
        compiler_params=_params("parallel", "parallel", "parallel"),
        name="nsa_attention_prompt",
    )(jnp.asarray(SLOPES_C_PIECES), proj, proj, b_gate_p, kc_a, vc, ks_a, vs, kw_a, vw)


def _slot_rows(ref, off, n, w):
    rows = jnp.concatenate([ref[:, off + s * w:off + (s + 1) * w] for s in range(n)]
                           + ([jnp.zeros((8 - n, w), f32)] if n < 8 else []), axis=0)
    return jnp.concatenate([rows, jnp.zeros((8, LANES - w), f32)], axis=-1) if w < LANES else rows


def _dil_sample_body(sl_ref, qkv_ref, c0_ref, c1_ref, c2_ref, o_ref):
    dh = A_HEAD_DIM
    scale = dh ** -0.5
    sect = A_GROUPS * A_OUT
    scores, news, v_cols = [], [], []
    for grp, c_ref in enumerate((c0_ref, c1_ref, c2_ref)):
        W = c_ref.shape[-1]
        d = A_DILATIONS[grp]
        q8 = _slot_rows(qkv_ref, grp * A_OUT, A_SLOTS, dh)
        kn8 = _slot_rows(qkv_ref, sect + grp * A_OUT, A_SLOTS, dh)
        vn8 = _slot_rows(qkv_ref, 2 * sect + grp * A_OUT, A_SLOTS, dh)
        q_cols = jnp.transpose(q8)[:dh]
        v_cols.append(jnp.transpose(vn8)[:dh])
        news.append(jnp.sum(q8 * kn8, axis=-1, keepdims=True) * scale)
        pos = lax.broadcasted_iota(jnp.int32, (1, W), 1)
        rows = [jnp.sum(c_ref[0, s] * q_cols[:, s:s + 1], axis=0, keepdims=True) for s in range(A_SLOTS)]
        slope = jnp.concatenate([jnp.full((1, 1), sl_ref[grp * A_SLOTS + s], f32) for s in range(A_SLOTS)], axis=0)
        sc = jnp.concatenate(rows, axis=0) * scale - slope * (W - pos).astype(f32)
        scores.append(jnp.where((pos & (d - 1)) == 0, sc, NEG_INF))
    m = functools.reduce(jnp.maximum, [jnp.max(s, axis=-1, keepdims=True) for s in scores] + news)
    den = jnp.zeros((A_SLOTS, 1), f32)
    acc = [jnp.zeros((dh, 1), f32) for _ in range(A_SLOTS)]
    for grp, c_ref in enumerate((c0_ref, c1_ref, c2_ref)):
        e = jnp.exp(scores[grp] - m)
        en = jnp.exp(news[grp] - m)
        den = den + jnp.sum(e, axis=-1, keepdims=True) + en
        for s in range(A_SLOTS):
            acc[s] = acc[s] + jnp.sum(c_ref[1, s] * e[s:s + 1, :], axis=-1, keepdims=True) \
                + en[s:s + 1, :] * v_cols[grp][:, s:s + 1]
    o_ref[...] = jnp.concatenate([acc[s] / den[s:s + 1, :] for s in range(A_SLOTS)], axis=-1)


def dil_sample(qkv, caches_t, layer):
    Bs = qkv.shape[0]
    for grp, c in enumerate(caches_t):
        assert c.shape[-1] == A_WINDOWS[grp]
    c_spec = lambda c: pl.BlockSpec((None, None) + c.shape[2:], lambda b, sl: (layer, b, 0, 0, 0, 0))
    return pl.pallas_call(
        _dil_sample_body,
        out_shape=jax.ShapeDtypeStruct((Bs, A_HEAD_DIM, A_SLOTS), f32),
        grid_spec=pltpu.PrefetchScalarGridSpec(
            num_scalar_prefetch=1,
            grid=(Bs,),
            in_specs=[pl.BlockSpec((None, 1, qkv.shape[-1]), lambda b, sl: (b, 0, 0))]
            + [c_spec(c) for c in caches_t],
            out_specs=pl.BlockSpec((None, A_HEAD_DIM, A_SLOTS), lambda b, sl: (b, 0, 0)),
        ),
        compiler_params=_params("parallel"),
        name="dil_sample",
    )(jnp.asarray(SLOPES_A.reshape(-1)), qkv, *caches_t)


def _xattn_sample_body(q_ref, c_ref, o_ref):
    dh = q_ref.shape[-1] // MEM_HEADS
    outs = []
    for h in range(MEM_HEADS):
        qh = q_ref[:, h * dh:(h + 1) * dh]
        s = jnp.sum(c_ref[:, 0, h, :] * qh, axis=-1, keepdims=True) * dh ** -0.5
        e = jnp.exp(s - jnp.max(s, axis=0, keepdims=True))
        outs.append(jnp.sum(e * c_ref[:, 1, h, :], axis=0, keepdims=True) / jnp.sum(e, axis=0, keepdims=True))
    o_ref[...] = jnp.concatenate(outs, axis=-1)


def xattn_sample(q, cache, layer):
    Bs, _, D = q.shape
    return pl.pallas_call(
        _xattn_sample_body,
        out_shape=jax.ShapeDtypeStruct((Bs, 1, D), f32),
        grid=(Bs,),
        in_specs=[pl.BlockSpec((None, 1, D), lambda b: (b, 0, 0)),
                  pl.BlockSpec((None, None) + cache.shape[2:], lambda b: (layer, b, 0, 0, 0, 0))],
        out_specs=pl.BlockSpec((None, 1, D), lambda b: (b, 0, 0)),
        compiler_params=_params("parallel"),
        name="xattn_sample",
    )(q, cache)


def _gla_step_body(q_ref, k_ref, v_ref, r_ref, gl_ref, w2_ref, bg_ref, gh_ref, s0_ref, y_ref, s_ref):
    z = _hdot(gl_ref[...], w2_ref[...]) + bg_ref[...]
    a = jnp.exp(_log_sigmoid(z) / B_GATE_TAU)
    pad = jnp.zeros((5, B_DK), f32)
    for h in range(B_HEADS):
        ks_ = slice(h * B_DK, (h + 1) * B_DK)
        vs_ = slice(h * B_DV, (h + 1) * B_DV)
        rows = jnp.concatenate([a[:, ks_], k_ref[:, ks_], q_ref[:, ks_] * B_DK ** -0.5, pad], axis=0)
        cols = jnp.transpose(rows)
        S = cols[:, 0:1] * s0_ref[h] + cols[:, 1:2] * v_ref[:, vs_]
        s_ref[h] = S
        o = jnp.sum(cols[:, 2:3] * S, axis=0, keepdims=True)
        rh = r_ref[:, vs_]
        y_ref[:, vs_] = (_rms(o, gh_ref[:, vs_]) * (rh * jax.nn.sigmoid(rh))).astype(y_ref.dtype)


def gla_step(proj, w_gate2p, b_gate, g_head, state, layer):
    Bs = proj.shape[0]
    HK, HV = B_HEADS * B_DK, B_HEADS * B_DV
    return pl.pallas_call(
        _gla_step_body,
        out_shape=[jax.ShapeDtypeStruct((Bs, 1, HV), bf16),
                   jax.ShapeDtypeStruct((Bs, B_HEADS, B_DK, B_DV), f32)],
        grid=(Bs,),
        in_specs=[
            pl.BlockSpec((None, 1, HK), lambda b: (b, 0, 0)),
            pl.BlockSpec((None, 1, HK), lambda b: (b, 0, 1)),
            pl.BlockSpec((None, 1, HV), lambda b: (b, 0, 1)),
            pl.BlockSpec((None, 1, HV), lambda b: (b, 0, 2)),
            pl.BlockSpec((None, 1, LANES), lambda b: (b, 0, (2 * HK + 2 * HV) // LANES)),
            pl.BlockSpec((LANES, HK), lambda b: (0, 0)),
            pl.BlockSpec((1, HK), lambda b: (0, 0)),
            pl.BlockSpec((1, HV), lambda b: (0, 0)),
            pl.BlockSpec((None, None, B_HEADS, B_DK, B_DV), lambda b: (layer, b, 0, 0, 0)),
        ],
        out_specs=[pl.BlockSpec((None, 1, HV), lambda b: (b, 0, 0)),
                   pl.BlockSpec((None, B_HEADS, B_DK, B_DV), lambda b: (b, 0, 0, 0))],
        compiler_params=_params("parallel"),
        name="gla_step",
    )(proj, proj, proj, proj, proj, w_gate2p, b_gate.reshape(1, HK), g_head.reshape(1, HV), state)


PAGES_PER_STEP = 8


def _cmp_sample_body(pt_ref, *refs):
    pages = refs[:PAGES_PER_STEP]
    (pk_ref, pv_ref, wk1_ref, wk2_ref, wv1_ref, wv2_ref, wkp_ref, wvp_ref,
     ok_ref, ov_ref, x_ref) = refs[PAGES_PER_STEP:]
    i = pl.program_id(1)
    H = C_CMP_HIDDEN
    n_tiles = x_ref.shape[0]
    per_kind = n_tiles // 2
    eye = (lax.broadcasted_iota(jnp.int32, (LANES, LANES), 0)
           == lax.broadcasted_iota(jnp.int32, (LANES, LANES), 1)).astype(bf16)
    for k in range(PAGES_PER_STEP):
        r0 = pl.multiple_of((i * PAGES_PER_STEP + k) * PAGE_SIZE, PAGE_SIZE)
        for kind in range(2):
            for tile in range(per_kind):
                xt = pages[k][kind, 2 * tile:2 * tile + 2].reshape(LANES, PAGE_SIZE).astype(bf16)
                x_ref[kind * per_kind + tile, pl.ds(r0, PAGE_SIZE), :] = _dot_t(eye, xt)

    @pl.when(i == pl.num_programs(1) - 1)
    def _():
        n = x_ref.shape[1] // C_CMP_STRIDE
        for kind, (p_ref, w1_ref, w2_ref, wp_ref, o_ref) in enumerate(
                ((pk_ref, wk1_ref, wk2_ref, wkp_ref, ok_ref), (pv_ref, wv1_ref, wv2_ref, wvp_ref, ov_ref))):
            posb = _cmp_pos_bias(p_ref, w1_ref)
            outs = []
            for tile in range(per_kind):
                ct = kind * per_kind + tile
                z2 = jnp.zeros((n, 4 * H), f32)
                for p in range(C_CMP_STRIDE):
                    xp = x_ref[ct, pl.ds(p, n, stride=C_CMP_STRIDE), :].astype(bf16)
                    z2 = z2 + jnp.dot(xp, wp_ref[p], preferred_element_type=f32)
                outs += [_cmp_mlp_tail(z, posb, w2_ref) for z in (z2[:, :2 * H], z2[:, 2 * H:])]
            o_ref[...] = jnp.concatenate(outs, axis=-1)


def _pair_block_diag(w1):
    dh = C_HEAD_DIM
    w = w1.reshape(C_CMP_STRIDE, dh, w1.shape[1])
    z = jnp.zeros_like(w)
    return jnp.concatenate([jnp.concatenate([w, z], axis=-1), jnp.concatenate([z, w], axis=-1)], axis=1)


def nsa_compress_sample(page_table, cache_t, layer, pos_k, pos_v, wk1, wk2, wv1, wv2):
    Bs, n_pages = page_table.shape
    G, dh = C_KV_HEADS, C_HEAD_DIM
    assert n_pages % PAGES_PER_STEP == 0 and cache_t.shape[2:] == (4, G, dh, PAGE_SIZE) and 2 * dh == LANES
    n_rows = n_pages * PAGE_SIZE
    n = n_rows // C_CMP_STRIDE
    page = lambda k: pl.BlockSpec((None, None, 2, G, dh, PAGE_SIZE),
                                  lambda b, i, pt, k=k: (layer, pt[b, i * PAGES_PER_STEP + k], 0, 0, 0, 0))
    consts = [pos_k, pos_v, wk1, wk2, wv1, wv2, _pair_block_diag(wk1), _pair_block_diag(wv1)]
    full = lambda a: pl.BlockSpec(a.shape, lambda b, i, pt: (0,) * a.ndim)
    o_spec = pl.BlockSpec((None, n, G * dh), lambda b, i, pt: (b, 0, 0))
    return pl.pallas_call(
        _cmp_sample_body,
        out_shape=[jax.ShapeDtypeStruct((Bs, n, G * dh), f32)] * 2,
        grid_spec=pltpu.PrefetchScalarGridSpec(
            num_scalar_prefetch=1,
            grid=(Bs, n_pages // PAGES_PER_STEP),
            in_specs=[page(k) for k in range(PAGES_PER_STEP)] + [full(a) for a in consts],
            out_specs=[o_spec, o_spec],
            scratch_shapes=[pltpu.VMEM((2 * G * dh // LANES, n_rows, LANES), f32)],
        ),
        compiler_params=_params("parallel", "arbitrary"),
        name="nsa_compress_sample",
    )(page_table, *([cache_t] * PAGES_PER_STEP), *consts)


def _cmp_topk_sample_body(sl_ref, q_ref, kc_ref, vc_ref, oc_ref, idx_ref, *, t):
    G, dh = C_KV_HEADS, C_HEAD_DIM
    n = kc_ref.shape[0]
    n_slc = t // C_SEL_BLOCK + 1
    NS = -(-n_slc // LANES) * LANES
    cpos = lax.broadcasted_iota(jnp.int32, (1, n), 1) * C_CMP_STRIDE + (C_CMP_BLOCK - 1)
    mask = cpos <= t
    c_start = lax.broadcasted_iota(jnp.int32, (n, NS), 0) * C_CMP_STRIDE
    jb = lax.broadcasted_iota(jnp.int32, (n, NS), 1)
    cover = ((c_start < (jb + 1) * C_SEL_BLOCK) & (c_start + C_CMP_BLOCK > jb * C_SEL_BLOCK)).astype(f32)
    blk = lax.broadcasted_iota(jnp.int32, (1, NS), 1)
    cur = t // C_SEL_BLOCK
    forced = (blk == 0) | (blk == cur) | (blk == cur - 1)
    lane = lax.broadcasted_iota(jnp.int32, (1, LANES), 1)
    o_parts, idx_rows = [], []
    for g in range(G):
        q8 = jnp.concatenate([q_ref[:, (g * C_HPG + hh) * dh:(g * C_HPG + hh + 1) * dh] for hh in range(C_HPG)]
                             + [jnp.zeros((8 - C_HPG, dh), f32)], axis=0)
        slope = jnp.concatenate([jnp.full((1, 1), sl_ref[g * C_HPG + hh], f32) for hh in range(C_HPG)]
                                + [jnp.zeros((8 - C_HPG, 1), f32)], axis=0)
        s = _dot_t(q8, kc_ref[:, g * dh:(g + 1) * dh], HIGHEST) * dh ** -0.5
        s = s - slope * (float(t) - cpos.astype(f32))
        s = jnp.where(mask, s, NEG_INF)
        m = jnp.max(s, axis=-1, keepdims=True)
        e = jnp.where(mask, jnp.exp(s - m), 0.0)
        p = e / jnp.maximum(jnp.sum(e, axis=-1, keepdims=True), TINY)
        o = _hdot(p, vc_ref[:, g * dh:(g + 1) * dh])
        o_parts += [o[hh:hh + 1] for hh in range(C_HPG)]
        imp = _hdot(jnp.sum(p[0:C_HPG], axis=0, keepdims=True), cover)
        score = jnp.where(blk * C_SEL_BLOCK <= t, jnp.where(forced, FORCE, imp), NEG_INF)
        row = jnp.zeros((1, LANES), jnp.int32)
        for r in range(C_TOPK):
            mx = jnp.max(score, axis=-1, keepdims=True)
            pick = jnp.min(jnp.where(score == mx, blk, NS), axis=-1, keepdims=True)
            row = jnp.where(lane == r, pick, row)
            score = jnp.where(blk == pick, REMOVED, score)
        idx_rows.append(row)
    oc_ref[...] = jnp.concatenate(o_parts, axis=-1)
    idx_ref[...] = jnp.concatenate(idx_rows + [jnp.zeros((8 - G, LANES), jnp.int32)], axis=0)


def nsa_cmp_topk_sample(proj, kc, vc, t):
    Bs = proj.shape[0]
    n, W = kc.shape[1:]
    HD = C_HEADS * C_HEAD_DIM
    return pl.pallas_call(
        functools.partial(_cmp_topk_sample_body, t=t),
        out_shape=[jax.ShapeDtypeStruct((Bs, 1, HD), f32), jax.ShapeDtypeStruct((Bs, 8, LANES), jnp.int32)],
        grid_spec=pltpu.PrefetchScalarGridSpec(
            num_scalar_prefetch=1,
            grid=(Bs,),
            in_specs=[pl.BlockSpec((None, 1, HD), lambda b, sl: (b, 0, 0)),
                      pl.BlockSpec((None, n, W), lambda b, sl: (b, 0, 0)),
                      pl.BlockSpec((None, n, W), lambda b, sl: (b, 0, 0))],
            out_specs=[pl.BlockSpec((None, 1, HD), lambda b, sl: (b, 0, 0)),
                       pl.BlockSpec((None, 8, LANES), lambda b, sl: (b, 0, 0))],
        ),
        compiler_params=_params("parallel"),
        name="nsa_cmp_topk_sample",
    )(jnp.asarray(SLOPES_C), proj, kc, vc)


def _sel_win_sample_body(pt_ref, ix_ref, sl_ref, *refs, t):
    blocks = refs[:C_TOPK]
    (win_ref, q_ref, ksn_ref, vsn_ref, kwn_ref, vwn_ref, gate_ref, bg_ref, oc_ref, o_ref) = refs[C_TOPK:]
    b = pl.program_id(0)
    g = pl.program_id(1)
    G, dh = C_KV_HEADS, C_HEAD_DIM
    scale = dh ** -0.5
    n_past = t // C_SEL_BLOCK
    per_page = PAGE_SIZE // C_SEL_BLOCK
    q8 = jnp.concatenate([q_ref[:, hh * dh:(hh + 1) * dh] for hh in range(C_HPG)]
                         + [jnp.zeros((8 - C_HPG, dh), f32)], axis=0)
    slope = jnp.concatenate([jnp.full((1, 1), sl_ref[g * C_HPG + hh], f32) for hh in range(C_HPG)]
                            + [jnp.zeros((8 - C_HPG, 1), f32)], axis=0)

    def own(ref):
        return functools.reduce(jnp.add, [jnp.where(g == gg, ref[:, gg * dh:(gg + 1) * dh], 0.0) for gg in range(G)])

    def softmax_pv(scores, values_t, s_new, v_new):
        m = functools.reduce(jnp.maximum, [jnp.max(s, axis=-1, keepdims=True) for s in scores] + [s_new])
        e_new = jnp.exp(s_new - m)
        den = e_new
        acc = e_new * v_new
        for s, vt in zip(scores, values_t):
            e = jnp.exp(s - m)
            den = den + jnp.sum(e, axis=-1, keepdims=True)
            acc = acc + _dot_t(e, vt, HIGHEST)
        return acc / den

    lane = lax.broadcasted_iota(jnp.int32, (1, PAGE_SIZE), 1)
    scores, values_t = [], []
    for s_i in range(C_TOPK):
        j = ix_ref[b, g * C_TOPK + s_i]
        jc = jnp.minimum(j, n_past - 1)
        kpos = (jc // per_page) * PAGE_SIZE + lane
        sc = _hdot(q8, blocks[s_i][0]) * scale - slope * (t - kpos).astype(f32)
        ok = ((lane >> C_SEL_SHIFT) == (jc % per_page)) & (j < n_past)
        scores.append(jnp.where(ok, sc, NEG_INF))
        values_t.append(blocks[s_i][1])
    s_new = jnp.sum(q8 * own(ksn_ref), axis=-1, keepdims=True) * scale
    o_s = softmax_pv(scores, values_t, s_new, own(vsn_ref))

    nw = win_ref.shape[-1]
    wdist = (nw - lax.broadcasted_iota(jnp.int32, (1, nw), 1)).astype(f32)
    sw = _hdot(q8, win_ref[0]) * scale - slope * wdist
    s_new = jnp.sum(q8 * own(kwn_ref), axis=-1, keepdims=True) * scale
    o_w = softmax_pv([sw], [win_ref[1]], s_new, own(vwn_ref))

    gs = jax.nn.sigmoid(gate_ref[...] + bg_ref[...])
    glane = lax.broadcasted_iota(jnp.int32, (1, LANES), 1)
    outs = []
    for hh in range(C_HPG):
        gate = [jnp.sum(jnp.where(glane == (g * C_HPG + hh) * 3 + r, gs, 0.0), axis=-1, keepdims=True)
                for r in range(3)]
        outs.append(gate[0] * oc_ref[:, hh * dh:(hh + 1) * dh] + gate[1] * o_s[hh:hh + 1] + gate[2] * o_w[hh:hh + 1])
    o_ref[...] = jnp.concatenate(outs, axis=-1).astype(o_ref.dtype)


def nsa_sel_win_sample(page_table, sel_idx, cache_t, win_t, layer, proj, b_gate_p, o_c, t):
    Bs = proj.shape[0]
    G, dh = C_KV_HEADS, C_HEAD_DIM
    GD = G * dh
    assert win_t.shape[-1] == C_WINDOW and t % C_SEL_BLOCK == 0
    n_past = t // C_SEL_BLOCK
    per_page = PAGE_SIZE // C_SEL_BLOCK

    def blk_spec(s_i):
        def imap(b, g, pt, ix, sl):
            j = jnp.minimum(ix[b, g * C_TOPK + s_i], n_past - 1)
            return (layer, pt[b, j // per_page], 1, g, 0, 0)
        return pl.BlockSpec((None, None, 2, None, dh, PAGE_SIZE), imap)

    row = lambda w, blk: pl.BlockSpec((None, 1, w), lambda b, g, pt, ix, sl, blk=blk: (b, 0, blk))
    base = C_HEADS * dh // GD
    gate_blk = (C_HEADS * dh + 6 * GD) // LANES
    return pl.pallas_call(
        functools.partial(_sel_win_sample_body, t=t),
        out_shape=jax.ShapeDtypeStruct((Bs, 1, C_HEADS * dh), bf16),
        grid_spec=pltpu.PrefetchScalarGridSpec(
            num_scalar_prefetch=3,
            grid=(Bs, G),
            in_specs=[blk_spec(s_i) for s_i in range(C_TOPK)] + [
                pl.BlockSpec((None, None, 2, None, dh, C_WINDOW), lambda b, g, pt, ix, sl: (layer, b, 0, g, 0, 0)),
                pl.BlockSpec((None, 1, C_HPG * dh), lambda b, g, pt, ix, sl: (b, 0, g)),
                row(GD, base + 2), row(GD, base + 3), row(GD, base + 4), row(GD, base + 5),
                row(LANES, gate_blk),
                pl.BlockSpec((1, LANES), lambda b, g, pt, ix, sl: (0, 0)),
                pl.BlockSpec((None, 1, C_HPG * dh), lambda b, g, pt, ix, sl: (b, 0, g)),
            ],
            out_specs=pl.BlockSpec((None, 1, C_HPG * dh), lambda b, g, pt, ix, sl: (b, 0, g)),
        ),
        compiler_params=_params("parallel", "parallel"),
        name="nsa_sel_win_sample",
    )(page_table, sel_idx, jnp.asarray(SLOPES_C), *([cache_t] * C_TOPK), win_t, proj, proj, proj, proj, proj,
      proj, b_gate_p, o_c)


def _norm_body(x_ref, g_ref, o_ref):
    o_ref[...] = _rms(x_ref[...], g_ref[...])


def rmsnorm_rows(x, g, *, tm=1024):
    M, D = x.shape
    tm = min(tm, M)
    return pl.pallas_call(
        _norm_body,
        out_shape=jax.ShapeDtypeStruct((M, D), f32),
        grid=(M // tm,),
        in_specs=[pl.BlockSpec((tm, D), lambda i: (i, 0)), pl.BlockSpec((1, D), lambda i: (0, 0))],
        out_specs=pl.BlockSpec((tm, D), lambda i: (i, 0)),
        compiler_params=_params("parallel"),
        name="rmsnorm",
    )(x, g.reshape(1, D))


def _position_minor(cache):
    return jnp.transpose(cache, (0, 1, 3, 4, 5, 2))


def kernel(x_prompt, x_sample, cache_dil_w128, cache_dil_w512, cache_dil_w2048, state_gla, cache_nsa_win,
           cache_nsa_kv, cache_mem_kv, page_table, mem_prompt, g_mix, g_cross, g_mem, g_ffn, g_final,
           w_a_qkv, w_a_o, w_b_in, w_b_gate2, b_b_gate, g_b_head, w_b_o, w_c_in, b_c_gate, c_pos_k, c_pos_v,
           w_c_k1, w_c_k2, w_c_v1, w_c_v2, w_c_o, w_x_q, w_x_kv, w_x_o, w_ffn_in, w_ffn_out):
    Bp, L, D = x_prompt.shape
    Bs = x_sample.shape[0]
    depth = g_mix.shape[0]
    n_pages = page_table.shape[1]
    t_s = n_pages * PAGE_SIZE
    G, dh = C_KV_HEADS, C_HEAD_DIM
    GD = G * dh
    HK, HV = B_HEADS * B_DK, B_HEADS * B_DV
    cast = lambda a: a.astype(bf16)

    wa_qkv, wa_o = cast(w_a_qkv), cast(w_a_o)
    nb = 2 * HK + HV
    wb_in = cast(jnp.concatenate(
        [w_b_in[..., :nb], w_b_in[..., nb + B_GATE_RANK:], w_b_in[..., nb:nb + B_GATE_RANK],
         jnp.zeros(w_b_in.shape[:2] + (LANES - B_GATE_RANK,), f32)], axis=-1))
    wb_gate2 = jnp.pad(w_b_gate2, ((0, 0), (0, LANES - B_GATE_RANK), (0, 0)))
    wb_o = cast(w_b_o)
    nc = -(-w_c_in.shape[-1] // LANES) * LANES
    wc_in = cast(jnp.pad(w_c_in, ((0, 0), (0, 0), (0, nc - w_c_in.shape[-1]))))
    bc_gate = jnp.pad(b_c_gate, ((0, 0), (0, LANES - b_c_gate.shape[-1])))
    half = C_CMP_STRIDE * dh
    two_chunk = lambda w: cast(jnp.concatenate([w[:, :half], w[:, half:]], axis=-1))
    wc_k1, wc_v1 = two_chunk(w_c_k1), two_chunk(w_c_v1)
    wc_k2, wc_v2, wc_o = cast(w_c_k2), cast(w_c_v2), cast(w_c_o)
    pos_k = c_pos_k.reshape(-1, 2, half)
    pos_v = c_pos_v.reshape(-1, 2, half)
    wx_q, wx_kv, wx_o = cast(w_x_q), cast(w_x_kv), cast(w_x_o)
    wf_in, wf_out = cast(w_ffn_in), cast(w_ffn_out)

    Mp = Bp * L
    x = x_prompt.reshape(Mp, D)
    mem2d = mem_prompt.reshape(-1, D)
    dil_p = [[] for _ in range(A_GROUPS)]
    gla_p, rows_p, win_p, mem_p = [], [], [], []
    for l in range(depth):
        kind, j = l % N_MIXERS, l // N_MIXERS
        if kind == 0:
            qkv = mm(x, wa_qkv[j], g=g_mix[l]).reshape(Bp, L, -1)
            outs = [dilated_group(qkv, grp) for grp in range(A_GROUPS)]
            x = dilated_merge_out([o.reshape(Mp, A_OUT) for o, _ in outs],
                                  [s.reshape(Mp, A_OUT) for _, s in outs], wa_o[j], x)
            sect = A_GROUPS * A_OUT
            for grp in range(A_GROUPS):
                w = min(A_WINDOWS[grp], L)
                tail = lambda t: qkv[:, L - w:, t * sect + grp * A_OUT:t * sect + (grp + 1) * A_OUT].reshape(
                    Bp, w, A_SLOTS, A_HEAD_DIM)
                dil_p[grp].append(jnp.stack([tail(1), tail(2)], axis=2))
        elif kind == 1:
            proj = mm(x, wb_in[j], g=g_mix[l]).reshape(Bp, L, -1)
            y, S = gla_prompt(proj, wb_gate2[j], b_b_gate[j], g_b_head[j].reshape(-1),
                              jnp.zeros((Bp, B_HEADS, B_DK, B_DV), f32))
            gla_p.append(S)
            x = mm(y.reshape(Mp, HV), wb_o[j], res=x)
        else:
            proj = mm(x, wc_in[j], g=g_mix[l]).reshape(Bp, L, -1)
            q_w = C_HEADS * dh
            rows = proj[..., q_w:q_w + 6 * GD].reshape(Bp, L, 6, G, dh)
            chunks = lambda a: a.reshape(Bp, L // C_CMP_STRIDE, C_CMP_STRIDE, G, dh).transpose(
                0, 3, 1, 2, 4).reshape(Bp, G, L // C_CMP_STRIDE, half)
            kc, vc = nsa_compress(chunks(rows[:, :, 0]), chunks(rows[:, :, 1]), pos_k[j], pos_v[j],
                                  wc_k1[j], wc_k2[j], wc_v1[j], wc_v2[j])
            seqs = [cast(rows[:, :, r].transpose(0, 2, 1, 3)) for r in range(2, 6)]
            o = nsa_attention_prompt(proj, bc_gate[j:j + 1], kc, vc, *seqs)
            x = mm(o.reshape(Mp, q_w), wc_o[j], res=x)
            rows_p.append(rows[:, :, :4])
            win_p.append(rows[:, L - min(C_WINDOW, L):, 4:])
        mem_kv = mm(mem2d, wx_kv[l], g=g_mem[l]).reshape(Bp, -1, 2 * D)
        mem_p.append(mem_kv.reshape(Bp, -1, 2, MEM_HEADS, D // MEM_HEADS))
        x = xattn_prompt(x.reshape(Bp, L, D), g_cross[l], wx_q[l], cast(mem_kv), wx_o[l]).reshape(Mp, D)
        x = ffn(x, g_ffn[l], wf_in[l], wf_out[l])
    y_prompt = rmsnorm_rows(x, g_final).reshape(Bp, L, D)

    x = x_sample.reshape(Bs, D)
    dil_t = [_position_minor(c) for c in (cache_dil_w128, cache_dil_w512, cache_dil_w2048)]
    nsa_t = _position_minor(cache_nsa_kv)
    win_t = _position_minor(cache_nsa_win)
    dil_s = [[] for _ in range(A_GROUPS)]
    gla_s, rows_s, win_s = [], [], []
    for l in range(depth):
        kind, j = l % N_MIXERS, l // N_MIXERS
        if kind == 0:
            qkv = mm(x, wa_qkv[j], g=g_mix[l]).reshape(Bs, 1, -1)
            o = dil_sample(qkv, dil_t, j)
            x = mm(o.transpose(0, 2, 1).reshape(Bs, A_OUT), wa_o[j], res=x)
            kv = qkv.reshape(Bs, 1, 3, A_GROUPS, A_SLOTS, A_HEAD_DIM)
            for grp in range(A_GROUPS):
                dil_s[grp].append(jnp.stack([kv[:, :, 1, grp], kv[:, :, 2, grp]], axis=2))
        elif kind == 1:
            proj = mm(x, wb_in[j], g=g_mix[l]).reshape(Bs, 1, -1)
            y, S = gla_step(proj, wb_gate2[j], b_b_gate[j], g_b_head[j].reshape(-1), state_gla, j)
            gla_s.append(S)
            x = mm(y.reshape(Bs, HV), wb_o[j], res=x)
        else:
            proj = mm(x, wc_in[j], g=g_mix[l]).reshape(Bs, 1, -1)
            q_w = C_HEADS * dh
            kc, vc = nsa_compress_sample(page_table, nsa_t, j, pos_k[j], pos_v[j],
                                         wc_k1[j], wc_k2[j], wc_v1[j], wc_v2[j])
            o_c, idx = nsa_cmp_topk_sample(proj, kc, vc, t_s)
            sel_idx = idx[:, :G, :C_TOPK].reshape(Bs, G * C_TOPK)
            o = nsa_sel_win_sample(page_table, sel_idx, nsa_t, win_t, j, proj, bc_gate[j:j + 1], o_c, t_s)
            x = mm(o.reshape(Bs, q_w), wc_o[j], res=x)
            rows = proj[..., q_w:q_w + 6 * GD].reshape(Bs, 1, 6, G, dh)
            rows_s.append(rows[:, :, :4])
            win_s.append(rows[:, :, 4:])
        q = mm(x, wx_q[l], g=g_cross[l]).reshape(Bs, 1, D)
        o = xattn_sample(q, cache_mem_kv, l)
        x = mm(o.reshape(Bs, D), wx_o[l], res=x)
        x = ffn(x, g_ffn[l], wf_in[l], wf_out[l])
    y_sample = rmsnorm_rows(x, g_final).reshape(Bs, 1, D)

    st = jnp.stack
    return (y_prompt, y_sample, st(dil_p[0]), st(dil_s[0]), st(dil_p[1]), st(dil_s[1]), st(dil_p[2]), st(dil_s[2]),
            st(gla_p), st(gla_s), st(win_p), st(win_s), st(rows_p), st(rows_s), st(mem_p))
```

```python
import functools

import jax
import jax.numpy as jnp
import numpy as np
from jax import lax
from jax.experimental import pallas as pl
from jax.experimental.pallas import tpu as pltpu

f32 = jnp.float32
bf16 = jnp.bfloat16

N_MIXERS = 3
A_WINDOWS = (128, 512, 2048)
A_DILATIONS = (1, 4, 16)
A_GROUPS = 3
A_SLOTS = 8
A_HEAD_DIM = 64
A_OUT = A_SLOTS * A_HEAD_DIM
A_WIN_STEPS = 128

B_HEADS = 4
B_DK = 128
B_DV = 256
B_GATE_RANK = 16
B_GATE_TAU = 16.0

C_HEADS = 16
C_KV_HEADS = 4
C_HPG = C_HEADS // C_KV_HEADS
C_HEAD_DIM = 64
C_CMP_BLOCK = 32
C_CMP_STRIDE = 16
C_SEL_BLOCK = 64
C_SEL_SHIFT = 6
C_TOPK = 16
C_WINDOW = 512
C_CMP_HIDDEN = 128

MEM_HEADS = 4
PAGE_SIZE = 128

RMS_EPS = 1e-6
NEG_INF = -1e30
FORCE = 1e30
TINY = 1e-30
REMOVED = -3e38

LANES = 128
VMEM_LIMIT_BYTES = 56 * 1024 * 1024
HIGHEST = lax.Precision.HIGHEST


def _alibi_slopes(n):
    return np.asarray(2.0 ** (-8.0 * np.arange(1, n + 1) / n), dtype=np.float32)


def _bf16_pieces(x, n):
    out, rest = [], np.asarray(x, np.float32)
    for _ in range(n):
        p = rest.astype(bf16).astype(np.float32)
        out.append(p)
        rest = (rest - p).astype(np.float32)
    return np.stack(out, axis=-1)


SLOPES_A = _alibi_slopes(A_GROUPS * A_SLOTS).reshape(A_GROUPS, A_SLOTS)
SLOPES_C = _alibi_slopes(C_HEADS)
N_PIECES = 3
SLOPES_C_PIECES = _bf16_pieces(SLOPES_C, N_PIECES).reshape(-1)
AUG = 2 * N_PIECES


def _params(*sem):
    return pltpu.CompilerParams(dimension_semantics=sem, vmem_limit_bytes=VMEM_LIMIT_BYTES)


def _rms(x, g):
    return x * lax.rsqrt(jnp.mean(x * x, axis=-1, keepdims=True) + RMS_EPS) * g


def _dot_t(a, b, precision=None):
    return lax.dot_general(a, b, (((1,), (1,)), ((), ())), precision=precision, preferred_element_type=f32)


def _hdot(a, b):
    return jnp.dot(a, b, precision=HIGHEST, preferred_element_type=f32)


def _pick_tile(n, target):
    best = LANES
    for t in range(LANES, min(n, target) + 1, LANES):
        if n % t == 0:
            best = t
    return best


def _mm_body(*refs, norm, res):
    it = iter(refs)
    x_ref = next(it)
    g_ref = next(it) if norm else None
    w_ref = next(it)
    r_ref = next(it) if res else None
    o_ref = next(it)
    xn_ref = next(it)

    @pl.when(pl.program_id(1) == 0)
    def _():
        x = x_ref[...].astype(f32)
        if norm:
            x = _rms(x, g_ref[...])
        xn_ref[...] = x.astype(bf16)

    acc = jnp.dot(xn_ref[...], w_ref[...], preferred_element_type=f32)
    if res:
        acc = acc + r_ref[...]
    o_ref[...] = acc.astype(o_ref.dtype)


def mm(x, w, *, g=None, res=None, out_dtype=f32, tm=1024, tn=1024):
    M, K = x.shape
    N = w.shape[1]
    tm = min(tm, M)
    tn = _pick_tile(N, tn)
    assert M % tm == 0 and N % tn == 0
    args = [x]
    specs = [pl.BlockSpec((tm, K), lambda i, j: (i, 0))]
    if g is not None:
        args.append(g.reshape(1, K))
        specs.append(pl.BlockSpec((1, K), lambda i, j: (0, 0)))
    args.append(w)
    specs.append(pl.BlockSpec((K, tn), lambda i, j: (0, j)))
    if res is not None:
        args.append(res)
        specs.append(pl.BlockSpec((tm, tn), lambda i, j: (i, j)))
    return pl.pallas_call(
        functools.partial(_mm_body, norm=g is not None, res=res is not None),
        out_shape=jax.ShapeDtypeStruct((M, N), out_dtype),
        grid=(M // tm, N // tn),
        in_specs=specs,
        out_specs=pl.BlockSpec((tm, tn), lambda i, j: (i, j)),
        scratch_shapes=[pltpu.VMEM((tm, K), bf16)],
        compiler_params=_params("parallel", "arbitrary"),
        name="mm",
    )(*args)


def _ffn_body(x_ref, g_ref, wa_ref, wb_ref, wo_ref, o_ref, xn_ref, acc_ref):
    j = pl.program_id(1)

    @pl.when(j == 0)
    def _():
        xn_ref[...] = _rms(x_ref[...], g_ref[...]).astype(bf16)
        acc_ref[...] = jnp.zeros_like(acc_ref)

    xn = xn_ref[...]
    a = jnp.dot(xn, wa_ref[...], preferred_element_type=f32)
    b = jnp.dot(xn, wb_ref[...], preferred_element_type=f32)
    h = (a * jax.nn.sigmoid(a) * b).astype(bf16)
    acc_ref[...] += jnp.dot(h, wo_ref[...], preferred_element_type=f32)

    @pl.when(j == pl.num_programs(1) - 1)
    def _():
        o_ref[...] = x_ref[...] + acc_ref[...]


def ffn(x, g, w_in, w_out, *, tm=1024, th=704):
    M, D = x.shape
    F = w_out.shape[0]
    tm = min(tm, M)
    th = _pick_tile(F, th)
    nh = F // th
    return pl.pallas_call(
        _ffn_body,
        out_shape=jax.ShapeDtypeStruct((M, D), f32),
        grid=(M // tm, nh),
        in_specs=[
            pl.BlockSpec((tm, D), lambda i, j: (i, 0)),
            pl.BlockSpec((1, D), lambda i, j: (0, 0)),
            pl.BlockSpec((D, th), lambda i, j: (0, j)),
            pl.BlockSpec((D, th), lambda i, j: (0, j + nh)),
            pl.BlockSpec((th, D), lambda i, j: (j, 0)),
        ],
        out_specs=pl.BlockSpec((tm, D), lambda i, j: (i, 0)),
        scratch_shapes=[pltpu.VMEM((tm, D), bf16), pltpu.VMEM((tm, D), f32)],
        compiler_params=_params("parallel", "arbitrary"),
        name="ffn",
    )(x, g.reshape(1, D), w_in, w_in, w_out)


def _xattn_body(x_ref, g_ref, wq_ref, kv_ref, wo_ref, o_ref, *, heads):
    x = x_ref[...]
    D = x.shape[-1]
    dh = D // heads
    xn = _rms(x, g_ref[...]).astype(bf16)
    q = jnp.dot(xn, wq_ref[...], preferred_element_type=f32).astype(bf16)
    outs = []
    for h in range(heads):
        kh = kv_ref[:, h * dh:(h + 1) * dh]
        vh = kv_ref[:, D + h * dh:D + (h + 1) * dh]
        s = _dot_t(q[:, h * dh:(h + 1) * dh], kh) * dh ** -0.5
        m = jnp.max(s, axis=-1, keepdims=True)
        e = jnp.exp(s - m)
        p = e / jnp.sum(e, axis=-1, keepdims=True)
        outs.append(jnp.dot(p.astype(bf16), vh, preferred_element_type=f32))
    o = jnp.concatenate(outs, axis=-1).astype(bf16)
    o_ref[...] = x + jnp.dot(o, wo_ref[...], preferred_element_type=f32)


def xattn_prompt(x, g, wq, kv, wo, *, tq=512):
    B, L, D = x.shape
    Mem = kv.shape[1]
    tq = min(tq, L)
    return pl.pallas_call(
        functools.partial(_xattn_body, heads=MEM_HEADS),
        out_shape=jax.ShapeDtypeStruct((B, L, D), f32),
        grid=(B, L // tq),
        in_specs=[
            pl.BlockSpec((None, tq, D), lambda b, i: (b, i, 0)),
            pl.BlockSpec((1, D), lambda b, i: (0, 0)),
            pl.BlockSpec((D, D), lambda b, i: (0, 0)),
            pl.BlockSpec((None, Mem, 2 * D), lambda b, i: (b, 0, 0)),
            pl.BlockSpec((D, D), lambda b, i: (0, 0)),
        ],
        out_specs=pl.BlockSpec((None, tq, D), lambda b, i: (b, i, 0)),
        compiler_params=_params("parallel", "parallel"),
        name="xattn_prompt",
    )(x, g.reshape(1, D), wq, kv, wo)


def _dil_body(sl_ref, q_ref, kc_ref, kp_ref, vc_ref, vp_ref, o_ref, lse_ref, *, grp, dil, tq):
    i = pl.program_id(1)
    c = pl.program_id(2)
    ws = A_WIN_STEPS
    scale = A_HEAD_DIM ** -0.5
    row = lax.broadcasted_iota(jnp.int32, (ws, 2 * ws), 0)
    col = lax.broadcasted_iota(jnp.int32, (ws, 2 * ws), 1)
    dist = row + ws - col
    in_band = (dist >= 0) & (dist <= ws)
    valid_first = in_band & ((col >= ws) | (i > 0))
    distf = (dist * dil).astype(f32)
    lane = lax.broadcasted_iota(jnp.int32, (1, LANES), 1)
    first = lane < A_HEAD_DIM
    slopes = [sl_ref[grp * A_SLOTS + 2 * c + half] for half in range(2)]

    def rows(ref, r, start, n):
        return ref[pl.ds(r + start * dil, n, stride=dil), :] if dil > 1 else ref[start:start + n, :]

    def residue(r, carry):
        for j in range(tq // ws):
            q = rows(q_ref, r, j * ws, ws).astype(bf16)
            if j == 0:
                k = jnp.concatenate([rows(kp_ref, r, 0, ws), rows(kc_ref, r, 0, ws)], axis=0)
                v = jnp.concatenate([rows(vp_ref, r, 0, ws), rows(vc_ref, r, 0, ws)], axis=0)
            else:
                k = rows(kc_ref, r, (j - 1) * ws, 2 * ws)
                v = rows(vc_ref, r, (j - 1) * ws, 2 * ws)
            k, v = k.astype(bf16), v.astype(bf16)
            valid = valid_first if j == 0 else in_band
            halves = []
            for half in range(2):
                qm = jnp.where(first == (half == 0), q, jnp.zeros_like(q))
                s = _dot_t(qm, k) * scale - slopes[half] * distf
                s = jnp.where(valid, s, NEG_INF)
                m = jnp.max(s, axis=-1, keepdims=True)
                e = jnp.exp(s - m)
                den = jnp.sum(e, axis=-1, keepdims=True)
                pv = jnp.dot(e.astype(bf16), v, preferred_element_type=f32) / den
                halves.append((pv, m + jnp.log(den)))
            o_t = jnp.where(first, halves[0][0], halves[1][0])
            l_t = jnp.where(first, halves[0][1], halves[1][1])
            if dil > 1:
                o_ref[pl.ds(r + j * ws * dil, ws, stride=dil), :] = o_t
                lse_ref[pl.ds(r + j * ws * dil, ws, stride=dil), :] = l_t
            else:
                o_ref[j * ws:(j + 1) * ws, :] = o_t
                lse_ref[j * ws:(j + 1) * ws, :] = l_t
        return carry

    if dil > 1:
        lax.fori_loop(0, dil, residue, 0)
    else:
        residue(0, 0)


DIL_TQ = (1024, 256, 128)


def dilated_group(qkv, grp):
    B, L, W3 = qkv.shape
    d = A_DILATIONS[grp]
    tq = min(DIL_TQ[grp], L // d)
    rows_blk = tq * d
    prev_blk = A_WIN_STEPS * d
    assert L % rows_blk == 0 and tq % A_WIN_STEPS == 0
    ratio = tq // A_WIN_STEPS
    tiles = A_OUT // LANES
    sect = A_GROUPS * tiles

    def spec(t, prev=False):
        col = lambda c: t * sect + grp * tiles + c
        if prev:
            return pl.BlockSpec((None, prev_blk, LANES),
                                lambda b, i, c, sl: (b, jnp.maximum(i * ratio - 1, 0), col(c)))
        return pl.BlockSpec((None, rows_blk, LANES), lambda b, i, c, sl: (b, i, col(c)))

    out_spec = pl.BlockSpec((None, rows_blk, LANES), lambda b, i, c, sl: (b, i, c))
    return pl.pallas_call(
        functools.partial(_dil_body, grp=grp, dil=d, tq=tq),
        out_shape=[jax.ShapeDtypeStruct((B, L, A_OUT), f32)] * 2,
        grid_spec=pltpu.PrefetchScalarGridSpec(
            num_scalar_prefetch=1,
            grid=(B, L // rows_blk, tiles),
            in_specs=[spec(0), spec(1), spec(1, True), spec(2), spec(2, True)],
            out_specs=[out_spec, out_spec],
        ),
        compiler_params=_params("parallel", "parallel", "parallel"),
        name=f"dilated_g{grp}",
    )(jnp.asarray(SLOPES_A.reshape(-1)), qkv, qkv, qkv, qkv, qkv)


def _dil_out_body(o0, o1, o2, l0, l1, l2, w_ref, r_ref, out_ref):
    ls = [l0[...], l1[...], l2[...]]
    m = jnp.maximum(jnp.maximum(ls[0], ls[1]), ls[2])
    es = [jnp.exp(l - m) for l in ls]
    den = es[0] + es[1] + es[2]
    o = (es[0] / den) * o0[...] + (es[1] / den) * o1[...] + (es[2] / den) * o2[...]
    out_ref[...] = r_ref[...] + jnp.dot(o.astype(bf16), w_ref[...], preferred_element_type=f32)


def dilated_merge_out(os, lses, w_o, res, *, tm=512):
    M, D = res.shape
    tm = min(tm, M)
    row = lambda n: pl.BlockSpec((tm, n), lambda i: (i, 0))
    return pl.pallas_call(
        _dil_out_body,
        out_shape=jax.ShapeDtypeStruct((M, D), f32),
        grid=(M // tm,),
        in_specs=[row(A_OUT)] * 6 + [pl.BlockSpec((A_OUT, D), lambda i: (0, 0)), row(D)],
        out_specs=row(D),
        compiler_params=_params("parallel"),
        name="dilated_merge_out",
    )(*os, *lses, w_o, res)


GLA_SUB = 16


def _log_sigmoid(z):
    return -(jnp.maximum(-z, 0.0) + jnp.log1p(jnp.exp(-jnp.abs(z))))


def _gla_body(q_ref, k_ref, v_ref, r_ref, gl_ref, w2_ref, bg_ref, gh_ref, s0_ref, y_ref, s_ref, a_ref):
    C = q_ref.shape[0]
    c = GLA_SUB

    @pl.when(pl.program_id(1) == 0)
    def _():
        s_ref[...] = s0_ref[...]

    z = _hdot(gl_ref[...], w2_ref[...]) + bg_ref[...]
    la = _log_sigmoid(z) / B_GATE_TAU
    rowC = lax.broadcasted_iota(jnp.int32, (C, C), 0)
    colC = lax.broadcasted_iota(jnp.int32, (C, C), 1)
    tri = (rowC >= colC).astype(f32)
    b_all = _hdot(tri, la)
    row_k = lax.broadcasted_iota(jnp.int32, (C, B_DK), 0)
    col_c = lax.broadcasted_iota(jnp.int32, (c, C), 1)
    row_c = lax.broadcasted_iota(jnp.int32, (c, 1), 0)

    for h in range(B_HEADS):
        bh = b_all[:, h * B_DK:(h + 1) * B_DK]
        qh = q_ref[:, h * B_DK:(h + 1) * B_DK] * B_DK ** -0.5
        kh = k_ref[:, h * B_DK:(h + 1) * B_DK]
        vh = v_ref[:, h * B_DV:(h + 1) * B_DV].astype(bf16)
        S = s_ref[h]
        o = jnp.dot((qh * jnp.exp(bh)).astype(bf16), S.astype(bf16), preferred_element_type=f32)

        for I in range(C // c):
            r0 = I * c
            qI, kI, bI = qh[r0:r0 + c], kh[r0:r0 + c], bh[r0:r0 + c]
            if I == 0:
                A_I = jnp.zeros((c, C), f32)
            else:
                beta = bh[r0 - 1:r0]
                qt = qI * jnp.exp(bI - beta)
                kt = kh * jnp.exp(jnp.where(row_k < r0, beta - bh, 0.0))
                A_I = jnp.where(col_c < r0, _dot_t(qt.astype(bf16), kt.astype(bf16)), 0.0)
            for j in range(c):
                ex = jnp.exp(jnp.minimum(bI - bI[j:j + 1], 0.0))
                tj = jnp.sum(qI * kI[j:j + 1] * ex, axis=-1, keepdims=True)
                tj = jnp.where(row_c >= j, tj, 0.0)
                A_I = jnp.where(col_c == r0 + j, tj, A_I)
            a_ref[r0:r0 + c, :] = A_I
        o = o + jnp.dot(a_ref[...].astype(bf16), vh, preferred_element_type=f32)

        b_last = bh[C - 1:C]
        kdec = jnp.transpose(kh * jnp.exp(b_last - bh)).astype(bf16)
        decay = jnp.transpose(jnp.broadcast_to(jnp.exp(b_last), (8, B_DK)))[:, 0:1]
        s_ref[h] = decay * S + jnp.dot(kdec, vh, preferred_element_type=f32)

        on = _rms(o, gh_ref[:, h * B_DV:(h + 1) * B_DV])
        rh = r_ref[:, h * B_DV:(h + 1) * B_DV]
        y_ref[:, h * B_DV:(h + 1) * B_DV] = (on * (rh * jax.nn.sigmoid(rh))).astype(y_ref.dtype)


def gla_prompt(proj, w_gate2p, b_gate, g_head, s0, *, chunk=128):
    B, L, _ = proj.shape
    HK, HV = B_HEADS * B_DK, B_HEADS * B_DV
    C = min(chunk, L)
    assert L % C == 0 and C % GLA_SUB == 0
    return pl.pallas_call(
        _gla_body,
        out_shape=[jax.ShapeDtypeStruct((B, L, HV), bf16),
                   jax.ShapeDtypeStruct((B, B_HEADS, B_DK, B_DV), f32)],
        grid=(B, L // C),
        in_specs=[
            pl.BlockSpec((None, C, HK), lambda b, i: (b, i, 0)),
            pl.BlockSpec((None, C, HK), lambda b, i: (b, i, 1)),
            pl.BlockSpec((None, C, HV), lambda b, i: (b, i, 1)),
            pl.BlockSpec((None, C, HV), lambda b, i: (b, i, 2)),
            pl.BlockSpec((None, C, LANES), lambda b, i: (b, i, (2 * HK + 2 * HV) // LANES)),
            pl.BlockSpec((LANES, HK), lambda b, i: (0, 0)),
            pl.BlockSpec((1, HK), lambda b, i: (0, 0)),
            pl.BlockSpec((1, HV), lambda b, i: (0, 0)),
            pl.BlockSpec((None, B_HEADS, B_DK, B_DV), lambda b, i: (b, 0, 0, 0)),
        ],
        out_specs=[pl.BlockSpec((None, C, HV), lambda b, i: (b, i, 0)),
                   pl.BlockSpec((None, B_HEADS, B_DK, B_DV), lambda b, i: (b, 0, 0, 0))],
        scratch_shapes=[pltpu.VMEM((C, C), f32)],
        compiler_params=_params("parallel", "arbitrary"),
        name="gla_prompt",
    )(proj, proj, proj, proj, proj, w_gate2p, b_gate.reshape(1, HK), g_head.reshape(1, HV), s0)


def _cmp_mlp_tail(z, posb, w2_ref):
    n = z.shape[0]
    H = C_CMP_HIDDEN
    hid = z[:, :H] + pltpu.roll(z[:, H:], n - 1, 0) + posb
    return jnp.dot(jax.nn.gelu(hid).astype(bf16), w2_ref[...], preferred_element_type=f32)


def _cmp_pos_bias(p_ref, w1_ref):
    H = C_CMP_HIDDEN
    pz = jnp.dot(p_ref[...].astype(bf16), w1_ref[...], preferred_element_type=f32)
    return pz[0:1, :H] + pz[1:2, H:]


def _cmp_body(xk_ref, xv_ref, pk_ref, pv_ref, wk1_ref, wk2_ref, wv1_ref, wv2_ref, ok_ref, ov_ref):
    for x_ref, p_ref, w1_ref, w2_ref, o_ref in ((xk_ref, pk_ref, wk1_ref, wk2_ref, ok_ref),
                                                (xv_ref, pv_ref, wv1_ref, wv2_ref, ov_ref)):
        z = jnp.dot(x_ref[...].astype(bf16), w1_ref[...], preferred_element_type=f32)
        o_ref[...] = _cmp_mlp_tail(z, _cmp_pos_bias(p_ref, w1_ref), w2_ref).astype(o_ref.dtype)


def nsa_compress(xk, xv, pos_k, pos_v, wk1, wk2, wv1, wv2):
    B, G, n, W = xk.shape
    x_spec = pl.BlockSpec((None, None, n, W), lambda b, g: (b, g, 0, 0))
    full = lambda a: pl.BlockSpec(a.shape, lambda b, g: (0,) * a.ndim)
    o_spec = pl.BlockSpec((None, None, n, C_HEAD_DIM), lambda b, g: (b, g, 0, 0))
    consts = [pos_k, pos_v, wk1, wk2, wv1, wv2]
    return pl.pallas_call(
        _cmp_body,
        out_shape=[jax.ShapeDtypeStruct((B, G, n, C_HEAD_DIM), bf16)] * 2,
        grid=(B, G),
        in_specs=[x_spec, x_spec] + [full(a) for a in consts],
        out_specs=[o_spec, o_spec],
        compiler_params=_params("parallel", "parallel"),
        name="nsa_compress",
    )(xk, xv, *consts)


def _topk_mask_t(score_t, blk_t):
    keep = jnp.zeros(score_t.shape, f32)
    big = float(score_t.shape[0])
    for _ in range(C_TOPK):
        mx = jnp.max(score_t, axis=0, keepdims=True)
        first = jnp.min(jnp.where(score_t == mx, blk_t, big), axis=0, keepdims=True)
        hit = blk_t == first
        keep = jnp.where(hit, jnp.where(mx > 0.5 * NEG_INF, 1.0, 0.0), keep)
        score_t = jnp.where(hit, REMOVED, score_t)
    return keep


def _reduce_rows(x, op):
    slabs = [x[r:r + 8] for r in range(0, x.shape[0], 8)]
    while len(slabs) > 1:
        slabs = [op(slabs[k], slabs[k + 1]) if k + 1 < len(slabs) else slabs[k] for k in range(0, len(slabs), 2)]
    red = jnp.max if op is jnp.maximum else jnp.sum
    return red(slabs[0], axis=0, keepdims=True)


def _softmax_rows_t(s, mask=None):
    e = jnp.exp(s - _reduce_rows(s, jnp.maximum))
    if mask is not None:
        e = jnp.where(mask, e, 0.0)
    return e / jnp.maximum(_reduce_rows(e, jnp.add), TINY)


def _nsa_body(sp_ref, q_ref, gate_ref, bg_ref, kc_ref, vc_ref, ks_ref, vs_ref, kw_ref, vw_ref,
              o_ref, m_ref, l_ref, acc_ref, gs_ref, sa_ref, sb_ref, *, tk):
    g = pl.program_id(1)
    i = pl.program_id(2)
    tq = q_ref.shape[0]
    R = C_HPG * tq
    L = ks_ref.shape[0]
    n_cmp = kc_ref.shape[0]
    NS = LANES
    dh = C_HEAD_DIM
    q0 = i * tq

    q_t = jnp.transpose(q_ref[...] * dh ** -0.5)
    prow = lax.broadcasted_iota(jnp.int32, (LANES - dh, 1), 0)
    cols = []
    for hh in range(C_HPG):
        pc = jnp.zeros((LANES - dh, 1), f32)
        for n in reversed(range(N_PIECES)):
            pc = jnp.where(prow < 2 * n + 2, sp_ref[(g * C_HPG + hh) * N_PIECES + n], pc)
        cols.append(jnp.concatenate([q_t[hh * dh:(hh + 1) * dh], jnp.broadcast_to(pc, (LANES - dh, tq))], axis=0))
    qa_t = jnp.concatenate(cols, axis=1).astype(bf16)
    t1 = q0 + lax.broadcasted_iota(jnp.int32, (1, tq), 1)
    t = jnp.concatenate([t1] * C_HPG, axis=1)

    cpos = lax.broadcasted_iota(jnp.int32, (n_cmp, 1), 0) * C_CMP_STRIDE + (C_CMP_BLOCK - 1)
    mask = cpos <= t
    s = jnp.where(mask, jnp.dot(kc_ref[...], qa_t, preferred_element_type=f32), NEG_INF)
    p = _softmax_rows_t(s, mask)
    o_c = jnp.dot(vc_ref[...], p.astype(bf16), preferred_element_type=f32)

    psum = p[:, 0:tq]
    for hh in range(1, C_HPG):
        psum = psum + p[:, hh * tq:(hh + 1) * tq]
    n_blk = -(-(L // C_SEL_BLOCK) // 8) * 8
    jb = lax.broadcasted_iota(jnp.int32, (n_blk, n_cmp), 0)
    c_start = lax.broadcasted_iota(jnp.int32, (n_blk, n_cmp), 1) * C_CMP_STRIDE
    cover = jnp.where(c_start < (jb + 1) * C_SEL_BLOCK,
                      jnp.where(c_start + C_CMP_BLOCK > jb * C_SEL_BLOCK, 1.0, 0.0), 0.0).astype(bf16)
    imp = jnp.zeros((n_blk, tq), f32)
    rest = psum
    for _ in range(N_PIECES):
        piece = rest.astype(bf16)
        imp = imp + jnp.dot(cover, piece, preferred_element_type=f32)
        rest = rest - piece.astype(f32)
    blk = lax.broadcasted_iota(jnp.int32, (n_blk, tq), 0)
    cur = t1 >> C_SEL_SHIFT
    forced = (blk == 0) | (blk == cur) | (blk == cur - 1)
    score = jnp.where(blk * C_SEL_BLOCK <= t1, jnp.where(forced, FORCE, imp), NEG_INF)
    keep = _topk_mask_t(score, blk.astype(f32))
    neg1 = jnp.concatenate([jnp.where(keep > 0.5, jnp.where(blk == cur, NEG_INF, 0.0), NEG_INF),
                            jnp.full((NS - n_blk, tq), NEG_INF, f32)], axis=0).astype(bf16)
    qa2_t = jnp.concatenate([qa_t, jnp.concatenate([neg1] * C_HPG, axis=1)], axis=0)

    d0 = pl.multiple_of(q0, tq)
    kpos = q0 + lax.broadcasted_iota(jnp.int32, (tq, 1), 0)
    sc = jnp.dot(ks_ref[pl.ds(d0, tq), 0:LANES], qa_t, preferred_element_type=f32)
    sc = jnp.where(kpos <= t, jnp.where((kpos >> C_SEL_SHIFT) == (t >> C_SEL_SHIFT), sc, NEG_INF), NEG_INF)
    m0 = _reduce_rows(sc, jnp.maximum)
    ee = jnp.exp(sc - m0)
    m_ref[...] = m0
    l_ref[...] = _reduce_rows(ee, jnp.add)
    acc_ref[...] = jnp.dot(vs_ref[:, pl.ds(d0, tq)], ee.astype(bf16), preferred_element_type=f32)

    n_tiles = q0 // tk + 1

    def scores(kt, s_ref):
        k0 = pl.multiple_of(jnp.minimum(kt, n_tiles - 1) * tk, tk)
        s_ref[...] = jnp.dot(ks_ref[pl.ds(k0, tk), :], qa2_t, preferred_element_type=f32)

    def absorb(kt, s_ref):
        k0 = pl.multiple_of(kt * tk, tk)
        sc = s_ref[...]
        m_old = m_ref[...]
        m_new = jnp.maximum(m_old, _reduce_rows(sc, jnp.maximum))
        alpha = jnp.exp(m_old - m_new)
        ee = jnp.exp(sc - m_new)
        l_ref[...] = alpha * l_ref[...] + _reduce_rows(ee, jnp.add)
        acc_ref[...] = alpha * acc_ref[...] + jnp.dot(vs_ref[:, pl.ds(k0, tk)], ee.astype(bf16),
                                                      preferred_element_type=f32)
        m_ref[...] = m_new

    scores(0, sa_ref)

    def pair(j, carry):
        scores(2 * j + 1, sb_ref)
        absorb(2 * j, sa_ref)
        scores(2 * j + 2, sa_ref)
        absorb(2 * j + 1, sb_ref)
        return carry

    lax.fori_loop(0, n_tiles // 2, pair, 0)

    @pl.when(n_tiles % 2 == 1)
    def _():
        absorb(n_tiles - 1, sa_ref)

    o_s = acc_ref[...] / l_ref[...]

    W = min(L, C_WINDOW + tq)
    w0 = pl.multiple_of(jnp.clip(q0 - C_WINDOW, 0, L - W), tq)
    wpos = w0 + lax.broadcasted_iota(jnp.int32, (W, 1), 0)
    s = jnp.dot(kw_ref[pl.ds(w0, W), :], qa_t, preferred_element_type=f32)
    s = jnp.where(wpos <= t, jnp.where(wpos >= t - C_WINDOW, s, NEG_INF), NEG_INF)
    e = jnp.exp(s - _reduce_rows(s, jnp.maximum))
    o_w = jnp.dot(vw_ref[:, pl.ds(w0, W)], e.astype(bf16), preferred_element_type=f32) / _reduce_rows(e, jnp.add)

    gs_ref[...] = jnp.transpose(jax.nn.sigmoid(gate_ref[...] + bg_ref[...]))

    def gate(r):
        return jnp.concatenate([gs_ref[pl.ds((g * C_HPG + hh) * 3 + r, 1), :] for hh in range(C_HPG)], axis=1)

    o = gate(0) * o_c + gate(1) * o_s + gate(2) * o_w
    o = jnp.concatenate([o[:, hh * tq:(hh + 1) * tq] for hh in range(C_HPG)], axis=0)
    o_ref[...] = jnp.transpose(o).astype(o_ref.dtype)


def _aug_keys(k, pos, one_hot_blocks):
    B, G, n, dh = k.shape
    hi = (pos // C_SEL_BLOCK) * C_SEL_BLOCK
    cols = np.zeros((n, LANES - dh), np.float32)
    for j in range(N_PIECES):
        cols[:, 2 * j] = hi
        cols[:, 2 * j + 1] = pos - hi
    parts = [k, jnp.broadcast_to(jnp.asarray(cols, bf16), (B, G, n, LANES - dh))]
    if one_hot_blocks:
        oh = (pos[:, None] // C_SEL_BLOCK == np.arange(LANES)[None, :]).astype(np.float32)
        parts.append(jnp.broadcast_to(jnp.asarray(oh, bf16), (B, G, n, LANES)))
    return jnp.concatenate(parts, axis=-1)


def nsa_attention_prompt(proj, b_gate_p, kc, vc, ks, vs, kw, vw, *, tq=128, tk=512):
    B, L, _ = proj.shape
    G, dh = C_KV_HEADS, C_HEAD_DIM
    tq = min(tq, L)
    tk = min(tk, L)
    assert tk % tq == 0 and L % tk == 0 and L // C_SEL_BLOCK <= LANES
    R = C_HPG * tq
    n_cmp = kc.shape[2]
    kc_a = _aug_keys(kc, np.arange(n_cmp) * C_CMP_STRIDE + (C_CMP_BLOCK - 1), False)
    ks_a = _aug_keys(ks, np.arange(L), True)
    kw_a = _aug_keys(kw, np.arange(L), False)
    gate_blk = (C_HEADS * dh + 6 * G * dh) // LANES
    seq = lambda n, w: pl.BlockSpec((None, None, n, w), lambda b, g, i, sp: (b, g, 0, 0))
    tr = lambda a: jnp.swapaxes(a, 2, 3)
    return pl.pallas_call(
        functools.partial(_nsa_body, tk=tk),
        out_shape=jax.ShapeDtypeStruct((B, L, C_HEADS * dh), bf16),
        grid_spec=pltpu.PrefetchScalarGridSpec(
            num_scalar_prefetch=1,
            grid=(B, G, L // tq),
            in_specs=[
                pl.BlockSpec((None, tq, C_HPG * dh), lambda b, g, i, sp: (b, i, g)),
                pl.BlockSpec((None, tq, LANES), lambda b, g, i, sp: (b, i, gate_blk)),
                pl.BlockSpec((1, LANES), lambda b, g, i, sp: (0, 0)),
                seq(n_cmp, LANES), seq(dh, n_cmp), seq(L, 2 * LANES), seq(dh, L), seq(L, LANES), seq(dh, L),
            ],
            out_specs=pl.BlockSpec((None, tq, C_HPG * dh), lambda b, g, i, sp: (b, i, g)),
            scratch_shapes=[pltpu.VMEM((1, R), f32), pltpu.VMEM((1, R), f32), pltpu.VMEM((dh, R), f32),
                            pltpu.VMEM((LANES, tq), f32), pltpu.VMEM((tk, R), f32), pltpu.VMEM((tk, R), f32)],
        ),
        compiler_params=_params("parallel", "parallel", "parallel"),
        name="nsa_attention_prompt",
    )(jnp.asarray(SLOPES_C_PIECES), proj, proj, b_gate_p, kc_a, tr(vc), ks_a, tr(vs), kw_a, tr(vw))


def _slot_rows(ref, off, n, w):
    rows = jnp.concatenate([ref[:, off + s * w:off + (s + 1) * w] for s in range(n)]
                           + ([jnp.zeros((8 - n, w), f32)] if n < 8 else []), axis=0)
    return jnp.concatenate([rows, jnp.zeros((8, LANES - w), f32)], axis=-1) if w < LANES else rows


def _dil_sample_body(sl_ref, qkv_ref, c0_ref, c1_ref, c2_ref, o_ref):
    dh = A_HEAD_DIM
    scale = dh ** -0.5
    sect = A_GROUPS * A_OUT
    scores, news, v_cols = [], [], []
    for grp, c_ref in enumerate((c0_ref, c1_ref, c2_ref)):
        W = c_ref.shape[-1]
        d = A_DILATIONS[grp]
        q8 = _slot_rows(qkv_ref, grp * A_OUT, A_SLOTS, dh)
        kn8 = _slot_rows(qkv_ref, sect + grp * A_OUT, A_SLOTS, dh)
        vn8 = _slot_rows(qkv_ref, 2 * sect + grp * A_OUT, A_SLOTS, dh)
        q_cols = jnp.transpose(q8)[:dh]
        v_cols.append(jnp.transpose(vn8)[:dh])
        news.append(jnp.sum(q8 * kn8, axis=-1, keepdims=True) * scale)
        pos = lax.broadcasted_iota(jnp.int32, (1, W), 1)
        rows = [jnp.sum(c_ref[0, s] * q_cols[:, s:s + 1], axis=0, keepdims=True) for s in range(A_SLOTS)]
        slope = jnp.concatenate([jnp.full((1, 1), sl_ref[grp * A_SLOTS + s], f32) for s in range(A_SLOTS)], axis=0)
        sc = jnp.concatenate(rows, axis=0) * scale - slope * (W - pos).astype(f32)
        scores.append(jnp.where((pos & (d - 1)) == 0, sc, NEG_INF))
    m = functools.reduce(jnp.maximum, [jnp.max(s, axis=-1, keepdims=True) for s in scores] + news)
    den = jnp.zeros((A_SLOTS, 1), f32)
    acc = [jnp.zeros((dh, 1), f32) for _ in range(A_SLOTS)]
    for grp, c_ref in enumerate((c0_ref, c1_ref, c2_ref)):
        e = jnp.exp(scores[grp] - m)
        en = jnp.exp(news[grp] - m)
        den = den + jnp.sum(e, axis=-1, keepdims=True) + en
        for s in range(A_SLOTS):
            acc[s] = acc[s] + jnp.sum(c_ref[1, s] * e[s:s + 1, :], axis=-1, keepdims=True) \
                + en[s:s + 1, :] * v_cols[grp][:, s:s + 1]
    o_ref[...] = jnp.concatenate([acc[s] / den[s:s + 1, :] for s in range(A_SLOTS)], axis=-1)


def dil_sample(qkv, caches_t, layer):
    Bs = qkv.shape[0]
    for grp, c in enumerate(caches_t):
        assert c.shape[-1] == A_WINDOWS[grp]
    c_spec = lambda c: pl.BlockSpec((None, None) + c.shape[2:], lambda b, sl: (layer, b, 0, 0, 0, 0))
    return pl.pallas_call(
        _dil_sample_body,
        out_shape=jax.ShapeDtypeStruct((Bs, A_HEAD_DIM, A_SLOTS), f32),
        grid_spec=pltpu.PrefetchScalarGridSpec(
            num_scalar_prefetch=1,
            grid=(Bs,),
            in_specs=[pl.BlockSpec((None, 1, qkv.shape[-1]), lambda b, sl: (b, 0, 0))]
            + [c_spec(c) for c in caches_t],
            out_specs=pl.BlockSpec((None, A_HEAD_DIM, A_SLOTS), lambda b, sl: (b, 0, 0)),
        ),
        compiler_params=_params("parallel"),
        name="dil_sample",
    )(jnp.asarray(SLOPES_A.reshape(-1)), qkv, *caches_t)


def _xattn_sample_body(q_ref, c_ref, o_ref):
    dh = q_ref.shape[-1] // MEM_HEADS
    outs = []
    for h in range(MEM_HEADS):
        qh = q_ref[:, h * dh:(h + 1) * dh]
        s = jnp.sum(c_ref[:, 0, h, :] * qh, axis=-1, keepdims=True) * dh ** -0.5
        e = jnp.exp(s - jnp.max(s, axis=0, keepdims=True))
        outs.append(jnp.sum(e * c_ref[:, 1, h, :], axis=0, keepdims=True) / jnp.sum(e, axis=0, keepdims=True))
    o_ref[...] = jnp.concatenate(outs, axis=-1)


def xattn_sample(q, cache, layer):
    Bs, _, D = q.shape
    return pl.pallas_call(
        _xattn_sample_body,
        out_shape=jax.ShapeDtypeStruct((Bs, 1, D), f32),
        grid=(Bs,),
        in_specs=[pl.BlockSpec((None, 1, D), lambda b: (b, 0, 0)),
                  pl.BlockSpec((None, None) + cache.shape[2:], lambda b: (layer, b, 0, 0, 0, 0))],
        out_specs=pl.BlockSpec((None, 1, D), lambda b: (b, 0, 0)),
        compiler_params=_params("parallel"),
        name="xattn_sample",
    )(q, cache)


def _gla_step_body(q_ref, k_ref, v_ref, r_ref, gl_ref, w2_ref, bg_ref, gh_ref, s0_ref, y_ref, s_ref):
    z = _hdot(gl_ref[...], w2_ref[...]) + bg_ref[...]
    a = jnp.exp(_log_sigmoid(z) / B_GATE_TAU)
    pad = jnp.zeros((5, B_DK), f32)
    for h in range(B_HEADS):
        ks_ = slice(h * B_DK, (h + 1) * B_DK)
        vs_ = slice(h * B_DV, (h + 1) * B_DV)
        rows = jnp.concatenate([a[:, ks_], k_ref[:, ks_], q_ref[:, ks_] * B_DK ** -0.5, pad], axis=0)
        cols = jnp.transpose(rows)
        S = cols[:, 0:1] * s0_ref[h] + cols[:, 1:2] * v_ref[:, vs_]
        s_ref[h] = S
        o = jnp.sum(cols[:, 2:3] * S, axis=0, keepdims=True)
        rh = r_ref[:, vs_]
        y_ref[:, vs_] = (_rms(o, gh_ref[:, vs_]) * (rh * jax.nn.sigmoid(rh))).astype(y_ref.dtype)


def gla_step(proj, w_gate2p, b_gate, g_head, state, layer):
    Bs = proj.shape[0]
    HK, HV = B_HEADS * B_DK, B_HEADS * B_DV
    return pl.pallas_call(
        _gla_step_body,
        out_shape=[jax.ShapeDtypeStruct((Bs, 1, HV), bf16),
                   jax.ShapeDtypeStruct((Bs, B_HEADS, B_DK, B_DV), f32)],
        grid=(Bs,),
        in_specs=[
            pl.BlockSpec((None, 1, HK), lambda b: (b, 0, 0)),
            pl.BlockSpec((None, 1, HK), lambda b: (b, 0, 1)),
            pl.BlockSpec((None, 1, HV), lambda b: (b, 0, 1)),
            pl.BlockSpec((None, 1, HV), lambda b: (b, 0, 2)),
            pl.BlockSpec((None, 1, LANES), lambda b: (b, 0, (2 * HK + 2 * HV) // LANES)),
            pl.BlockSpec((LANES, HK), lambda b: (0, 0)),
            pl.BlockSpec((1, HK), lambda b: (0, 0)),
            pl.BlockSpec((1, HV), lambda b: (0, 0)),
            pl.BlockSpec((None, None, B_HEADS, B_DK, B_DV), lambda b: (layer, b, 0, 0, 0)),
        ],
        out_specs=[pl.BlockSpec((None, 1, HV), lambda b: (b, 0, 0)),
                   pl.BlockSpec((None, B_HEADS, B_DK, B_DV), lambda b: (b, 0, 0, 0))],
        compiler_params=_params("parallel"),
        name="gla_step",
    )(proj, proj, proj, proj, proj, w_gate2p, b_gate.reshape(1, HK), g_head.reshape(1, HV), state)


PAGES_PER_STEP = 8


def _cmp_sample_body(pt_ref, *refs):
    pages = refs[:PAGES_PER_STEP]
    (pk_ref, pv_ref, wk1_ref, wk2_ref, wv1_ref, wv2_ref, wkp_ref, wvp_ref,
     ok_ref, ov_ref, x_ref) = refs[PAGES_PER_STEP:]
    i = pl.program_id(1)
    H = C_CMP_HIDDEN
    n_tiles = x_ref.shape[0]
    per_kind = n_tiles // 2
    eye = (lax.broadcasted_iota(jnp.int32, (LANES, LANES), 0)
           == lax.broadcasted_iota(jnp.int32, (LANES, LANES), 1)).astype(bf16)
    for k in range(PAGES_PER_STEP):
        r0 = pl.multiple_of((i * PAGES_PER_STEP + k) * PAGE_SIZE, PAGE_SIZE)
        for kind in range(2):
            for tile in range(per_kind):
                xt = pages[k][kind, 2 * tile:2 * tile + 2].reshape(LANES, PAGE_SIZE).astype(bf16)
                x_ref[kind * per_kind + tile, pl.ds(r0, PAGE_SIZE), :] = _dot_t(eye, xt)

    @pl.when(i == pl.num_programs(1) - 1)
    def _():
        n = x_ref.shape[1] // C_CMP_STRIDE
        for kind, (p_ref, w1_ref, w2_ref, wp_ref, o_ref) in enumerate(
                ((pk_ref, wk1_ref, wk2_ref, wkp_ref, ok_ref), (pv_ref, wv1_ref, wv2_ref, wvp_ref, ov_ref))):
            posb = _cmp_pos_bias(p_ref, w1_ref)
            outs = []
            for tile in range(per_kind):
                ct = kind * per_kind + tile
                z2 = jnp.zeros((n, 4 * H), f32)
                for p in range(C_CMP_STRIDE):
                    xp = x_ref[ct, pl.ds(p, n, stride=C_CMP_STRIDE), :].astype(bf16)
                    z2 = z2 + jnp.dot(xp, wp_ref[p], preferred_element_type=f32)
                outs += [_cmp_mlp_tail(z, posb, w2_ref) for z in (z2[:, :2 * H], z2[:, 2 * H:])]
            o_ref[...] = jnp.concatenate(outs, axis=-1)


def _pair_block_diag(w1):
    dh = C_HEAD_DIM
    w = w1.reshape(C_CMP_STRIDE, dh, w1.shape[1])
    z = jnp.zeros_like(w)
    return jnp.concatenate([jnp.concatenate([w, z], axis=-1), jnp.concatenate([z, w], axis=-1)], axis=1)


def nsa_compress_sample(page_table, cache_t, layer, pos_k, pos_v, wk1, wk2, wv1, wv2):
    Bs, n_pages = page_table.shape
    G, dh = C_KV_HEADS, C_HEAD_DIM
    assert n_pages % PAGES_PER_STEP == 0 and cache_t.shape[2:] == (4, G, dh, PAGE_SIZE) and 2 * dh == LANES
    n_rows = n_pages * PAGE_SIZE
    n = n_rows // C_CMP_STRIDE
    page = lambda k: pl.BlockSpec((None, None, 2, G, dh, PAGE_SIZE),
                                  lambda b, i, pt, k=k: (layer, pt[b, i * PAGES_PER_STEP + k], 0, 0, 0, 0))
    consts = [pos_k, pos_v, wk1, wk2, wv1, wv2, _pair_block_diag(wk1), _pair_block_diag(wv1)]
    full = lambda a: pl.BlockSpec(a.shape, lambda b, i, pt: (0,) * a.ndim)
    o_spec = pl.BlockSpec((None, n, G * dh), lambda b, i, pt: (b, 0, 0))
    return pl.pallas_call(
        _cmp_sample_body,
        out_shape=[jax.ShapeDtypeStruct((Bs, n, G * dh), f32)] * 2,
        grid_spec=pltpu.PrefetchScalarGridSpec(
            num_scalar_prefetch=1,
            grid=(Bs, n_pages // PAGES_PER_STEP),
            in_specs=[page(k) for k in range(PAGES_PER_STEP)] + [full(a) for a in consts],
            out_specs=[o_spec, o_spec],
            scratch_shapes=[pltpu.VMEM((2 * G * dh // LANES, n_rows, LANES), f32)],
        ),
        compiler_params=_params("parallel", "arbitrary"),
        name="nsa_compress_sample",
    )(page_table, *([cache_t] * PAGES_PER_STEP), *consts)


def _cmp_topk_sample_body(sl_ref, q_ref, kc_ref, vc_ref, oc_ref, idx_ref, *, t):
    G, dh = C_KV_HEADS, C_HEAD_DIM
    n = kc_ref.shape[0]
    n_slc = t // C_SEL_BLOCK + 1
    NS = -(-n_slc // LANES) * LANES
    cpos = lax.broadcasted_iota(jnp.int32, (1, n), 1) * C_CMP_STRIDE + (C_CMP_BLOCK - 1)
    mask = cpos <= t
    c_start = lax.broadcasted_iota(jnp.int32, (n, NS), 0) * C_CMP_STRIDE
    jb = lax.broadcasted_iota(jnp.int32, (n, NS), 1)
    cover = ((c_start < (jb + 1) * C_SEL_BLOCK) & (c_start + C_CMP_BLOCK > jb * C_SEL_BLOCK)).astype(f32)
    blk = lax.broadcasted_iota(jnp.int32, (1, NS), 1)
    cur = t // C_SEL_BLOCK
    forced = (blk == 0) | (blk == cur) | (blk == cur - 1)
    lane = lax.broadcasted_iota(jnp.int32, (1, LANES), 1)
    o_parts, idx_rows = [], []
    for g in range(G):
        q8 = jnp.concatenate([q_ref[:, (g * C_HPG + hh) * dh:(g * C_HPG + hh + 1) * dh] for hh in range(C_HPG)]
                             + [jnp.zeros((8 - C_HPG, dh), f32)], axis=0)
        slope = jnp.concatenate([jnp.full((1, 1), sl_ref[g * C_HPG + hh], f32) for hh in range(C_HPG)]
                                + [jnp.zeros((8 - C_HPG, 1), f32)], axis=0)
        s = _dot_t(q8, kc_ref[:, g * dh:(g + 1) * dh], HIGHEST) * dh ** -0.5
        s = s - slope * (float(t) - cpos.astype(f32))
        s = jnp.where(mask, s, NEG_INF)
        m = jnp.max(s, axis=-1, keepdims=True)
        e = jnp.where(mask, jnp.exp(s - m), 0.0)
        p = e / jnp.maximum(jnp.sum(e, axis=-1, keepdims=True), TINY)
        o = _hdot(p, vc_ref[:, g * dh:(g + 1) * dh])
        o_parts += [o[hh:hh + 1] for hh in range(C_HPG)]
        imp = _hdot(jnp.sum(p[0:C_HPG], axis=0, keepdims=True), cover)
        score = jnp.where(blk * C_SEL_BLOCK <= t, jnp.where(forced, FORCE, imp), NEG_INF)
        row = jnp.zeros((1, LANES), jnp.int32)
        for r in range(C_TOPK):
            mx = jnp.max(score, axis=-1, keepdims=True)
            pick = jnp.min(jnp.where(score == mx, blk, NS), axis=-1, keepdims=True)
            row = jnp.where(lane == r, pick, row)
            score = jnp.where(blk == pick, REMOVED, score)
        idx_rows.append(row)
    oc_ref[...] = jnp.concatenate(o_parts, axis=-1)
    idx_ref[...] = jnp.concatenate(idx_rows + [jnp.zeros((8 - G, LANES), jnp.int32)], axis=0)


def nsa_cmp_topk_sample(proj, kc, vc, t):
    Bs = proj.shape[0]
    n, W = kc.shape[1:]
    HD = C_HEADS * C_HEAD_DIM
    return pl.pallas_call(
        functools.partial(_cmp_topk_sample_body, t=t),
        out_shape=[jax.ShapeDtypeStruct((Bs, 1, HD), f32), jax.ShapeDtypeStruct((Bs, 8, LANES), jnp.int32)],
        grid_spec=pltpu.PrefetchScalarGridSpec(
            num_scalar_prefetch=1,
            grid=(Bs,),
            in_specs=[pl.BlockSpec((None, 1, HD), lambda b, sl: (b, 0, 0)),
                      pl.BlockSpec((None, n, W), lambda b, sl: (b, 0, 0)),
                      pl.BlockSpec((None, n, W), lambda b, sl: (b, 0, 0))],
            out_specs=[pl.BlockSpec((None, 1, HD), lambda b, sl: (b, 0, 0)),
                       pl.BlockSpec((None, 8, LANES), lambda b, sl: (b, 0, 0))],
        ),
        compiler_params=_params("parallel"),
        name="nsa_cmp_topk_sample",
    )(jnp.asarray(SLOPES_C), proj, kc, vc)


def _sel_win_sample_body(pt_ref, ix_ref, sl_ref, *refs, t):
    blocks = refs[:C_TOPK]
    (win_ref, q_ref, ksn_ref, vsn_ref, kwn_ref, vwn_ref, gate_ref, bg_ref, oc_ref, o_ref) = refs[C_TOPK:]
    b = pl.program_id(0)
    g = pl.program_id(1)
    G, dh = C_KV_HEADS, C_HEAD_DIM
    scale = dh ** -0.5
    n_past = t // C_SEL_BLOCK
    per_page = PAGE_SIZE // C_SEL_BLOCK
    q8 = jnp.concatenate([q_ref[:, hh * dh:(hh + 1) * dh] for hh in range(C_HPG)]
                         + [jnp.zeros((8 - C_HPG, dh), f32)], axis=0)
    slope = jnp.concatenate([jnp.full((1, 1), sl_ref[g * C_HPG + hh], f32) for hh in range(C_HPG)]
                            + [jnp.zeros((8 - C_HPG, 1), f32)], axis=0)

    def own(ref):
        return functools.reduce(jnp.add, [jnp.where(g == gg, ref[:, gg * dh:(gg + 1) * dh], 0.0) for gg in range(G)])

    def softmax_pv(scores, values_t, s_new, v_new):
        m = functools.reduce(jnp.maximum, [jnp.max(s, axis=-1, keepdims=True) for s in scores] + [s_new])
        e_new = jnp.exp(s_new - m)
        den = e_new
        acc = e_new * v_new
        for s, vt in zip(scores, values_t):
            e = jnp.exp(s - m)
            den = den + jnp.sum(e, axis=-1, keepdims=True)
            acc = acc + _dot_t(e, vt, HIGHEST)
        return acc / den

    lane = lax.broadcasted_iota(jnp.int32, (1, PAGE_SIZE), 1)
    scores, values_t = [], []
    for s_i in range(C_TOPK):
        j = ix_ref[b, g * C_TOPK + s_i]
        jc = jnp.minimum(j, n_past - 1)
        kpos = (jc // per_page) * PAGE_SIZE + lane
        sc = _hdot(q8, blocks[s_i][0]) * scale - slope * (t - kpos).astype(f32)
        ok = ((lane >> C_SEL_SHIFT) == (jc % per_page)) & (j < n_past)
        scores.append(jnp.where(ok, sc, NEG_INF))
        values_t.append(blocks[s_i][1])
    s_new = jnp.sum(q8 * own(ksn_ref), axis=-1, keepdims=True) * scale
    o_s = softmax_pv(scores, values_t, s_new, own(vsn_ref))

    nw = win_ref.shape[-1]
    wdist = (nw - lax.broadcasted_iota(jnp.int32, (1, nw), 1)).astype(f32)
    sw = _hdot(q8, win_ref[0]) * scale - slope * wdist
    s_new = jnp.sum(q8 * own(kwn_ref), axis=-1, keepdims=True) * scale
    o_w = softmax_pv([sw], [win_ref[1]], s_new, own(vwn_ref))

    gs = jax.nn.sigmoid(gate_ref[...] + bg_ref[...])
    glane = lax.broadcasted_iota(jnp.int32, (1, LANES), 1)
    outs = []
    for hh in range(C_HPG):
        gate = [jnp.sum(jnp.where(glane == (g * C_HPG + hh) * 3 + r, gs, 0.0), axis=-1, keepdims=True)
                for r in range(3)]
        outs.append(gate[0] * oc_ref[:, hh * dh:(hh + 1) * dh] + gate[1] * o_s[hh:hh + 1] + gate[2] * o_w[hh:hh + 1])
    o_ref[...] = jnp.concatenate(outs, axis=-1).astype(o_ref.dtype)


def nsa_sel_win_sample(page_table, sel_idx, cache_t, win_t, layer, proj, b_gate_p, o_c, t):
    Bs = proj.shape[0]
    G, dh = C_KV_HEADS, C_HEAD_DIM
    GD = G * dh
    assert win_t.shape[-1] == C_WINDOW and t % C_SEL_BLOCK == 0
    n_past = t // C_SEL_BLOCK
    per_page = PAGE_SIZE // C_SEL_BLOCK

    def blk_spec(s_i):
        def imap(b, g, pt, ix, sl):
            j = jnp.minimum(ix[b, g * C_TOPK + s_i], n_past - 1)
            return (layer, pt[b, j // per_page], 1, g, 0, 0)
        return pl.BlockSpec((None, None, 2, None, dh, PAGE_SIZE), imap)

    row = lambda w, blk: pl.BlockSpec((None, 1, w), lambda b, g, pt, ix, sl, blk=blk: (b, 0, blk))
    base = C_HEADS * dh // GD
    gate_blk = (C_HEADS * dh + 6 * GD) // LANES
    return pl.pallas_call(
        functools.partial(_sel_win_sample_body, t=t),
        out_shape=jax.ShapeDtypeStruct((Bs, 1, C_HEADS * dh), bf16),
        grid_spec=pltpu.PrefetchScalarGridSpec(
            num_scalar_prefetch=3,
            grid=(Bs, G),
            in_specs=[blk_spec(s_i) for s_i in range(C_TOPK)] + [
                pl.BlockSpec((None, None, 2, None, dh, C_WINDOW), lambda b, g, pt, ix, sl: (layer, b, 0, g, 0, 0)),
                pl.BlockSpec((None, 1, C_HPG * dh), lambda b, g, pt, ix, sl: (b, 0, g)),
                row(GD, base + 2), row(GD, base + 3), row(GD, base + 4), row(GD, base + 5),
                row(LANES, gate_blk),
                pl.BlockSpec((1, LANES), lambda b, g, pt, ix, sl: (0, 0)),
                pl.BlockSpec((None, 1, C_HPG * dh), lambda b, g, pt, ix, sl: (b, 0, g)),
            ],
            out_specs=pl.BlockSpec((None, 1, C_HPG * dh), lambda b, g, pt, ix, sl: (b, 0, g)),
        ),
        compiler_params=_params("parallel", "parallel"),
        name="nsa_sel_win_sample",
    )(page_table, sel_idx, jnp.asarray(SLOPES_C), *([cache_t] * C_TOPK), win_t, proj, proj, proj, proj, proj,
      proj, b_gate_p, o_c)


def _norm_body(x_ref, g_ref, o_ref):
    o_ref[...] = _rms(x_ref[...], g_ref[...])


def rmsnorm_rows(x, g, *, tm=1024):
    M, D = x.shape
    tm = min(tm, M)
    return pl.pallas_call(
        _norm_body,
        out_shape=jax.ShapeDtypeStruct((M, D), f32),
        grid=(M // tm,),
        in_specs=[pl.BlockSpec((tm, D), lambda i: (i, 0)), pl.BlockSpec((1, D), lambda i: (0, 0))],
        out_specs=pl.BlockSpec((tm, D), lambda i: (i, 0)),
        compiler_params=_params("parallel"),
        name="rmsnorm",
    )(x, g.reshape(1, D))


def _position_minor(cache):
    return jnp.transpose(cache, (0, 1, 3, 4, 5, 2))


def kernel(x_prompt, x_sample, cache_dil_w128, cache_dil_w512, cache_dil_w2048, state_gla, cache_nsa_win,
           cache_nsa_kv, cache_mem_kv, page_table, mem_prompt, g_mix, g_cross, g_mem, g_ffn, g_final,
           w_a_qkv, w_a_o, w_b_in, w_b_gate2, b_b_gate, g_b_head, w_b_o, w_c_in, b_c_gate, c_pos_k, c_pos_v,
           w_c_k1, w_c_k2, w_c_v1, w_c_v2, w_c_o, w_x_q, w_x_kv, w_x_o, w_ffn_in, w_ffn_out):
    Bp, L, D = x_prompt.shape
    Bs = x_sample.shape[0]
    depth = g_mix.shape[0]
    n_pages = page_table.shape[1]
    t_s = n_pages * PAGE_SIZE
    G, dh = C_KV_HEADS, C_HEAD_DIM
    GD = G * dh
    HK, HV = B_HEADS * B_DK, B_HEADS * B_DV
    cast = lambda a: a.astype(bf16)

    wa_qkv, wa_o = cast(w_a_qkv), cast(w_a_o)
    nb = 2 * HK + HV
    wb_in = cast(jnp.concatenate(
        [w_b_in[..., :nb], w_b_in[..., nb + B_GATE_RANK:], w_b_in[..., nb:nb + B_GATE_RANK],
         jnp.zeros(w_b_in.shape[:2] + (LANES - B_GATE_RANK,), f32)], axis=-1))
    wb_gate2 = jnp.pad(w_b_gate2, ((0, 0), (0, LANES - B_GATE_RANK), (0, 0)))
    wb_o = cast(w_b_o)
    nc = -(-w_c_in.shape[-1] // LANES) * LANES
    wc_in = cast(jnp.pad(w_c_in, ((0, 0), (0, 0), (0, nc - w_c_in.shape[-1]))))
    bc_gate = jnp.pad(b_c_gate, ((0, 0), (0, LANES - b_c_gate.shape[-1])))
    half = C_CMP_STRIDE * dh
    two_chunk = lambda w: cast(jnp.concatenate([w[:, :half], w[:, half:]], axis=-1))
    wc_k1, wc_v1 = two_chunk(w_c_k1), two_chunk(w_c_v1)
    wc_k2, wc_v2, wc_o = cast(w_c_k2), cast(w_c_v2), cast(w_c_o)
    pos_k = c_pos_k.reshape(-1, 2, half)
    pos_v = c_pos_v.reshape(-1, 2, half)
    wx_q, wx_kv, wx_o = cast(w_x_q), cast(w_x_kv), cast(w_x_o)
    wf_in, wf_out = cast(w_ffn_in), cast(w_ffn_out)

    Mp = Bp * L
    x = x_prompt.reshape(Mp, D)
    mem2d = mem_prompt.reshape(-1, D)
    dil_p = [[] for _ in range(A_GROUPS)]
    gla_p, rows_p, win_p, mem_p = [], [], [], []
    for l in range(depth):
        kind, j = l % N_MIXERS, l // N_MIXERS
        if kind == 0:
            qkv = mm(x, wa_qkv[j], g=g_mix[l]).reshape(Bp, L, -1)
            outs = [dilated_group(qkv, grp) for grp in range(A_GROUPS)]
            x = dilated_merge_out([o.reshape(Mp, A_OUT) for o, _ in outs],
                                  [s.reshape(Mp, A_OUT) for _, s in outs], wa_o[j], x)
            sect = A_GROUPS * A_OUT
            for grp in range(A_GROUPS):
                w = min(A_WINDOWS[grp], L)
                tail = lambda t: qkv[:, L - w:, t * sect + grp * A_OUT:t * sect + (grp + 1) * A_OUT].reshape(
                    Bp, w, A_SLOTS, A_HEAD_DIM)
                dil_p[grp].append(jnp.stack([tail(1), tail(2)], axis=2))
        elif kind == 1:
            proj = mm(x, wb_in[j], g=g_mix[l]).reshape(Bp, L, -1)
            y, S = gla_prompt(proj, wb_gate2[j], b_b_gate[j], g_b_head[j].reshape(-1),
                              jnp.zeros((Bp, B_HEADS, B_DK, B_DV), f32))
            gla_p.append(S)
            x = mm(y.reshape(Mp, HV), wb_o[j], res=x)
        else:
            proj = mm(x, wc_in[j], g=g_mix[l]).reshape(Bp, L, -1)
            q_w = C_HEADS * dh
            rows = proj[..., q_w:q_w + 6 * GD].reshape(Bp, L, 6, G, dh)
            chunks = lambda a: a.reshape(Bp, L // C_CMP_STRIDE, C_CMP_STRIDE, G, dh).transpose(
                0, 3, 1, 2, 4).reshape(Bp, G, L // C_CMP_STRIDE, half)
            kc, vc = nsa_compress(chunks(rows[:, :, 0]), chunks(rows[:, :, 1]), pos_k[j], pos_v[j],
                                  wc_k1[j], wc_k2[j], wc_v1[j], wc_v2[j])
            seqs = [cast(rows[:, :, r].transpose(0, 2, 1, 3)) for r in range(2, 6)]
            o = nsa_attention_prompt(proj, bc_gate[j:j + 1], kc, vc, *seqs)
            x = mm(o.reshape(Mp, q_w), wc_o[j], res=x)
            rows_p.append(rows[:, :, :4])
            win_p.append(rows[:, L - min(C_WINDOW, L):, 4:])
        mem_kv = mm(mem2d, wx_kv[l], g=g_mem[l]).reshape(Bp, -1, 2 * D)
        mem_p.append(mem_kv.reshape(Bp, -1, 2, MEM_HEADS, D // MEM_HEADS))
        x = xattn_prompt(x.reshape(Bp, L, D), g_cross[l], wx_q[l], cast(mem_kv), wx_o[l]).reshape(Mp, D)
        x = ffn(x, g_ffn[l], wf_in[l], wf_out[l])
    y_prompt = rmsnorm_rows(x, g_final).reshape(Bp, L, D)

    x = x_sample.reshape(Bs, D)
    dil_t = [_position_minor(c) for c in (cache_dil_w128, cache_dil_w512, cache_dil_w2048)]
    nsa_t = _position_minor(cache_nsa_kv)
    win_t = _position_minor(cache_nsa_win)
    dil_s = [[] for _ in range(A_GROUPS)]
    gla_s, rows_s, win_s = [], [], []
    for l in range(depth):
        kind, j = l % N_MIXERS, l // N_MIXERS
        if kind == 0:
            qkv = mm(x, wa_qkv[j], g=g_mix[l]).reshape(Bs, 1, -1)
            o = dil_sample(qkv, dil_t, j)
            x = mm(o.transpose(0, 2, 1).reshape(Bs, A_OUT), wa_o[j], res=x)
            kv = qkv.reshape(Bs, 1, 3, A_GROUPS, A_SLOTS, A_HEAD_DIM)
            for grp in range(A_GROUPS):
                dil_s[grp].append(jnp.stack([kv[:, :, 1, grp], kv[:, :, 2, grp]], axis=2))
        elif kind == 1:
            proj = mm(x, wb_in[j], g=g_mix[l]).reshape(Bs, 1, -1)
            y, S = gla_step(proj, wb_gate2[j], b_b_gate[j], g_b_head[j].reshape(-1), state_gla, j)
            gla_s.append(S)
            x = mm(y.reshape(Bs, HV), wb_o[j], res=x)
        else:
            proj = mm(x, wc_in[j], g=g_mix[l]).reshape(Bs, 1, -1)
            q_w = C_HEADS * dh
            kc, vc = nsa_compress_sample(page_table, nsa_t, j, pos_k[j], pos_v[j],
                                         wc_k1[j], wc_k2[j], wc_v1[j], wc_v2[j])
            o_c, idx = nsa_cmp_topk_sample(proj, kc, vc, t_s)
            sel_idx = idx[:, :G, :C_TOPK].reshape(Bs, G * C_TOPK)
            o = nsa_sel_win_sample(page_table, sel_idx, nsa_t, win_t, j, proj, bc_gate[j:j + 1], o_c, t_s)
            x = mm(o.reshape(Bs, q_w), wc_o[j], res=x)
            rows = proj[..., q_w:q_w + 6 * GD].reshape(Bs, 1, 6, G, dh)
            rows_s.append(rows[:, :, :4])
            win_s.append(rows[:, :, 4:])
        q = mm(x, wx_q[l], g=g_cross[l]).reshape(Bs, 1, D)
        o = xattn_sample(q, cache_mem_kv, l)
        x = mm(o.reshape(Bs, D), wx_o[l], res=x)
        x = ffn(x, g_ffn[l], wf_in[l], wf_out[l])
    y_sample = rmsnorm_rows(x, g_final).reshape(Bs, 1, D)

    st = jnp.stack
    return (y_prompt, y_sample, st(dil_p[0]), st(dil_s[0]), st(dil_p[1]), st(dil_s[1]), st(dil_p[2]), st(dil_s[2]),
            st(gla_p), st(gla_s), st(win_p), st(win_s), st(rows_p), st(rows_s), st(mem_p))
```

```python
import functools

import jax
import jax.numpy as jnp
import numpy as np
from jax import lax
from jax.experimental import pallas as pl
from jax.experimental.pallas import tpu as pltpu

f32 = jnp.float32
bf16 = jnp.bfloat16

N_MIXERS = 3
A_WINDOWS = (128, 512, 2048)
A_DILATIONS = (1, 4, 16)
A_GROUPS = 3
A_SLOTS = 8
A_HEAD_DIM = 64
A_OUT = A_SLOTS * A_HEAD_DIM
A_WIN_STEPS = 128

B_HEADS = 4
B_DK = 128
B_DV = 256
B_GATE_RANK = 16
B_GATE_TAU = 16.0

C_HEADS = 16
C_KV_HEADS = 4
C_HPG = C_HEADS // C_KV_HEADS
C_HEAD_DIM = 64
C_CMP_BLOCK = 32
C_CMP_STRIDE = 16
C_SEL_BLOCK = 64
C_SEL_SHIFT = 6
C_TOPK = 16
C_WINDOW = 512
C_CMP_HIDDEN = 128

MEM_HEADS = 4
PAGE_SIZE = 128

RMS_EPS = 1e-6
NEG_INF = -1e30
FORCE = 1e30
TINY = 1e-30
REMOVED = -3e38

LANES = 128
VMEM_LIMIT_BYTES = 56 * 1024 * 1024
HIGHEST = lax.Precision.HIGHEST


def _alibi_slopes(n):
    return np.asarray(2.0 ** (-8.0 * np.arange(1, n + 1) / n), dtype=np.float32)


def _bf16_pieces(x, n):
    out, rest = [], np.asarray(x, np.float32)
    for _ in range(n):
        p = rest.astype(bf16).astype(np.float32)
        out.append(p)
        rest = (rest - p).astype(np.float32)
    return np.stack(out, axis=-1)


SLOPES_A = _alibi_slopes(A_GROUPS * A_SLOTS).reshape(A_GROUPS, A_SLOTS)
SLOPES_C = _alibi_slopes(C_HEADS)
N_PIECES = 3
SLOPES_C_PIECES = _bf16_pieces(SLOPES_C, N_PIECES).reshape(-1)
AUG = 2 * N_PIECES


def _params(*sem):
    return pltpu.CompilerParams(dimension_semantics=sem, vmem_limit_bytes=VMEM_LIMIT_BYTES)


def _rms(x, g):
    return x * lax.rsqrt(jnp.mean(x * x, axis=-1, keepdims=True) + RMS_EPS) * g


def _dot_t(a, b, precision=None):
    return lax.dot_general(a, b, (((1,), (1,)), ((), ())), precision=precision, preferred_element_type=f32)


def _hdot(a, b):
    return jnp.dot(a, b, precision=HIGHEST, preferred_element_type=f32)


def _weight_spec(w, block, index):
    if isinstance(w, tuple):
        stack, layer = w
        return stack, pl.BlockSpec((None,) + block, lambda *ids: (layer,) + index(*ids))
    return w, pl.BlockSpec(block, index)


def _weight_shape(w):
    return w[0].shape[1:] if isinstance(w, tuple) else w.shape


def _pick_tile(n, target):
    best = LANES
    for t in range(LANES, min(n, target) + 1, LANES):
        if n % t == 0:
            best = t
    return best


def _mm_body(*refs, norm, res):
    it = iter(refs)
    x_ref = next(it)
    g_ref = next(it) if norm else None
    w_ref = next(it)
    r_ref = next(it) if res else None
    o_ref = next(it)
    xn_ref = next(it)

    @pl.when(pl.program_id(1) == 0)
    def _():
        x = x_ref[...].astype(f32)
        if norm:
            x = _rms(x, g_ref[...])
        xn_ref[...] = x.astype(bf16)

    acc = jnp.dot(xn_ref[...], w_ref[...], preferred_element_type=f32)
    if res:
        acc = acc + r_ref[...]
    o_ref[...] = acc.astype(o_ref.dtype)


def mm(x, w, *, g=None, res=None, out_dtype=f32, tm=1024, tn=1024):
    M, K = x.shape
    N = _weight_shape(w)[1]
    tm = min(tm, M)
    tn = _pick_tile(N, tn)
    assert M % tm == 0 and N % tn == 0
    args = [x]
    specs = [pl.BlockSpec((tm, K), lambda i, j: (i, 0))]
    if g is not None:
        args.append(g.reshape(1, K))
        specs.append(pl.BlockSpec((1, K), lambda i, j: (0, 0)))
    w_arr, w_spec = _weight_spec(w, (K, tn), lambda i, j: (0, j))
    args.append(w_arr)
    specs.append(w_spec)
    if res is not None:
        args.append(res)
        specs.append(pl.BlockSpec((tm, tn), lambda i, j: (i, j)))
    return pl.pallas_call(
        functools.partial(_mm_body, norm=g is not None, res=res is not None),
        out_shape=jax.ShapeDtypeStruct((M, N), out_dtype),
        grid=(M // tm, N // tn),
        in_specs=specs,
        out_specs=pl.BlockSpec((tm, tn), lambda i, j: (i, j)),
        scratch_shapes=[pltpu.VMEM((tm, K), bf16)],
        compiler_params=_params("parallel", "arbitrary"),
        name="mm",
    )(*args)


def _ffn_body(x_ref, g_ref, wa_ref, wb_ref, wo_ref, o_ref, xn_ref, acc_ref):
    j = pl.program_id(1)

    @pl.when(j == 0)
    def _():
        xn_ref[...] = _rms(x_ref[...], g_ref[...]).astype(bf16)
        acc_ref[...] = jnp.zeros_like(acc_ref)

    xn = xn_ref[...]
    a = jnp.dot(xn, wa_ref[...], preferred_element_type=f32)
    b = jnp.dot(xn, wb_ref[...], preferred_element_type=f32)
    h = (a * jax.nn.sigmoid(a) * b).astype(bf16)
    acc_ref[...] += jnp.dot(h, wo_ref[...], preferred_element_type=f32)

    @pl.when(j == pl.num_programs(1) - 1)
    def _():
        o_ref[...] = x_ref[...] + acc_ref[...]


def ffn(x, g, w_in, w_out, *, tm=1024, th=704):
    M, D = x.shape
    F = _weight_shape(w_out)[0]
    tm = min(tm, M)
    th = _pick_tile(F, th)
    nh = F // th
    wa, wa_spec = _weight_spec(w_in, (D, th), lambda i, j: (0, j))
    wb, wb_spec = _weight_spec(w_in, (D, th), lambda i, j: (0, j + nh))
    wo, wo_spec = _weight_spec(w_out, (th, D), lambda i, j: (j, 0))
    return pl.pallas_call(
        _ffn_body,
        out_shape=jax.ShapeDtypeStruct((M, D), f32),
        grid=(M // tm, nh),
        in_specs=[
            pl.BlockSpec((tm, D), lambda i, j: (i, 0)),
            pl.BlockSpec((1, D), lambda i, j: (0, 0)),
            wa_spec, wb_spec, wo_spec,
        ],
        out_specs=pl.BlockSpec((tm, D), lambda i, j: (i, 0)),
        scratch_shapes=[pltpu.VMEM((tm, D), bf16), pltpu.VMEM((tm, D), f32)],
        compiler_params=_params("parallel", "arbitrary"),
        name="ffn",
    )(x, g.reshape(1, D), wa, wb, wo)


def _xattn_body(x_ref, g_ref, wq_ref, kv_ref, wo_ref, o_ref, *, heads):
    x = x_ref[...]
    D = x.shape[-1]
    dh = D // heads
    xn = _rms(x, g_ref[...]).astype(bf16)
    q = jnp.dot(xn, wq_ref[...], preferred_element_type=f32).astype(bf16)
    outs = []
    for h in range(heads):
        kh = kv_ref[:, h * dh:(h + 1) * dh]
        vh = kv_ref[:, D + h * dh:D + (h + 1) * dh]
        s = _dot_t(q[:, h * dh:(h + 1) * dh], kh) * dh ** -0.5
        m = jnp.max(s, axis=-1, keepdims=True)
        e = jnp.exp(s - m)
        p = e / jnp.sum(e, axis=-1, keepdims=True)
        outs.append(jnp.dot(p.astype(bf16), vh, preferred_element_type=f32))
    o = jnp.concatenate(outs, axis=-1).astype(bf16)
    o_ref[...] = x + jnp.dot(o, wo_ref[...], preferred_element_type=f32)


def xattn_prompt(x, g, wq, kv, wo, *, tq=512):
    B, L, D = x.shape
    Mem = kv.shape[1]
    tq = min(tq, L)
    wq, wq_spec = _weight_spec(wq, (D, D), lambda b, i: (0, 0))
    wo, wo_spec = _weight_spec(wo, (D, D), lambda b, i: (0, 0))
    return pl.pallas_call(
        functools.partial(_xattn_body, heads=MEM_HEADS),
        out_shape=jax.ShapeDtypeStruct((B, L, D), f32),
        grid=(B, L // tq),
        in_specs=[
            pl.BlockSpec((None, tq, D), lambda b, i: (b, i, 0)),
            pl.BlockSpec((1, D), lambda b, i: (0, 0)),
            wq_spec,
            pl.BlockSpec((None, Mem, 2 * D), lambda b, i: (b, 0, 0)),
            wo_spec,
        ],
        out_specs=pl.BlockSpec((None, tq, D), lambda b, i: (b, i, 0)),
        compiler_params=_params("parallel", "parallel"),
        name="xattn_prompt",
    )(x, g.reshape(1, D), wq, kv, wo)


def _dil_body(sl_ref, q_ref, kc_ref, kp_ref, vc_ref, vp_ref, o_ref, lse_ref, *, grp, dil, tq):
    i = pl.program_id(1)
    c = pl.program_id(2)
    ws = A_WIN_STEPS
    scale = A_HEAD_DIM ** -0.5
    row = lax.broadcasted_iota(jnp.int32, (ws, 2 * ws), 0)
    col = lax.broadcasted_iota(jnp.int32, (ws, 2 * ws), 1)
    dist = row + ws - col
    in_band = (dist >= 0) & (dist <= ws)
    valid_first = in_band & ((col >= ws) | (i > 0))
    distf = (dist * dil).astype(f32)
    lane = lax.broadcasted_iota(jnp.int32, (1, LANES), 1)
    first = lane < A_HEAD_DIM
    slopes = [sl_ref[grp * A_SLOTS + 2 * c + half] for half in range(2)]

    def rows(ref, r, start, n):
        return ref[pl.ds(r + start * dil, n, stride=dil), :] if dil > 1 else ref[start:start + n, :]

    def residue(r, carry):
        for j in range(tq // ws):
            q = rows(q_ref, r, j * ws, ws).astype(bf16)
            if j == 0:
                k = jnp.concatenate([rows(kp_ref, r, 0, ws), rows(kc_ref, r, 0, ws)], axis=0)
                v = jnp.concatenate([rows(vp_ref, r, 0, ws), rows(vc_ref, r, 0, ws)], axis=0)
            else:
                k = rows(kc_ref, r, (j - 1) * ws, 2 * ws)
                v = rows(vc_ref, r, (j - 1) * ws, 2 * ws)
            k, v = k.astype(bf16), v.astype(bf16)
            valid = valid_first if j == 0 else in_band
            halves = []
            for half in range(2):
                qm = jnp.where(first == (half == 0), q, jnp.zeros_like(q))
                s = _dot_t(qm, k) * scale - slopes[half] * distf
                s = jnp.where(valid, s, NEG_INF)
                m = jnp.max(s, axis=-1, keepdims=True)
                e = jnp.exp(s - m)
                den = jnp.sum(e, axis=-1, keepdims=True)
                pv = jnp.dot(e.astype(bf16), v, preferred_element_type=f32) / den
                halves.append((pv, m + jnp.log(den)))
            o_t = jnp.where(first, halves[0][0], halves[1][0])
            l_t = jnp.where(first, halves[0][1], halves[1][1])
            if dil > 1:
                o_ref[pl.ds(r + j * ws * dil, ws, stride=dil), :] = o_t
                lse_ref[pl.ds(r + j * ws * dil, ws, stride=dil), :] = l_t
            else:
                o_ref[j * ws:(j + 1) * ws, :] = o_t
                lse_ref[j * ws:(j + 1) * ws, :] = l_t
        return carry

    unroll = min(dil, DIL_UNROLL)

    def trip(u, carry):
        for k in range(unroll):
            residue(u * unroll + k, carry)
        return carry

    if dil > unroll:
        lax.fori_loop(0, dil // unroll, trip, 0)
    else:
        trip(0, 0)


DIL_TQ = (1024, 256, 128)
DIL_UNROLL = 4


def dilated_group(qkv, grp):
    B, L, W3 = qkv.shape
    d = A_DILATIONS[grp]
    tq = min(DIL_TQ[grp], L // d)
    rows_blk = tq * d
    prev_blk = A_WIN_STEPS * d
    assert L % rows_blk == 0 and tq % A_WIN_STEPS == 0
    ratio = tq // A_WIN_STEPS
    tiles = A_OUT // LANES
    sect = A_GROUPS * tiles

    def spec(t, prev=False):
        col = lambda c: t * sect + grp * tiles + c
        if prev:
            return pl.BlockSpec((None, prev_blk, LANES),
                                lambda b, i, c, sl: (b, jnp.maximum(i * ratio - 1, 0), col(c)))
        return pl.BlockSpec((None, rows_blk, LANES), lambda b, i, c, sl: (b, i, col(c)))

    out_spec = pl.BlockSpec((None, rows_blk, LANES), lambda b, i, c, sl: (b, i, c))
    return pl.pallas_call(
        functools.partial(_dil_body, grp=grp, dil=d, tq=tq),
        out_shape=[jax.ShapeDtypeStruct((B, L, A_OUT), f32)] * 2,
        grid_spec=pltpu.PrefetchScalarGridSpec(
            num_scalar_prefetch=1,
            grid=(B, L // rows_blk, tiles),
            in_specs=[spec(0), spec(1), spec(1, True), spec(2), spec(2, True)],
            out_specs=[out_spec, out_spec],
        ),
        compiler_params=_params("parallel", "parallel", "parallel"),
        name=f"dilated_g{grp}",
    )(jnp.asarray(SLOPES_A.reshape(-1)), qkv, qkv, qkv, qkv, qkv)


def _dil_out_body(o0, o1, o2, l0, l1, l2, w_ref, r_ref, out_ref):
    ls = [l0[...], l1[...], l2[...]]
    m = jnp.maximum(jnp.maximum(ls[0], ls[1]), ls[2])
    es = [jnp.exp(l - m) for l in ls]
    den = es[0] + es[1] + es[2]
    o = (es[0] / den) * o0[...] + (es[1] / den) * o1[...] + (es[2] / den) * o2[...]
    out_ref[...] = r_ref[...] + jnp.dot(o.astype(bf16), w_ref[...], preferred_element_type=f32)


def dilated_merge_out(os, lses, w_o, res, *, tm=512):
    M, D = res.shape
    tm = min(tm, M)
    row = lambda n: pl.BlockSpec((tm, n), lambda i: (i, 0))
    w_o, w_spec = _weight_spec(w_o, (A_OUT, D), lambda i: (0, 0))
    return pl.pallas_call(
        _dil_out_body,
        out_shape=jax.ShapeDtypeStruct((M, D), f32),
        grid=(M // tm,),
        in_specs=[row(A_OUT)] * 6 + [w_spec, row(D)],
        out_specs=row(D),
        compiler_params=_params("parallel"),
        name="dilated_merge_out",
    )(*os, *lses, w_o, res)


GLA_SUB = 16


def _log_sigmoid(z):
    return -(jnp.maximum(-z, 0.0) + jnp.log1p(jnp.exp(-jnp.abs(z))))


def _gla_body(q_ref, k_ref, v_ref, r_ref, gl_ref, w2_ref, bg_ref, gh_ref, s0_ref, y_ref, s_ref, a_ref):
    C = q_ref.shape[0]
    c = GLA_SUB

    @pl.when(pl.program_id(1) == 0)
    def _():
        s_ref[...] = s0_ref[...]

    z = _hdot(gl_ref[...], w2_ref[...]) + bg_ref[...]
    la = _log_sigmoid(z) / B_GATE_TAU
    rowC = lax.broadcasted_iota(jnp.int32, (C, C), 0)
    colC = lax.broadcasted_iota(jnp.int32, (C, C), 1)
    tri = (rowC >= colC).astype(f32)
    b_all = _hdot(tri, la)
    row_k = lax.broadcasted_iota(jnp.int32, (C, B_DK), 0)
    col_c = lax.broadcasted_iota(jnp.int32, (c, C), 1)
    row_c = lax.broadcasted_iota(jnp.int32, (c, 1), 0)

    for h in range(B_HEADS):
        bh = b_all[:, h * B_DK:(h + 1) * B_DK]
        qh = q_ref[:, h * B_DK:(h + 1) * B_DK] * B_DK ** -0.5
        kh = k_ref[:, h * B_DK:(h + 1) * B_DK]
        vh = v_ref[:, h * B_DV:(h + 1) * B_DV].astype(bf16)
        S = s_ref[h]
        o = jnp.dot((qh * jnp.exp(bh)).astype(bf16), S.astype(bf16), preferred_element_type=f32)

        for I in range(C // c):
            r0 = I * c
            qI, kI, bI = qh[r0:r0 + c], kh[r0:r0 + c], bh[r0:r0 + c]
            if I == 0:
                A_I = jnp.zeros((c, C), f32)
            else:
                beta = bh[r0 - 1:r0]
                qt = qI * jnp.exp(bI - beta)
                kt = kh * jnp.exp(jnp.where(row_k < r0, beta - bh, 0.0))
                A_I = jnp.where(col_c < r0, _dot_t(qt.astype(bf16), kt.astype(bf16)), 0.0)
            for j in range(c):
                ex = jnp.exp(jnp.minimum(bI - bI[j:j + 1], 0.0))
                tj = jnp.sum(qI * kI[j:j + 1] * ex, axis=-1, keepdims=True)
                tj = jnp.where(row_c >= j, tj, 0.0)
                A_I = jnp.where(col_c == r0 + j, tj, A_I)
            a_ref[r0:r0 + c, :] = A_I
        o = o + jnp.dot(a_ref[...].astype(bf16), vh, preferred_element_type=f32)

        b_last = bh[C - 1:C]
        kdec = jnp.transpose(kh * jnp.exp(b_last - bh)).astype(bf16)
        decay = jnp.transpose(jnp.broadcast_to(jnp.exp(b_last), (8, B_DK)))[:, 0:1]
        s_ref[h] = decay * S + jnp.dot(kdec, vh, preferred_element_type=f32)

        on = _rms(o, gh_ref[:, h * B_DV:(h + 1) * B_DV])
        rh = r_ref[:, h * B_DV:(h + 1) * B_DV]
        y_ref[:, h * B_DV:(h + 1) * B_DV] = (on * (rh * jax.nn.sigmoid(rh))).astype(y_ref.dtype)


def gla_prompt(proj, w_gate2p, b_gate, g_head, s0, *, chunk=128):
    B, L, _ = proj.shape
    HK, HV = B_HEADS * B_DK, B_HEADS * B_DV
    C = min(chunk, L)
    assert L % C == 0 and C % GLA_SUB == 0
    return pl.pallas_call(
        _gla_body,
        out_shape=[jax.ShapeDtypeStruct((B, L, HV), bf16),
                   jax.ShapeDtypeStruct((B, B_HEADS, B_DK, B_DV), f32)],
        grid=(B, L // C),
        in_specs=[
            pl.BlockSpec((None, C, HK), lambda b, i: (b, i, 0)),
            pl.BlockSpec((None, C, HK), lambda b, i: (b, i, 1)),
            pl.BlockSpec((None, C, HV), lambda b, i: (b, i, 1)),
            pl.BlockSpec((None, C, HV), lambda b, i: (b, i, 2)),
            pl.BlockSpec((None, C, LANES), lambda b, i: (b, i, (2 * HK + 2 * HV) // LANES)),
            pl.BlockSpec((LANES, HK), lambda b, i: (0, 0)),
            pl.BlockSpec((1, HK), lambda b, i: (0, 0)),
            pl.BlockSpec((1, HV), lambda b, i: (0, 0)),
            pl.BlockSpec((None, B_HEADS, B_DK, B_DV), lambda b, i: (b, 0, 0, 0)),
        ],
        out_specs=[pl.BlockSpec((None, C, HV), lambda b, i: (b, i, 0)),
                   pl.BlockSpec((None, B_HEADS, B_DK, B_DV), lambda b, i: (b, 0, 0, 0))],
        scratch_shapes=[pltpu.VMEM((C, C), f32)],
        compiler_params=_params("parallel", "arbitrary"),
        name="gla_prompt",
    )(proj, proj, proj, proj, proj, w_gate2p, b_gate.reshape(1, HK), g_head.reshape(1, HV), s0)


def _cmp_mlp_tail(z, posb, w2_ref):
    n = z.shape[0]
    H = C_CMP_HIDDEN
    hid = z[:, :H] + pltpu.roll(z[:, H:], n - 1, 0) + posb
    return jnp.dot(jax.nn.gelu(hid).astype(bf16), w2_ref[...], preferred_element_type=f32)


def _cmp_pos_bias(p_ref, w1_ref):
    H = C_CMP_HIDDEN
    pz = jnp.dot(p_ref[...].astype(bf16), w1_ref[...], preferred_element_type=f32)
    return pz[0:1, :H] + pz[1:2, H:]


def _cmp_body(xk_ref, xv_ref, pk_ref, pv_ref, wk1_ref, wk2_ref, wv1_ref, wv2_ref, ok_ref, ov_ref):
    for x_ref, p_ref, w1_ref, w2_ref, o_ref in ((xk_ref, pk_ref, wk1_ref, wk2_ref, ok_ref),
                                                (xv_ref, pv_ref, wv1_ref, wv2_ref, ov_ref)):
        z = jnp.dot(x_ref[...].astype(bf16), w1_ref[...], preferred_element_type=f32)
        o_ref[...] = _cmp_mlp_tail(z, _cmp_pos_bias(p_ref, w1_ref), w2_ref).astype(o_ref.dtype)


def nsa_compress(xk, xv, pos_k, pos_v, wk1, wk2, wv1, wv2):
    B, G, n, W = xk.shape
    x_spec = pl.BlockSpec((None, None, n, W), lambda b, g: (b, g, 0, 0))
    full = lambda a: pl.BlockSpec(a.shape, lambda b, g: (0,) * a.ndim)
    o_spec = pl.BlockSpec((None, None, n, C_HEAD_DIM), lambda b, g: (b, g, 0, 0))
    consts = [pos_k, pos_v, wk1, wk2, wv1, wv2]
    return pl.pallas_call(
        _cmp_body,
        out_shape=[jax.ShapeDtypeStruct((B, G, n, C_HEAD_DIM), bf16)] * 2,
        grid=(B, G),
        in_specs=[x_spec, x_spec] + [full(a) for a in consts],
        out_specs=[o_spec, o_spec],
        compiler_params=_params("parallel", "parallel"),
        name="nsa_compress",
    )(xk, xv, *consts)


def _topk_mask_t(score_t, blk_t):
    keep = jnp.zeros(score_t.shape, f32)
    big = float(score_t.shape[0])
    for _ in range(C_TOPK):
        mx = jnp.max(score_t, axis=0, keepdims=True)
        first = jnp.min(jnp.where(score_t == mx, blk_t, big), axis=0, keepdims=True)
        hit = blk_t == first
        keep = jnp.where(hit, jnp.where(mx > 0.5 * NEG_INF, 1.0, 0.0), keep)
        score_t = jnp.where(hit, REMOVED, score_t)
    return keep


def _reduce_rows(x, op):
    slabs = [x[r:r + 8] for r in range(0, x.shape[0], 8)]
    while len(slabs) > 1:
        slabs = [op(slabs[k], slabs[k + 1]) if k + 1 < len(slabs) else slabs[k] for k in range(0, len(slabs), 2)]
    red = jnp.max if op is jnp.maximum else jnp.sum
    return red(slabs[0], axis=0, keepdims=True)


def _softmax_rows_t(s, mask=None):
    e = jnp.exp(s - _reduce_rows(s, jnp.maximum))
    if mask is not None:
        e = jnp.where(mask, e, 0.0)
    return e / jnp.maximum(_reduce_rows(e, jnp.add), TINY)


def _nsa_body(sp_ref, q_ref, gate_ref, bg_ref, kc_ref, vc_ref, ks_ref, vs_ref, kw_ref, vw_ref,
              o_ref, m_ref, l_ref, acc_ref, gs_ref, sa_ref, sb_ref, *, tk):
    g = pl.program_id(1)
    i = pl.program_id(2)
    tq = q_ref.shape[0]
    R = C_HPG * tq
    L = ks_ref.shape[0]
    n_cmp = kc_ref.shape[0]
    NS = LANES
    dh = C_HEAD_DIM
    q0 = i * tq

    q_t = jnp.transpose(q_ref[...] * dh ** -0.5)
    prow = lax.broadcasted_iota(jnp.int32, (LANES - dh, 1), 0)
    cols = []
    for hh in range(C_HPG):
        pc = jnp.zeros((LANES - dh, 1), f32)
        for n in reversed(range(N_PIECES)):
            pc = jnp.where(prow < 2 * n + 2, sp_ref[(g * C_HPG + hh) * N_PIECES + n], pc)
        cols.append(jnp.concatenate([q_t[hh * dh:(hh + 1) * dh], jnp.broadcast_to(pc, (LANES - dh, tq))], axis=0))
    qa_t = jnp.concatenate(cols, axis=1).astype(bf16)
    t1 = q0 + lax.broadcasted_iota(jnp.int32, (1, tq), 1)
    t = jnp.concatenate([t1] * C_HPG, axis=1)

    cpos = lax.broadcasted_iota(jnp.int32, (n_cmp, 1), 0) * C_CMP_STRIDE + (C_CMP_BLOCK - 1)
    mask = cpos <= t
    s = jnp.where(mask, jnp.dot(kc_ref[...], qa_t, preferred_element_type=f32), NEG_INF)
    p = _softmax_rows_t(s, mask)
    o_c = jnp.dot(vc_ref[...], p.astype(bf16), preferred_element_type=f32)

    psum = p[:, 0:tq]
    for hh in range(1, C_HPG):
        psum = psum + p[:, hh * tq:(hh + 1) * tq]
    n_blk = -(-(L // C_SEL_BLOCK) // 8) * 8
    jb = lax.broadcasted_iota(jnp.int32, (n_blk, n_cmp), 0)
    c_start = lax.broadcasted_iota(jnp.int32, (n_blk, n_cmp), 1) * C_CMP_STRIDE
    cover = jnp.where(c_start < (jb + 1) * C_SEL_BLOCK,
                      jnp.where(c_start + C_CMP_BLOCK > jb * C_SEL_BLOCK, 1.0, 0.0), 0.0).astype(bf16)
    imp = jnp.zeros((n_blk, tq), f32)
    rest = psum
    for _ in range(N_PIECES):
        piece = rest.astype(bf16)
        imp = imp + jnp.dot(cover, piece, preferred_element_type=f32)
        rest = rest - piece.astype(f32)
    blk = lax.broadcasted_iota(jnp.int32, (n_blk, tq), 0)
    cur = t1 >> C_SEL_SHIFT
    forced = (blk == 0) | (blk == cur) | (blk == cur - 1)
    score = jnp.where(blk * C_SEL_BLOCK <= t1, jnp.where(forced, FORCE, imp), NEG_INF)
    keep = _topk_mask_t(score, blk.astype(f32))
    neg1 = jnp.concatenate([jnp.where(keep > 0.5, jnp.where(blk == cur, NEG_INF, 0.0), NEG_INF),
                            jnp.full((NS - n_blk, tq), NEG_INF, f32)], axis=0).astype(bf16)
    qa2_t = jnp.concatenate([qa_t, jnp.concatenate([neg1] * C_HPG, axis=1)], axis=0)

    d0 = pl.multiple_of(q0, tq)
    kpos = q0 + lax.broadcasted_iota(jnp.int32, (tq, 1), 0)
    sc = jnp.dot(ks_ref[pl.ds(d0, tq), 0:LANES], qa_t, preferred_element_type=f32)
    sc = jnp.where(kpos <= t, jnp.where((kpos >> C_SEL_SHIFT) == (t >> C_SEL_SHIFT), sc, NEG_INF), NEG_INF)
    m0 = _reduce_rows(sc, jnp.maximum)
    ee = jnp.exp(sc - m0)
    m_ref[...] = m0
    l_ref[...] = _reduce_rows(ee, jnp.add)
    acc_ref[...] = jnp.dot(vs_ref[:, pl.ds(d0, tq)], ee.astype(bf16), preferred_element_type=f32)

    n_tiles = q0 // tk + 1

    def scores(kt, s_ref):
        k0 = pl.multiple_of(jnp.minimum(kt, n_tiles - 1) * tk, tk)
        s_ref[...] = jnp.dot(ks_ref[pl.ds(k0, tk), :], qa2_t, preferred_element_type=f32)

    def absorb(kt, s_ref):
        k0 = pl.multiple_of(kt * tk, tk)
        sc = s_ref[...]
        m_old = m_ref[...]
        m_new = jnp.maximum(m_old, _reduce_rows(sc, jnp.maximum))
        alpha = jnp.exp(m_old - m_new)
        ee = jnp.exp(sc - m_new)
        l_ref[...] = alpha * l_ref[...] + _reduce_rows(ee, jnp.add)
        acc_ref[...] = alpha * acc_ref[...] + jnp.dot(vs_ref[:, pl.ds(k0, tk)], ee.astype(bf16),
                                                      preferred_element_type=f32)
        m_ref[...] = m_new

    scores(0, sa_ref)

    def pair(j, carry):
        scores(2 * j + 1, sb_ref)
        absorb(2 * j, sa_ref)
        scores(2 * j + 2, sa_ref)
        absorb(2 * j + 1, sb_ref)
        return carry

    lax.fori_loop(0, n_tiles // 2, pair, 0)

    @pl.when(n_tiles % 2 == 1)
    def _():
        absorb(n_tiles - 1, sa_ref)

    o_s = acc_ref[...] / l_ref[...]

    W = min(L, C_WINDOW + tq)
    w0 = pl.multiple_of(jnp.clip(q0 - C_WINDOW, 0, L - W), tq)
    wpos = w0 + lax.broadcasted_iota(jnp.int32, (W, 1), 0)
    s = jnp.dot(kw_ref[pl.ds(w0, W), :], qa_t, preferred_element_type=f32)
    s = jnp.where(wpos <= t, jnp.where(wpos >= t - C_WINDOW, s, NEG_INF), NEG_INF)
    e = jnp.exp(s - _reduce_rows(s, jnp.maximum))
    o_w = jnp.dot(vw_ref[:, pl.ds(w0, W)], e.astype(bf16), preferred_element_type=f32) / _reduce_rows(e, jnp.add)

    gs_ref[...] = jnp.transpose(jax.nn.sigmoid(gate_ref[...] + bg_ref[...]))

    def gate(r):
        return jnp.concatenate([gs_ref[pl.ds((g * C_HPG + hh) * 3 + r, 1), :] for hh in range(C_HPG)], axis=1)

    o = gate(0) * o_c + gate(1) * o_s + gate(2) * o_w
    o = jnp.concatenate([o[:, hh * tq:(hh + 1) * tq] for hh in range(C_HPG)], axis=0)
    o_ref[...] = jnp.transpose(o).astype(o_ref.dtype)


def _aug_keys(k, pos, one_hot_blocks):
    B, G, n, dh = k.shape
    hi = (pos // C_SEL_BLOCK) * C_SEL_BLOCK
    cols = np.zeros((n, LANES - dh), np.float32)
    for j in range(N_PIECES):
        cols[:, 2 * j] = hi
        cols[:, 2 * j + 1] = pos - hi
    parts = [k, jnp.broadcast_to(jnp.asarray(cols, bf16), (B, G, n, LANES - dh))]
    if one_hot_blocks:
        oh = (pos[:, None] // C_SEL_BLOCK == np.arange(LANES)[None, :]).astype(np.float32)
        parts.append(jnp.broadcast_to(jnp.asarray(oh, bf16), (B, G, n, LANES)))
    return jnp.concatenate(parts, axis=-1)


def nsa_attention_prompt(proj, b_gate_p, kc, vc, ks, vs, kw, vw, *, tq=128, tk=512):
    B, L, _ = proj.shape
    G, dh = C_KV_HEADS, C_HEAD_DIM
    tq = min(tq, L)
    tk = min(tk, L)
    assert tk % tq == 0 and L % tk == 0 and L // C_SEL_BLOCK <= LANES
    R = C_HPG * tq
    n_cmp = kc.shape[2]
    kc_a = _aug_keys(kc, np.arange(n_cmp) * C_CMP_STRIDE + (C_CMP_BLOCK - 1), False)
    ks_a = _aug_keys(ks, np.arange(L), True)
    kw_a = _aug_keys(kw, np.arange(L), False)
    gate_blk = (C_HEADS * dh + 6 * G * dh) // LANES
    seq = lambda n, w: pl.BlockSpec((None, None, n, w), lambda b, g, i, sp: (b, g, 0, 0))
    tr = lambda a: jnp.swapaxes(a, 2, 3)
    return pl.pallas_call(
        functools.partial(_nsa_body, tk=tk),
        out_shape=jax.ShapeDtypeStruct((B, L, C_HEADS * dh), bf16),
        grid_spec=pltpu.PrefetchScalarGridSpec(
            num_scalar_prefetch=1,
            grid=(B, G, L // tq),
            in_specs=[
                pl.BlockSpec((None, tq, C_HPG * dh), lambda b, g, i, sp: (b, i, g)),
                pl.BlockSpec((None, tq, LANES), lambda b, g, i, sp: (b, i, gate_blk)),
                pl.BlockSpec((1, LANES), lambda b, g, i, sp: (0, 0)),
                seq(n_cmp, LANES), seq(dh, n_cmp), seq(L, 2 * LANES), seq(dh, L), seq(L, LANES), seq(dh, L),
            ],
            out_specs=pl.BlockSpec((None, tq, C_HPG * dh), lambda b, g, i, sp: (b, i, g)),
            scratch_shapes=[pltpu.VMEM((1, R), f32), pltpu.VMEM((1, R), f32), pltpu.VMEM((dh, R), f32),
                            pltpu.VMEM((LANES, tq), f32), pltpu.VMEM((tk, R), f32), pltpu.VMEM((tk, R), f32)],
        ),
        compiler_params=_params("parallel", "parallel", "parallel"),
        name="nsa_attention_prompt",
    )(jnp.asarray(SLOPES_C_PIECES), proj, proj, b_gate_p, kc_a, tr(vc), ks_a, tr(vs), kw_a, tr(vw))


def _slot_rows(ref, off, n, w):
    rows = jnp.concatenate([ref[:, off + s * w:off + (s + 1) * w] for s in range(n)]
                           + ([jnp.zeros((8 - n, w), f32)] if n < 8 else []), axis=0)
    return jnp.concatenate([rows, jnp.zeros((8, LANES - w), f32)], axis=-1) if w < LANES else rows


def _dil_sample_body(sl_ref, qkv_ref, c0_ref, c1_ref, c2_ref, o_ref):
    dh = A_HEAD_DIM
    scale = dh ** -0.5
    sect = A_GROUPS * A_OUT
    scores, news, v_cols = [], [], []
    for grp, c_ref in enumerate((c0_ref, c1_ref, c2_ref)):
        W = c_ref.shape[-1]
        d = A_DILATIONS[grp]
        q8 = _slot_rows(qkv_ref, grp * A_OUT, A_SLOTS, dh)
        kn8 = _slot_rows(qkv_ref, sect + grp * A_OUT, A_SLOTS, dh)
        vn8 = _slot_rows(qkv_ref, 2 * sect + grp * A_OUT, A_SLOTS, dh)
        q_cols = jnp.transpose(q8)[:dh]
        v_cols.append(jnp.transpose(vn8)[:dh])
        news.append(jnp.sum(q8 * kn8, axis=-1, keepdims=True) * scale)
        pos = lax.broadcasted_iota(jnp.int32, (1, W), 1)
        rows = [jnp.sum(c_ref[0, s] * q_cols[:, s:s + 1], axis=0, keepdims=True) for s in range(A_SLOTS)]
        slope = jnp.concatenate([jnp.full((1, 1), sl_ref[grp * A_SLOTS + s], f32) for s in range(A_SLOTS)], axis=0)
        sc = jnp.concatenate(rows, axis=0) * scale - slope * (W - pos).astype(f32)
        scores.append(jnp.where((pos & (d - 1)) == 0, sc, NEG_INF))
    m = functools.reduce(jnp.maximum, [jnp.max(s, axis=-1, keepdims=True) for s in scores] + news)
    den = jnp.zeros((A_SLOTS, 1), f32)
    acc = [jnp.zeros((dh, 1), f32) for _ in range(A_SLOTS)]
    for grp, c_ref in enumerate((c0_ref, c1_ref, c2_ref)):
        e = jnp.exp(scores[grp] - m)
        en = jnp.exp(news[grp] - m)
        den = den + jnp.sum(e, axis=-1, keepdims=True) + en
        for s in range(A_SLOTS):
            acc[s] = acc[s] + jnp.sum(c_ref[1, s] * e[s:s + 1, :], axis=-1, keepdims=True) \
                + en[s:s + 1, :] * v_cols[grp][:, s:s + 1]
    o_ref[...] = jnp.concatenate([acc[s] / den[s:s + 1, :] for s in range(A_SLOTS)], axis=-1)


def dil_sample(qkv, caches_t, layer):
    Bs = qkv.shape[0]
    for grp, c in enumerate(caches_t):
        assert c.shape[-1] == A_WINDOWS[grp]
    c_spec = lambda c: pl.BlockSpec((None, None) + c.shape[2:], lambda b, sl: (layer, b, 0, 0, 0, 0))
    return pl.pallas_call(
        _dil_sample_body,
        out_shape=jax.ShapeDtypeStruct((Bs, A_HEAD_DIM, A_SLOTS), f32),
        grid_spec=pltpu.PrefetchScalarGridSpec(
            num_scalar_prefetch=1,
            grid=(Bs,),
            in_specs=[pl.BlockSpec((None, 1, qkv.shape[-1]), lambda b, sl: (b, 0, 0))]
            + [c_spec(c) for c in caches_t],
            out_specs=pl.BlockSpec((None, A_HEAD_DIM, A_SLOTS), lambda b, sl: (b, 0, 0)),
        ),
        compiler_params=_params("parallel"),
        name="dil_sample",
    )(jnp.asarray(SLOPES_A.reshape(-1)), qkv, *caches_t)


def _xattn_sample_body(q_ref, c_ref, o_ref):
    dh = q_ref.shape[-1] // MEM_HEADS
    outs = []
    for h in range(MEM_HEADS):
        qh = q_ref[:, h * dh:(h + 1) * dh]
        s = jnp.sum(c_ref[:, 0, h, :] * qh, axis=-1, keepdims=True) * dh ** -0.5
        e = jnp.exp(s - jnp.max(s, axis=0, keepdims=True))
        outs.append(jnp.sum(e * c_ref[:, 1, h, :], axis=0, keepdims=True) / jnp.sum(e, axis=0, keepdims=True))
    o_ref[...] = jnp.concatenate(outs, axis=-1)


def xattn_sample(q, cache, layer):
    Bs, _, D = q.shape
    return pl.pallas_call(
        _xattn_sample_body,
        out_shape=jax.ShapeDtypeStruct((Bs, 1, D), f32),
        grid=(Bs,),
        in_specs=[pl.BlockSpec((None, 1, D), lambda b: (b, 0, 0)),
                  pl.BlockSpec((None, None) + cache.shape[2:], lambda b: (layer, b, 0, 0, 0, 0))],
        out_specs=pl.BlockSpec((None, 1, D), lambda b: (b, 0, 0)),
        compiler_params=_params("parallel"),
        name="xattn_sample",
    )(q, cache)


def _gla_step_body(q_ref, k_ref, v_ref, r_ref, gl_ref, w2_ref, bg_ref, gh_ref, s0_ref, y_ref, s_ref):
    z = _hdot(gl_ref[...], w2_ref[...]) + bg_ref[...]
    a = jnp.exp(_log_sigmoid(z) / B_GATE_TAU)
    pad = jnp.zeros((5, B_DK), f32)
    for h in range(B_HEADS):
        ks_ = slice(h * B_DK, (h + 1) * B_DK)
        vs_ = slice(h * B_DV, (h + 1) * B_DV)
        rows = jnp.concatenate([a[:, ks_], k_ref[:, ks_], q_ref[:, ks_] * B_DK ** -0.5, pad], axis=0)
        cols = jnp.transpose(rows)
        S = cols[:, 0:1] * s0_ref[h] + cols[:, 1:2] * v_ref[:, vs_]
        s_ref[h] = S
        o = jnp.sum(cols[:, 2:3] * S, axis=0, keepdims=True)
        rh = r_ref[:, vs_]
        y_ref[:, vs_] = (_rms(o, gh_ref[:, vs_]) * (rh * jax.nn.sigmoid(rh))).astype(y_ref.dtype)


def gla_step(proj, w_gate2p, b_gate, g_head, state, layer):
    Bs = proj.shape[0]
    HK, HV = B_HEADS * B_DK, B_HEADS * B_DV
    return pl.pallas_call(
        _gla_step_body,
        out_shape=[jax.ShapeDtypeStruct((Bs, 1, HV), bf16),
                   jax.ShapeDtypeStruct((Bs, B_HEADS, B_DK, B_DV), f32)],
        grid=(Bs,),
        in_specs=[
            pl.BlockSpec((None, 1, HK), lambda b: (b, 0, 0)),
            pl.BlockSpec((None, 1, HK), lambda b: (b, 0, 1)),
            pl.BlockSpec((None, 1, HV), lambda b: (b, 0, 1)),
            pl.BlockSpec((None, 1, HV), lambda b: (b, 0, 2)),
            pl.BlockSpec((None, 1, LANES), lambda b: (b, 0, (2 * HK + 2 * HV) // LANES)),
            pl.BlockSpec((LANES, HK), lambda b: (0, 0)),
            pl.BlockSpec((1, HK), lambda b: (0, 0)),
            pl.BlockSpec((1, HV), lambda b: (0, 0)),
            pl.BlockSpec((None, None, B_HEADS, B_DK, B_DV), lambda b: (layer, b, 0, 0, 0)),
        ],
        out_specs=[pl.BlockSpec((None, 1, HV), lambda b: (b, 0, 0)),
                   pl.BlockSpec((None, B_HEADS, B_DK, B_DV), lambda b: (b, 0, 0, 0))],
        compiler_params=_params("parallel"),
        name="gla_step",
    )(proj, proj, proj, proj, proj, w_gate2p, b_gate.reshape(1, HK), g_head.reshape(1, HV), state)


PAGES_PER_STEP = 8


def _cmp_sample_body(pt_ref, *refs):
    pages = refs[:PAGES_PER_STEP]
    (pk_ref, pv_ref, wk1_ref, wk2_ref, wv1_ref, wv2_ref, wkp_ref, wvp_ref,
     ok_ref, ov_ref, x_ref) = refs[PAGES_PER_STEP:]
    i = pl.program_id(1)
    H = C_CMP_HIDDEN
    n_tiles = x_ref.shape[0]
    per_kind = n_tiles // 2
    per_page = PAGE_SIZE // C_CMP_STRIDE
    out_row = lax.broadcasted_iota(jnp.int32, (PAGE_SIZE, PAGE_SIZE), 0)
    in_row = lax.broadcasted_iota(jnp.int32, (PAGE_SIZE, PAGE_SIZE), 1)
    shift = per_page.bit_length() - 1
    perm = (in_row == (out_row & (per_page - 1)) * C_CMP_STRIDE + (out_row >> shift)).astype(bf16)
    for k in range(PAGES_PER_STEP):
        c0 = pl.multiple_of((i * PAGES_PER_STEP + k) * per_page, per_page)
        for kind in range(2):
            for tile in range(per_kind):
                xt = pages[k][kind, 2 * tile:2 * tile + 2].reshape(LANES, PAGE_SIZE).astype(bf16)
                xs = _dot_t(perm, xt)
                for p in range(C_CMP_STRIDE):
                    x_ref[kind * per_kind + tile, p, pl.ds(c0, per_page), :] = xs[p * per_page:(p + 1) * per_page]

    @pl.when(i == pl.num_programs(1) - 1)
    def _():
        n = x_ref.shape[2]
        for kind, (p_ref, w1_ref, w2_ref, wp_ref, o_ref) in enumerate(
                ((pk_ref, wk1_ref, wk2_ref, wkp_ref, ok_ref), (pv_ref, wv1_ref, wv2_ref, wvp_ref, ov_ref))):
            posb = _cmp_pos_bias(p_ref, w1_ref)
            outs = []
            for tile in range(per_kind):
                ct = kind * per_kind + tile
                z2 = jnp.zeros((n, 4 * H), f32)
                for p in range(C_CMP_STRIDE):
                    xp = x_ref[ct, p].astype(bf16)
                    z2 = z2 + jnp.dot(xp, wp_ref[p], preferred_element_type=f32)
                outs += [_cmp_mlp_tail(z, posb, w2_ref) for z in (z2[:, :2 * H], z2[:, 2 * H:])]
            o_ref[...] = jnp.concatenate(outs, axis=-1)


def _pair_block_diag(w1):
    dh = C_HEAD_DIM
    w = w1.reshape(C_CMP_STRIDE, dh, w1.shape[1])
    z = jnp.zeros_like(w)
    return jnp.concatenate([jnp.concatenate([w, z], axis=-1), jnp.concatenate([z, w], axis=-1)], axis=1)


def nsa_compress_sample(page_table, cache_t, layer, pos_k, pos_v, wk1, wk2, wv1, wv2):
    Bs, n_pages = page_table.shape
    G, dh = C_KV_HEADS, C_HEAD_DIM
    assert n_pages % PAGES_PER_STEP == 0 and cache_t.shape[2:] == (4, G, dh, PAGE_SIZE) and 2 * dh == LANES
    n_rows = n_pages * PAGE_SIZE
    n = n_rows // C_CMP_STRIDE
    page = lambda k: pl.BlockSpec((None, None, 2, G, dh, PAGE_SIZE),
                                  lambda b, i, pt, k=k: (layer, pt[b, i * PAGES_PER_STEP + k], 0, 0, 0, 0))
    consts = [pos_k, pos_v, wk1, wk2, wv1, wv2, _pair_block_diag(wk1), _pair_block_diag(wv1)]
    full = lambda a: pl.BlockSpec(a.shape, lambda b, i, pt: (0,) * a.ndim)
    o_spec = pl.BlockSpec((None, n, G * dh), lambda b, i, pt: (b, 0, 0))
    return pl.pallas_call(
        _cmp_sample_body,
        out_shape=[jax.ShapeDtypeStruct((Bs, n, G * dh), f32)] * 2,
        grid_spec=pltpu.PrefetchScalarGridSpec(
            num_scalar_prefetch=1,
            grid=(Bs, n_pages // PAGES_PER_STEP),
            in_specs=[page(k) for k in range(PAGES_PER_STEP)] + [full(a) for a in consts],
            out_specs=[o_spec, o_spec],
            scratch_shapes=[pltpu.VMEM((2 * G * dh // LANES, C_CMP_STRIDE, n, LANES), f32)],
        ),
        compiler_params=_params("parallel", "arbitrary"),
        name="nsa_compress_sample",
    )(page_table, *([cache_t] * PAGES_PER_STEP), *consts)


def _cmp_topk_sample_body(sl_ref, q_ref, kc_ref, vc_ref, oc_ref, idx_ref, *, t):
    G, dh = C_KV_HEADS, C_HEAD_DIM
    n = kc_ref.shape[0]
    n_slc = t // C_SEL_BLOCK + 1
    NS = -(-n_slc // LANES) * LANES
    cpos = lax.broadcasted_iota(jnp.int32, (1, n), 1) * C_CMP_STRIDE + (C_CMP_BLOCK - 1)
    mask = cpos <= t
    c_start = lax.broadcasted_iota(jnp.int32, (n, NS), 0) * C_CMP_STRIDE
    jb = lax.broadcasted_iota(jnp.int32, (n, NS), 1)
    cover = ((c_start < (jb + 1) * C_SEL_BLOCK) & (c_start + C_CMP_BLOCK > jb * C_SEL_BLOCK)).astype(f32)
    blk = lax.broadcasted_iota(jnp.int32, (1, NS), 1)
    cur = t // C_SEL_BLOCK
    forced = (blk == 0) | (blk == cur) | (blk == cur - 1)
    lane = lax.broadcasted_iota(jnp.int32, (1, LANES), 1)
    o_parts, idx_rows = [], []
    for g in range(G):
        q8 = jnp.concatenate([q_ref[:, (g * C_HPG + hh) * dh:(g * C_HPG + hh + 1) * dh] for hh in range(C_HPG)]
                             + [jnp.zeros((8 - C_HPG, dh), f32)], axis=0)
        slope = jnp.concatenate([jnp.full((1, 1), sl_ref[g * C_HPG + hh], f32) for hh in range(C_HPG)]
                                + [jnp.zeros((8 - C_HPG, 1), f32)], axis=0)
        s = _dot_t(q8, kc_ref[:, g * dh:(g + 1) * dh], HIGHEST) * dh ** -0.5
        s = s - slope * (float(t) - cpos.astype(f32))
        s = jnp.where(mask, s, NEG_INF)
        m = jnp.max(s, axis=-1, keepdims=True)
        e = jnp.where(mask, jnp.exp(s - m), 0.0)
        p = e / jnp.maximum(jnp.sum(e, axis=-1, keepdims=True), TINY)
        o = _hdot(p, vc_ref[:, g * dh:(g + 1) * dh])
        o_parts += [o[hh:hh + 1] for hh in range(C_HPG)]
        imp = _hdot(jnp.sum(p[0:C_HPG], axis=0, keepdims=True), cover)
        idx_rows.append(jnp.where(blk * C_SEL_BLOCK <= t, jnp.where(forced, FORCE, imp), NEG_INF))
    oc_ref[...] = jnp.concatenate(o_parts, axis=-1)
    score = jnp.concatenate(idx_rows + [jnp.full((8 - G, NS), NEG_INF, f32)], axis=0)
    idx = jnp.zeros((8, LANES), jnp.int32)
    for r in range(C_TOPK):
        mx = jnp.max(score, axis=-1, keepdims=True)
        pick = jnp.min(jnp.where(score == mx, blk, NS), axis=-1, keepdims=True)
        idx = jnp.where(lane == r, pick, idx)
        score = jnp.where(blk == pick, REMOVED, score)
    idx_ref[...] = idx


def nsa_cmp_topk_sample(proj, kc, vc, t):
    Bs = proj.shape[0]
    n, W = kc.shape[1:]
    HD = C_HEADS * C_HEAD_DIM
    return pl.pallas_call(
        functools.partial(_cmp_topk_sample_body, t=t),
        out_shape=[jax.ShapeDtypeStruct((Bs, 1, HD), f32), jax.ShapeDtypeStruct((Bs, 8, LANES), jnp.int32)],
        grid_spec=pltpu.PrefetchScalarGridSpec(
            num_scalar_prefetch=1,
            grid=(Bs,),
            in_specs=[pl.BlockSpec((None, 1, HD), lambda b, sl: (b, 0, 0)),
                      pl.BlockSpec((None, n, W), lambda b, sl: (b, 0, 0)),
                      pl.BlockSpec((None, n, W), lambda b, sl: (b, 0, 0))],
            out_specs=[pl.BlockSpec((None, 1, HD), lambda b, sl: (b, 0, 0)),
                       pl.BlockSpec((None, 8, LANES), lambda b, sl: (b, 0, 0))],
        ),
        compiler_params=_params("parallel"),
        name="nsa_cmp_topk_sample",
    )(jnp.asarray(SLOPES_C), proj, kc, vc)


def _sel_win_sample_body(pt_ref, ix_ref, sl_ref, *refs, t):
    blocks = refs[:C_TOPK]
    (win_ref, q_ref, ksn_ref, vsn_ref, kwn_ref, vwn_ref, gate_ref, bg_ref, oc_ref, o_ref) = refs[C_TOPK:]
    b = pl.program_id(0)
    g = pl.program_id(1)
    G, dh = C_KV_HEADS, C_HEAD_DIM
    scale = dh ** -0.5
    n_past = t // C_SEL_BLOCK
    per_page = PAGE_SIZE // C_SEL_BLOCK
    q8 = jnp.concatenate([q_ref[:, hh * dh:(hh + 1) * dh] for hh in range(C_HPG)]
                         + [jnp.zeros((8 - C_HPG, dh), f32)], axis=0)
    q8b = q8.astype(bf16)
    slope = jnp.concatenate([jnp.full((1, 1), sl_ref[g * C_HPG + hh], f32) for hh in range(C_HPG)]
                            + [jnp.zeros((8 - C_HPG, 1), f32)], axis=0)

    def own(ref):
        return functools.reduce(jnp.add, [jnp.where(g == gg, ref[:, gg * dh:(gg + 1) * dh], 0.0) for gg in range(G)])

    def softmax_pv(scores, values_t, s_new, v_new):
        m = functools.reduce(jnp.maximum, [jnp.max(s, axis=-1, keepdims=True) for s in scores] + [s_new])
        e_new = jnp.exp(s_new - m)
        den = e_new
        acc = e_new * v_new
        for s, vt in zip(scores, values_t):
            e = jnp.exp(s - m)
            den = den + jnp.sum(e, axis=-1, keepdims=True)
            acc = acc + _dot_t(e.astype(bf16), vt.astype(bf16))
        return acc / den

    lane = lax.broadcasted_iota(jnp.int32, (1, PAGE_SIZE), 1)
    scores, values_t = [], []
    for s_i in range(C_TOPK):
        j = ix_ref[b, g * C_TOPK + s_i]
        jc = jnp.minimum(j, n_past - 1)
        kpos = (jc // per_page) * PAGE_SIZE + lane
        sc = jnp.dot(q8b, blocks[s_i][0].astype(bf16), preferred_element_type=f32) * scale \
            - slope * (t - kpos).astype(f32)
        ok = ((lane >> C_SEL_SHIFT) == (jc % per_page)) & (j < n_past)
        scores.append(jnp.where(ok, sc, NEG_INF))
        values_t.append(blocks[s_i][1])
    s_new = jnp.sum(q8 * own(ksn_ref), axis=-1, keepdims=True) * scale
    o_s = softmax_pv(scores, values_t, s_new, own(vsn_ref))

    nw = win_ref.shape[-1]
    wdist = (nw - lax.broadcasted_iota(jnp.int32, (1, nw), 1)).astype(f32)
    sw = jnp.dot(q8b, win_ref[0].astype(bf16), preferred_element_type=f32) * scale - slope * wdist
    s_new = jnp.sum(q8 * own(kwn_ref), axis=-1, keepdims=True) * scale
    o_w = softmax_pv([sw], [win_ref[1]], s_new, own(vwn_ref))

    gs = jax.nn.sigmoid(gate_ref[...] + bg_ref[...])
    glane = lax.broadcasted_iota(jnp.int32, (1, LANES), 1)
    outs = []
    for hh in range(C_HPG):
        gate = [jnp.sum(jnp.where(glane == (g * C_HPG + hh) * 3 + r, gs, 0.0), axis=-1, keepdims=True)
                for r in range(3)]
        outs.append(gate[0] * oc_ref[:, hh * dh:(hh + 1) * dh] + gate[1] * o_s[hh:hh + 1] + gate[2] * o_w[hh:hh + 1])
    o_ref[...] = jnp.concatenate(outs, axis=-1).astype(o_ref.dtype)


def nsa_sel_win_sample(page_table, sel_idx, cache_t, win_t, layer, proj, b_gate_p, o_c, t):
    Bs = proj.shape[0]
    G, dh = C_KV_HEADS, C_HEAD_DIM
    GD = G * dh
    assert win_t.shape[-1] == C_WINDOW and t % C_SEL_BLOCK == 0
    n_past = t // C_SEL_BLOCK
    per_page = PAGE_SIZE // C_SEL_BLOCK

    def blk_spec(s_i):
        def imap(b, g, pt, ix, sl):
            j = jnp.minimum(ix[b, g * C_TOPK + s_i], n_past - 1)
            return (layer, pt[b, j // per_page], 1, g, 0, 0)
        return pl.BlockSpec((None, None, 2, None, dh, PAGE_SIZE), imap)

    row = lambda w, blk: pl.BlockSpec((None, 1, w), lambda b, g, pt, ix, sl, blk=blk: (b, 0, blk))
    base = C_HEADS * dh // GD
    gate_blk = (C_HEADS * dh + 6 * GD) // LANES
    return pl.pallas_call(
        functools.partial(_sel_win_sample_body, t=t),
        out_shape=jax.ShapeDtypeStruct((Bs, 1, C_HEADS * dh), bf16),
        grid_spec=pltpu.PrefetchScalarGridSpec(
            num_scalar_prefetch=3,
            grid=(Bs, G),
            in_specs=[blk_spec(s_i) for s_i in range(C_TOPK)] + [
                pl.BlockSpec((None, None, 2, None, dh, C_WINDOW), lambda b, g, pt, ix, sl: (layer, b, 0, g, 0, 0)),
                pl.BlockSpec((None, 1, C_HPG * dh), lambda b, g, pt, ix, sl: (b, 0, g)),
                row(GD, base + 2), row(GD, base + 3), row(GD, base + 4), row(GD, base + 5),
                row(LANES, gate_blk),
                pl.BlockSpec((1, LANES), lambda b, g, pt, ix, sl: (0, 0)),
                pl.BlockSpec((None, 1, C_HPG * dh), lambda b, g, pt, ix, sl: (b, 0, g)),
            ],
            out_specs=pl.BlockSpec((None, 1, C_HPG * dh), lambda b, g, pt, ix, sl: (b, 0, g)),
        ),
        compiler_params=_params("parallel", "parallel"),
        name="nsa_sel_win_sample",
    )(page_table, sel_idx, jnp.asarray(SLOPES_C), *([cache_t] * C_TOPK), win_t, proj, proj, proj, proj, proj,
      proj, b_gate_p, o_c)


def _norm_body(x_ref, g_ref, o_ref):
    o_ref[...] = _rms(x_ref[...], g_ref[...])


def rmsnorm_rows(x, g, *, tm=1024):
    M, D = x.shape
    tm = min(tm, M)
    return pl.pallas_call(
        _norm_body,
        out_shape=jax.ShapeDtypeStruct((M, D), f32),
        grid=(M // tm,),
        in_specs=[pl.BlockSpec((tm, D), lambda i: (i, 0)), pl.BlockSpec((1, D), lambda i: (0, 0))],
        out_specs=pl.BlockSpec((tm, D), lambda i: (i, 0)),
        compiler_params=_params("parallel"),
        name="rmsnorm",
    )(x, g.reshape(1, D))


def _position_minor(cache):
    return jnp.transpose(cache, (0, 1, 3, 4, 5, 2))


def kernel(x_prompt, x_sample, cache_dil_w128, cache_dil_w512, cache_dil_w2048, state_gla, cache_nsa_win,
           cache_nsa_kv, cache_mem_kv, page_table, mem_prompt, g_mix, g_cross, g_mem, g_ffn, g_final,
           w_a_qkv, w_a_o, w_b_in, w_b_gate2, b_b_gate, g_b_head, w_b_o, w_c_in, b_c_gate, c_pos_k, c_pos_v,
           w_c_k1, w_c_k2, w_c_v1, w_c_v2, w_c_o, w_x_q, w_x_kv, w_x_o, w_ffn_in, w_ffn_out):
    Bp, L, D = x_prompt.shape
    Bs = x_sample.shape[0]
    depth = g_mix.shape[0]
    n_pages = page_table.shape[1]
    t_s = n_pages * PAGE_SIZE
    G, dh = C_KV_HEADS, C_HEAD_DIM
    GD = G * dh
    HK, HV = B_HEADS * B_DK, B_HEADS * B_DV
    cast = lambda a: a.astype(bf16)

    wa_qkv, wa_o = cast(w_a_qkv), cast(w_a_o)
    nb = 2 * HK + HV
    wb_in = cast(jnp.concatenate(
        [w_b_in[..., :nb], w_b_in[..., nb + B_GATE_RANK:], w_b_in[..., nb:nb + B_GATE_RANK],
         jnp.zeros(w_b_in.shape[:2] + (LANES - B_GATE_RANK,), f32)], axis=-1))
    wb_gate2 = jnp.pad(w_b_gate2, ((0, 0), (0, LANES - B_GATE_RANK), (0, 0)))
    wb_o = cast(w_b_o)
    nc = -(-w_c_in.shape[-1] // LANES) * LANES
    wc_in = cast(jnp.pad(w_c_in, ((0, 0), (0, 0), (0, nc - w_c_in.shape[-1]))))
    bc_gate = jnp.pad(b_c_gate, ((0, 0), (0, LANES - b_c_gate.shape[-1])))
    half = C_CMP_STRIDE * dh
    two_chunk = lambda w: cast(jnp.concatenate([w[:, :half], w[:, half:]], axis=-1))
    wc_k1, wc_v1 = two_chunk(w_c_k1), two_chunk(w_c_v1)
    wc_k2, wc_v2, wc_o = cast(w_c_k2), cast(w_c_v2), cast(w_c_o)
    pos_k = c_pos_k.reshape(-1, 2, half)
    pos_v = c_pos_v.reshape(-1, 2, half)
    wx_q, wx_kv, wx_o = cast(w_x_q), cast(w_x_kv), cast(w_x_o)
    wf_in, wf_out = cast(w_ffn_in), cast(w_ffn_out)

    Mp = Bp * L
    x = x_prompt.reshape(Mp, D)
    mem2d = mem_prompt.reshape(-1, D)
    dil_p = [[] for _ in range(A_GROUPS)]
    gla_p, rows_p, win_p, mem_p = [], [], [], []
    for l in range(depth):
        kind, j = l % N_MIXERS, l // N_MIXERS
        if kind == 0:
            qkv = mm(x, (wa_qkv, j), g=g_mix[l]).reshape(Bp, L, -1)
            outs = [dilated_group(qkv, grp) for grp in range(A_GROUPS)]
            x = dilated_merge_out([o.reshape(Mp, A_OUT) for o, _ in outs],
                                  [s.reshape(Mp, A_OUT) for _, s in outs], (wa_o, j), x)
            sect = A_GROUPS * A_OUT
            for grp in range(A_GROUPS):
                w = min(A_WINDOWS[grp], L)
                tail = lambda t: qkv[:, L - w:, t * sect + grp * A_OUT:t * sect + (grp + 1) * A_OUT].reshape(
                    Bp, w, A_SLOTS, A_HEAD_DIM)
                dil_p[grp].append(jnp.stack([tail(1), tail(2)], axis=2))
        elif kind == 1:
            proj = mm(x, (wb_in, j), g=g_mix[l]).reshape(Bp, L, -1)
            y, S = gla_prompt(proj, wb_gate2[j], b_b_gate[j], g_b_head[j].reshape(-1),
                              jnp.zeros((Bp, B_HEADS, B_DK, B_DV), f32))
            gla_p.append(S)
            x = mm(y.reshape(Mp, HV), (wb_o, j), res=x)
        else:
            proj = mm(x, (wc_in, j), g=g_mix[l]).reshape(Bp, L, -1)
            q_w = C_HEADS * dh
            rows = proj[..., q_w:q_w + 6 * GD].reshape(Bp, L, 6, G, dh)
            chunks = lambda a: a.reshape(Bp, L // C_CMP_STRIDE, C_CMP_STRIDE, G, dh).transpose(
                0, 3, 1, 2, 4).reshape(Bp, G, L // C_CMP_STRIDE, half)
            kc, vc = nsa_compress(chunks(rows[:, :, 0]), chunks(rows[:, :, 1]), pos_k[j], pos_v[j],
                                  wc_k1[j], wc_k2[j], wc_v1[j], wc_v2[j])
            seqs = [cast(rows[:, :, r].transpose(0, 2, 1, 3)) for r in range(2, 6)]
            o = nsa_attention_prompt(proj, bc_gate[j:j + 1], kc, vc, *seqs)
            x = mm(o.reshape(Mp, q_w), (wc_o, j), res=x)
            rows_p.append(rows[:, :, :4])
            win_p.append(rows[:, L - min(C_WINDOW, L):, 4:])
        mem_kv = mm(mem2d, (wx_kv, l), g=g_mem[l]).reshape(Bp, -1, 2 * D)
        mem_p.append(mem_kv.reshape(Bp, -1, 2, MEM_HEADS, D // MEM_HEADS))
        x = xattn_prompt(x.reshape(Bp, L, D), g_cross[l], (wx_q, l), cast(mem_kv), (wx_o, l)).reshape(Mp, D)
        x = ffn(x, g_ffn[l], (wf_in, l), (wf_out, l))
    y_prompt = rmsnorm_rows(x, g_final).reshape(Bp, L, D)

    x = x_sample.reshape(Bs, D)
    dil_t = [_position_minor(c) for c in (cache_dil_w128, cache_dil_w512, cache_dil_w2048)]
    nsa_t = _position_minor(cache_nsa_kv)
    win_t = _position_minor(cache_nsa_win)
    dil_s = [[] for _ in range(A_GROUPS)]
    gla_s, rows_s, win_s = [], [], []
    for l in range(depth):
        kind, j = l % N_MIXERS, l // N_MIXERS
        if kind == 0:
            qkv = mm(x, (wa_qkv, j), g=g_mix[l]).reshape(Bs, 1, -1)
            o = dil_sample(qkv, dil_t, j)
            x = mm(o.transpose(0, 2, 1).reshape(Bs, A_OUT), (wa_o, j), res=x)
            kv = qkv.reshape(Bs, 1, 3, A_GROUPS, A_SLOTS, A_HEAD_DIM)
            for grp in range(A_GROUPS):
                dil_s[grp].append(jnp.stack([kv[:, :, 1, grp], kv[:, :, 2, grp]], axis=2))
        elif kind == 1:
            proj = mm(x, (wb_in, j), g=g_mix[l]).reshape(Bs, 1, -1)
            y, S = gla_step(proj, wb_gate2[j], b_b_gate[j], g_b_head[j].reshape(-1), state_gla, j)
            gla_s.append(S)
            x = mm(y.reshape(Bs, HV), (wb_o, j), res=x)
        else:
            proj = mm(x, (wc_in, j), g=g_mix[l]).reshape(Bs, 1, -1)
            q_w = C_HEADS * dh
            kc, vc = nsa_compress_sample(page_table, nsa_t, j, pos_k[j], pos_v[j],
                                         wc_k1[j], wc_k2[j], wc_v1[j], wc_v2[j])
            o_c, idx = nsa_cmp_topk_sample(proj, kc, vc, t_s)
            sel_idx = idx[:, :G, :C_TOPK].reshape(Bs, G * C_TOPK)
            o = nsa_sel_win_sample(page_table, sel_idx, nsa_t, win_t, j, proj, bc_gate[j:j + 1], o_c, t_s)
            x = mm(o.reshape(Bs, q_w), (wc_o, j), res=x)
            rows = proj[..., q_w:q_w + 6 * GD].reshape(Bs, 1, 6, G, dh)
            rows_s.append(rows[:, :, :4])
            win_s.append(rows[:, :, 4:])
        q = mm(x, (wx_q, l), g=g_cross[l]).reshape(Bs, 1, D)
        o = xattn_sample(q, cache_mem_kv, l)
        x = mm(o.reshape(Bs, D), (wx_o, l), res=x)
        x = ffn(x, g_ffn[l], (wf_in, l), (wf_out, l))
    y_sample = rmsnorm_rows(x, g_final).reshape(Bs, 1, D)

    st = jnp.stack
    return (y_prompt, y_sample, st(dil_p[0]), st(dil_s[0]), st(dil_p[1]), st(dil_s[1]), st(dil_p[2]), st(dil_s[2]),
            st(gla_p), st(gla_s), st(win_p), st(win_s), st(rows_p), st(rows_s), st(mem_p))
```

```python
import functools

import jax
import jax.numpy as jnp
import numpy as np
from jax import lax
from jax.experimental import pallas as pl
from jax.experimental.pallas import tpu as pltpu

f32 = jnp.float32
bf16 = jnp.bfloat16

N_MIXERS = 3
A_WINDOWS = (128, 512, 2048)
A_DILATIONS = (1, 4, 16)
A_GROUPS = 3
A_SLOTS = 8
A_HEAD_DIM = 64
A_OUT = A_SLOTS * A_HEAD_DIM
A_WIN_STEPS = 128

B_HEADS = 4
B_DK = 128
B_DV = 256
B_GATE_RANK = 16
B_GATE_TAU = 16.0

C_HEADS = 16
C_KV_HEADS = 4
C_HPG = C_HEADS // C_KV_HEADS
C_HEAD_DIM = 64
C_CMP_BLOCK = 32
C_CMP_STRIDE = 16
C_SEL_BLOCK = 64
C_SEL_SHIFT = 6
C_TOPK = 16
C_WINDOW = 512
C_CMP_HIDDEN = 128

MEM_HEADS = 4
PAGE_SIZE = 128

RMS_EPS = 1e-6
NEG_INF = -1e30
FORCE = 1e30
TINY = 1e-30
REMOVED = -3e38

LANES = 128
VMEM_LIMIT_BYTES = 56 * 1024 * 1024
HIGHEST = lax.Precision.HIGHEST


def _alibi_slopes(n):
    return np.asarray(2.0 ** (-8.0 * np.arange(1, n + 1) / n), dtype=np.float32)


def _bf16_pieces(x, n):
    out, rest = [], np.asarray(x, np.float32)
    for _ in range(n):
        p = rest.astype(bf16).astype(np.float32)
        out.append(p)
        rest = (rest - p).astype(np.float32)
    return np.stack(out, axis=-1)


SLOPES_A = _alibi_slopes(A_GROUPS * A_SLOTS).reshape(A_GROUPS, A_SLOTS)
SLOPES_C = _alibi_slopes(C_HEADS)
N_PIECES = 3
SLOPES_C_PIECES = _bf16_pieces(SLOPES_C, N_PIECES).reshape(-1)
AUG = 2 * N_PIECES


def _params(*sem):
    return pltpu.CompilerParams(dimension_semantics=sem, vmem_limit_bytes=VMEM_LIMIT_BYTES)


def _rms(x, g):
    return x * lax.rsqrt(jnp.mean(x * x, axis=-1, keepdims=True) + RMS_EPS) * g


def _dot_t(a, b, precision=None):
    return lax.dot_general(a, b, (((1,), (1,)), ((), ())), precision=precision, preferred_element_type=f32)


def _hdot(a, b):
    return jnp.dot(a, b, precision=HIGHEST, preferred_element_type=f32)


def _weight_spec(w, block, index):
    if isinstance(w, tuple):
        stack, layer = w
        return stack, pl.BlockSpec((None,) + block, lambda *ids: (layer,) + index(*ids))
    return w, pl.BlockSpec(block, index)


def _weight_shape(w):
    return w[0].shape[1:] if isinstance(w, tuple) else w.shape


def _pick_tile(n, target):
    best = LANES
    for t in range(LANES, min(n, target) + 1, LANES):
        if n % t == 0:
            best = t
    return best


def _mm_body(*refs, norm, res):
    it = iter(refs)
    x_ref = next(it)
    g_ref = next(it) if norm else None
    w_ref = next(it)
    r_ref = next(it) if res else None
    o_ref = next(it)
    xn_ref = next(it)

    @pl.when(pl.program_id(1) == 0)
    def _():
        x = x_ref[...].astype(f32)
        if norm:
            x = _rms(x, g_ref[...])
        xn_ref[...] = x.astype(bf16)

    acc = jnp.dot(xn_ref[...], w_ref[...], preferred_element_type=f32)
    if res:
        acc = acc + r_ref[...]
    o_ref[...] = acc.astype(o_ref.dtype)


def mm(x, w, *, g=None, res=None, out_dtype=f32, tm=1024, tn=1024):
    M, K = x.shape
    N = _weight_shape(w)[1]
    tm = min(tm, M)
    tn = _pick_tile(N, tn)
    assert M % tm == 0 and N % tn == 0
    args = [x]
    specs = [pl.BlockSpec((tm, K), lambda i, j: (i, 0))]
    if g is not None:
        args.append(g.reshape(1, K))
        specs.append(pl.BlockSpec((1, K), lambda i, j: (0, 0)))
    w_arr, w_spec = _weight_spec(w, (K, tn), lambda i, j: (0, j))
    args.append(w_arr)
    specs.append(w_spec)
    if res is not None:
        args.append(res)
        specs.append(pl.BlockSpec((tm, tn), lambda i, j: (i, j)))
    return pl.pallas_call(
        functools.partial(_mm_body, norm=g is not None, res=res is not None),
        out_shape=jax.ShapeDtypeStruct((M, N), out_dtype),
        grid=(M // tm, N // tn),
        in_specs=specs,
        out_specs=pl.BlockSpec((tm, tn), lambda i, j: (i, j)),
        scratch_shapes=[pltpu.VMEM((tm, K), bf16)],
        compiler_params=_params("parallel", "arbitrary"),
        name="mm",
    )(*args)


def _ffn_body(x_ref, g_ref, wa_ref, wb_ref, wo_ref, o_ref, xn_ref, acc_ref):
    j = pl.program_id(1)

    @pl.when(j == 0)
    def _():
        xn_ref[...] = _rms(x_ref[...], g_ref[...]).astype(bf16)
        acc_ref[...] = jnp.zeros_like(acc_ref)

    xn = xn_ref[...]
    a = jnp.dot(xn, wa_ref[...], preferred_element_type=f32)
    b = jnp.dot(xn, wb_ref[...], preferred_element_type=f32)
    h = (a * jax.nn.sigmoid(a) * b).astype(bf16)
    acc_ref[...] += jnp.dot(h, wo_ref[...], preferred_element_type=f32)

    @pl.when(j == pl.num_programs(1) - 1)
    def _():
        o_ref[...] = x_ref[...] + acc_ref[...]


def ffn(x, g, w_in, w_out, *, tm=1024, th=704):
    M, D = x.shape
    F = _weight_shape(w_out)[0]
    tm = min(tm, M)
    th = _pick_tile(F, th)
    nh = F // th
    wa, wa_spec = _weight_spec(w_in, (D, th), lambda i, j: (0, j))
    wb, wb_spec = _weight_spec(w_in, (D, th), lambda i, j: (0, j + nh))
    wo, wo_spec = _weight_spec(w_out, (th, D), lambda i, j: (j, 0))
    return pl.pallas_call(
        _ffn_body,
        out_shape=jax.ShapeDtypeStruct((M, D), f32),
        grid=(M // tm, nh),
        in_specs=[
            pl.BlockSpec((tm, D), lambda i, j: (i, 0)),
            pl.BlockSpec((1, D), lambda i, j: (0, 0)),
            wa_spec, wb_spec, wo_spec,
        ],
        out_specs=pl.BlockSpec((tm, D), lambda i, j: (i, 0)),
        scratch_shapes=[pltpu.VMEM((tm, D), bf16), pltpu.VMEM((tm, D), f32)],
        compiler_params=_params("parallel", "arbitrary"),
        name="ffn",
    )(x, g.reshape(1, D), wa, wb, wo)


def _xattn_body(x_ref, g_ref, wq_ref, kv_ref, wo_ref, o_ref, *, heads):
    x = x_ref[...]
    D = x.shape[-1]
    dh = D // heads
    xn = _rms(x, g_ref[...]).astype(bf16)
    q = jnp.dot(xn, wq_ref[...], preferred_element_type=f32).astype(bf16)
    outs = []
    for h in range(heads):
        kh = kv_ref[:, h * dh:(h + 1) * dh]
        vh = kv_ref[:, D + h * dh:D + (h + 1) * dh]
        s = _dot_t(q[:, h * dh:(h + 1) * dh], kh) * dh ** -0.5
        m = jnp.max(s, axis=-1, keepdims=True)
        e = jnp.exp(s - m)
        p = e / jnp.sum(e, axis=-1, keepdims=True)
        outs.append(jnp.dot(p.astype(bf16), vh, preferred_element_type=f32))
    o = jnp.concatenate(outs, axis=-1).astype(bf16)
    o_ref[...] = x + jnp.dot(o, wo_ref[...], preferred_element_type=f32)


def xattn_prompt(x, g, wq, kv, wo, *, tq=512):
    B, L, D = x.shape
    Mem = kv.shape[1]
    tq = min(tq, L)
    wq, wq_spec = _weight_spec(wq, (D, D), lambda b, i: (0, 0))
    wo, wo_spec = _weight_spec(wo, (D, D), lambda b, i: (0, 0))
    return pl.pallas_call(
        functools.partial(_xattn_body, heads=MEM_HEADS),
        out_shape=jax.ShapeDtypeStruct((B, L, D), f32),
        grid=(B, L // tq),
        in_specs=[
            pl.BlockSpec((None, tq, D), lambda b, i: (b, i, 0)),
            pl.BlockSpec((1, D), lambda b, i: (0, 0)),
            wq_spec,
            pl.BlockSpec((None, Mem, 2 * D), lambda b, i: (b, 0, 0)),
            wo_spec,
        ],
        out_specs=pl.BlockSpec((None, tq, D), lambda b, i: (b, i, 0)),
        compiler_params=_params("parallel", "parallel"),
        name="xattn_prompt",
    )(x, g.reshape(1, D), wq, kv, wo)


def _dil_body(sl_ref, q_ref, kc_ref, kp_ref, vc_ref, vp_ref, o_ref, lse_ref, *, grp, dil, tq):
    i = pl.program_id(1)
    c = pl.program_id(2)
    ws = A_WIN_STEPS
    scale = A_HEAD_DIM ** -0.5
    row = lax.broadcasted_iota(jnp.int32, (ws, 2 * ws), 0)
    col = lax.broadcasted_iota(jnp.int32, (ws, 2 * ws), 1)
    dist = row + ws - col
    in_band = (dist >= 0) & (dist <= ws)
    valid_first = in_band & ((col >= ws) | (i > 0))
    distf = (dist * dil).astype(f32)
    lane = lax.broadcasted_iota(jnp.int32, (1, LANES), 1)
    first = lane < A_HEAD_DIM
    slopes = [sl_ref[grp * A_SLOTS + 2 * c + half] for half in range(2)]

    def rows(ref, r, start, n):
        return ref[pl.ds(r + start * dil, n, stride=dil), :] if dil > 1 else ref[start:start + n, :]

    def residue(r, carry):
        for j in range(tq // ws):
            q = rows(q_ref, r, j * ws, ws).astype(bf16)
            if j == 0:
                k = jnp.concatenate([rows(kp_ref, r, 0, ws), rows(kc_ref, r, 0, ws)], axis=0)
                v = jnp.concatenate([rows(vp_ref, r, 0, ws), rows(vc_ref, r, 0, ws)], axis=0)
            else:
                k = rows(kc_ref, r, (j - 1) * ws, 2 * ws)
                v = rows(vc_ref, r, (j - 1) * ws, 2 * ws)
            k, v = k.astype(bf16), v.astype(bf16)
            valid = valid_first if j == 0 else in_band
            halves = []
            for half in range(2):
                qm = jnp.where(first == (half == 0), q, jnp.zeros_like(q))
                s = _dot_t(qm, k) * scale - slopes[half] * distf
                s = jnp.where(valid, s, NEG_INF)
                m = jnp.max(s, axis=-1, keepdims=True)
                e = jnp.exp(s - m)
                den = jnp.sum(e, axis=-1, keepdims=True)
                pv = jnp.dot(e.astype(bf16), v, preferred_element_type=f32) / den
                halves.append((pv, m + jnp.log(den)))
            o_t = jnp.where(first, halves[0][0], halves[1][0])
            l_t = jnp.where(first, halves[0][1], halves[1][1])
            if dil > 1:
                o_ref[pl.ds(r + j * ws * dil, ws, stride=dil), :] = o_t
                lse_ref[pl.ds(r + j * ws * dil, ws, stride=dil), :] = l_t
            else:
                o_ref[j * ws:(j + 1) * ws, :] = o_t
                lse_ref[j * ws:(j + 1) * ws, :] = l_t
        return carry

    unroll = min(dil, DIL_UNROLL)

    def trip(u, carry):
        for k in range(unroll):
            residue(u * unroll + k, carry)
        return carry

    if dil > unroll:
        lax.fori_loop(0, dil // unroll, trip, 0)
    else:
        trip(0, 0)


DIL_TQ = (1024, 256, 128)
DIL_UNROLL = 4


def dilated_group(qkv, grp):
    B, L, W3 = qkv.shape
    d = A_DILATIONS[grp]
    tq = min(DIL_TQ[grp], L // d)
    rows_blk = tq * d
    prev_blk = A_WIN_STEPS * d
    assert L % rows_blk == 0 and tq % A_WIN_STEPS == 0
    ratio = tq // A_WIN_STEPS
    tiles = A_OUT // LANES
    sect = A_GROUPS * tiles

    def spec(t, prev=False):
        col = lambda c: t * sect + grp * tiles + c
        if prev:
            return pl.BlockSpec((None, prev_blk, LANES),
                                lambda b, i, c, sl: (b, jnp.maximum(i * ratio - 1, 0), col(c)))
        return pl.BlockSpec((None, rows_blk, LANES), lambda b, i, c, sl: (b, i, col(c)))

    out_spec = pl.BlockSpec((None, rows_blk, LANES), lambda b, i, c, sl: (b, i, c))
    return pl.pallas_call(
        functools.partial(_dil_body, grp=grp, dil=d, tq=tq),
        out_shape=[jax.ShapeDtypeStruct((B, L, A_OUT), f32)] * 2,
        grid_spec=pltpu.PrefetchScalarGridSpec(
            num_scalar_prefetch=1,
            grid=(B, L // rows_blk, tiles),
            in_specs=[spec(0), spec(1), spec(1, True), spec(2), spec(2, True)],
            out_specs=[out_spec, out_spec],
        ),
        compiler_params=_params("parallel", "parallel", "parallel"),
        name=f"dilated_g{grp}",
    )(jnp.asarray(SLOPES_A.reshape(-1)), qkv, qkv, qkv, qkv, qkv)


def _dil_out_body(o0, o1, o2, l0, l1, l2, w_ref, r_ref, out_ref):
    ls = [l0[...], l1[...], l2[...]]
    m = jnp.maximum(jnp.maximum(ls[0], ls[1]), ls[2])
    es = [jnp.exp(l - m) for l in ls]
    den = es[0] + es[1] + es[2]
    o = (es[0] / den) * o0[...] + (es[1] / den) * o1[...] + (es[2] / den) * o2[...]
    out_ref[...] = r_ref[...] + jnp.dot(o.astype(bf16), w_ref[...], preferred_element_type=f32)


def dilated_merge_out(os, lses, w_o, res, *, tm=512):
    M, D = res.shape
    tm = min(tm, M)
    row = lambda n: pl.BlockSpec((tm, n), lambda i: (i, 0))
    w_o, w_spec = _weight_spec(w_o, (A_OUT, D), lambda i: (0, 0))
    return pl.pallas_call(
        _dil_out_body,
        out_shape=jax.ShapeDtypeStruct((M, D), f32),
        grid=(M // tm,),
        in_specs=[row(A_OUT)] * 6 + [w_spec, row(D)],
        out_specs=row(D),
        compiler_params=_params("parallel"),
        name="dilated_merge_out",
    )(*os, *lses, w_o, res)


GLA_SUB = 16


def _log_sigmoid(z):
    return -(jnp.maximum(-z, 0.0) + jnp.log1p(jnp.exp(-jnp.abs(z))))


def _gla_body(q_ref, k_ref, v_ref, r_ref, gl_ref, w2_ref, bg_ref, gh_ref, s0_ref, y_ref, s_ref, a_ref):
    C = q_ref.shape[0]
    c = GLA_SUB

    @pl.when(pl.program_id(1) == 0)
    def _():
        s_ref[...] = s0_ref[...]

    z = _hdot(gl_ref[...], w2_ref[...]) + bg_ref[...]
    la = _log_sigmoid(z) / B_GATE_TAU
    rowC = lax.broadcasted_iota(jnp.int32, (C, C), 0)
    colC = lax.broadcasted_iota(jnp.int32, (C, C), 1)
    tri = (rowC >= colC).astype(f32)
    b_all = _hdot(tri, la)
    row_k = lax.broadcasted_iota(jnp.int32, (C, B_DK), 0)
    col_c = lax.broadcasted_iota(jnp.int32, (c, C), 1)
    row_c = lax.broadcasted_iota(jnp.int32, (c, 1), 0)

    for h in range(B_HEADS):
        bh = b_all[:, h * B_DK:(h + 1) * B_DK]
        qh = q_ref[:, h * B_DK:(h + 1) * B_DK] * B_DK ** -0.5
        kh = k_ref[:, h * B_DK:(h + 1) * B_DK]
        vh = v_ref[:, h * B_DV:(h + 1) * B_DV].astype(bf16)
        S = s_ref[h]
        o = jnp.dot((qh * jnp.exp(bh)).astype(bf16), S.astype(bf16), preferred_element_type=f32)

        for I in range(C // c):
            r0 = I * c
            qI, kI, bI = qh[r0:r0 + c], kh[r0:r0 + c], bh[r0:r0 + c]
            if I == 0:
                A_I = jnp.zeros((c, C), f32)
            else:
                beta = bh[r0 - 1:r0]
                qt = qI * jnp.exp(bI - beta)
                kt = kh * jnp.exp(jnp.where(row_k < r0, beta - bh, 0.0))
                A_I = jnp.where(col_c < r0, _dot_t(qt.astype(bf16), kt.astype(bf16)), 0.0)
            for j in range(c):
                ex = jnp.exp(jnp.minimum(bI - bI[j:j + 1], 0.0))
                tj = jnp.sum(qI * kI[j:j + 1] * ex, axis=-1, keepdims=True)
                tj = jnp.where(row_c >= j, tj, 0.0)
                A_I = jnp.where(col_c == r0 + j, tj, A_I)
            a_ref[r0:r0 + c, :] = A_I
        o = o + jnp.dot(a_ref[...].astype(bf16), vh, preferred_element_type=f32)

        b_last = bh[C - 1:C]
        kdec = jnp.transpose(kh * jnp.exp(b_last - bh)).astype(bf16)
        decay = jnp.transpose(jnp.broadcast_to(jnp.exp(b_last), (8, B_DK)))[:, 0:1]
        s_ref[h] = decay * S + jnp.dot(kdec, vh, preferred_element_type=f32)

        on = _rms(o, gh_ref[:, h * B_DV:(h + 1) * B_DV])
        rh = r_ref[:, h * B_DV:(h + 1) * B_DV]
        y_ref[:, h * B_DV:(h + 1) * B_DV] = (on * (rh * jax.nn.sigmoid(rh))).astype(y_ref.dtype)


def gla_prompt(proj, w_gate2p, b_gate, g_head, s0, *, chunk=128):
    B, L, _ = proj.shape
    HK, HV = B_HEADS * B_DK, B_HEADS * B_DV
    C = min(chunk, L)
    assert L % C == 0 and C % GLA_SUB == 0
    return pl.pallas_call(
        _gla_body,
        out_shape=[jax.ShapeDtypeStruct((B, L, HV), bf16),
                   jax.ShapeDtypeStruct((B, B_HEADS, B_DK, B_DV), f32)],
        grid=(B, L // C),
        in_specs=[
            pl.BlockSpec((None, C, HK), lambda b, i: (b, i, 0)),
            pl.BlockSpec((None, C, HK), lambda b, i: (b, i, 1)),
            pl.BlockSpec((None, C, HV), lambda b, i: (b, i, 1)),
            pl.BlockSpec((None, C, HV), lambda b, i: (b, i, 2)),
            pl.BlockSpec((None, C, LANES), lambda b, i: (b, i, (2 * HK + 2 * HV) // LANES)),
            pl.BlockSpec((LANES, HK), lambda b, i: (0, 0)),
            pl.BlockSpec((1, HK), lambda b, i: (0, 0)),
            pl.BlockSpec((1, HV), lambda b, i: (0, 0)),
            pl.BlockSpec((None, B_HEADS, B_DK, B_DV), lambda b, i: (b, 0, 0, 0)),
        ],
        out_specs=[pl.BlockSpec((None, C, HV), lambda b, i: (b, i, 0)),
                   pl.BlockSpec((None, B_HEADS, B_DK, B_DV), lambda b, i: (b, 0, 0, 0))],
        scratch_shapes=[pltpu.VMEM((C, C), f32)],
        compiler_params=_params("parallel", "arbitrary"),
        name="gla_prompt",
    )(proj, proj, proj, proj, proj, w_gate2p, b_gate.reshape(1, HK), g_head.reshape(1, HV), s0)


def _cmp_mlp_tail(z, posb, w2_ref):
    n = z.shape[0]
    H = C_CMP_HIDDEN
    hid = z[:, :H] + pltpu.roll(z[:, H:], n - 1, 0) + posb
    return jnp.dot(jax.nn.gelu(hid).astype(bf16), w2_ref[...], preferred_element_type=f32)


def _cmp_pos_bias(p_ref, w1_ref):
    H = C_CMP_HIDDEN
    pz = jnp.dot(p_ref[...].astype(bf16), w1_ref[...], preferred_element_type=f32)
    return pz[0:1, :H] + pz[1:2, H:]


def _cmp_body(xk_ref, xv_ref, pk_ref, pv_ref, wk1_ref, wk2_ref, wv1_ref, wv2_ref, ok_ref, ov_ref):
    for x_ref, p_ref, w1_ref, w2_ref, o_ref in ((xk_ref, pk_ref, wk1_ref, wk2_ref, ok_ref),
                                                (xv_ref, pv_ref, wv1_ref, wv2_ref, ov_ref)):
        z = jnp.dot(x_ref[...].astype(bf16), w1_ref[...], preferred_element_type=f32)
        o_ref[...] = _cmp_mlp_tail(z, _cmp_pos_bias(p_ref, w1_ref), w2_ref).astype(o_ref.dtype)


def nsa_compress(xk, xv, pos_k, pos_v, wk1, wk2, wv1, wv2):
    B, G, n, W = xk.shape
    x_spec = pl.BlockSpec((None, None, n, W), lambda b, g: (b, g, 0, 0))
    full = lambda a: pl.BlockSpec(a.shape, lambda b, g: (0,) * a.ndim)
    o_spec = pl.BlockSpec((None, None, n, C_HEAD_DIM), lambda b, g: (b, g, 0, 0))
    consts = [pos_k, pos_v, wk1, wk2, wv1, wv2]
    return pl.pallas_call(
        _cmp_body,
        out_shape=[jax.ShapeDtypeStruct((B, G, n, C_HEAD_DIM), bf16)] * 2,
        grid=(B, G),
        in_specs=[x_spec, x_spec] + [full(a) for a in consts],
        out_specs=[o_spec, o_spec],
        compiler_params=_params("parallel", "parallel"),
        name="nsa_compress",
    )(xk, xv, *consts)


def _topk_mask_t(score_t, blk_t):
    keep = jnp.zeros(score_t.shape, f32)
    big = float(score_t.shape[0])
    for _ in range(C_TOPK):
        mx = jnp.max(score_t, axis=0, keepdims=True)
        first = jnp.min(jnp.where(score_t == mx, blk_t, big), axis=0, keepdims=True)
        hit = blk_t == first
        keep = jnp.where(hit, jnp.where(mx > 0.5 * NEG_INF, 1.0, 0.0), keep)
        score_t = jnp.where(hit, REMOVED, score_t)
    return keep


def _reduce_rows(x, op):
    slabs = [x[r:r + 8] for r in range(0, x.shape[0], 8)]
    while len(slabs) > 1:
        slabs = [op(slabs[k], slabs[k + 1]) if k + 1 < len(slabs) else slabs[k] for k in range(0, len(slabs), 2)]
    red = jnp.max if op is jnp.maximum else jnp.sum
    return red(slabs[0], axis=0, keepdims=True)


def _softmax_rows_t(s, mask=None):
    e = jnp.exp(s - _reduce_rows(s, jnp.maximum))
    if mask is not None:
        e = jnp.where(mask, e, 0.0)
    return e / jnp.maximum(_reduce_rows(e, jnp.add), TINY)


def _nsa_body(sp_ref, q_ref, gate_ref, bg_ref, kc_ref, vc_ref, ks_ref, vs_ref, kw_ref, vw_ref,
              o_ref, m_ref, l_ref, acc_ref, gs_ref, sa_ref, sb_ref, *, tk):
    g = pl.program_id(1)
    i = pl.program_id(2)
    tq = q_ref.shape[0]
    R = C_HPG * tq
    L = ks_ref.shape[0]
    n_cmp = kc_ref.shape[0]
    NS = LANES
    dh = C_HEAD_DIM
    q0 = i * tq

    q_t = jnp.transpose(q_ref[...] * dh ** -0.5)
    prow = lax.broadcasted_iota(jnp.int32, (LANES - dh, 1), 0)
    cols = []
    for hh in range(C_HPG):
        pc = jnp.zeros((LANES - dh, 1), f32)
        for n in reversed(range(N_PIECES)):
            pc = jnp.where(prow < 2 * n + 2, sp_ref[(g * C_HPG + hh) * N_PIECES + n], pc)
        cols.append(jnp.concatenate([q_t[hh * dh:(hh + 1) * dh], jnp.broadcast_to(pc, (LANES - dh, tq))], axis=0))
    qa_t = jnp.concatenate(cols, axis=1).astype(bf16)
    t1 = q0 + lax.broadcasted_iota(jnp.int32, (1, tq), 1)
    t = jnp.concatenate([t1] * C_HPG, axis=1)

    W = min(L, C_WINDOW + tq)
    w0 = pl.multiple_of(jnp.clip(q0 - C_WINDOW, 0, L - W), tq)
    wpos = w0 + lax.broadcasted_iota(jnp.int32, (W, 1), 0)
    s = jnp.dot(kw_ref[pl.ds(w0, W), :], qa_t, preferred_element_type=f32)
    s = jnp.where(wpos <= t, jnp.where(wpos >= t - C_WINDOW, s, NEG_INF), NEG_INF)
    e = jnp.exp(s - _reduce_rows(s, jnp.maximum))
    o_w = jnp.dot(vw_ref[:, pl.ds(w0, W)], e.astype(bf16), preferred_element_type=f32) / _reduce_rows(e, jnp.add)

    cpos = lax.broadcasted_iota(jnp.int32, (n_cmp, 1), 0) * C_CMP_STRIDE + (C_CMP_BLOCK - 1)
    mask = cpos <= t
    s = jnp.where(mask, jnp.dot(kc_ref[...], qa_t, preferred_element_type=f32), NEG_INF)
    p = _softmax_rows_t(s, mask)
    o_c = jnp.dot(vc_ref[...], p.astype(bf16), preferred_element_type=f32)

    psum = p[:, 0:tq]
    for hh in range(1, C_HPG):
        psum = psum + p[:, hh * tq:(hh + 1) * tq]
    n_blk = -(-(L // C_SEL_BLOCK) // 8) * 8
    jb = lax.broadcasted_iota(jnp.int32, (n_blk, n_cmp), 0)
    c_start = lax.broadcasted_iota(jnp.int32, (n_blk, n_cmp), 1) * C_CMP_STRIDE
    cover = jnp.where(c_start < (jb + 1) * C_SEL_BLOCK,
                      jnp.where(c_start + C_CMP_BLOCK > jb * C_SEL_BLOCK, 1.0, 0.0), 0.0).astype(bf16)
    imp = jnp.zeros((n_blk, tq), f32)
    rest = psum
    for _ in range(N_PIECES):
        piece = rest.astype(bf16)
        imp = imp + jnp.dot(cover, piece, preferred_element_type=f32)
        rest = rest - piece.astype(f32)
    blk = lax.broadcasted_iota(jnp.int32, (n_blk, tq), 0)
    cur = t1 >> C_SEL_SHIFT
    forced = (blk == 0) | (blk == cur) | (blk == cur - 1)
    score = jnp.where(blk * C_SEL_BLOCK <= t1, jnp.where(forced, FORCE, imp), NEG_INF)
    keep = _topk_mask_t(score, blk.astype(f32))
    neg1 = jnp.concatenate([jnp.where(keep > 0.5, jnp.where(blk == cur, NEG_INF, 0.0), NEG_INF),
                            jnp.full((NS - n_blk, tq), NEG_INF, f32)], axis=0).astype(bf16)
    qa2_t = jnp.concatenate([qa_t, jnp.concatenate([neg1] * C_HPG, axis=1)], axis=0)

    d0 = pl.multiple_of(q0, tq)
    kpos = q0 + lax.broadcasted_iota(jnp.int32, (tq, 1), 0)
    sc = jnp.dot(ks_ref[pl.ds(d0, tq), 0:LANES], qa_t, preferred_element_type=f32)
    sc = jnp.where(kpos <= t, jnp.where((kpos >> C_SEL_SHIFT) == (t >> C_SEL_SHIFT), sc, NEG_INF), NEG_INF)
    m0 = _reduce_rows(sc, jnp.maximum)
    ee = jnp.exp(sc - m0)
    m_ref[...] = m0
    l_ref[...] = _reduce_rows(ee, jnp.add)
    acc_ref[...] = jnp.dot(vs_ref[:, pl.ds(d0, tq)], ee.astype(bf16), preferred_element_type=f32)

    n_tiles = q0 // tk + 1

    def scores(kt, s_ref):
        k0 = pl.multiple_of(jnp.minimum(kt, n_tiles - 1) * tk, tk)
        s_ref[...] = jnp.dot(ks_ref[pl.ds(k0, tk), :], qa2_t, preferred_element_type=f32)

    def absorb(kt, s_ref):
        k0 = pl.multiple_of(kt * tk, tk)
        sc = s_ref[...]
        m_old = m_ref[...]
        m_new = jnp.maximum(m_old, _reduce_rows(sc, jnp.maximum))
        alpha = jnp.exp(m_old - m_new)
        ee = jnp.exp(sc - m_new)
        l_ref[...] = alpha * l_ref[...] + _reduce_rows(ee, jnp.add)
        acc_ref[...] = alpha * acc_ref[...] + jnp.dot(vs_ref[:, pl.ds(k0, tk)], ee.astype(bf16),
                                                      preferred_element_type=f32)
        m_ref[...] = m_new

    scores(0, sa_ref)

    def pair(j, carry):
        scores(2 * j + 1, sb_ref)
        absorb(2 * j, sa_ref)
        scores(2 * j + 2, sa_ref)
        absorb(2 * j + 1, sb_ref)
        return carry

    lax.fori_loop(0, n_tiles // 2, pair, 0)

    @pl.when(n_tiles % 2 == 1)
    def _():
        absorb(n_tiles - 1, sa_ref)

    o_s = acc_ref[...] / l_ref[...]

    gs_ref[...] = jnp.transpose(jax.nn.sigmoid(gate_ref[...] + bg_ref[...]))

    def gate(r):
        return jnp.concatenate([gs_ref[pl.ds((g * C_HPG + hh) * 3 + r, 1), :] for hh in range(C_HPG)], axis=1)

    o = gate(0) * o_c + gate(1) * o_s + gate(2) * o_w
    o = jnp.concatenate([o[:, hh * tq:(hh + 1) * tq] for hh in range(C_HPG)], axis=0)
    o_ref[...] = jnp.transpose(o).astype(o_ref.dtype)


def _aug_keys(k, pos, one_hot_blocks):
    B, G, n, dh = k.shape
    hi = (pos // C_SEL_BLOCK) * C_SEL_BLOCK
    cols = np.zeros((n, LANES - dh), np.float32)
    for j in range(N_PIECES):
        cols[:, 2 * j] = hi
        cols[:, 2 * j + 1] = pos - hi
    parts = [k, jnp.broadcast_to(jnp.asarray(cols, bf16), (B, G, n, LANES - dh))]
    if one_hot_blocks:
        oh = (pos[:, None] // C_SEL_BLOCK == np.arange(LANES)[None, :]).astype(np.float32)
        parts.append(jnp.broadcast_to(jnp.asarray(oh, bf16), (B, G, n, LANES)))
    return jnp.concatenate(parts, axis=-1)


def nsa_attention_prompt(proj, b_gate_p, kc, vc, ks, vs, kw, vw, *, tq=128, tk=512):
    B, L, _ = proj.shape
    G, dh = C_KV_HEADS, C_HEAD_DIM
    tq = min(tq, L)
    tk = min(tk, L)
    assert tk % tq == 0 and L % tk == 0 and L // C_SEL_BLOCK <= LANES
    R = C_HPG * tq
    n_cmp = kc.shape[2]
    kc_a = _aug_keys(kc, np.arange(n_cmp) * C_CMP_STRIDE + (C_CMP_BLOCK - 1), False)
    ks_a = _aug_keys(ks, np.arange(L), True)
    kw_a = _aug_keys(kw, np.arange(L), False)
    gate_blk = (C_HEADS * dh + 6 * G * dh) // LANES
    seq = lambda n, w: pl.BlockSpec((None, None, n, w), lambda b, g, i, sp: (b, g, 0, 0))
    tr = lambda a: jnp.swapaxes(a, 2, 3)
    return pl.pallas_call(
        functools.partial(_nsa_body, tk=tk),
        out_shape=jax.ShapeDtypeStruct((B, L, C_HEADS * dh), bf16),
        grid_spec=pltpu.PrefetchScalarGridSpec(
            num_scalar_prefetch=1,
            grid=(B, G, L // tq),
            in_specs=[
                pl.BlockSpec((None, tq, C_HPG * dh), lambda b, g, i, sp: (b, i, g)),
                pl.BlockSpec((None, tq, LANES), lambda b, g, i, sp: (b, i, gate_blk)),
                pl.BlockSpec((1, LANES), lambda b, g, i, sp: (0, 0)),
                seq(n_cmp, LANES), seq(dh, n_cmp), seq(L, 2 * LANES), seq(dh, L), seq(L, LANES), seq(dh, L),
            ],
            out_specs=pl.BlockSpec((None, tq, C_HPG * dh), lambda b, g, i, sp: (b, i, g)),
            scratch_shapes=[pltpu.VMEM((1, R), f32), pltpu.VMEM((1, R), f32), pltpu.VMEM((dh, R), f32),
                            pltpu.VMEM((LANES, tq), f32), pltpu.VMEM((tk, R), f32), pltpu.VMEM((tk, R), f32)],
        ),
        compiler_params=_params("parallel", "parallel", "parallel"),
        name="nsa_attention_prompt",
    )(jnp.asarray(SLOPES_C_PIECES), proj, proj, b_gate_p, kc_a, tr(vc), ks_a, tr(vs), kw_a, tr(vw))


def _slot_rows(ref, off, n, w):
    rows = jnp.concatenate([ref[:, off + s * w:off + (s + 1) * w] for s in range(n)]
                           + ([jnp.zeros((8 - n, w), f32)] if n < 8 else []), axis=0)
    return jnp.concatenate([rows, jnp.zeros((8, LANES - w), f32)], axis=-1) if w < LANES else rows


def _dil_sample_body(sl_ref, qkv_ref, c0_ref, c1_ref, c2_ref, o_ref):
    dh = A_HEAD_DIM
    scale = dh ** -0.5
    sect = A_GROUPS * A_OUT
    scores, news, v_cols = [], [], []
    for grp, c_ref in enumerate((c0_ref, c1_ref, c2_ref)):
        W = c_ref.shape[-1]
        d = A_DILATIONS[grp]
        q8 = _slot_rows(qkv_ref, grp * A_OUT, A_SLOTS, dh)
        kn8 = _slot_rows(qkv_ref, sect + grp * A_OUT, A_SLOTS, dh)
        vn8 = _slot_rows(qkv_ref, 2 * sect + grp * A_OUT, A_SLOTS, dh)
        q_cols = jnp.transpose(q8)[:dh]
        v_cols.append(jnp.transpose(vn8)[:dh])
        news.append(jnp.sum(q8 * kn8, axis=-1, keepdims=True) * scale)
        pos = lax.broadcasted_iota(jnp.int32, (1, W), 1)
        rows = [jnp.sum(c_ref[0, s] * q_cols[:, s:s + 1], axis=0, keepdims=True) for s in range(A_SLOTS)]
        slope = jnp.concatenate([jnp.full((1, 1), sl_ref[grp * A_SLOTS + s], f32) for s in range(A_SLOTS)], axis=0)
        sc = jnp.concatenate(rows, axis=0) * scale - slope * (W - pos).astype(f32)
        scores.append(jnp.where((pos & (d - 1)) == 0, sc, NEG_INF))
    m = functools.reduce(jnp.maximum, [jnp.max(s, axis=-1, keepdims=True) for s in scores] + news)
    den = jnp.zeros((A_SLOTS, 1), f32)
    acc = [jnp.zeros((dh, 1), f32) for _ in range(A_SLOTS)]
    for grp, c_ref in enumerate((c0_ref, c1_ref, c2_ref)):
        e = jnp.exp(scores[grp] - m)
        en = jnp.exp(news[grp] - m)
        den = den + jnp.sum(e, axis=-1, keepdims=True) + en
        for s in range(A_SLOTS):
            acc[s] = acc[s] + jnp.sum(c_ref[1, s] * e[s:s + 1, :], axis=-1, keepdims=True) \
                + en[s:s + 1, :] * v_cols[grp][:, s:s + 1]
    o_ref[...] = jnp.concatenate([acc[s] / den[s:s + 1, :] for s in range(A_SLOTS)], axis=-1)


def dil_sample(qkv, caches_t, layer):
    Bs = qkv.shape[0]
    for grp, c in enumerate(caches_t):
        assert c.shape[-1] == A_WINDOWS[grp]
    c_spec = lambda c: pl.BlockSpec((None, None) + c.shape[2:], lambda b, sl: (layer, b, 0, 0, 0, 0))
    return pl.pallas_call(
        _dil_sample_body,
        out_shape=jax.ShapeDtypeStruct((Bs, A_HEAD_DIM, A_SLOTS), f32),
        grid_spec=pltpu.PrefetchScalarGridSpec(
            num_scalar_prefetch=1,
            grid=(Bs,),
            in_specs=[pl.BlockSpec((None, 1, qkv.shape[-1]), lambda b, sl: (b, 0, 0))]
            + [c_spec(c) for c in caches_t],
            out_specs=pl.BlockSpec((None, A_HEAD_DIM, A_SLOTS), lambda b, sl: (b, 0, 0)),
        ),
        compiler_params=_params("parallel"),
        name="dil_sample",
    )(jnp.asarray(SLOPES_A.reshape(-1)), qkv, *caches_t)


def _xattn_sample_body(q_ref, c_ref, o_ref):
    dh = q_ref.shape[-1] // MEM_HEADS
    outs = []
    for h in range(MEM_HEADS):
        qh = q_ref[:, h * dh:(h + 1) * dh]
        s = jnp.sum(c_ref[:, 0, h, :] * qh, axis=-1, keepdims=True) * dh ** -0.5
        e = jnp.exp(s - jnp.max(s, axis=0, keepdims=True))
        outs.append(jnp.sum(e * c_ref[:, 1, h, :], axis=0, keepdims=True) / jnp.sum(e, axis=0, keepdims=True))
    o_ref[...] = jnp.concatenate(outs, axis=-1)


def xattn_sample(q, cache, layer):
    Bs, _, D = q.shape
    return pl.pallas_call(
        _xattn_sample_body,
        out_shape=jax.ShapeDtypeStruct((Bs, 1, D), f32),
        grid=(Bs,),
        in_specs=[pl.BlockSpec((None, 1, D), lambda b: (b, 0, 0)),
                  pl.BlockSpec((None, None) + cache.shape[2:], lambda b: (layer, b, 0, 0, 0, 0))],
        out_specs=pl.BlockSpec((None, 1, D), lambda b: (b, 0, 0)),
        compiler_params=_params("parallel"),
        name="xattn_sample",
    )(q, cache)


def _gla_step_body(q_ref, k_ref, v_ref, r_ref, gl_ref, w2_ref, bg_ref, gh_ref, s0_ref, y_ref, s_ref):
    z = _hdot(gl_ref[...], w2_ref[...]) + bg_ref[...]
    a = jnp.exp(_log_sigmoid(z) / B_GATE_TAU)
    pad = jnp.zeros((5, B_DK), f32)
    for h in range(B_HEADS):
        ks_ = slice(h * B_DK, (h + 1) * B_DK)
        vs_ = slice(h * B_DV, (h + 1) * B_DV)
        rows = jnp.concatenate([a[:, ks_], k_ref[:, ks_], q_ref[:, ks_] * B_DK ** -0.5, pad], axis=0)
        cols = jnp.transpose(rows)
        S = cols[:, 0:1] * s0_ref[h] + cols[:, 1:2] * v_ref[:, vs_]
        s_ref[h] = S
        o = jnp.sum(cols[:, 2:3] * S, axis=0, keepdims=True)
        rh = r_ref[:, vs_]
        y_ref[:, vs_] = (_rms(o, gh_ref[:, vs_]) * (rh * jax.nn.sigmoid(rh))).astype(y_ref.dtype)


def gla_step(proj, w_gate2p, b_gate, g_head, state, layer):
    Bs = proj.shape[0]
    HK, HV = B_HEADS * B_DK, B_HEADS * B_DV
    return pl.pallas_call(
        _gla_step_body,
        out_shape=[jax.ShapeDtypeStruct((Bs, 1, HV), bf16),
                   jax.ShapeDtypeStruct((Bs, B_HEADS, B_DK, B_DV), f32)],
        grid=(Bs,),
        in_specs=[
            pl.BlockSpec((None, 1, HK), lambda b: (b, 0, 0)),
            pl.BlockSpec((None, 1, HK), lambda b: (b, 0, 1)),
            pl.BlockSpec((None, 1, HV), lambda b: (b, 0, 1)),
            pl.BlockSpec((None, 1, HV), lambda b: (b, 0, 2)),
            pl.BlockSpec((None, 1, LANES), lambda b: (b, 0, (2 * HK + 2 * HV) // LANES)),
            pl.BlockSpec((LANES, HK), lambda b: (0, 0)),
            pl.BlockSpec((1, HK), lambda b: (0, 0)),
            pl.BlockSpec((1, HV), lambda b: (0, 0)),
            pl.BlockSpec((None, None, B_HEADS, B_DK, B_DV), lambda b: (layer, b, 0, 0, 0)),
        ],
        out_specs=[pl.BlockSpec((None, 1, HV), lambda b: (b, 0, 0)),
                   pl.BlockSpec((None, B_HEADS, B_DK, B_DV), lambda b: (b, 0, 0, 0))],
        compiler_params=_params("parallel"),
        name="gla_step",
    )(proj, proj, proj, proj, proj, w_gate2p, b_gate.reshape(1, HK), g_head.reshape(1, HV), state)


PAGES_PER_STEP = 8


def _cmp_sample_body(pt_ref, *refs):
    pages = refs[:PAGES_PER_STEP]
    (pk_ref, pv_ref, wk1_ref, wk2_ref, wv1_ref, wv2_ref, wkp_ref, wvp_ref,
     ok_ref, ov_ref, x_ref) = refs[PAGES_PER_STEP:]
    i = pl.program_id(1)
    H = C_CMP_HIDDEN
    n_tiles = x_ref.shape[0]
    per_kind = n_tiles // 2
    per_page = PAGE_SIZE // C_CMP_STRIDE
    out_row = lax.broadcasted_iota(jnp.int32, (PAGE_SIZE, PAGE_SIZE), 0)
    in_row = lax.broadcasted_iota(jnp.int32, (PAGE_SIZE, PAGE_SIZE), 1)
    shift = per_page.bit_length() - 1
    perm = (in_row == (out_row & (per_page - 1)) * C_CMP_STRIDE + (out_row >> shift)).astype(bf16)
    for k in range(PAGES_PER_STEP):
        c0 = pl.multiple_of((i * PAGES_PER_STEP + k) * per_page, per_page)
        for kind in range(2):
            for tile in range(per_kind):
                xt = pages[k][kind, 2 * tile:2 * tile + 2].reshape(LANES, PAGE_SIZE).astype(bf16)
                xs = _dot_t(perm, xt)
                for p in range(C_CMP_STRIDE):
                    x_ref[kind * per_kind + tile, p, pl.ds(c0, per_page), :] = xs[p * per_page:(p + 1) * per_page]

    @pl.when(i == pl.num_programs(1) - 1)
    def _():
        n = x_ref.shape[2]
        for kind, (p_ref, w1_ref, w2_ref, wp_ref, o_ref) in enumerate(
                ((pk_ref, wk1_ref, wk2_ref, wkp_ref, ok_ref), (pv_ref, wv1_ref, wv2_ref, wvp_ref, ov_ref))):
            posb = _cmp_pos_bias(p_ref, w1_ref)
            outs = []
            for tile in range(per_kind):
                ct = kind * per_kind + tile
                z2 = jnp.zeros((n, 4 * H), f32)
                for pp in range(C_CMP_STRIDE // 2):
                    xp = jnp.concatenate([x_ref[ct, 2 * pp], x_ref[ct, 2 * pp + 1]], axis=-1).astype(bf16)
                    z2 = z2 + jnp.dot(xp, wp_ref[pp], preferred_element_type=f32)
                outs += [_cmp_mlp_tail(z, posb, w2_ref) for z in (z2[:, :2 * H], z2[:, 2 * H:])]
            o_ref[...] = jnp.concatenate(outs, axis=-1)


def _pair_block_diag(w1):
    dh = C_HEAD_DIM
    w = w1.reshape(C_CMP_STRIDE, dh, w1.shape[1])
    z = jnp.zeros_like(w)
    bd = jnp.concatenate([jnp.concatenate([w, z], axis=-1), jnp.concatenate([z, w], axis=-1)], axis=1)
    return bd.reshape(C_CMP_STRIDE // 2, 4 * dh, 2 * w1.shape[1])


def nsa_compress_sample(page_table, cache_t, layer, pos_k, pos_v, wk1, wk2, wv1, wv2):
    Bs, n_pages = page_table.shape
    G, dh = C_KV_HEADS, C_HEAD_DIM
    assert n_pages % PAGES_PER_STEP == 0 and cache_t.shape[2:] == (4, G, dh, PAGE_SIZE) and 2 * dh == LANES
    n_rows = n_pages * PAGE_SIZE
    n = n_rows // C_CMP_STRIDE
    page = lambda k: pl.BlockSpec((None, None, 2, G, dh, PAGE_SIZE),
                                  lambda b, i, pt, k=k: (layer, pt[b, i * PAGES_PER_STEP + k], 0, 0, 0, 0))
    consts = [pos_k, pos_v, wk1, wk2, wv1, wv2, _pair_block_diag(wk1), _pair_block_diag(wv1)]
    full = lambda a: pl.BlockSpec(a.shape, lambda b, i, pt: (0,) * a.ndim)
    o_spec = pl.BlockSpec((None, n, G * dh), lambda b, i, pt: (b, 0, 0))
    return pl.pallas_call(
        _cmp_sample_body,
        out_shape=[jax.ShapeDtypeStruct((Bs, n, G * dh), f32)] * 2,
        grid_spec=pltpu.PrefetchScalarGridSpec(
            num_scalar_prefetch=1,
            grid=(Bs, n_pages // PAGES_PER_STEP),
            in_specs=[page(k) for k in range(PAGES_PER_STEP)] + [full(a) for a in consts],
            out_specs=[o_spec, o_spec],
            scratch_shapes=[pltpu.VMEM((2 * G * dh // LANES, C_CMP_STRIDE, n, LANES), f32)],
        ),
        compiler_params=_params("parallel", "arbitrary"),
        name="nsa_compress_sample",
    )(page_table, *([cache_t] * PAGES_PER_STEP), *consts)


def _cmp_topk_sample_body(sl_ref, q_ref, kc_ref, vc_ref, oc_ref, idx_ref, *, t):
    G, dh = C_KV_HEADS, C_HEAD_DIM
    n = kc_ref.shape[0]
    n_slc = t // C_SEL_BLOCK + 1
    NS = -(-n_slc // LANES) * LANES
    cpos = lax.broadcasted_iota(jnp.int32, (1, n), 1) * C_CMP_STRIDE + (C_CMP_BLOCK - 1)
    mask = cpos <= t
    c_start = lax.broadcasted_iota(jnp.int32, (n, NS), 0) * C_CMP_STRIDE
    jb = lax.broadcasted_iota(jnp.int32, (n, NS), 1)
    cover = jnp.where(c_start < (jb + 1) * C_SEL_BLOCK,
                      jnp.where(c_start + C_CMP_BLOCK > jb * C_SEL_BLOCK, 1.0, 0.0), 0.0).astype(bf16)
    blk = lax.broadcasted_iota(jnp.int32, (1, NS), 1)
    cur = t // C_SEL_BLOCK
    forced = (blk == 0) | (blk == cur) | (blk == cur - 1)
    lane = lax.broadcasted_iota(jnp.int32, (1, LANES), 1)
    o_parts, p_rows = [], []
    for g in range(G):
        q8 = jnp.concatenate([q_ref[:, (g * C_HPG + hh) * dh:(g * C_HPG + hh + 1) * dh] for hh in range(C_HPG)]
                             + [jnp.zeros((8 - C_HPG, dh), f32)], axis=0)
        slope = jnp.concatenate([jnp.full((1, 1), sl_ref[g * C_HPG + hh], f32) for hh in range(C_HPG)]
                                + [jnp.zeros((8 - C_HPG, 1), f32)], axis=0)
        s = _dot_t(q8.astype(bf16), kc_ref[:, g * dh:(g + 1) * dh].astype(bf16)) * dh ** -0.5
        s = s - slope * (float(t) - cpos.astype(f32))
        s = jnp.where(mask, s, NEG_INF)
        m = jnp.max(s, axis=-1, keepdims=True)
        e = jnp.where(mask, jnp.exp(s - m), 0.0)
        p = e / jnp.maximum(jnp.sum(e, axis=-1, keepdims=True), TINY)
        o = jnp.dot(p.astype(bf16), vc_ref[:, g * dh:(g + 1) * dh].astype(bf16), preferred_element_type=f32)
        o_parts += [o[hh:hh + 1] for hh in range(C_HPG)]
        p_rows.append(jnp.sum(p[0:C_HPG], axis=0, keepdims=True))
    oc_ref[...] = jnp.concatenate(o_parts, axis=-1)
    rest = jnp.concatenate(p_rows + [jnp.zeros((8 - G, n), f32)], axis=0)
    imp = jnp.zeros((8, NS), f32)
    for _ in range(N_PIECES):
        piece = rest.astype(bf16)
        imp = imp + jnp.dot(piece, cover, preferred_element_type=f32)
        rest = rest - piece.astype(f32)
    row = lax.broadcasted_iota(jnp.int32, (8, 1), 0)
    score = jnp.where((blk * C_SEL_BLOCK <= t) & (row < G), jnp.where(forced, FORCE, imp), NEG_INF)
    rank = jnp.zeros((8, NS), f32)
    for k in range(1, NS):
        other = pltpu.roll(score, k, 1)
        ahead = jnp.where(blk >= k, jnp.where(other >= score, 1.0, 0.0), jnp.where(other > score, 1.0, 0.0))
        rank = rank + ahead
    blk_f = blk.astype(f32)
    idx = jnp.zeros((8, LANES), f32)
    for r in range(C_TOPK):
        pick = jnp.sum(jnp.where(rank == float(r), blk_f, 0.0), axis=-1, keepdims=True)
        idx = jnp.where(lane == r, pick, idx)
    idx_ref[...] = idx.astype(jnp.int32)


def nsa_cmp_topk_sample(proj, kc, vc, t):
    Bs = proj.shape[0]
    n, W = kc.shape[1:]
    HD = C_HEADS * C_HEAD_DIM
    return pl.pallas_call(
        functools.partial(_cmp_topk_sample_body, t=t),
        out_shape=[jax.ShapeDtypeStruct((Bs, 1, HD), f32), jax.ShapeDtypeStruct((Bs, 8, LANES), jnp.int32)],
        grid_spec=pltpu.PrefetchScalarGridSpec(
            num_scalar_prefetch=1,
            grid=(Bs,),
            in_specs=[pl.BlockSpec((None, 1, HD), lambda b, sl: (b, 0, 0)),
                      pl.BlockSpec((None, n, W), lambda b, sl: (b, 0, 0)),
                      pl.BlockSpec((None, n, W), lambda b, sl: (b, 0, 0))],
            out_specs=[pl.BlockSpec((None, 1, HD), lambda b, sl: (b, 0, 0)),
                       pl.BlockSpec((None, 8, LANES), lambda b, sl: (b, 0, 0))],
        ),
        compiler_params=_params("parallel"),
        name="nsa_cmp_topk_sample",
    )(jnp.asarray(SLOPES_C), proj, kc, vc)


def _sel_win_sample_body(pt_ref, ix_ref, sl_ref, *refs, t):
    blocks = refs[:C_TOPK]
    (win_ref, q_ref, ksn_ref, vsn_ref, kwn_ref, vwn_ref, gate_ref, bg_ref, oc_ref, o_ref) = refs[C_TOPK:]
    b = pl.program_id(0)
    g = pl.program_id(1)
    G, dh = C_KV_HEADS, C_HEAD_DIM
    scale = dh ** -0.5
    n_past = t // C_SEL_BLOCK
    per_page = PAGE_SIZE // C_SEL_BLOCK
    q8 = jnp.concatenate([q_ref[:, hh * dh:(hh + 1) * dh] for hh in range(C_HPG)]
                         + [jnp.zeros((8 - C_HPG, dh), f32)], axis=0)
    q8b = q8.astype(bf16)
    slope = jnp.concatenate([jnp.full((1, 1), sl_ref[g * C_HPG + hh], f32) for hh in range(C_HPG)]
                            + [jnp.zeros((8 - C_HPG, 1), f32)], axis=0)

    def own(ref):
        return functools.reduce(jnp.add, [jnp.where(g == gg, ref[:, gg * dh:(gg + 1) * dh], 0.0) for gg in range(G)])

    def softmax_pv(scores, values_t, s_new, v_new):
        m = functools.reduce(jnp.maximum, [jnp.max(s, axis=-1, keepdims=True) for s in scores] + [s_new])
        e_new = jnp.exp(s_new - m)
        den = e_new
        acc = e_new * v_new
        for s, vt in zip(scores, values_t):
            e = jnp.exp(s - m)
            den = den + jnp.sum(e, axis=-1, keepdims=True)
            acc = acc + _dot_t(e.astype(bf16), vt.astype(bf16))
        return acc / den

    lane = lax.broadcasted_iota(jnp.int32, (1, PAGE_SIZE), 1)
    scores, values_t = [], []
    for s_i in range(C_TOPK):
        j = ix_ref[b, g * C_TOPK + s_i]
        jc = jnp.minimum(j, n_past - 1)
        kpos = (jc // per_page) * PAGE_SIZE + lane
        sc = jnp.dot(q8b, blocks[s_i][0].astype(bf16), preferred_element_type=f32) * scale \
            - slope * (t - kpos).astype(f32)
        ok = ((lane >> C_SEL_SHIFT) == (jc % per_page)) & (j < n_past)
        scores.append(jnp.where(ok, sc, NEG_INF))
        values_t.append(blocks[s_i][1])
    s_new = jnp.sum(q8 * own(ksn_ref), axis=-1, keepdims=True) * scale
    o_s = softmax_pv(scores, values_t, s_new, own(vsn_ref))

    nw = win_ref.shape[-1]
    wdist = (nw - lax.broadcasted_iota(jnp.int32, (1, nw), 1)).astype(f32)
    sw = jnp.dot(q8b, win_ref[0].astype(bf16), preferred_element_type=f32) * scale - slope * wdist
    s_new = jnp.sum(q8 * own(kwn_ref), axis=-1, keepdims=True) * scale
    o_w = softmax_pv([sw], [win_ref[1]], s_new, own(vwn_ref))

    gs = jax.nn.sigmoid(gate_ref[...] + bg_ref[...])
    glane = lax.broadcasted_iota(jnp.int32, (1, LANES), 1)
    outs = []
    for hh in range(C_HPG):
        gate = [jnp.sum(jnp.where(glane == (g * C_HPG + hh) * 3 + r, gs, 0.0), axis=-1, keepdims=True)
                for r in range(3)]
        outs.append(gate[0] * oc_ref[:, hh * dh:(hh + 1) * dh] + gate[1] * o_s[hh:hh + 1] + gate[2] * o_w[hh:hh + 1])
    o_ref[...] = jnp.concatenate(outs, axis=-1).astype(o_ref.dtype)


def nsa_sel_win_sample(page_table, sel_idx, cache_t, win_t, layer, proj, b_gate_p, o_c, t):
    Bs = proj.shape[0]
    G, dh = C_KV_HEADS, C_HEAD_DIM
    GD = G * dh
    assert win_t.shape[-1] == C_WINDOW and t % C_SEL_BLOCK == 0
    n_past = t // C_SEL_BLOCK
    per_page = PAGE_SIZE // C_SEL_BLOCK

    def blk_spec(s_i):
        def imap(b, g, pt, ix, sl):
            j = jnp.minimum(ix[b, g * C_TOPK + s_i], n_past - 1)
            return (layer, pt[b, j // per_page], 1, g, 0, 0)
        return pl.BlockSpec((None, None, 2, None, dh, PAGE_SIZE), imap)

    row = lambda w, blk: pl.BlockSpec((None, 1, w), lambda b, g, pt, ix, sl, blk=blk: (b, 0, blk))
    base = C_HEADS * dh // GD
    gate_blk = (C_HEADS * dh + 6 * GD) // LANES
    return pl.pallas_call(
        functools.partial(_sel_win_sample_body, t=t),
        out_shape=jax.ShapeDtypeStruct((Bs, 1, C_HEADS * dh), bf16),
        grid_spec=pltpu.PrefetchScalarGridSpec(
            num_scalar_prefetch=3,
            grid=(Bs, G),
            in_specs=[blk_spec(s_i) for s_i in range(C_TOPK)] + [
                pl.BlockSpec((None, None, 2, None, dh, C_WINDOW), lambda b, g, pt, ix, sl: (layer, b, 0, g, 0, 0)),
                pl.BlockSpec((None, 1, C_HPG * dh), lambda b, g, pt, ix, sl: (b, 0, g)),
                row(GD, base + 2), row(GD, base + 3), row(GD, base + 4), row(GD, base + 5),
                row(LANES, gate_blk),
                pl.BlockSpec((1, LANES), lambda b, g, pt, ix, sl: (0, 0)),
                pl.BlockSpec((None, 1, C_HPG * dh), lambda b, g, pt, ix, sl: (b, 0, g)),
            ],
            out_specs=pl.BlockSpec((None, 1, C_HPG * dh), lambda b, g, pt, ix, sl: (b, 0, g)),
        ),
        compiler_params=_params("parallel", "parallel"),
        name="nsa_sel_win_sample",
    )(page_table, sel_idx, jnp.asarray(SLOPES_C), *([cache_t] * C_TOPK), win_t, proj, proj, proj, proj, proj,
      proj, b_gate_p, o_c)


def _norm_body(x_ref, g_ref, o_ref):
    o_ref[...] = _rms(x_ref[...], g_ref[...])


def rmsnorm_rows(x, g, *, tm=1024):
    M, D = x.shape
    tm = min(tm, M)
    return pl.pallas_call(
        _norm_body,
        out_shape=jax.ShapeDtypeStruct((M, D), f32),
        grid=(M // tm,),
        in_specs=[pl.BlockSpec((tm, D), lambda i: (i, 0)), pl.BlockSpec((1, D), lambda i: (0, 0))],
        out_specs=pl.BlockSpec((tm, D), lambda i: (i, 0)),
        compiler_params=_params("parallel"),
        name="rmsnorm",
    )(x, g.reshape(1, D))


def _position_minor(cache):
    return jnp.transpose(cache, (0, 1, 3, 4, 5, 2))


def kernel(x_prompt, x_sample, cache_dil_w128, cache_dil_w512, cache_dil_w2048, state_gla, cache_nsa_win,
           cache_nsa_kv, cache_mem_kv, page_table, mem_prompt, g_mix, g_cross, g_mem, g_ffn, g_final,
           w_a_qkv, w_a_o, w_b_in, w_b_gate2, b_b_gate, g_b_head, w_b_o, w_c_in, b_c_gate, c_pos_k, c_pos_v,
           w_c_k1, w_c_k2, w_c_v1, w_c_v2, w_c_o, w_x_q, w_x_kv, w_x_o, w_ffn_in, w_ffn_out):
    Bp, L, D = x_prompt.shape
    Bs = x_sample.shape[0]
    depth = g_mix.shape[0]
    n_pages = page_table.shape[1]
    t_s = n_pages * PAGE_SIZE
    G, dh = C_KV_HEADS, C_HEAD_DIM
    GD = G * dh
    HK, HV = B_HEADS * B_DK, B_HEADS * B_DV
    cast = lambda a: a.astype(bf16)

    wa_qkv, wa_o = cast(w_a_qkv), cast(w_a_o)
    nb = 2 * HK + HV
    wb_in = cast(jnp.concatenate(
        [w_b_in[..., :nb], w_b_in[..., nb + B_GATE_RANK:], w_b_in[..., nb:nb + B_GATE_RANK],
         jnp.zeros(w_b_in.shape[:2] + (LANES - B_GATE_RANK,), f32)], axis=-1))
    wb_gate2 = jnp.pad(w_b_gate2, ((0, 0), (0, LANES - B_GATE_RANK), (0, 0)))
    wb_o = cast(w_b_o)
    nc = -(-w_c_in.shape[-1] // LANES) * LANES
    wc_in = cast(jnp.pad(w_c_in, ((0, 0), (0, 0), (0, nc - w_c_in.shape[-1]))))
    bc_gate = jnp.pad(b_c_gate, ((0, 0), (0, LANES - b_c_gate.shape[-1])))
    half = C_CMP_STRIDE * dh
    two_chunk = lambda w: cast(jnp.concatenate([w[:, :half], w[:, half:]], axis=-1))
    wc_k1, wc_v1 = two_chunk(w_c_k1), two_chunk(w_c_v1)
    wc_k2, wc_v2, wc_o = cast(w_c_k2), cast(w_c_v2), cast(w_c_o)
    pos_k = c_pos_k.reshape(-1, 2, half)
    pos_v = c_pos_v.reshape(-1, 2, half)
    wx_q, wx_kv, wx_o = cast(w_x_q), cast(w_x_kv), cast(w_x_o)
    wf_in, wf_out = cast(w_ffn_in), cast(w_ffn_out)

    Mp = Bp * L
    x = x_prompt.reshape(Mp, D)
    mem2d = mem_prompt.reshape(-1, D)
    dil_p = [[] for _ in range(A_GROUPS)]
    gla_p, rows_p, win_p, mem_p = [], [], [], []
    for l in range(depth):
        kind, j = l % N_MIXERS, l // N_MIXERS
        if kind == 0:
            qkv = mm(x, (wa_qkv, j), g=g_mix[l]).reshape(Bp, L, -1)
            outs = [dilated_group(qkv, grp) for grp in range(A_GROUPS)]
            x = dilated_merge_out([o.reshape(Mp, A_OUT) for o, _ in outs],
                                  [s.reshape(Mp, A_OUT) for _, s in outs], (wa_o, j), x)
            sect = A_GROUPS * A_OUT
            for grp in range(A_GROUPS):
                w = min(A_WINDOWS[grp], L)
                tail = lambda t: qkv[:, L - w:, t * sect + grp * A_OUT:t * sect + (grp + 1) * A_OUT].reshape(
                    Bp, w, A_SLOTS, A_HEAD_DIM)
                dil_p[grp].append(jnp.stack([tail(1), tail(2)], axis=2))
        elif kind == 1:
            proj = mm(x, (wb_in, j), g=g_mix[l]).reshape(Bp, L, -1)
            y, S = gla_prompt(proj, wb_gate2[j], b_b_gate[j], g_b_head[j].reshape(-1),
                              jnp.zeros((Bp, B_HEADS, B_DK, B_DV), f32))
            gla_p.append(S)
            x = mm(y.reshape(Mp, HV), (wb_o, j), res=x)
        else:
            proj = mm(x, (wc_in, j), g=g_mix[l]).reshape(Bp, L, -1)
            q_w = C_HEADS * dh
            rows = proj[..., q_w:q_w + 6 * GD].reshape(Bp, L, 6, G, dh)
            chunks = lambda a: a.reshape(Bp, L // C_CMP_STRIDE, C_CMP_STRIDE, G, dh).transpose(
                0, 3, 1, 2, 4).reshape(Bp, G, L // C_CMP_STRIDE, half)
            kc, vc = nsa_compress(chunks(rows[:, :, 0]), chunks(rows[:, :, 1]), pos_k[j], pos_v[j],
                                  wc_k1[j], wc_k2[j], wc_v1[j], wc_v2[j])
            seqs = [cast(rows[:, :, r].transpose(0, 2, 1, 3)) for r in range(2, 6)]
            o = nsa_attention_prompt(proj, bc_gate[j:j + 1], kc, vc, *seqs)
            x = mm(o.reshape(Mp, q_w), (wc_o, j), res=x)
            rows_p.append(rows[:, :, :4])
            win_p.append(rows[:, L - min(C_WINDOW, L):, 4:])
        mem_kv = mm(mem2d, (wx_kv, l), g=g_mem[l]).reshape(Bp, -1, 2 * D)
        mem_p.append(mem_kv.reshape(Bp, -1, 2, MEM_HEADS, D // MEM_HEADS))
        x = xattn_prompt(x.reshape(Bp, L, D), g_cross[l], (wx_q, l), cast(mem_kv), (wx_o, l)).reshape(Mp, D)
        x = ffn(x, g_ffn[l], (wf_in, l), (wf_out, l))
    y_prompt = rmsnorm_rows(x, g_final).reshape(Bp, L, D)

    x = x_sample.reshape(Bs, D)
    dil_t = [_position_minor(c) for c in (cache_dil_w128, cache_dil_w512, cache_dil_w2048)]
    nsa_t = _position_minor(cache_nsa_kv)
    win_t = _position_minor(cache_nsa_win)
    dil_s = [[] for _ in range(A_GROUPS)]
    gla_s, rows_s, win_s = [], [], []
    for l in range(depth):
        kind, j = l % N_MIXERS, l // N_MIXERS
        if kind == 0:
            qkv = mm(x, (wa_qkv, j), g=g_mix[l]).reshape(Bs, 1, -1)
            o = dil_sample(qkv, dil_t, j)
            x = mm(o.transpose(0, 2, 1).reshape(Bs, A_OUT), (wa_o, j), res=x)
            kv = qkv.reshape(Bs, 1, 3, A_GROUPS, A_SLOTS, A_HEAD_DIM)
            for grp in range(A_GROUPS):
                dil_s[grp].append(jnp.stack([kv[:, :, 1, grp], kv[:, :, 2, grp]], axis=2))
        elif kind == 1:
            proj = mm(x, (wb_in, j), g=g_mix[l]).reshape(Bs, 1, -1)
            y, S = gla_step(proj, wb_gate2[j], b_b_gate[j], g_b_head[j].reshape(-1), state_gla, j)
            gla_s.append(S)
            x = mm(y.reshape(Bs, HV), (wb_o, j), res=x)
        else:
            proj = mm(x, (wc_in, j), g=g_mix[l]).reshape(Bs, 1, -1)
            q_w = C_HEADS * dh
            kc, vc = nsa_compress_sample(page_table, nsa_t, j, pos_k[j], pos_v[j],
                                         wc_k1[j], wc_k2[j], wc_v1[j], wc_v2[j])
            o_c, idx = nsa_cmp_topk_sample(proj, kc, vc, t_s)
            sel_idx = idx[:, :G, :C_TOPK].reshape(Bs, G * C_TOPK)
            o = nsa_sel_win_sample(page_table, sel_idx, nsa_t, win_t, j, proj, bc_gate[j:j + 1], o_c, t_s)
            x = mm(o.reshape(Bs, q_w), (wc_o, j), res=x)
            rows = proj[..., q_w:q_w + 6 * GD].reshape(Bs, 1, 6, G, dh)
            rows_s.append(rows[:, :, :4])
            win_s.append(rows[:, :, 4:])
        q = mm(x, (wx_q, l), g=g_cross[l]).reshape(Bs, 1, D)
        o = xattn_sample(q, cache_mem_kv, l)
        x = mm(o.reshape(Bs, D), (wx_o, l), res=x)
        x = ffn(x, g_ffn[l], (wf_in, l), (wf_out, l))
    y_sample = rmsnorm_rows(x, g_final).reshape(Bs, 1, D)

    st = jnp.stack
    return (y_prompt, y_sample, st(dil_p[0]), st(dil_s[0]), st(dil_p[1]), st(dil_s[1]), st(dil_p[2]), st(dil_s[2]),
            st(gla_p), st(gla_s), st(win_p), st(win_s), st(rows_p), st(rows_s), st(mem_p))
```

```python
import functools

import jax
import jax.numpy as jnp
import numpy as np
from jax import lax
from jax.experimental import pallas as pl
from jax.experimental.pallas import tpu as pltpu

f32 = jnp.float32
bf16 = jnp.bfloat16

N_MIXERS = 3
A_WINDOWS = (128, 512, 2048)
A_DILATIONS = (1, 4, 16)
A_GROUPS = 3
A_SLOTS = 8
A_HEAD_DIM = 64
A_OUT = A_SLOTS * A_HEAD_DIM
A_WIN_STEPS = 128

B_HEADS = 4
B_DK = 128
B_DV = 256
B_GATE_RANK = 16
B_GATE_TAU = 16.0

C_HEADS = 16
C_KV_HEADS = 4
C_HPG = C_HEADS // C_KV_HEADS
C_HEAD_DIM = 64
C_CMP_BLOCK = 32
C_CMP_STRIDE = 16
C_SEL_BLOCK = 64
C_SEL_SHIFT = 6
C_TOPK = 16
C_WINDOW = 512
C_CMP_HIDDEN = 128

MEM_HEADS = 4
PAGE_SIZE = 128

RMS_EPS = 1e-6
NEG_INF = -1e30
FORCE = 1e30
TINY = 1e-30
REMOVED = -3e38

LANES = 128
VMEM_LIMIT_BYTES = 56 * 1024 * 1024
HIGHEST = lax.Precision.HIGHEST


def _alibi_slopes(n):
    return np.asarray(2.0 ** (-8.0 * np.arange(1, n + 1) / n), dtype=np.float32)


def _bf16_pieces(x, n):
    out, rest = [], np.asarray(x, np.float64)
    for _ in range(n):
        p = rest.astype(bf16).astype(np.float64)
        out.append(p)
        rest = rest - p
    return np.stack(out, axis=-1).astype(np.float32)


LOG2E = 1.4426950408889634
SLOPES_A = _alibi_slopes(A_GROUPS * A_SLOTS).reshape(A_GROUPS, A_SLOTS)
SLOPES_C = _alibi_slopes(C_HEADS)
N_PIECES = 3
SLOPE_PIECES = 4
SLOPES_C_PIECES = _bf16_pieces(SLOPES_C.astype(np.float64) * LOG2E, SLOPE_PIECES).reshape(-1)
VAL_ROWS = C_HEAD_DIM + 16


def _params(*sem):
    return pltpu.CompilerParams(dimension_semantics=sem, vmem_limit_bytes=VMEM_LIMIT_BYTES)


def _rms(x, g):
    return x * lax.rsqrt(jnp.mean(x * x, axis=-1, keepdims=True) + RMS_EPS) * g


def _dot_t(a, b, precision=None):
    return lax.dot_general(a, b, (((1,), (1,)), ((), ())), precision=precision, preferred_element_type=f32)


def _hdot(a, b):
    return jnp.dot(a, b, precision=HIGHEST, preferred_element_type=f32)


def _weight_spec(w, block, index):
    if isinstance(w, tuple):
        stack, layer = w
        return stack, pl.BlockSpec((None,) + block, lambda *ids: (layer,) + index(*ids))
    return w, pl.BlockSpec(block, index)


def _weight_shape(w):
    return w[0].shape[1:] if isinstance(w, tuple) else w.shape


def _pick_tile(n, target):
    best = LANES
    for t in range(LANES, min(n, target) + 1, LANES):
        if n % t == 0:
            best = t
    return best


def _mm_body(*refs, norm, res):
    it = iter(refs)
    x_ref = next(it)
    g_ref = next(it) if norm else None
    w_ref = next(it)
    r_ref = next(it) if res else None
    o_ref = next(it)
    xn_ref = next(it)

    @pl.when(pl.program_id(1) == 0)
    def _():
        x = x_ref[...].astype(f32)
        if norm:
            x = _rms(x, g_ref[...])
        xn_ref[...] = x.astype(bf16)

    acc = jnp.dot(xn_ref[...], w_ref[...], preferred_element_type=f32)
    if res:
        acc = acc + r_ref[...]
    o_ref[...] = acc.astype(o_ref.dtype)


def mm(x, w, *, g=None, res=None, out_dtype=f32, tm=1024, tn=1024):
    M, K = x.shape
    N = _weight_shape(w)[1]
    tm = min(tm, M)
    tn = _pick_tile(N, tn)
    assert M % tm == 0 and N % tn == 0
    args = [x]
    specs = [pl.BlockSpec((tm, K), lambda i, j: (i, 0))]
    if g is not None:
        args.append(g.reshape(1, K))
        specs.append(pl.BlockSpec((1, K), lambda i, j: (0, 0)))
    w_arr, w_spec = _weight_spec(w, (K, tn), lambda i, j: (0, j))
    args.append(w_arr)
    specs.append(w_spec)
    if res is not None:
        args.append(res)
        specs.append(pl.BlockSpec((tm, tn), lambda i, j: (i, j)))
    return pl.pallas_call(
        functools.partial(_mm_body, norm=g is not None, res=res is not None),
        out_shape=jax.ShapeDtypeStruct((M, N), out_dtype),
        grid=(M // tm, N // tn),
        in_specs=specs,
        out_specs=pl.BlockSpec((tm, tn), lambda i, j: (i, j)),
        scratch_shapes=[pltpu.VMEM((tm, K), bf16)],
        compiler_params=_params("parallel", "arbitrary"),
        name="mm",
    )(*args)


def _ffn_body(x_ref, g_ref, wa_ref, wb_ref, wo_ref, o_ref, xn_ref, acc_ref):
    j = pl.program_id(1)

    @pl.when(j == 0)
    def _():
        xn_ref[...] = _rms(x_ref[...], g_ref[...]).astype(bf16)
        acc_ref[...] = jnp.zeros_like(acc_ref)

    xn = xn_ref[...]
    a = jnp.dot(xn, wa_ref[...], preferred_element_type=f32)
    b = jnp.dot(xn, wb_ref[...], preferred_element_type=f32)
    h = (a * jax.nn.sigmoid(a) * b).astype(bf16)
    acc_ref[...] += jnp.dot(h, wo_ref[...], preferred_element_type=f32)

    @pl.when(j == pl.num_programs(1) - 1)
    def _():
        o_ref[...] = x_ref[...] + acc_ref[...]


def ffn(x, g, w_in, w_out, *, tm=1024, th=704):
    M, D = x.shape
    F = _weight_shape(w_out)[0]
    tm = min(tm, M)
    th = _pick_tile(F, th)
    nh = F // th
    wa, wa_spec = _weight_spec(w_in, (D, th), lambda i, j: (0, j))
    wb, wb_spec = _weight_spec(w_in, (D, th), lambda i, j: (0, j + nh))
    wo, wo_spec = _weight_spec(w_out, (th, D), lambda i, j: (j, 0))
    return pl.pallas_call(
        _ffn_body,
        out_shape=jax.ShapeDtypeStruct((M, D), f32),
        grid=(M // tm, nh),
        in_specs=[
            pl.BlockSpec((tm, D), lambda i, j: (i, 0)),
            pl.BlockSpec((1, D), lambda i, j: (0, 0)),
            wa_spec, wb_spec, wo_spec,
        ],
        out_specs=pl.BlockSpec((tm, D), lambda i, j: (i, 0)),
        scratch_shapes=[pltpu.VMEM((tm, D), bf16), pltpu.VMEM((tm, D), f32)],
        compiler_params=_params("parallel", "arbitrary"),
        name="ffn",
    )(x, g.reshape(1, D), wa, wb, wo)


def _xattn_body(x_ref, g_ref, wq_ref, kv_ref, wo_ref, o_ref, *, heads):
    x = x_ref[...]
    D = x.shape[-1]
    dh = D // heads
    xn = _rms(x, g_ref[...]).astype(bf16)
    q = jnp.dot(xn, wq_ref[...], preferred_element_type=f32).astype(bf16)
    outs = []
    for h in range(heads):
        kh = kv_ref[:, h * dh:(h + 1) * dh]
        vh = kv_ref[:, D + h * dh:D + (h + 1) * dh]
        s = _dot_t(q[:, h * dh:(h + 1) * dh], kh) * dh ** -0.5
        m = jnp.max(s, axis=-1, keepdims=True)
        e = jnp.exp(s - m)
        outs.append(jnp.dot(e.astype(bf16), vh, preferred_element_type=f32) / jnp.sum(e, axis=-1, keepdims=True))
    o = jnp.concatenate(outs, axis=-1).astype(bf16)
    o_ref[...] = x + jnp.dot(o, wo_ref[...], preferred_element_type=f32)


def xattn_prompt(x, g, wq, kv, wo, *, tq=512):
    B, L, D = x.shape
    Mem = kv.shape[1]
    tq = min(tq, L)
    wq, wq_spec = _weight_spec(wq, (D, D), lambda b, i: (0, 0))
    wo, wo_spec = _weight_spec(wo, (D, D), lambda b, i: (0, 0))
    return pl.pallas_call(
        functools.partial(_xattn_body, heads=MEM_HEADS),
        out_shape=jax.ShapeDtypeStruct((B, L, D), f32),
        grid=(B, L // tq),
        in_specs=[
            pl.BlockSpec((None, tq, D), lambda b, i: (b, i, 0)),
            pl.BlockSpec((1, D), lambda b, i: (0, 0)),
            wq_spec,
            pl.BlockSpec((None, Mem, 2 * D), lambda b, i: (b, 0, 0)),
            wo_spec,
        ],
        out_specs=pl.BlockSpec((None, tq, D), lambda b, i: (b, i, 0)),
        compiler_params=_params("parallel", "parallel"),
        name="xattn_prompt",
    )(x, g.reshape(1, D), wq, kv, wo)


def _dil_body(sl_ref, q_ref, kc_ref, kp_ref, vc_ref, vp_ref, o_ref, lse_ref, *, grp, dil, tq):
    i = pl.program_id(1)
    c = pl.program_id(2)
    ws = A_WIN_STEPS
    scale = A_HEAD_DIM ** -0.5
    row = lax.broadcasted_iota(jnp.int32, (ws, 2 * ws), 0)
    col = lax.broadcasted_iota(jnp.int32, (ws, 2 * ws), 1)
    dist = row + ws - col
    in_band = (dist >= 0) & (dist <= ws)
    valid_first = in_band & ((col >= ws) | (i > 0))
    distf = (dist * dil).astype(f32)
    lane = lax.broadcasted_iota(jnp.int32, (1, LANES), 1)
    first = lane < A_HEAD_DIM
    slopes = [sl_ref[grp * A_SLOTS + 2 * c + half] for half in range(2)]

    def rows(ref, r, start, n):
        return ref[pl.ds(r + start * dil, n, stride=dil), :] if dil > 1 else ref[start:start + n, :]

    def residue(r, carry):
        for j in range(tq // ws):
            q = rows(q_ref, r, j * ws, ws).astype(bf16)
            if j == 0:
                k = jnp.concatenate([rows(kp_ref, r, 0, ws), rows(kc_ref, r, 0, ws)], axis=0)
                v = jnp.concatenate([rows(vp_ref, r, 0, ws), rows(vc_ref, r, 0, ws)], axis=0)
            else:
                k = rows(kc_ref, r, (j - 1) * ws, 2 * ws)
                v = rows(vc_ref, r, (j - 1) * ws, 2 * ws)
            k, v = k.astype(bf16), v.astype(bf16)
            valid = valid_first if j == 0 else in_band
            halves = []
            for half in range(2):
                qm = jnp.where(first == (half == 0), q, jnp.zeros_like(q))
                s = _dot_t(qm, k) * scale - slopes[half] * distf
                s = jnp.where(valid, s, NEG_INF)
                m = jnp.max(s, axis=-1, keepdims=True)
                e = jnp.exp(s - m)
                den = jnp.sum(e, axis=-1, keepdims=True)
                pv = jnp.dot(e.astype(bf16), v, preferred_element_type=f32) / den
                halves.append((pv, m + jnp.log(den)))
            o_t = jnp.where(first, halves[0][0], halves[1][0])
            l_t = jnp.where(first, halves[0][1], halves[1][1])
            if dil > 1:
                o_ref[pl.ds(r + j * ws * dil, ws, stride=dil), :] = o_t
                lse_ref[pl.ds(r + j * ws * dil, ws, stride=dil), :] = l_t
            else:
                o_ref[j * ws:(j + 1) * ws, :] = o_t
                lse_ref[j * ws:(j + 1) * ws, :] = l_t
        return carry

    unroll = min(dil, DIL_UNROLL)

    def trip(u, carry):
        for k in range(unroll):
            residue(u * unroll + k, carry)
        return carry

    if dil > unroll:
        lax.fori_loop(0, dil // unroll, trip, 0)
    else:
        trip(0, 0)


DIL_TQ = (1024, 256, 128)
DIL_UNROLL = 4


def dilated_group(qkv, grp):
    B, L, W3 = qkv.shape
    d = A_DILATIONS[grp]
    tq = min(DIL_TQ[grp], L // d)
    rows_blk = tq * d
    prev_blk = A_WIN_STEPS * d
    assert L % rows_blk == 0 and tq % A_WIN_STEPS == 0
    ratio = tq // A_WIN_STEPS
    tiles = A_OUT // LANES
    sect = A_GROUPS * tiles

    def spec(t, prev=False):
        col = lambda c: t * sect + grp * tiles + c
        if prev:
            return pl.BlockSpec((None, prev_blk, LANES),
                                lambda b, i, c, sl: (b, jnp.maximum(i * ratio - 1, 0), col(c)))
        return pl.BlockSpec((None, rows_blk, LANES), lambda b, i, c, sl: (b, i, col(c)))

    out_spec = pl.BlockSpec((None, rows_blk, LANES), lambda b, i, c, sl: (b, i, c))
    return pl.pallas_call(
        functools.partial(_dil_body, grp=grp, dil=d, tq=tq),
        out_shape=[jax.ShapeDtypeStruct((B, L, A_OUT), f32)] * 2,
        grid_spec=pltpu.PrefetchScalarGridSpec(
            num_scalar_prefetch=1,
            grid=(B, L // rows_blk, tiles),
            in_specs=[spec(0), spec(1), spec(1, True), spec(2), spec(2, True)],
            out_specs=[out_spec, out_spec],
        ),
        compiler_params=_params("parallel", "parallel", "parallel"),
        name=f"dilated_g{grp}",
    )(jnp.asarray(SLOPES_A.reshape(-1)), qkv, qkv, qkv, qkv, qkv)


def _dil_out_body(o0, o1, o2, l0, l1, l2, w_ref, r_ref, out_ref):
    ls = [l0[...], l1[...], l2[...]]
    m = jnp.maximum(jnp.maximum(ls[0], ls[1]), ls[2])
    es = [jnp.exp(l - m) for l in ls]
    den = es[0] + es[1] + es[2]
    o = (es[0] / den) * o0[...] + (es[1] / den) * o1[...] + (es[2] / den) * o2[...]
    out_ref[...] = r_ref[...] + jnp.dot(o.astype(bf16), w_ref[...], preferred_element_type=f32)


def dilated_merge_out(os, lses, w_o, res, *, tm=512):
    M, D = res.shape
    tm = min(tm, M)
    row = lambda n: pl.BlockSpec((tm, n), lambda i: (i, 0))
    w_o, w_spec = _weight_spec(w_o, (A_OUT, D), lambda i: (0, 0))
    return pl.pallas_call(
        _dil_out_body,
        out_shape=jax.ShapeDtypeStruct((M, D), f32),
        grid=(M // tm,),
        in_specs=[row(A_OUT)] * 6 + [w_spec, row(D)],
        out_specs=row(D),
        compiler_params=_params("parallel"),
        name="dilated_merge_out",
    )(*os, *lses, w_o, res)


GLA_SUB = 16


def _log_sigmoid(z):
    return -(jnp.maximum(-z, 0.0) + jnp.log1p(jnp.exp(-jnp.abs(z))))


def _gla_body(q_ref, k_ref, v_ref, r_ref, gl_ref, w2_ref, bg_ref, gh_ref, s0_ref, y_ref, s_ref, a_ref):
    C = q_ref.shape[0]
    c = GLA_SUB

    @pl.when(pl.program_id(1) == 0)
    def _():
        s_ref[...] = s0_ref[...]

    z = _hdot(gl_ref[...], w2_ref[...]) + bg_ref[...]
    la = _log_sigmoid(z) / B_GATE_TAU
    rowC = lax.broadcasted_iota(jnp.int32, (C, C), 0)
    colC = lax.broadcasted_iota(jnp.int32, (C, C), 1)
    tri = (rowC >= colC).astype(f32)
    b_all = _hdot(tri, la)
    row_k = lax.broadcasted_iota(jnp.int32, (C, B_DK), 0)
    col_c = lax.broadcasted_iota(jnp.int32, (c, C), 1)
    row_c = lax.broadcasted_iota(jnp.int32, (c, 1), 0)

    for h in range(B_HEADS):
        bh = b_all[:, h * B_DK:(h + 1) * B_DK]
        qh = q_ref[:, h * B_DK:(h + 1) * B_DK] * B_DK ** -0.5
        kh = k_ref[:, h * B_DK:(h + 1) * B_DK]
        vh = v_ref[:, h * B_DV:(h + 1) * B_DV].astype(bf16)
        S = s_ref[h]
        o = jnp.dot((qh * jnp.exp(bh)).astype(bf16), S.astype(bf16), preferred_element_type=f32)

        for I in range(C // c):
            r0 = I * c
            qI, kI, bI = qh[r0:r0 + c], kh[r0:r0 + c], bh[r0:r0 + c]
            if I == 0:
                A_I = jnp.zeros((c, C), f32)
            else:
                beta = bh[r0 - 1:r0]
                qt = qI * jnp.exp(bI - beta)
                kt = kh * jnp.exp(jnp.where(row_k < r0, beta - bh, 0.0))
                A_I = jnp.where(col_c < r0, _dot_t(qt.astype(bf16), kt.astype(bf16)), 0.0)
            for j in range(c):
                ex = jnp.exp(jnp.minimum(bI - bI[j:j + 1], 0.0))
                tj = jnp.sum(qI * kI[j:j + 1] * ex, axis=-1, keepdims=True)
                tj = jnp.where(row_c >= j, tj, 0.0)
                A_I = jnp.where(col_c == r0 + j, tj, A_I)
            a_ref[r0:r0 + c, :] = A_I
        o = o + jnp.dot(a_ref[...].astype(bf16), vh, preferred_element_type=f32)

        b_last = bh[C - 1:C]
        kdec = jnp.transpose(kh * jnp.exp(b_last - bh)).astype(bf16)
        decay = jnp.transpose(jnp.broadcast_to(jnp.exp(b_last), (8, B_DK)))[:, 0:1]
        s_ref[h] = decay * S + jnp.dot(kdec, vh, preferred_element_type=f32)

        on = _rms(o, gh_ref[:, h * B_DV:(h + 1) * B_DV])
        rh = r_ref[:, h * B_DV:(h + 1) * B_DV]
        y_ref[:, h * B_DV:(h + 1) * B_DV] = (on * (rh * jax.nn.sigmoid(rh))).astype(y_ref.dtype)


def gla_prompt(proj, w_gate2p, b_gate, g_head, s0, *, chunk=128):
    B, L, _ = proj.shape
    HK, HV = B_HEADS * B_DK, B_HEADS * B_DV
    C = min(chunk, L)
    assert L % C == 0 and C % GLA_SUB == 0
    return pl.pallas_call(
        _gla_body,
        out_shape=[jax.ShapeDtypeStruct((B, L, HV), bf16),
                   jax.ShapeDtypeStruct((B, B_HEADS, B_DK, B_DV), f32)],
        grid=(B, L // C),
        in_specs=[
            pl.BlockSpec((None, C, HK), lambda b, i: (b, i, 0)),
            pl.BlockSpec((None, C, HK), lambda b, i: (b, i, 1)),
            pl.BlockSpec((None, C, HV), lambda b, i: (b, i, 1)),
            pl.BlockSpec((None, C, HV), lambda b, i: (b, i, 2)),
            pl.BlockSpec((None, C, LANES), lambda b, i: (b, i, (2 * HK + 2 * HV) // LANES)),
            pl.BlockSpec((LANES, HK), lambda b, i: (0, 0)),
            pl.BlockSpec((1, HK), lambda b, i: (0, 0)),
            pl.BlockSpec((1, HV), lambda b, i: (0, 0)),
            pl.BlockSpec((None, B_HEADS, B_DK, B_DV), lambda b, i: (b, 0, 0, 0)),
        ],
        out_specs=[pl.BlockSpec((None, C, HV), lambda b, i: (b, i, 0)),
                   pl.BlockSpec((None, B_HEADS, B_DK, B_DV), lambda b, i: (b, 0, 0, 0))],
        scratch_shapes=[pltpu.VMEM((C, C), f32)],
        compiler_params=_params("parallel", "arbitrary"),
        name="gla_prompt",
    )(proj, proj, proj, proj, proj, w_gate2p, b_gate.reshape(1, HK), g_head.reshape(1, HV), s0)


def _cmp_mlp_tail(z, posb, w2_ref):
    n = z.shape[0]
    H = C_CMP_HIDDEN
    hid = z[:, :H] + pltpu.roll(z[:, H:], n - 1, 0) + posb
    return jnp.dot(jax.nn.gelu(hid).astype(bf16), w2_ref[...], preferred_element_type=f32)


def _cmp_pos_bias(p_ref, w1_ref):
    H = C_CMP_HIDDEN
    pz = jnp.dot(p_ref[...].astype(bf16), w1_ref[...], preferred_element_type=f32)
    return pz[0:1, :H] + pz[1:2, H:]


def _cmp_body(xk_ref, xv_ref, pk_ref, pv_ref, wk1_ref, wk2_ref, wv1_ref, wv2_ref, ok_ref, ov_ref):
    for x_ref, p_ref, w1_ref, w2_ref, o_ref in ((xk_ref, pk_ref, wk1_ref, wk2_ref, ok_ref),
                                                (xv_ref, pv_ref, wv1_ref, wv2_ref, ov_ref)):
        z = jnp.dot(x_ref[...].astype(bf16), w1_ref[...], preferred_element_type=f32)
        o_ref[...] = _cmp_mlp_tail(z, _cmp_pos_bias(p_ref, w1_ref), w2_ref).astype(o_ref.dtype)


def nsa_compress(xk, xv, pos_k, pos_v, wk1, wk2, wv1, wv2):
    B, G, n, W = xk.shape
    x_spec = pl.BlockSpec((None, None, n, W), lambda b, g: (b, g, 0, 0))
    full = lambda a: pl.BlockSpec(a.shape, lambda b, g: (0,) * a.ndim)
    o_spec = pl.BlockSpec((None, None, n, C_HEAD_DIM), lambda b, g: (b, g, 0, 0))
    consts = [pos_k, pos_v, wk1, wk2, wv1, wv2]
    return pl.pallas_call(
        _cmp_body,
        out_shape=[jax.ShapeDtypeStruct((B, G, n, C_HEAD_DIM), bf16)] * 2,
        grid=(B, G),
        in_specs=[x_spec, x_spec] + [full(a) for a in consts],
        out_specs=[o_spec, o_spec],
        compiler_params=_params("parallel", "parallel"),
        name="nsa_compress",
    )(xk, xv, *consts)


def _topk_mask_t(score_t, blk_t):
    keep = jnp.zeros(score_t.shape, f32)
    big = float(score_t.shape[0])
    for _ in range(C_TOPK):
        mx = jnp.max(score_t, axis=0, keepdims=True)
        first = jnp.min(jnp.where(score_t == mx, blk_t, big), axis=0, keepdims=True)
        hit = blk_t == first
        keep = jnp.where(hit, jnp.where(mx > 0.5 * NEG_INF, 1.0, 0.0), keep)
        score_t = jnp.where(hit, REMOVED, score_t)
    return keep


def _reduce_rows(x, op):
    slabs = [x[r:r + 8] for r in range(0, x.shape[0], 8)]
    while len(slabs) > 1:
        slabs = [op(slabs[k], slabs[k + 1]) if k + 1 < len(slabs) else slabs[k] for k in range(0, len(slabs), 2)]
    red = jnp.max if op is jnp.maximum else jnp.sum
    return red(slabs[0], axis=0, keepdims=True)


def _softmax2_rows_t(s, mask):
    e = jnp.where(mask, jnp.exp2(s - _reduce_rows(s, jnp.maximum)), 0.0)
    return e / jnp.maximum(_reduce_rows(e, jnp.add), TINY)


def _nsa_body(sp_ref, q_ref, gate_ref, bg_ref, kc_ref, vc_ref, ks_ref, vs_ref, kw_ref, vw_ref,
              o_ref, m_ref, acc_ref, gs_ref, sa_ref, sb_ref, *, tk):
    g = pl.program_id(1)
    i = pl.program_id(2)
    tq = q_ref.shape[0]
    R = C_HPG * tq
    L = ks_ref.shape[0]
    n_cmp = kc_ref.shape[0]
    NS = LANES
    dh = C_HEAD_DIM
    q0 = i * tq

    q_t = jnp.transpose(q_ref[...] * (dh ** -0.5 * LOG2E))
    prow = lax.broadcasted_iota(jnp.int32, (LANES - dh, 1), 0)
    cols = []
    for hh in range(C_HPG):
        pc = jnp.zeros((LANES - dh, 1), f32)
        for n in reversed(range(SLOPE_PIECES)):
            pc = jnp.where(prow < 2 * n + 2, sp_ref[(g * C_HPG + hh) * SLOPE_PIECES + n], pc)
        cols.append(jnp.concatenate([q_t[hh * dh:(hh + 1) * dh], jnp.broadcast_to(pc, (LANES - dh, tq))], axis=0))
    qa_t = jnp.concatenate(cols, axis=1).astype(bf16)
    t1 = q0 + lax.broadcasted_iota(jnp.int32, (1, tq), 1)
    t = jnp.concatenate([t1] * C_HPG, axis=1)

    W = min(L, C_WINDOW + tq)
    w0 = pl.multiple_of(jnp.clip(q0 - C_WINDOW, 0, L - W), tq)
    wpos = w0 + lax.broadcasted_iota(jnp.int32, (W, 1), 0)
    s = jnp.dot(kw_ref[pl.ds(w0, W), :], qa_t, preferred_element_type=f32)
    s = jnp.where(wpos <= t, jnp.where(wpos >= t - C_WINDOW, s, NEG_INF), NEG_INF)
    e = jnp.exp2(s - _reduce_rows(s, jnp.maximum))
    o_w = jnp.dot(vw_ref[:, pl.ds(w0, W)], e.astype(bf16), preferred_element_type=f32)
    o_w = o_w[0:dh] / o_w[dh:dh + 1]

    cpos = lax.broadcasted_iota(jnp.int32, (n_cmp, 1), 0) * C_CMP_STRIDE + (C_CMP_BLOCK - 1)
    mask = cpos <= t
    s = jnp.where(mask, jnp.dot(kc_ref[...], qa_t, preferred_element_type=f32), NEG_INF)
    p = _softmax2_rows_t(s, mask)
    o_c = jnp.dot(vc_ref[...], p.astype(bf16), preferred_element_type=f32)

    psum = p[:, 0:tq]
    for hh in range(1, C_HPG):
        psum = psum + p[:, hh * tq:(hh + 1) * tq]
    n_blk = -(-(L // C_SEL_BLOCK) // 8) * 8
    jb = lax.broadcasted_iota(jnp.int32, (n_blk, n_cmp), 0)
    c_start = lax.broadcasted_iota(jnp.int32, (n_blk, n_cmp), 1) * C_CMP_STRIDE
    cover = jnp.where(c_start < (jb + 1) * C_SEL_BLOCK,
                      jnp.where(c_start + C_CMP_BLOCK > jb * C_SEL_BLOCK, 1.0, 0.0), 0.0).astype(bf16)
    imp = jnp.zeros((n_blk, tq), f32)
    rest = psum
    for _ in range(N_PIECES):
        piece = rest.astype(bf16)
        imp = imp + jnp.dot(cover, piece, preferred_element_type=f32)
        rest = rest - piece.astype(f32)
    blk = lax.broadcasted_iota(jnp.int32, (n_blk, tq), 0)
    cur = t1 >> C_SEL_SHIFT
    forced = (blk == 0) | (blk == cur) | (blk == cur - 1)
    score = jnp.where(blk * C_SEL_BLOCK <= t1, jnp.where(forced, FORCE, imp), NEG_INF)
    keep = _topk_mask_t(score, blk.astype(f32))
    neg1 = jnp.concatenate([jnp.where(keep > 0.5, jnp.where(blk == cur, NEG_INF, 0.0), NEG_INF),
                            jnp.full((NS - n_blk, tq), NEG_INF, f32)], axis=0).astype(bf16)
    qa2_t = jnp.concatenate([qa_t, jnp.concatenate([neg1] * C_HPG, axis=1)], axis=0)

    d0 = pl.multiple_of(q0, tq)
    kpos = q0 + lax.broadcasted_iota(jnp.int32, (tq, 1), 0)
    sc = jnp.dot(ks_ref[pl.ds(d0, tq), 0:LANES], qa_t, preferred_element_type=f32)
    sc = jnp.where(kpos <= t, jnp.where((kpos >> C_SEL_SHIFT) == (t >> C_SEL_SHIFT), sc, NEG_INF), NEG_INF)
    m0 = _reduce_rows(sc, jnp.maximum)
    m_ref[...] = m0
    acc_ref[...] = jnp.dot(vs_ref[:, pl.ds(d0, tq)], jnp.exp2(sc - m0).astype(bf16), preferred_element_type=f32)

    n_tiles = q0 // tk + 1

    def scores(kt, s_ref):
        k0 = pl.multiple_of(jnp.minimum(kt, n_tiles - 1) * tk, tk)
        s_ref[...] = jnp.dot(ks_ref[pl.ds(k0, tk), :], qa2_t, preferred_element_type=f32)

    def absorb(kt, s_ref):
        k0 = pl.multiple_of(kt * tk, tk)
        sc = s_ref[...]
        m_old = m_ref[...]
        m_new = jnp.maximum(m_old, _reduce_rows(sc, jnp.maximum))
        ee = jnp.exp2(sc - m_new).astype(bf16)
        acc_ref[...] = jnp.exp2(m_old - m_new) * acc_ref[...] + jnp.dot(vs_ref[:, pl.ds(k0, tk)], ee,
                                                                         preferred_element_type=f32)
        m_ref[...] = m_new

    scores(0, sa_ref)

    def pair(j, carry):
        scores(2 * j + 1, sb_ref)
        absorb(2 * j, sa_ref)
        scores(2 * j + 2, sa_ref)
        absorb(2 * j + 1, sb_ref)
        return carry

    lax.fori_loop(0, n_tiles // 2, pair, 0)

    @pl.when(n_tiles % 2 == 1)
    def _():
        absorb(n_tiles - 1, sa_ref)

    o_s = acc_ref[0:dh, :] / acc_ref[dh:dh + 1, :]

    gs_ref[...] = jnp.transpose(jax.nn.sigmoid(gate_ref[...] + bg_ref[...]))

    def gate(r):
        return jnp.concatenate([gs_ref[pl.ds((g * C_HPG + hh) * 3 + r, 1), :] for hh in range(C_HPG)], axis=1)

    o = gate(0) * o_c + gate(1) * o_s + gate(2) * o_w
    o = jnp.concatenate([o[:, hh * tq:(hh + 1) * tq] for hh in range(C_HPG)], axis=0)
    o_ref[...] = jnp.transpose(o).astype(o_ref.dtype)


def _aug_keys(k, pos, one_hot_blocks):
    B, G, n, dh = k.shape
    hi = (pos // C_SEL_BLOCK) * C_SEL_BLOCK
    cols = np.zeros((n, LANES - dh), np.float32)
    for j in range(SLOPE_PIECES):
        cols[:, 2 * j] = hi
        cols[:, 2 * j + 1] = pos - hi
    parts = [k, jnp.broadcast_to(jnp.asarray(cols, bf16), (B, G, n, LANES - dh))]
    if one_hot_blocks:
        oh = (pos[:, None] // C_SEL_BLOCK == np.arange(LANES)[None, :]).astype(np.float32)
        parts.append(jnp.broadcast_to(jnp.asarray(oh, bf16), (B, G, n, LANES)))
    return jnp.concatenate(parts, axis=-1)


def nsa_attention_prompt(proj, b_gate_p, kc, vc, ks, vs, kw, vw, *, tq=128, tk=512):
    B, L, _ = proj.shape
    G, dh = C_KV_HEADS, C_HEAD_DIM
    tq = min(tq, L)
    tk = min(tk, L)
    assert tk % tq == 0 and L % tk == 0 and L // C_SEL_BLOCK <= LANES
    R = C_HPG * tq
    n_cmp = kc.shape[2]
    kc_a = _aug_keys(kc, np.arange(n_cmp) * C_CMP_STRIDE + (C_CMP_BLOCK - 1), False)
    ks_a = _aug_keys(ks, np.arange(L), True)
    kw_a = _aug_keys(kw, np.arange(L), False)
    gate_blk = (C_HEADS * dh + 6 * G * dh) // LANES
    seq = lambda n, w: pl.BlockSpec((None, None, n, w), lambda b, g, i, sp: (b, g, 0, 0))
    tr = lambda a: jnp.swapaxes(a, 2, 3)

    def with_ones(a):
        n = a.shape[2]
        tail = np.zeros((VAL_ROWS - dh, n), np.float32)
        tail[0] = 1.0
        return jnp.concatenate([tr(a), jnp.broadcast_to(jnp.asarray(tail, bf16), (B, G, VAL_ROWS - dh, n))], axis=2)

    return pl.pallas_call(
        functools.partial(_nsa_body, tk=tk),
        out_shape=jax.ShapeDtypeStruct((B, L, C_HEADS * dh), bf16),
        grid_spec=pltpu.PrefetchScalarGridSpec(
            num_scalar_prefetch=1,
            grid=(B, G, L // tq),
            in_specs=[
                pl.BlockSpec((None, tq, C_HPG * dh), lambda b, g, i, sp: (b, i, g)),
                pl.BlockSpec((None, tq, LANES), lambda b, g, i, sp: (b, i, gate_blk)),
                pl.BlockSpec((1, LANES), lambda b, g, i, sp: (0, 0)),
                seq(n_cmp, LANES), seq(dh, n_cmp), seq(L, 2 * LANES), seq(VAL_ROWS, L), seq(L, LANES),
                seq(VAL_ROWS, L),
            ],
            out_specs=pl.BlockSpec((None, tq, C_HPG * dh), lambda b, g, i, sp: (b, i, g)),
            scratch_shapes=[pltpu.VMEM((1, R), f32), pltpu.VMEM((VAL_ROWS, R), f32),
                            pltpu.VMEM((LANES, tq), f32), pltpu.VMEM((tk, R), f32), pltpu.VMEM((tk, R), f32)],
        ),
        compiler_params=_params("parallel", "parallel", "parallel"),
        name="nsa_attention_prompt",
    )(jnp.asarray(SLOPES_C_PIECES), proj, proj, b_gate_p, kc_a, tr(vc), ks_a, with_ones(vs), kw_a, with_ones(vw))


def _slot_rows(ref, off, n, w):
    rows = jnp.concatenate([ref[:, off + s * w:off + (s + 1) * w] for s in range(n)]
                           + ([jnp.zeros((8 - n, w), f32)] if n < 8 else []), axis=0)
    return jnp.concatenate([rows, jnp.zeros((8, LANES - w), f32)], axis=-1) if w < LANES else rows


def _dil_sample_body(sl_ref, qkv_ref, c0_ref, c1_ref, c2_ref, o_ref):
    dh = A_HEAD_DIM
    scale = dh ** -0.5
    sect = A_GROUPS * A_OUT
    scores, news, v_cols = [], [], []
    for grp, c_ref in enumerate((c0_ref, c1_ref, c2_ref)):
        W = c_ref.shape[-1]
        d = A_DILATIONS[grp]
        q8 = _slot_rows(qkv_ref, grp * A_OUT, A_SLOTS, dh)
        kn8 = _slot_rows(qkv_ref, sect + grp * A_OUT, A_SLOTS, dh)
        vn8 = _slot_rows(qkv_ref, 2 * sect + grp * A_OUT, A_SLOTS, dh)
        q_cols = jnp.transpose(q8)[:dh]
        v_cols.append(jnp.transpose(vn8)[:dh])
        news.append(jnp.sum(q8 * kn8, axis=-1, keepdims=True) * scale)
        pos = lax.broadcasted_iota(jnp.int32, (1, W), 1)
        rows = [jnp.sum(c_ref[0, s] * q_cols[:, s:s + 1], axis=0, keepdims=True) for s in range(A_SLOTS)]
        slope = jnp.concatenate([jnp.full((1, 1), sl_ref[grp * A_SLOTS + s], f32) for s in range(A_SLOTS)], axis=0)
        sc = jnp.concatenate(rows, axis=0) * scale - slope * (W - pos).astype(f32)
        scores.append(jnp.where((pos & (d - 1)) == 0, sc, NEG_INF))
    m = functools.reduce(jnp.maximum, [jnp.max(s, axis=-1, keepdims=True) for s in scores] + news)
    den = jnp.zeros((A_SLOTS, 1), f32)
    acc = [jnp.zeros((dh, 1), f32) for _ in range(A_SLOTS)]
    for grp, c_ref in enumerate((c0_ref, c1_ref, c2_ref)):
        e = jnp.exp(scores[grp] - m)
        en = jnp.exp(news[grp] - m)
        den = den + jnp.sum(e, axis=-1, keepdims=True) + en
        for s in range(A_SLOTS):
            acc[s] = acc[s] + jnp.sum(c_ref[1, s] * e[s:s + 1, :], axis=-1, keepdims=True) \
                + en[s:s + 1, :] * v_cols[grp][:, s:s + 1]
    o_ref[...] = jnp.concatenate([acc[s] / den[s:s + 1, :] for s in range(A_SLOTS)], axis=-1)


def dil_sample(qkv, caches_t, layer):
    Bs = qkv.shape[0]
    for grp, c in enumerate(caches_t):
        assert c.shape[-1] == A_WINDOWS[grp]
    c_spec = lambda c: pl.BlockSpec((None, None) + c.shape[2:], lambda b, sl: (layer, b, 0, 0, 0, 0))
    return pl.pallas_call(
        _dil_sample_body,
        out_shape=jax.ShapeDtypeStruct((Bs, A_HEAD_DIM, A_SLOTS), f32),
        grid_spec=pltpu.PrefetchScalarGridSpec(
            num_scalar_prefetch=1,
            grid=(Bs,),
            in_specs=[pl.BlockSpec((None, 1, qkv.shape[-1]), lambda b, sl: (b, 0, 0))]
            + [c_spec(c) for c in caches_t],
            out_specs=pl.BlockSpec((None, A_HEAD_DIM, A_SLOTS), lambda b, sl: (b, 0, 0)),
        ),
        compiler_params=_params("parallel"),
        name="dil_sample",
    )(jnp.asarray(SLOPES_A.reshape(-1)), qkv, *caches_t)


def _xattn_sample_body(q_ref, c_ref, o_ref):
    dh = q_ref.shape[-1] // MEM_HEADS
    outs = []
    for h in range(MEM_HEADS):
        qh = q_ref[:, h * dh:(h + 1) * dh]
        s = jnp.sum(c_ref[:, 0, h, :] * qh, axis=-1, keepdims=True) * dh ** -0.5
        e = jnp.exp(s - jnp.max(s, axis=0, keepdims=True))
        outs.append(jnp.sum(e * c_ref[:, 1, h, :], axis=0, keepdims=True) / jnp.sum(e, axis=0, keepdims=True))
    o_ref[...] = jnp.concatenate(outs, axis=-1)


def xattn_sample(q, cache, layer):
    Bs, _, D = q.shape
    return pl.pallas_call(
        _xattn_sample_body,
        out_shape=jax.ShapeDtypeStruct((Bs, 1, D), f32),
        grid=(Bs,),
        in_specs=[pl.BlockSpec((None, 1, D), lambda b: (b, 0, 0)),
                  pl.BlockSpec((None, None) + cache.shape[2:], lambda b: (layer, b, 0, 0, 0, 0))],
        out_specs=pl.BlockSpec((None, 1, D), lambda b: (b, 0, 0)),
        compiler_params=_params("parallel"),
        name="xattn_sample",
    )(q, cache)


def _gla_step_body(q_ref, k_ref, v_ref, r_ref, gl_ref, w2_ref, bg_ref, gh_ref, s0_ref, y_ref, s_ref):
    z = _hdot(gl_ref[...], w2_ref[...]) + bg_ref[...]
    a = jnp.exp(_log_sigmoid(z) / B_GATE_TAU)
    pad = jnp.zeros((5, B_DK), f32)
    for h in range(B_HEADS):
        ks_ = slice(h * B_DK, (h + 1) * B_DK)
        vs_ = slice(h * B_DV, (h + 1) * B_DV)
        rows = jnp.concatenate([a[:, ks_], k_ref[:, ks_], q_ref[:, ks_] * B_DK ** -0.5, pad], axis=0)
        cols = jnp.transpose(rows)
        S = cols[:, 0:1] * s0_ref[h] + cols[:, 1:2] * v_ref[:, vs_]
        s_ref[h] = S
        o = jnp.sum(cols[:, 2:3] * S, axis=0, keepdims=True)
        rh = r_ref[:, vs_]
        y_ref[:, vs_] = (_rms(o, gh_ref[:, vs_]) * (rh * jax.nn.sigmoid(rh))).astype(y_ref.dtype)


def gla_step(proj, w_gate2p, b_gate, g_head, state, layer):
    Bs = proj.shape[0]
    HK, HV = B_HEADS * B_DK, B_HEADS * B_DV
    return pl.pallas_call(
        _gla_step_body,
        out_shape=[jax.ShapeDtypeStruct((Bs, 1, HV), bf16),
                   jax.ShapeDtypeStruct((Bs, B_HEADS, B_DK, B_DV), f32)],
        grid=(Bs,),
        in_specs=[
            pl.BlockSpec((None, 1, HK), lambda b: (b, 0, 0)),
            pl.BlockSpec((None, 1, HK), lambda b: (b, 0, 1)),
            pl.BlockSpec((None, 1, HV), lambda b: (b, 0, 1)),
            pl.BlockSpec((None, 1, HV), lambda b: (b, 0, 2)),
            pl.BlockSpec((None, 1, LANES), lambda b: (b, 0, (2 * HK + 2 * HV) // LANES)),
            pl.BlockSpec((LANES, HK), lambda b: (0, 0)),
            pl.BlockSpec((1, HK), lambda b: (0, 0)),
            pl.BlockSpec((1, HV), lambda b: (0, 0)),
            pl.BlockSpec((None, None, B_HEADS, B_DK, B_DV), lambda b: (layer, b, 0, 0, 0)),
        ],
        out_specs=[pl.BlockSpec((None, 1, HV), lambda b: (b, 0, 0)),
                   pl.BlockSpec((None, B_HEADS, B_DK, B_DV), lambda b: (b, 0, 0, 0))],
        compiler_params=_params("parallel"),
        name="gla_step",
    )(proj, proj, proj, proj, proj, w_gate2p, b_gate.reshape(1, HK), g_head.reshape(1, HV), state)


PAGES_PER_STEP = 8


def _cmp_sample_body(pt_ref, *refs):
    pages = refs[:PAGES_PER_STEP]
    (pk_ref, pv_ref, wk1_ref, wk2_ref, wv1_ref, wv2_ref, wkp_ref, wvp_ref,
     ok_ref, ov_ref, x_ref) = refs[PAGES_PER_STEP:]
    i = pl.program_id(1)
    H = C_CMP_HIDDEN
    n_tiles = x_ref.shape[0]
    per_kind = n_tiles // 2
    per_page = PAGE_SIZE // C_CMP_STRIDE
    out_row = lax.broadcasted_iota(jnp.int32, (PAGE_SIZE, PAGE_SIZE), 0)
    in_row = lax.broadcasted_iota(jnp.int32, (PAGE_SIZE, PAGE_SIZE), 1)
    shift = per_page.bit_length() - 1
    perm = (in_row == (out_row & (per_page - 1)) * C_CMP_STRIDE + (out_row >> shift)).astype(bf16)
    for k in range(PAGES_PER_STEP):
        c0 = pl.multiple_of((i * PAGES_PER_STEP + k) * per_page, per_page)
        for kind in range(2):
            for tile in range(per_kind):
                xt = pages[k][kind, 2 * tile:2 * tile + 2].reshape(LANES, PAGE_SIZE).astype(bf16)
                xs = _dot_t(perm, xt)
                for p in range(C_CMP_STRIDE):
                    x_ref[kind * per_kind + tile, p, pl.ds(c0, per_page), :] = xs[p * per_page:(p + 1) * per_page]

    @pl.when(i == pl.num_programs(1) - 1)
    def _():
        n = x_ref.shape[2]
        for kind, (p_ref, w1_ref, w2_ref, wp_ref, o_ref) in enumerate(
                ((pk_ref, wk1_ref, wk2_ref, wkp_ref, ok_ref), (pv_ref, wv1_ref, wv2_ref, wvp_ref, ov_ref))):
            posb = _cmp_pos_bias(p_ref, w1_ref)
            outs = []
            for tile in range(per_kind):
                ct = kind * per_kind + tile
                z2 = jnp.zeros((n, 4 * H), f32)
                for pp in range(C_CMP_STRIDE // 2):
                    xp = jnp.concatenate([x_ref[ct, 2 * pp], x_ref[ct, 2 * pp + 1]], axis=-1).astype(bf16)
                    z2 = z2 + jnp.dot(xp, wp_ref[pp], preferred_element_type=f32)
                outs += [_cmp_mlp_tail(z, posb, w2_ref) for z in (z2[:, :2 * H], z2[:, 2 * H:])]
            o_ref[...] = jnp.concatenate(outs, axis=-1)


def _pair_block_diag(w1):
    dh = C_HEAD_DIM
    w = w1.reshape(C_CMP_STRIDE, dh, w1.shape[1])
    z = jnp.zeros_like(w)
    bd = jnp.concatenate([jnp.concatenate([w, z], axis=-1), jnp.concatenate([z, w], axis=-1)], axis=1)
    return bd.reshape(C_CMP_STRIDE // 2, 4 * dh, 2 * w1.shape[1])


def nsa_compress_sample(page_table, cache_t, layer, pos_k, pos_v, wk1, wk2, wv1, wv2):
    Bs, n_pages = page_table.shape
    G, dh = C_KV_HEADS, C_HEAD_DIM
    assert n_pages % PAGES_PER_STEP == 0 and cache_t.shape[2:] == (4, G, dh, PAGE_SIZE) and 2 * dh == LANES
    n_rows = n_pages * PAGE_SIZE
    n = n_rows // C_CMP_STRIDE
    page = lambda k: pl.BlockSpec((None, None, 2, G, dh, PAGE_SIZE),
                                  lambda b, i, pt, k=k: (layer, pt[b, i * PAGES_PER_STEP + k], 0, 0, 0, 0))
    consts = [pos_k, pos_v, wk1, wk2, wv1, wv2, _pair_block_diag(wk1), _pair_block_diag(wv1)]
    full = lambda a: pl.BlockSpec(a.shape, lambda b, i, pt: (0,) * a.ndim)
    o_spec = pl.BlockSpec((None, n, G * dh), lambda b, i, pt: (b, 0, 0))
    return pl.pallas_call(
        _cmp_sample_body,
        out_shape=[jax.ShapeDtypeStruct((Bs, n, G * dh), f32)] * 2,
        grid_spec=pltpu.PrefetchScalarGridSpec(
            num_scalar_prefetch=1,
            grid=(Bs, n_pages // PAGES_PER_STEP),
            in_specs=[page(k) for k in range(PAGES_PER_STEP)] + [full(a) for a in consts],
            out_specs=[o_spec, o_spec],
            scratch_shapes=[pltpu.VMEM((2 * G * dh // LANES, C_CMP_STRIDE, n, LANES), f32)],
        ),
        compiler_params=_params("parallel", "arbitrary"),
        name="nsa_compress_sample",
    )(page_table, *([cache_t] * PAGES_PER_STEP), *consts)


def _cmp_topk_sample_body(sl_ref, q_ref, kc_ref, vc_ref, oc_ref, idx_ref, *, t):
    G, dh = C_KV_HEADS, C_HEAD_DIM
    n = kc_ref.shape[0]
    n_slc = t // C_SEL_BLOCK + 1
    NS = -(-n_slc // LANES) * LANES
    cpos = lax.broadcasted_iota(jnp.int32, (1, n), 1) * C_CMP_STRIDE + (C_CMP_BLOCK - 1)
    mask = cpos <= t
    c_start = lax.broadcasted_iota(jnp.int32, (n, NS), 0) * C_CMP_STRIDE
    jb = lax.broadcasted_iota(jnp.int32, (n, NS), 1)
    cover = jnp.where(c_start < (jb + 1) * C_SEL_BLOCK,
                      jnp.where(c_start + C_CMP_BLOCK > jb * C_SEL_BLOCK, 1.0, 0.0), 0.0).astype(bf16)
    blk = lax.broadcasted_iota(jnp.int32, (1, NS), 1)
    cur = t // C_SEL_BLOCK
    forced = (blk == 0) | (blk == cur) | (blk == cur - 1)
    lane = lax.broadcasted_iota(jnp.int32, (1, LANES), 1)
    o_parts, p_rows = [], []
    for g in range(G):
        q8 = jnp.concatenate([q_ref[:, (g * C_HPG + hh) * dh:(g * C_HPG + hh + 1) * dh] for hh in range(C_HPG)]
                             + [jnp.zeros((8 - C_HPG, dh), f32)], axis=0)
        slope = jnp.concatenate([jnp.full((1, 1), sl_ref[g * C_HPG + hh], f32) for hh in range(C_HPG)]
                                + [jnp.zeros((8 - C_HPG, 1), f32)], axis=0)
        s = _dot_t(q8.astype(bf16), kc_ref[:, g * dh:(g + 1) * dh].astype(bf16)) * dh ** -0.5
        s = s - slope * (float(t) - cpos.astype(f32))
        s = jnp.where(mask, s, NEG_INF)
        m = jnp.max(s, axis=-1, keepdims=True)
        e = jnp.where(mask, jnp.exp(s - m), 0.0)
        p = e / jnp.maximum(jnp.sum(e, axis=-1, keepdims=True), TINY)
        o = jnp.dot(p.astype(bf16), vc_ref[:, g * dh:(g + 1) * dh].astype(bf16), preferred_element_type=f32)
        o_parts += [o[hh:hh + 1] for hh in range(C_HPG)]
        p_rows.append(jnp.sum(p[0:C_HPG], axis=0, keepdims=True))
    oc_ref[...] = jnp.concatenate(o_parts, axis=-1)
    rest = jnp.concatenate(p_rows + [jnp.zeros((8 - G, n), f32)], axis=0)
    imp = jnp.zeros((8, NS), f32)
    for _ in range(N_PIECES):
        piece = rest.astype(bf16)
        imp = imp + jnp.dot(piece, cover, preferred_element_type=f32)
        rest = rest - piece.astype(f32)
    row = lax.broadcasted_iota(jnp.int32, (8, 1), 0)
    score = jnp.where((blk * C_SEL_BLOCK <= t) & (row < G), jnp.where(forced, FORCE, imp), NEG_INF)
    rank = jnp.zeros((8, NS), f32)
    for k in range(1, NS):
        other = pltpu.roll(score, k, 1)
        ahead = jnp.where(blk >= k, jnp.where(other >= score, 1.0, 0.0), jnp.where(other > score, 1.0, 0.0))
        rank = rank + ahead
    blk_f = blk.astype(f32)
    idx = jnp.zeros((8, LANES), f32)
    for r in range(C_TOPK):
        pick = jnp.sum(jnp.where(rank == float(r), blk_f, 0.0), axis=-1, keepdims=True)
        idx = jnp.where(lane == r, pick, idx)
    idx_ref[...] = idx.astype(jnp.int32)


def nsa_cmp_topk_sample(proj, kc, vc, t):
    Bs = proj.shape[0]
    n, W = kc.shape[1:]
    HD = C_HEADS * C_HEAD_DIM
    return pl.pallas_call(
        functools.partial(_cmp_topk_sample_body, t=t),
        out_shape=[jax.ShapeDtypeStruct((Bs, 1, HD), f32), jax.ShapeDtypeStruct((Bs, 8, LANES), jnp.int32)],
        grid_spec=pltpu.PrefetchScalarGridSpec(
            num_scalar_prefetch=1,
            grid=(Bs,),
            in_specs=[pl.BlockSpec((None, 1, HD), lambda b, sl: (b, 0, 0)),
                      pl.BlockSpec((None, n, W), lambda b, sl: (b, 0, 0)),
                      pl.BlockSpec((None, n, W), lambda b, sl: (b, 0, 0))],
            out_specs=[pl.BlockSpec((None, 1, HD), lambda b, sl: (b, 0, 0)),
                       pl.BlockSpec((None, 8, LANES), lambda b, sl: (b, 0, 0))],
        ),
        compiler_params=_params("parallel"),
        name="nsa_cmp_topk_sample",
    )(jnp.asarray(SLOPES_C), proj, kc, vc)


def _sel_win_sample_body(pt_ref, ix_ref, sl_ref, *refs, t):
    blocks = refs[:C_TOPK]
    (win_ref, q_ref, ksn_ref, vsn_ref, kwn_ref, vwn_ref, gate_ref, bg_ref, oc_ref, o_ref) = refs[C_TOPK:]
    b = pl.program_id(0)
    g = pl.program_id(1)
    G, dh = C_KV_HEADS, C_HEAD_DIM
    scale = dh ** -0.5
    n_past = t // C_SEL_BLOCK
    per_page = PAGE_SIZE // C_SEL_BLOCK
    q8 = jnp.concatenate([q_ref[:, hh * dh:(hh + 1) * dh] for hh in range(C_HPG)]
                         + [jnp.zeros((8 - C_HPG, dh), f32)], axis=0)
    q8b = q8.astype(bf16)
    slope = jnp.concatenate([jnp.full((1, 1), sl_ref[g * C_HPG + hh], f32) for hh in range(C_HPG)]
                            + [jnp.zeros((8 - C_HPG, 1), f32)], axis=0)

    def own(ref):
        return functools.reduce(jnp.add, [jnp.where(g == gg, ref[:, gg * dh:(gg + 1) * dh], 0.0) for gg in range(G)])

    def softmax_pv(scores, values_t, s_new, v_new):
        m = functools.reduce(jnp.maximum, [jnp.max(s, axis=-1, keepdims=True) for s in scores] + [s_new])
        e_new = jnp.exp(s_new - m)
        den = e_new
        acc = e_new * v_new
        for s, vt in zip(scores, values_t):
            e = jnp.exp(s - m)
            den = den + jnp.sum(e, axis=-1, keepdims=True)
            acc = acc + _dot_t(e.astype(bf16), vt.astype(bf16))
        return acc / den

    lane = lax.broadcasted_iota(jnp.int32, (1, PAGE_SIZE), 1)
    scores, values_t = [], []
    for s_i in range(C_TOPK):
        j = ix_ref[b, g * C_TOPK + s_i]
        jc = jnp.minimum(j, n_past - 1)
        kpos = (jc // per_page) * PAGE_SIZE + lane
        sc = jnp.dot(q8b, blocks[s_i][0].astype(bf16), preferred_element_type=f32) * scale \
            - slope * (t - kpos).astype(f32)
        ok = ((lane >> C_SEL_SHIFT) == (jc % per_page)) & (j < n_past)
        scores.append(jnp.where(ok, sc, NEG_INF))
        values_t.append(blocks[s_i][1])
    s_new = jnp.sum(q8 * own(ksn_ref), axis=-1, keepdims=True) * scale
    o_s = softmax_pv(scores, values_t, s_new, own(vsn_ref))

    nw = win_ref.shape[-1]
    wdist = (nw - lax.broadcasted_iota(jnp.int32, (1, nw), 1)).astype(f32)
    sw = jnp.dot(q8b, win_ref[0].astype(bf16), preferred_element_type=f32) * scale - slope * wdist
    s_new = jnp.sum(q8 * own(kwn_ref), axis=-1, keepdims=True) * scale
    o_w = softmax_pv([sw], [win_ref[1]], s_new, own(vwn_ref))

    gs = jax.nn.sigmoid(gate_ref[...] + bg_ref[...])
    glane = lax.broadcasted_iota(jnp.int32, (1, LANES), 1)
    outs = []
    for hh in range(C_HPG):
        gate = [jnp.sum(jnp.where(glane == (g * C_HPG + hh) * 3 + r, gs, 0.0), axis=-1, keepdims=True)
                for r in range(3)]
        outs.append(gate[0] * oc_ref[:, hh * dh:(hh + 1) * dh] + gate[1] * o_s[hh:hh + 1] + gate[2] * o_w[hh:hh + 1])
    o_ref[...] = jnp.concatenate(outs, axis=-1).astype(o_ref.dtype)


def nsa_sel_win_sample(page_table, sel_idx, cache_t, win_t, layer, proj, b_gate_p, o_c, t):
    Bs = proj.shape[0]
    G, dh = C_KV_HEADS, C_HEAD_DIM
    GD = G * dh
    assert win_t.shape[-1] == C_WINDOW and t % C_SEL_BLOCK == 0
    n_past = t // C_SEL_BLOCK
    per_page = PAGE_SIZE // C_SEL_BLOCK

    def blk_spec(s_i):
        def imap(b, g, pt, ix, sl):
            j = jnp.minimum(ix[b, g * C_TOPK + s_i], n_past - 1)
            return (layer, pt[b, j // per_page], 1, g, 0, 0)
        return pl.BlockSpec((None, None, 2, None, dh, PAGE_SIZE), imap)

    row = lambda w, blk: pl.BlockSpec((None, 1, w), lambda b, g, pt, ix, sl, blk=blk: (b, 0, blk))
    base = C_HEADS * dh // GD
    gate_blk = (C_HEADS * dh + 6 * GD) // LANES
    return pl.pallas_call(
        functools.partial(_sel_win_sample_body, t=t),
        out_shape=jax.ShapeDtypeStruct((Bs, 1, C_HEADS * dh), bf16),
        grid_spec=pltpu.PrefetchScalarGridSpec(
            num_scalar_prefetch=3,
            grid=(Bs, G),
            in_specs=[blk_spec(s_i) for s_i in range(C_TOPK)] + [
                pl.BlockSpec((None, None, 2, None, dh, C_WINDOW), lambda b, g, pt, ix, sl: (layer, b, 0, g, 0, 0)),
                pl.BlockSpec((None, 1, C_HPG * dh), lambda b, g, pt, ix, sl: (b, 0, g)),
                row(GD, base + 2), row(GD, base + 3), row(GD, base + 4), row(GD, base + 5),
                row(LANES, gate_blk),
                pl.BlockSpec((1, LANES), lambda b, g, pt, ix, sl: (0, 0)),
                pl.BlockSpec((None, 1, C_HPG * dh), lambda b, g, pt, ix, sl: (b, 0, g)),
            ],
            out_specs=pl.BlockSpec((None, 1, C_HPG * dh), lambda b, g, pt, ix, sl: (b, 0, g)),
        ),
        compiler_params=_params("parallel", "parallel"),
        name="nsa_sel_win_sample",
    )(page_table, sel_idx, jnp.asarray(SLOPES_C), *([cache_t] * C_TOPK), win_t, proj, proj, proj, proj, proj,
      proj, b_gate_p, o_c)


def _norm_body(x_ref, g_ref, o_ref):
    o_ref[...] = _rms(x_ref[...], g_ref[...])


def rmsnorm_rows(x, g, *, tm=1024):
    M, D = x.shape
    tm = min(tm, M)
    return pl.pallas_call(
        _norm_body,
        out_shape=jax.ShapeDtypeStruct((M, D), f32),
        grid=(M // tm,),
        in_specs=[pl.BlockSpec((tm, D), lambda i: (i, 0)), pl.BlockSpec((1, D), lambda i: (0, 0))],
        out_specs=pl.BlockSpec((tm, D), lambda i: (i, 0)),
        compiler_params=_params("parallel"),
        name="rmsnorm",
    )(x, g.reshape(1, D))


def _position_minor(cache):
    return jnp.transpose(cache, (0, 1, 3, 4, 5, 2))


def kernel(x_prompt, x_sample, cache_dil_w128, cache_dil_w512, cache_dil_w2048, state_gla, cache_nsa_win,
           cache_nsa_kv, cache_mem_kv, page_table, mem_prompt, g_mix, g_cross, g_mem, g_ffn, g_final,
           w_a_qkv, w_a_o, w_b_in, w_b_gate2, b_b_gate, g_b_head, w_b_o, w_c_in, b_c_gate, c_pos_k, c_pos_v,
           w_c_k1, w_c_k2, w_c_v1, w_c_v2, w_c_o, w_x_q, w_x_kv, w_x_o, w_ffn_in, w_ffn_out):
    Bp, L, D = x_prompt.shape
    Bs = x_sample.shape[0]
    depth = g_mix.shape[0]
    n_pages = page_table.shape[1]
    t_s = n_pages * PAGE_SIZE
    G, dh = C_KV_HEADS, C_HEAD_DIM
    GD = G * dh
    HK, HV = B_HEADS * B_DK, B_HEADS * B_DV
    cast = lambda a: a.astype(bf16)

    wa_qkv, wa_o = cast(w_a_qkv), cast(w_a_o)
    nb = 2 * HK + HV
    wb_in = cast(jnp.concatenate(
        [w_b_in[..., :nb], w_b_in[..., nb + B_GATE_RANK:], w_b_in[..., nb:nb + B_GATE_RANK],
         jnp.zeros(w_b_in.shape[:2] + (LANES - B_GATE_RANK,), f32)], axis=-1))
    wb_gate2 = jnp.pad(w_b_gate2, ((0, 0), (0, LANES - B_GATE_RANK), (0, 0)))
    wb_o = cast(w_b_o)
    nc = -(-w_c_in.shape[-1] // LANES) * LANES
    wc_in = cast(jnp.pad(w_c_in, ((0, 0), (0, 0), (0, nc - w_c_in.shape[-1]))))
    bc_gate = jnp.pad(b_c_gate, ((0, 0), (0, LANES - b_c_gate.shape[-1])))
    half = C_CMP_STRIDE * dh
    two_chunk = lambda w: cast(jnp.concatenate([w[:, :half], w[:, half:]], axis=-1))
    wc_k1, wc_v1 = two_chunk(w_c_k1), two_chunk(w_c_v1)
    wc_k2, wc_v2, wc_o = cast(w_c_k2), cast(w_c_v2), cast(w_c_o)
    pos_k = c_pos_k.reshape(-1, 2, half)
    pos_v = c_pos_v.reshape(-1, 2, half)
    wx_q, wx_kv, wx_o = cast(w_x_q), cast(w_x_kv), cast(w_x_o)
    wf_in, wf_out = cast(w_ffn_in), cast(w_ffn_out)

    Mp = Bp * L
    x = x_prompt.reshape(Mp, D)
    mem2d = mem_prompt.reshape(-1, D)
    dil_p = [[] for _ in range(A_GROUPS)]
    gla_p, rows_p, win_p, mem_p = [], [], [], []
    for l in range(depth):
        kind, j = l % N_MIXERS, l // N_MIXERS
        if kind == 0:
            qkv = mm(x, (wa_qkv, j), g=g_mix[l]).reshape(Bp, L, -1)
            outs = [dilated_group(qkv, grp) for grp in range(A_GROUPS)]
            x = dilated_merge_out([o.reshape(Mp, A_OUT) for o, _ in outs],
                                  [s.reshape(Mp, A_OUT) for _, s in outs], (wa_o, j), x)
            sect = A_GROUPS * A_OUT
            for grp in range(A_GROUPS):
                w = min(A_WINDOWS[grp], L)
                tail = lambda t: qkv[:, L - w:, t * sect + grp * A_OUT:t * sect + (grp + 1) * A_OUT].reshape(
                    Bp, w, A_SLOTS, A_HEAD_DIM)
                dil_p[grp].append(jnp.stack([tail(1), tail(2)], axis=2))
        elif kind == 1:
            proj = mm(x, (wb_in, j), g=g_mix[l]).reshape(Bp, L, -1)
            y, S = gla_prompt(proj, wb_gate2[j], b_b_gate[j], g_b_head[j].reshape(-1),
                              jnp.zeros((Bp, B_HEADS, B_DK, B_DV), f32))
            gla_p.append(S)
            x = mm(y.reshape(Mp, HV), (wb_o, j), res=x)
        else:
            proj = mm(x, (wc_in, j), g=g_mix[l]).reshape(Bp, L, -1)
            q_w = C_HEADS * dh
            rows = proj[..., q_w:q_w + 6 * GD].reshape(Bp, L, 6, G, dh)
            chunks = lambda a: a.reshape(Bp, L // C_CMP_STRIDE, C_CMP_STRIDE, G, dh).transpose(
                0, 3, 1, 2, 4).reshape(Bp, G, L // C_CMP_STRIDE, half)
            kc, vc = nsa_compress(chunks(rows[:, :, 0]), chunks(rows[:, :, 1]), pos_k[j], pos_v[j],
                                  wc_k1[j], wc_k2[j], wc_v1[j], wc_v2[j])
            seqs = [cast(rows[:, :, r].transpose(0, 2, 1, 3)) for r in range(2, 6)]
            o = nsa_attention_prompt(proj, bc_gate[j:j + 1], kc, vc, *seqs)
            x = mm(o.reshape(Mp, q_w), (wc_o, j), res=x)
            rows_p.append(rows[:, :, :4])
            win_p.append(rows[:, L - min(C_WINDOW, L):, 4:])
        mem_kv = mm(mem2d, (wx_kv, l), g=g_mem[l]).reshape(Bp, -1, 2 * D)
        mem_p.append(mem_kv.reshape(Bp, -1, 2, MEM_HEADS, D // MEM_HEADS))
        x = xattn_prompt(x.reshape(Bp, L, D), g_cross[l], (wx_q, l), cast(mem_kv), (wx_o, l)).reshape(Mp, D)
        x = ffn(x, g_ffn[l], (wf_in, l), (wf_out, l))
    y_prompt = rmsnorm_rows(x, g_final).reshape(Bp, L, D)

    x = x_sample.reshape(Bs, D)
    dil_t = [_position_minor(c) for c in (cache_dil_w128, cache_dil_w512, cache_dil_w2048)]
    nsa_t = _position_minor(cache_nsa_kv)
    win_t = _position_minor(cache_nsa_win)
    dil_s = [[] for _ in range(A_GROUPS)]
    gla_s, rows_s, win_s = [], [], []
    for l in range(depth):
        kind, j = l % N_MIXERS, l // N_MIXERS
        if kind == 0:
            qkv = mm(x, (wa_qkv, j), g=g_mix[l]).reshape(Bs, 1, -1)
            o = dil_sample(qkv, dil_t, j)
            x = mm(o.transpose(0, 2, 1).reshape(Bs, A_OUT), (wa_o, j), res=x)
            kv = qkv.reshape(Bs, 1, 3, A_GROUPS, A_SLOTS, A_HEAD_DIM)
            for grp in range(A_GROUPS):
                dil_s[grp].append(jnp.stack([kv[:, :, 1, grp], kv[:, :, 2, grp]], axis=2))
        elif kind == 1:
            proj = mm(x, (wb_in, j), g=g_mix[l]).reshape(Bs, 1, -1)
            y, S = gla_step(proj, wb_gate2[j], b_b_gate[j], g_b_head[j].reshape(-1), state_gla, j)
            gla_s.append(S)
            x = mm(y.reshape(Bs, HV), (wb_o, j), res=x)
        else:
            proj = mm(x, (wc_in, j), g=g_mix[l]).reshape(Bs, 1, -1)
            q_w = C_HEADS * dh
            kc, vc = nsa_compress_sample(page_table, nsa_t, j, pos_k[j], pos_v[j],
                                         wc_k1[j], wc_k2[j], wc_v1[j], wc_v2[j])
            o_c, idx = nsa_cmp_topk_sample(proj, kc, vc, t_s)
            sel_idx = idx[:, :G, :C_TOPK].reshape(Bs, G * C_TOPK)
            o = nsa_sel_win_sample(page_table, sel_idx, nsa_t, win_t, j, proj, bc_gate[j:j + 1], o_c, t_s)
            x = mm(o.reshape(Bs, q_w), (wc_o, j), res=x)
            rows = proj[..., q_w:q_w + 6 * GD].reshape(Bs, 1, 6, G, dh)
            rows_s.append(rows[:, :, :4])
            win_s.append(rows[:, :, 4:])
        q = mm(x, (wx_q, l), g=g_cross[l]).reshape(Bs, 1, D)
        o = xattn_sample(q, cache_mem_kv, l)
        x = mm(o.reshape(Bs, D), (wx_o, l), res=x)
        x = ffn(x, g_ffn[l], (wf_in, l), (wf_out, l))
    y_sample = rmsnorm_rows(x, g_final).reshape(Bs, 1, D)

    st = jnp.stack
    return (y_prompt, y_sample, st(dil_p[0]), st(dil_s[0]), st(dil_p[1]), st(dil_s[1]), st(dil_p[2]), st(dil_s[2]),
            st(gla_p), st(gla_s), st(win_p), st(win_s), st(rows_p), st(rows_s), st(mem_p))
```

```python
import functools

import jax
import jax.numpy as jnp
import numpy as np
from jax import lax
from jax.experimental import pallas as pl
from jax.experimental.pallas import tpu as pltpu

f32 = jnp.float32
bf16 = jnp.bfloat16

N_MIXERS = 3
A_WINDOWS = (128, 512, 2048)
A_DILATIONS = (1, 4, 16)
A_GROUPS = 3
A_SLOTS = 8
A_HEAD_DIM = 64
A_OUT = A_SLOTS * A_HEAD_DIM
A_WIN_STEPS = 128

B_HEADS = 4
B_DK = 128
B_DV = 256
B_GATE_RANK = 16
B_GATE_TAU = 16.0

C_HEADS = 16
C_KV_HEADS = 4
C_HPG = C_HEADS // C_KV_HEADS
C_HEAD_DIM = 64
C_CMP_BLOCK = 32
C_CMP_STRIDE = 16
C_SEL_BLOCK = 64
C_SEL_SHIFT = 6
C_TOPK = 16
C_WINDOW = 512
C_CMP_HIDDEN = 128

MEM_HEADS = 4
PAGE_SIZE = 128

RMS_EPS = 1e-6
NEG_INF = -1e30
FORCE = 1e30
TINY = 1e-30
REMOVED = -3e38

LANES = 128
VMEM_LIMIT_BYTES = 56 * 1024 * 1024
HIGHEST = lax.Precision.HIGHEST


def _alibi_slopes(n):
    return np.asarray(2.0 ** (-8.0 * np.arange(1, n + 1) / n), dtype=np.float32)


def _bf16_pieces(x, n):
    out, rest = [], np.asarray(x, np.float64)
    for _ in range(n):
        p = rest.astype(bf16).astype(np.float64)
        out.append(p)
        rest = rest - p
    return np.stack(out, axis=-1).astype(np.float32)


LOG2E = 1.4426950408889634
SLOPES_A = _alibi_slopes(A_GROUPS * A_SLOTS).reshape(A_GROUPS, A_SLOTS)
SLOPES_C = _alibi_slopes(C_HEADS)
N_PIECES = 3
SLOPE_PIECES = 4
SLOPES_C_PIECES = _bf16_pieces(SLOPES_C.astype(np.float64) * LOG2E, SLOPE_PIECES).reshape(-1)
VAL_ROWS = C_HEAD_DIM + 16


def _params(*sem):
    return pltpu.CompilerParams(dimension_semantics=sem, vmem_limit_bytes=VMEM_LIMIT_BYTES)


def _rms(x, g):
    return x * lax.rsqrt(jnp.mean(x * x, axis=-1, keepdims=True) + RMS_EPS) * g


def _dot_t(a, b, precision=None):
    return lax.dot_general(a, b, (((1,), (1,)), ((), ())), precision=precision, preferred_element_type=f32)


def _hdot(a, b):
    return jnp.dot(a, b, precision=HIGHEST, preferred_element_type=f32)


def _weight_spec(w, block, index):
    if isinstance(w, tuple):
        stack, layer = w
        return stack, pl.BlockSpec((None,) + block, lambda *ids: (layer,) + index(*ids))
    return w, pl.BlockSpec(block, index)


def _weight_shape(w):
    return w[0].shape[1:] if isinstance(w, tuple) else w.shape


def _pick_tile(n, target):
    best = LANES
    for t in range(LANES, min(n, target) + 1, LANES):
        if n % t == 0:
            best = t
    return best


def _mm_body(*refs, norm, res):
    it = iter(refs)
    x_ref = next(it)
    g_ref = next(it) if norm else None
    w_ref = next(it)
    r_ref = next(it) if res else None
    o_ref = next(it)
    xn_ref = next(it)

    @pl.when(pl.program_id(1) == 0)
    def _():
        x = x_ref[...].astype(f32)
        if norm:
            x = _rms(x, g_ref[...])
        xn_ref[...] = x.astype(bf16)

    acc = jnp.dot(xn_ref[...], w_ref[...], preferred_element_type=f32)
    if res:
        acc = acc + r_ref[...]
    o_ref[...] = acc.astype(o_ref.dtype)


def mm(x, w, *, g=None, res=None, out_dtype=f32, tm=1024, tn=1024):
    M, K = x.shape
    N = _weight_shape(w)[1]
    tm = min(tm, M)
    tn = _pick_tile(N, tn)
    assert M % tm == 0 and N % tn == 0
    args = [x]
    specs = [pl.BlockSpec((tm, K), lambda i, j: (i, 0))]
    if g is not None:
        args.append(g.reshape(1, K))
        specs.append(pl.BlockSpec((1, K), lambda i, j: (0, 0)))
    w_arr, w_spec = _weight_spec(w, (K, tn), lambda i, j: (0, j))
    args.append(w_arr)
    specs.append(w_spec)
    if res is not None:
        args.append(res)
        specs.append(pl.BlockSpec((tm, tn), lambda i, j: (i, j)))
    return pl.pallas_call(
        functools.partial(_mm_body, norm=g is not None, res=res is not None),
        out_shape=jax.ShapeDtypeStruct((M, N), out_dtype),
        grid=(M // tm, N // tn),
        in_specs=specs,
        out_specs=pl.BlockSpec((tm, tn), lambda i, j: (i, j)),
        scratch_shapes=[pltpu.VMEM((tm, K), bf16)],
        compiler_params=_params("parallel", "arbitrary"),
        name="mm",
    )(*args)


def _ffn_body(x_ref, g_ref, wa_ref, wb_ref, wo_ref, o_ref, xn_ref, acc_ref):
    j = pl.program_id(1)

    @pl.when(j == 0)
    def _():
        xn_ref[...] = _rms(x_ref[...], g_ref[...]).astype(bf16)
        acc_ref[...] = jnp.zeros_like(acc_ref)

    xn = xn_ref[...]
    a = jnp.dot(xn, wa_ref[...], preferred_element_type=f32)
    b = jnp.dot(xn, wb_ref[...], preferred_element_type=f32)
    h = (a * jax.nn.sigmoid(a) * b).astype(bf16)
    acc_ref[...] += jnp.dot(h, wo_ref[...], preferred_element_type=f32)

    @pl.when(j == pl.num_programs(1) - 1)
    def _():
        o_ref[...] = x_ref[...] + acc_ref[...]


def ffn(x, g, w_in, w_out, *, tm=1024, th=256):
    M, D = x.shape
    F = _weight_shape(w_out)[0]
    tm = min(tm, M)
    th = _pick_tile(F, th)
    nh = F // th
    wa, wa_spec = _weight_spec(w_in, (D, th), lambda i, j: (0, j))
    wb, wb_spec = _weight_spec(w_in, (D, th), lambda i, j: (0, j + nh))
    wo, wo_spec = _weight_spec(w_out, (th, D), lambda i, j: (j, 0))
    return pl.pallas_call(
        _ffn_body,
        out_shape=jax.ShapeDtypeStruct((M, D), f32),
        grid=(M // tm, nh),
        in_specs=[
            pl.BlockSpec((tm, D), lambda i, j: (i, 0)),
            pl.BlockSpec((1, D), lambda i, j: (0, 0)),
            wa_spec, wb_spec, wo_spec,
        ],
        out_specs=pl.BlockSpec((tm, D), lambda i, j: (i, 0)),
        scratch_shapes=[pltpu.VMEM((tm, D), bf16), pltpu.VMEM((tm, D), f32)],
        compiler_params=_params("parallel", "arbitrary"),
        name="ffn",
    )(x, g.reshape(1, D), wa, wb, wo)


def _xattn_body(x_ref, g_ref, wq_ref, kv_ref, wo_ref, o_ref, *, heads):
    x = x_ref[...]
    D = x.shape[-1]
    dh = D // heads
    xn = _rms(x, g_ref[...]).astype(bf16)
    q = jnp.dot(xn, wq_ref[...], preferred_element_type=f32).astype(bf16)
    outs = []
    for h in range(heads):
        kh = kv_ref[:, h * dh:(h + 1) * dh]
        vh = kv_ref[:, D + h * dh:D + (h + 1) * dh]
        s = _dot_t(q[:, h * dh:(h + 1) * dh], kh) * dh ** -0.5
        m = jnp.max(s, axis=-1, keepdims=True)
        e = jnp.exp(s - m)
        outs.append(jnp.dot(e.astype(bf16), vh, preferred_element_type=f32) / jnp.sum(e, axis=-1, keepdims=True))
    o = jnp.concatenate(outs, axis=-1).astype(bf16)
    o_ref[...] = x + jnp.dot(o, wo_ref[...], preferred_element_type=f32)


def xattn_prompt(x, g, wq, kv, wo, *, tq=512):
    B, L, D = x.shape
    Mem = kv.shape[1]
    tq = min(tq, L)
    wq, wq_spec = _weight_spec(wq, (D, D), lambda b, i: (0, 0))
    wo, wo_spec = _weight_spec(wo, (D, D), lambda b, i: (0, 0))
    return pl.pallas_call(
        functools.partial(_xattn_body, heads=MEM_HEADS),
        out_shape=jax.ShapeDtypeStruct((B, L, D), f32),
        grid=(B, L // tq),
        in_specs=[
            pl.BlockSpec((None, tq, D), lambda b, i: (b, i, 0)),
            pl.BlockSpec((1, D), lambda b, i: (0, 0)),
            wq_spec,
            pl.BlockSpec((None, Mem, 2 * D), lambda b, i: (b, 0, 0)),
            wo_spec,
        ],
        out_specs=pl.BlockSpec((None, tq, D), lambda b, i: (b, i, 0)),
        compiler_params=_params("parallel", "parallel"),
        name="xattn_prompt",
    )(x, g.reshape(1, D), wq, kv, wo)


def _dil_body(sl_ref, q_ref, kc_ref, kp_ref, vc_ref, vp_ref, o_ref, lse_ref, *, grp, dil, tq):
    i = pl.program_id(1)
    c = pl.program_id(2)
    ws = A_WIN_STEPS
    scale = A_HEAD_DIM ** -0.5
    row = lax.broadcasted_iota(jnp.int32, (ws, 2 * ws), 0)
    col = lax.broadcasted_iota(jnp.int32, (ws, 2 * ws), 1)
    dist = row + ws - col
    in_band = (dist >= 0) & (dist <= ws)
    valid_first = in_band & ((col >= ws) | (i > 0))
    distf = (dist * dil).astype(f32)
    lane = lax.broadcasted_iota(jnp.int32, (1, LANES), 1)
    first = lane < A_HEAD_DIM
    slopes = [sl_ref[grp * A_SLOTS + 2 * c + half] for half in range(2)]

    def rows(ref, r, start, n):
        return ref[pl.ds(r + start * dil, n, stride=dil), :] if dil > 1 else ref[start:start + n, :]

    def residue(r, carry):
        for j in range(tq // ws):
            q = rows(q_ref, r, j * ws, ws).astype(bf16)
            if j == 0:
                k = jnp.concatenate([rows(kp_ref, r, 0, ws), rows(kc_ref, r, 0, ws)], axis=0)
                v = jnp.concatenate([rows(vp_ref, r, 0, ws), rows(vc_ref, r, 0, ws)], axis=0)
            else:
                k = rows(kc_ref, r, (j - 1) * ws, 2 * ws)
                v = rows(vc_ref, r, (j - 1) * ws, 2 * ws)
            k, v = k.astype(bf16), v.astype(bf16)
            valid = valid_first if j == 0 else in_band
            halves = []
            for half in range(2):
                qm = jnp.where(first == (half == 0), q, jnp.zeros_like(q))
                s = _dot_t(qm, k) * scale - slopes[half] * distf
                s = jnp.where(valid, s, NEG_INF)
                m = jnp.max(s, axis=-1, keepdims=True)
                e = jnp.exp(s - m)
                den = jnp.sum(e, axis=-1, keepdims=True)
                pv = jnp.dot(e.astype(bf16), v, preferred_element_type=f32) / den
                halves.append((pv, m + jnp.log(den)))
            o_t = jnp.where(first, halves[0][0], halves[1][0])
            l_t = jnp.where(first, halves[0][1], halves[1][1])
            if dil > 1:
                o_ref[pl.ds(r + j * ws * dil, ws, stride=dil), :] = o_t
                lse_ref[pl.ds(r + j * ws * dil, ws, stride=dil), :] = l_t
            else:
                o_ref[j * ws:(j + 1) * ws, :] = o_t
                lse_ref[j * ws:(j + 1) * ws, :] = l_t
        return carry

    unroll = min(dil, DIL_UNROLL)

    def trip(u, carry):
        for k in range(unroll):
            residue(u * unroll + k, carry)
        return carry

    if dil > unroll:
        lax.fori_loop(0, dil // unroll, trip, 0)
    else:
        trip(0, 0)


DIL_TQ = (1024, 256, 128)
DIL_UNROLL = 4


def dilated_group(qkv, grp):
    B, L, W3 = qkv.shape
    d = A_DILATIONS[grp]
    tq = min(DIL_TQ[grp], L // d)
    rows_blk = tq * d
    prev_blk = A_WIN_STEPS * d
    assert L % rows_blk == 0 and tq % A_WIN_STEPS == 0
    ratio = tq // A_WIN_STEPS
    tiles = A_OUT // LANES
    sect = A_GROUPS * tiles

    def spec(t, prev=False):
        col = lambda c: t * sect + grp * tiles + c
        if prev:
            return pl.BlockSpec((None, prev_blk, LANES),
                                lambda b, i, c, sl: (b, jnp.maximum(i * ratio - 1, 0), col(c)))
        return pl.BlockSpec((None, rows_blk, LANES), lambda b, i, c, sl: (b, i, col(c)))

    out_spec = pl.BlockSpec((None, rows_blk, LANES), lambda b, i, c, sl: (b, i, c))
    return pl.pallas_call(
        functools.partial(_dil_body, grp=grp, dil=d, tq=tq),
        out_shape=[jax.ShapeDtypeStruct((B, L, A_OUT), f32)] * 2,
        grid_spec=pltpu.PrefetchScalarGridSpec(
            num_scalar_prefetch=1,
            grid=(B, L // rows_blk, tiles),
            in_specs=[spec(0), spec(1), spec(1, True), spec(2), spec(2, True)],
            out_specs=[out_spec, out_spec],
        ),
        compiler_params=_params("parallel", "parallel", "parallel"),
        name=f"dilated_g{grp}",
    )(jnp.asarray(SLOPES_A.reshape(-1)), qkv, qkv, qkv, qkv, qkv)


def _dil_out_body(o0, o1, o2, l0, l1, l2, w_ref, r_ref, out_ref):
    ls = [l0[...], l1[...], l2[...]]
    m = jnp.maximum(jnp.maximum(ls[0], ls[1]), ls[2])
    es = [jnp.exp(l - m) for l in ls]
    den = es[0] + es[1] + es[2]
    o = (es[0] / den) * o0[...] + (es[1] / den) * o1[...] + (es[2] / den) * o2[...]
    out_ref[...] = r_ref[...] + jnp.dot(o.astype(bf16), w_ref[...], preferred_element_type=f32)


def dilated_merge_out(os, lses, w_o, res, *, tm=512):
    M, D = res.shape
    tm = min(tm, M)
    row = lambda n: pl.BlockSpec((tm, n), lambda i: (i, 0))
    w_o, w_spec = _weight_spec(w_o, (A_OUT, D), lambda i: (0, 0))
    return pl.pallas_call(
        _dil_out_body,
        out_shape=jax.ShapeDtypeStruct((M, D), f32),
        grid=(M // tm,),
        in_specs=[row(A_OUT)] * 6 + [w_spec, row(D)],
        out_specs=row(D),
        compiler_params=_params("parallel"),
        name="dilated_merge_out",
    )(*os, *lses, w_o, res)


GLA_SUB = 16


def _log_sigmoid(z):
    return -(jnp.maximum(-z, 0.0) + jnp.log1p(jnp.exp(-jnp.abs(z))))


def _gla_body(q_ref, k_ref, v_ref, r_ref, gl_ref, w2_ref, bg_ref, gh_ref, s0_ref, y_ref, s_ref, a_ref):
    C = q_ref.shape[0]
    c = GLA_SUB

    @pl.when(pl.program_id(1) == 0)
    def _():
        s_ref[...] = s0_ref[...]

    z = _hdot(gl_ref[...], w2_ref[...]) + bg_ref[...]
    la = _log_sigmoid(z) / B_GATE_TAU
    rowC = lax.broadcasted_iota(jnp.int32, (C, C), 0)
    colC = lax.broadcasted_iota(jnp.int32, (C, C), 1)
    tri = (rowC >= colC).astype(f32)
    b_all = _hdot(tri, la)
    row_k = lax.broadcasted_iota(jnp.int32, (C, B_DK), 0)
    col_c = lax.broadcasted_iota(jnp.int32, (c, C), 1)
    row_c = lax.broadcasted_iota(jnp.int32, (c, 1), 0)

    for h in range(B_HEADS):
        bh = b_all[:, h * B_DK:(h + 1) * B_DK]
        qh = q_ref[:, h * B_DK:(h + 1) * B_DK] * B_DK ** -0.5
        kh = k_ref[:, h * B_DK:(h + 1) * B_DK]
        vh = v_ref[:, h * B_DV:(h + 1) * B_DV].astype(bf16)
        S = s_ref[h]
        o = jnp.dot((qh * jnp.exp(bh)).astype(bf16), S.astype(bf16), preferred_element_type=f32)

        for I in range(C // c):
            r0 = I * c
            qI, kI, bI = qh[r0:r0 + c], kh[r0:r0 + c], bh[r0:r0 + c]
            if I == 0:
                A_I = jnp.zeros((c, C), f32)
            else:
                beta = bh[r0 - 1:r0]
                qt = qI * jnp.exp(bI - beta)
                kt = kh * jnp.exp(jnp.where(row_k < r0, beta - bh, 0.0))
                A_I = jnp.where(col_c < r0, _dot_t(qt.astype(bf16), kt.astype(bf16)), 0.0)
            for j in range(c):
                ex = jnp.exp(jnp.minimum(bI - bI[j:j + 1], 0.0))
                tj = jnp.sum(qI * kI[j:j + 1] * ex, axis=-1, keepdims=True)
                tj = jnp.where(row_c >= j, tj, 0.0)
                A_I = jnp.where(col_c == r0 + j, tj, A_I)
            a_ref[r0:r0 + c, :] = A_I
        o = o + jnp.dot(a_ref[...].astype(bf16), vh, preferred_element_type=f32)

        b_last = bh[C - 1:C]
        kdec = jnp.transpose(kh * jnp.exp(b_last - bh)).astype(bf16)
        decay = jnp.transpose(jnp.broadcast_to(jnp.exp(b_last), (8, B_DK)))[:, 0:1]
        s_ref[h] = decay * S + jnp.dot(kdec, vh, preferred_element_type=f32)

        on = _rms(o, gh_ref[:, h * B_DV:(h + 1) * B_DV])
        rh = r_ref[:, h * B_DV:(h + 1) * B_DV]
        y_ref[:, h * B_DV:(h + 1) * B_DV] = (on * (rh * jax.nn.sigmoid(rh))).astype(y_ref.dtype)


def gla_prompt(proj, w_gate2p, b_gate, g_head, s0, *, chunk=128):
    B, L, _ = proj.shape
    HK, HV = B_HEADS * B_DK, B_HEADS * B_DV
    C = min(chunk, L)
    assert L % C == 0 and C % GLA_SUB == 0
    return pl.pallas_call(
        _gla_body,
        out_shape=[jax.ShapeDtypeStruct((B, L, HV), bf16),
                   jax.ShapeDtypeStruct((B, B_HEADS, B_DK, B_DV), f32)],
        grid=(B, L // C),
        in_specs=[
            pl.BlockSpec((None, C, HK), lambda b, i: (b, i, 0)),
            pl.BlockSpec((None, C, HK), lambda b, i: (b, i, 1)),
            pl.BlockSpec((None, C, HV), lambda b, i: (b, i, 1)),
            pl.BlockSpec((None, C, HV), lambda b, i: (b, i, 2)),
            pl.BlockSpec((None, C, LANES), lambda b, i: (b, i, (2 * HK + 2 * HV) // LANES)),
            pl.BlockSpec((LANES, HK), lambda b, i: (0, 0)),
            pl.BlockSpec((1, HK), lambda b, i: (0, 0)),
            pl.BlockSpec((1, HV), lambda b, i: (0, 0)),
            pl.BlockSpec((None, B_HEADS, B_DK, B_DV), lambda b, i: (b, 0, 0, 0)),
        ],
        out_specs=[pl.BlockSpec((None, C, HV), lambda b, i: (b, i, 0)),
                   pl.BlockSpec((None, B_HEADS, B_DK, B_DV), lambda b, i: (b, 0, 0, 0))],
        scratch_shapes=[pltpu.VMEM((C, C), f32)],
        compiler_params=_params("parallel", "arbitrary"),
        name="gla_prompt",
    )(proj, proj, proj, proj, proj, w_gate2p, b_gate.reshape(1, HK), g_head.reshape(1, HV), s0)


def _cmp_mlp_tail(z, posb, w2_ref):
    n = z.shape[0]
    H = C_CMP_HIDDEN
    hid = z[:, :H] + pltpu.roll(z[:, H:], n - 1, 0) + posb
    return jnp.dot(jax.nn.gelu(hid).astype(bf16), w2_ref[...], preferred_element_type=f32)


def _cmp_pos_bias(p_ref, w1_ref):
    H = C_CMP_HIDDEN
    pz = jnp.dot(p_ref[...].astype(bf16), w1_ref[...], preferred_element_type=f32)
    return pz[0:1, :H] + pz[1:2, H:]


def _cmp_body(xk_ref, xv_ref, pk_ref, pv_ref, wk1_ref, wk2_ref, wv1_ref, wv2_ref, ok_ref, ov_ref):
    for x_ref, p_ref, w1_ref, w2_ref, o_ref in ((xk_ref, pk_ref, wk1_ref, wk2_ref, ok_ref),
                                                (xv_ref, pv_ref, wv1_ref, wv2_ref, ov_ref)):
        z = jnp.dot(x_ref[...].astype(bf16), w1_ref[...], preferred_element_type=f32)
        o_ref[...] = _cmp_mlp_tail(z, _cmp_pos_bias(p_ref, w1_ref), w2_ref).astype(o_ref.dtype)


def nsa_compress(xk, xv, pos_k, pos_v, wk1, wk2, wv1, wv2):
    B, G, n, W = xk.shape
    x_spec = pl.BlockSpec((None, None, n, W), lambda b, g: (b, g, 0, 0))
    full = lambda a: pl.BlockSpec(a.shape, lambda b, g: (0,) * a.ndim)
    o_spec = pl.BlockSpec((None, None, n, C_HEAD_DIM), lambda b, g: (b, g, 0, 0))
    consts = [pos_k, pos_v, wk1, wk2, wv1, wv2]
    return pl.pallas_call(
        _cmp_body,
        out_shape=[jax.ShapeDtypeStruct((B, G, n, C_HEAD_DIM), bf16)] * 2,
        grid=(B, G),
        in_specs=[x_spec, x_spec] + [full(a) for a in consts],
        out_specs=[o_spec, o_spec],
        compiler_params=_params("parallel", "parallel"),
        name="nsa_compress",
    )(xk, xv, *consts)


def _topk_mask_t(score_t, blk_t):
    keep = jnp.zeros(score_t.shape, f32)
    big = float(score_t.shape[0])
    for _ in range(C_TOPK):
        mx = jnp.max(score_t, axis=0, keepdims=True)
        first = jnp.min(jnp.where(score_t == mx, blk_t, big), axis=0, keepdims=True)
        hit = blk_t == first
        keep = jnp.where(hit, jnp.where(mx > 0.5 * NEG_INF, 1.0, 0.0), keep)
        score_t = jnp.where(hit, REMOVED, score_t)
    return keep


def _reduce_rows(x, op):
    slabs = [x[r:r + 8] for r in range(0, x.shape[0], 8)]
    while len(slabs) > 1:
        slabs = [op(slabs[k], slabs[k + 1]) if k + 1 < len(slabs) else slabs[k] for k in range(0, len(slabs), 2)]
    red = jnp.max if op is jnp.maximum else jnp.sum
    return red(slabs[0], axis=0, keepdims=True)


def _softmax2_rows_t(s, mask):
    e = jnp.where(mask, jnp.exp2(s - _reduce_rows(s, jnp.maximum)), 0.0)
    return e / jnp.maximum(_reduce_rows(e, jnp.add), TINY)


def _nsa_body(sp_ref, q_ref, gate_ref, bg_ref, kc_ref, vc_ref, ks_ref, vs_ref, kw_ref, vw_ref,
              o_ref, m_ref, acc_ref, gs_ref, sa_ref, sb_ref, *, tk):
    g = pl.program_id(1)
    i = pl.program_id(2)
    tq = q_ref.shape[0]
    R = C_HPG * tq
    L = ks_ref.shape[0]
    n_cmp = kc_ref.shape[0]
    NS = LANES
    dh = C_HEAD_DIM
    q0 = i * tq

    q_t = jnp.transpose(q_ref[...] * (dh ** -0.5 * LOG2E))
    prow = lax.broadcasted_iota(jnp.int32, (LANES - dh, 1), 0)
    cols = []
    for hh in range(C_HPG):
        pc = jnp.zeros((LANES - dh, 1), f32)
        for n in reversed(range(SLOPE_PIECES)):
            pc = jnp.where(prow < 2 * n + 2, sp_ref[(g * C_HPG + hh) * SLOPE_PIECES + n], pc)
        cols.append(jnp.concatenate([q_t[hh * dh:(hh + 1) * dh], jnp.broadcast_to(pc, (LANES - dh, tq))], axis=0))
    qa_t = jnp.concatenate(cols, axis=1).astype(bf16)
    t1 = q0 + lax.broadcasted_iota(jnp.int32, (1, tq), 1)
    t = jnp.concatenate([t1] * C_HPG, axis=1)

    W = C_WINDOW + tq
    w0 = pl.multiple_of(q0, tq)
    s = jnp.dot(kw_ref[pl.ds(w0, W), :], qa_t, preferred_element_type=f32)
    slabs = []
    for r0 in range(0, W, LANES):
        wpos = q0 - C_WINDOW + r0 + lax.broadcasted_iota(jnp.int32, (LANES, 1), 0)
        sl = s[r0:r0 + LANES]
        if r0 < tq - 1:
            sl = jnp.where(wpos >= t - C_WINDOW, sl, NEG_INF)
        if r0 + LANES - 1 > C_WINDOW:
            sl = jnp.where(wpos <= t, sl, NEG_INF)
        slabs.append(sl)
    s = jnp.concatenate(slabs, axis=0)
    e = jnp.exp2(s - _reduce_rows(s, jnp.maximum))
    o_w = jnp.dot(vw_ref[:, pl.ds(w0, W)], e.astype(bf16), preferred_element_type=f32)
    o_w = o_w[0:dh] / o_w[dh:dh + 1]

    cpos = lax.broadcasted_iota(jnp.int32, (n_cmp, 1), 0) * C_CMP_STRIDE + (C_CMP_BLOCK - 1)
    mask = cpos <= t
    s = jnp.where(mask, jnp.dot(kc_ref[...], qa_t, preferred_element_type=f32), NEG_INF)
    p = _softmax2_rows_t(s, mask)
    o_c = jnp.dot(vc_ref[...], p.astype(bf16), preferred_element_type=f32)

    psum = p[:, 0:tq]
    for hh in range(1, C_HPG):
        psum = psum + p[:, hh * tq:(hh + 1) * tq]
    n_blk = -(-(L // C_SEL_BLOCK) // 8) * 8
    jb = lax.broadcasted_iota(jnp.int32, (n_blk, n_cmp), 0)
    c_start = lax.broadcasted_iota(jnp.int32, (n_blk, n_cmp), 1) * C_CMP_STRIDE
    cover = jnp.where(c_start < (jb + 1) * C_SEL_BLOCK,
                      jnp.where(c_start + C_CMP_BLOCK > jb * C_SEL_BLOCK, 1.0, 0.0), 0.0).astype(bf16)
    imp = jnp.zeros((n_blk, tq), f32)
    rest = psum
    for _ in range(N_PIECES):
        piece = rest.astype(bf16)
        imp = imp + jnp.dot(cover, piece, preferred_element_type=f32)
        rest = rest - piece.astype(f32)
    blk = lax.broadcasted_iota(jnp.int32, (n_blk, tq), 0)
    cur = t1 >> C_SEL_SHIFT
    forced = (blk == 0) | (blk == cur) | (blk == cur - 1)
    score = jnp.where(blk * C_SEL_BLOCK <= t1, jnp.where(forced, FORCE, imp), NEG_INF)
    keep = _topk_mask_t(score, blk.astype(f32))
    neg1 = jnp.concatenate([jnp.where(keep > 0.5, jnp.where(blk == cur, NEG_INF, 0.0), NEG_INF),
                            jnp.full((NS - n_blk, tq), NEG_INF, f32)], axis=0).astype(bf16)
    qa2_t = jnp.concatenate([qa_t, jnp.concatenate([neg1] * C_HPG, axis=1)], axis=0)

    d0 = pl.multiple_of(q0, tq)
    kpos = q0 + lax.broadcasted_iota(jnp.int32, (tq, 1), 0)
    sc = jnp.dot(ks_ref[pl.ds(d0, tq), 0:LANES], qa_t, preferred_element_type=f32)
    sc = jnp.where(kpos <= t, jnp.where((kpos >> C_SEL_SHIFT) == (t >> C_SEL_SHIFT), sc, NEG_INF), NEG_INF)
    m0 = _reduce_rows(sc, jnp.maximum)
    m_ref[...] = m0
    acc_ref[...] = jnp.dot(vs_ref[:, pl.ds(d0, tq)], jnp.exp2(sc - m0).astype(bf16), preferred_element_type=f32)

    n_tiles = q0 // tk + 1

    def scores(kt, s_ref):
        k0 = pl.multiple_of(jnp.minimum(kt, n_tiles - 1) * tk, tk)
        s_ref[...] = jnp.dot(ks_ref[pl.ds(k0, tk), :], qa2_t, preferred_element_type=f32)

    def absorb(kt, s_ref):
        k0 = pl.multiple_of(kt * tk, tk)
        sc = s_ref[...]
        m_old = m_ref[...]
        m_new = jnp.maximum(m_old, _reduce_rows(sc, jnp.maximum))
        ee = jnp.exp2(sc - m_new).astype(bf16)
        acc_ref[...] = jnp.exp2(m_old - m_new) * acc_ref[...] + jnp.dot(vs_ref[:, pl.ds(k0, tk)], ee,
                                                                         preferred_element_type=f32)
        m_ref[...] = m_new

    scores(0, sa_ref)

    def pair(j, carry):
        scores(2 * j + 1, sb_ref)
        absorb(2 * j, sa_ref)
        scores(2 * j + 2, sa_ref)
        absorb(2 * j + 1, sb_ref)
        return carry

    lax.fori_loop(0, n_tiles // 2, pair, 0)

    @pl.when(n_tiles % 2 == 1)
    def _():
        absorb(n_tiles - 1, sa_ref)

    o_s = acc_ref[0:dh, :] / acc_ref[dh:dh + 1, :]

    gs_ref[...] = jnp.transpose(jax.nn.sigmoid(gate_ref[...] + bg_ref[...]))

    def gate(r):
        return jnp.concatenate([gs_ref[pl.ds((g * C_HPG + hh) * 3 + r, 1), :] for hh in range(C_HPG)], axis=1)

    o = gate(0) * o_c + gate(1) * o_s + gate(2) * o_w
    o = jnp.concatenate([o[:, hh * tq:(hh + 1) * tq] for hh in range(C_HPG)], axis=0)
    o_ref[...] = jnp.transpose(o).astype(o_ref.dtype)


def _aug_keys(k, pos, one_hot_blocks):
    B, G, n, dh = k.shape
    hi = (pos // C_SEL_BLOCK) * C_SEL_BLOCK
    cols = np.zeros((n, LANES - dh), np.float32)
    for j in range(SLOPE_PIECES):
        cols[:, 2 * j] = hi
        cols[:, 2 * j + 1] = pos - hi
    parts = [k, jnp.broadcast_to(jnp.asarray(cols, bf16), (B, G, n, LANES - dh))]
    if one_hot_blocks:
        oh = (pos[:, None] // C_SEL_BLOCK == np.arange(LANES)[None, :]).astype(np.float32)
        parts.append(jnp.broadcast_to(jnp.asarray(oh, bf16), (B, G, n, LANES)))
    return jnp.concatenate(parts, axis=-1)


def nsa_attention_prompt(proj, b_gate_p, kc, vc, ks, vs, kw, vw, *, tq=256, tk=512):
    B, L, _ = proj.shape
    G, dh = C_KV_HEADS, C_HEAD_DIM
    tq = min(tq, L)
    tk = min(tk, L)
    assert tk % tq == 0 and L % tk == 0 and L // C_SEL_BLOCK <= LANES
    R = C_HPG * tq
    n_cmp = kc.shape[2]
    kc_a = _aug_keys(kc, np.arange(n_cmp) * C_CMP_STRIDE + (C_CMP_BLOCK - 1), False)
    ks_a = _aug_keys(ks, np.arange(L), True)
    kw_a = _aug_keys(kw, np.arange(L), False)
    lead = np.zeros((C_WINDOW, LANES), np.float32)
    lead[:, dh:dh + 2 * SLOPE_PIECES:2] = NEG_INF
    kw_a = jnp.concatenate([jnp.broadcast_to(jnp.asarray(lead, bf16), (B, G, C_WINDOW, LANES)), kw_a], axis=2)
    gate_blk = (C_HEADS * dh + 6 * G * dh) // LANES
    seq = lambda n, w: pl.BlockSpec((None, None, n, w), lambda b, g, i, sp: (b, g, 0, 0))
    tr = lambda a: jnp.swapaxes(a, 2, 3)

    def with_ones(a):
        n = a.shape[2]
        tail = np.zeros((VAL_ROWS - dh, n), np.float32)
        tail[0] = 1.0
        return jnp.concatenate([tr(a), jnp.broadcast_to(jnp.asarray(tail, bf16), (B, G, VAL_ROWS - dh, n))], axis=2)

    return pl.pallas_call(
        functools.partial(_nsa_body, tk=tk),
        out_shape=jax.ShapeDtypeStruct((B, L, C_HEADS * dh), bf16),
        grid_spec=pltpu.PrefetchScalarGridSpec(
            num_scalar_prefetch=1,
            grid=(B, G, L // tq),
            in_specs=[
                pl.BlockSpec((None, tq, C_HPG * dh), lambda b, g, i, sp: (b, i, g)),
                pl.BlockSpec((None, tq, LANES), lambda b, g, i, sp: (b, i, gate_blk)),
                pl.BlockSpec((1, LANES), lambda b, g, i, sp: (0, 0)),
                seq(n_cmp, LANES), seq(dh, n_cmp), seq(L, 2 * LANES), seq(VAL_ROWS, L), seq(L + C_WINDOW, LANES),
                seq(VAL_ROWS, L + C_WINDOW),
            ],
            out_specs=pl.BlockSpec((None, tq, C_HPG * dh), lambda b, g, i, sp: (b, i, g)),
            scratch_shapes=[pltpu.VMEM((1, R), f32), pltpu.VMEM((VAL_ROWS, R), f32),
                            pltpu.VMEM((LANES, tq), f32), pltpu.VMEM((tk, R), f32), pltpu.VMEM((tk, R), f32)],
        ),
        compiler_params=_params("parallel", "parallel", "parallel"),
        name="nsa_attention_prompt",
    )(jnp.asarray(SLOPES_C_PIECES), proj, proj, b_gate_p, kc_a, tr(vc), ks_a, with_ones(vs), kw_a,
      jnp.pad(with_ones(vw), ((0, 0), (0, 0), (0, 0), (C_WINDOW, 0))))


def _slot_rows(ref, off, n, w):
    rows = jnp.concatenate([ref[:, off + s * w:off + (s + 1) * w] for s in range(n)]
                           + ([jnp.zeros((8 - n, w), f32)] if n < 8 else []), axis=0)
    return jnp.concatenate([rows, jnp.zeros((8, LANES - w), f32)], axis=-1) if w < LANES else rows


def _dil_sample_body(sl_ref, qkv_ref, c0_ref, c1_ref, c2_ref, o_ref):
    dh = A_HEAD_DIM
    scale = dh ** -0.5
    sect = A_GROUPS * A_OUT
    scores, news, v_cols = [], [], []
    for grp, c_ref in enumerate((c0_ref, c1_ref, c2_ref)):
        W = c_ref.shape[-1]
        d = A_DILATIONS[grp]
        q8 = _slot_rows(qkv_ref, grp * A_OUT, A_SLOTS, dh)
        kn8 = _slot_rows(qkv_ref, sect + grp * A_OUT, A_SLOTS, dh)
        vn8 = _slot_rows(qkv_ref, 2 * sect + grp * A_OUT, A_SLOTS, dh)
        q_cols = jnp.transpose(q8)[:dh]
        v_cols.append(jnp.transpose(vn8)[:dh])
        news.append(jnp.sum(q8 * kn8, axis=-1, keepdims=True) * scale)
        pos = lax.broadcasted_iota(jnp.int32, (1, W), 1)
        rows = [jnp.sum(c_ref[0, s] * q_cols[:, s:s + 1], axis=0, keepdims=True) for s in range(A_SLOTS)]
        slope = jnp.concatenate([jnp.full((1, 1), sl_ref[grp * A_SLOTS + s], f32) for s in range(A_SLOTS)], axis=0)
        sc = jnp.concatenate(rows, axis=0) * scale - slope * (W - pos).astype(f32)
        scores.append(jnp.where((pos & (d - 1)) == 0, sc, NEG_INF))
    m = functools.reduce(jnp.maximum, [jnp.max(s, axis=-1, keepdims=True) for s in scores] + news)
    den = jnp.zeros((A_SLOTS, 1), f32)
    acc = [jnp.zeros((dh, 1), f32) for _ in range(A_SLOTS)]
    for grp, c_ref in enumerate((c0_ref, c1_ref, c2_ref)):
        e = jnp.exp(scores[grp] - m)
        en = jnp.exp(news[grp] - m)
        den = den + jnp.sum(e, axis=-1, keepdims=True) + en
        for s in range(A_SLOTS):
            acc[s] = acc[s] + jnp.sum(c_ref[1, s] * e[s:s + 1, :], axis=-1, keepdims=True) \
                + en[s:s + 1, :] * v_cols[grp][:, s:s + 1]
    o_ref[...] = jnp.concatenate([acc[s] / den[s:s + 1, :] for s in range(A_SLOTS)], axis=-1)


def dil_sample(qkv, caches_t, layer):
    Bs = qkv.shape[0]
    for grp, c in enumerate(caches_t):
        assert c.shape[-1] == A_WINDOWS[grp]
    c_spec = lambda c: pl.BlockSpec((None, None) + c.shape[2:], lambda b, sl: (layer, b, 0, 0, 0, 0))
    return pl.pallas_call(
        _dil_sample_body,
        out_shape=jax.ShapeDtypeStruct((Bs, A_HEAD_DIM, A_SLOTS), f32),
        grid_spec=pltpu.PrefetchScalarGridSpec(
            num_scalar_prefetch=1,
            grid=(Bs,),
            in_specs=[pl.BlockSpec((None, 1, qkv.shape[-1]), lambda b, sl: (b, 0, 0))]
            + [c_spec(c) for c in caches_t],
            out_specs=pl.BlockSpec((None, A_HEAD_DIM, A_SLOTS), lambda b, sl: (b, 0, 0)),
        ),
        compiler_params=_params("parallel"),
        name="dil_sample",
    )(jnp.asarray(SLOPES_A.reshape(-1)), qkv, *caches_t)


def _xattn_sample_body(q_ref, c_ref, o_ref):
    dh = q_ref.shape[-1] // MEM_HEADS
    outs = []
    for h in range(MEM_HEADS):
        qh = q_ref[:, h * dh:(h + 1) * dh]
        s = jnp.sum(c_ref[:, 0, h, :] * qh, axis=-1, keepdims=True) * dh ** -0.5
        e = jnp.exp(s - jnp.max(s, axis=0, keepdims=True))
        outs.append(jnp.sum(e * c_ref[:, 1, h, :], axis=0, keepdims=True) / jnp.sum(e, axis=0, keepdims=True))
    o_ref[...] = jnp.concatenate(outs, axis=-1)


def xattn_sample(q, cache, layer):
    Bs, _, D = q.shape
    return pl.pallas_call(
        _xattn_sample_body,
        out_shape=jax.ShapeDtypeStruct((Bs, 1, D), f32),
        grid=(Bs,),
        in_specs=[pl.BlockSpec((None, 1, D), lambda b: (b, 0, 0)),
                  pl.BlockSpec((None, None) + cache.shape[2:], lambda b: (layer, b, 0, 0, 0, 0))],
        out_specs=pl.BlockSpec((None, 1, D), lambda b: (b, 0, 0)),
        compiler_params=_params("parallel"),
        name="xattn_sample",
    )(q, cache)


def _gla_step_body(q_ref, k_ref, v_ref, r_ref, gl_ref, w2_ref, bg_ref, gh_ref, s0_ref, y_ref, s_ref):
    z = _hdot(gl_ref[...], w2_ref[...]) + bg_ref[...]
    a = jnp.exp(_log_sigmoid(z) / B_GATE_TAU)
    pad = jnp.zeros((5, B_DK), f32)
    for h in range(B_HEADS):
        ks_ = slice(h * B_DK, (h + 1) * B_DK)
        vs_ = slice(h * B_DV, (h + 1) * B_DV)
        rows = jnp.concatenate([a[:, ks_], k_ref[:, ks_], q_ref[:, ks_] * B_DK ** -0.5, pad], axis=0)
        cols = jnp.transpose(rows)
        S = cols[:, 0:1] * s0_ref[h] + cols[:, 1:2] * v_ref[:, vs_]
        s_ref[h] = S
        o = jnp.sum(cols[:, 2:3] * S, axis=0, keepdims=True)
        rh = r_ref[:, vs_]
        y_ref[:, vs_] = (_rms(o, gh_ref[:, vs_]) * (rh * jax.nn.sigmoid(rh))).astype(y_ref.dtype)


def gla_step(proj, w_gate2p, b_gate, g_head, state, layer):
    Bs = proj.shape[0]
    HK, HV = B_HEADS * B_DK, B_HEADS * B_DV
    return pl.pallas_call(
        _gla_step_body,
        out_shape=[jax.ShapeDtypeStruct((Bs, 1, HV), bf16),
                   jax.ShapeDtypeStruct((Bs, B_HEADS, B_DK, B_DV), f32)],
        grid=(Bs,),
        in_specs=[
            pl.BlockSpec((None, 1, HK), lambda b: (b, 0, 0)),
            pl.BlockSpec((None, 1, HK), lambda b: (b, 0, 1)),
            pl.BlockSpec((None, 1, HV), lambda b: (b, 0, 1)),
            pl.BlockSpec((None, 1, HV), lambda b: (b, 0, 2)),
            pl.BlockSpec((None, 1, LANES), lambda b: (b, 0, (2 * HK + 2 * HV) // LANES)),
            pl.BlockSpec((LANES, HK), lambda b: (0, 0)),
            pl.BlockSpec((1, HK), lambda b: (0, 0)),
            pl.BlockSpec((1, HV), lambda b: (0, 0)),
            pl.BlockSpec((None, None, B_HEADS, B_DK, B_DV), lambda b: (layer, b, 0, 0, 0)),
        ],
        out_specs=[pl.BlockSpec((None, 1, HV), lambda b: (b, 0, 0)),
                   pl.BlockSpec((None, B_HEADS, B_DK, B_DV), lambda b: (b, 0, 0, 0))],
        compiler_params=_params("parallel"),
        name="gla_step",
    )(proj, proj, proj, proj, proj, w_gate2p, b_gate.reshape(1, HK), g_head.reshape(1, HV), state)


PAGES_PER_STEP = 8


def _cmp_sample_body(pt_ref, *refs):
    pages = refs[:PAGES_PER_STEP]
    (pk_ref, pv_ref, wk1_ref, wk2_ref, wv1_ref, wv2_ref, wkp_ref, wvp_ref,
     ok_ref, ov_ref, x_ref) = refs[PAGES_PER_STEP:]
    i = pl.program_id(1)
    H = C_CMP_HIDDEN
    n_tiles = x_ref.shape[0]
    per_kind = n_tiles // 2
    per_page = PAGE_SIZE // C_CMP_STRIDE
    out_row = lax.broadcasted_iota(jnp.int32, (PAGE_SIZE, PAGE_SIZE), 0)
    in_row = lax.broadcasted_iota(jnp.int32, (PAGE_SIZE, PAGE_SIZE), 1)
    shift = per_page.bit_length() - 1
    perm = (in_row == (out_row & (per_page - 1)) * C_CMP_STRIDE + (out_row >> shift)).astype(bf16)
    for k in range(PAGES_PER_STEP):
        c0 = pl.multiple_of((i * PAGES_PER_STEP + k) * per_page, per_page)
        for kind in range(2):
            for tile in range(per_kind):
                xt = pages[k][kind, 2 * tile:2 * tile + 2].reshape(LANES, PAGE_SIZE).astype(bf16)
                xs = _dot_t(perm, xt)
                for p in range(C_CMP_STRIDE):
                    x_ref[kind * per_kind + tile, p, pl.ds(c0, per_page), :] = xs[p * per_page:(p + 1) * per_page]

    @pl.when(i == pl.num_programs(1) - 1)
    def _():
        n = x_ref.shape[2]
        for kind, (p_ref, w1_ref, w2_ref, wp_ref, o_ref) in enumerate(
                ((pk_ref, wk1_ref, wk2_ref, wkp_ref, ok_ref), (pv_ref, wv1_ref, wv2_ref, wvp_ref, ov_ref))):
            posb = _cmp_pos_bias(p_ref, w1_ref)
            outs = []
            for tile in range(per_kind):
                ct = kind * per_kind + tile
                z2 = jnp.zeros((n, 4 * H), f32)
                for pp in range(C_CMP_STRIDE // 2):
                    xp = jnp.concatenate([x_ref[ct, 2 * pp], x_ref[ct, 2 * pp + 1]], axis=-1).astype(bf16)
                    z2 = z2 + jnp.dot(xp, wp_ref[pp], preferred_element_type=f32)
                outs += [_cmp_mlp_tail(z, posb, w2_ref) for z in (z2[:, :2 * H], z2[:, 2 * H:])]
            o_ref[...] = jnp.concatenate(outs, axis=-1)


def _pair_block_diag(w1):
    dh = C_HEAD_DIM
    w = w1.reshape(C_CMP_STRIDE, dh, w1.shape[1])
    z = jnp.zeros_like(w)
    bd = jnp.concatenate([jnp.concatenate([w, z], axis=-1), jnp.concatenate([z, w], axis=-1)], axis=1)
    return bd.reshape(C_CMP_STRIDE // 2, 4 * dh, 2 * w1.shape[1])


def nsa_compress_sample(page_table, cache_t, layer, pos_k, pos_v, wk1, wk2, wv1, wv2):
    Bs, n_pages = page_table.shape
    G, dh = C_KV_HEADS, C_HEAD_DIM
    assert n_pages % PAGES_PER_STEP == 0 and cache_t.shape[2:] == (4, G, dh, PAGE_SIZE) and 2 * dh == LANES
    n_rows = n_pages * PAGE_SIZE
    n = n_rows // C_CMP_STRIDE
    page = lambda k: pl.BlockSpec((None, None, 2, G, dh, PAGE_SIZE),
                                  lambda b, i, pt, k=k: (layer, pt[b, i * PAGES_PER_STEP + k], 0, 0, 0, 0))
    consts = [pos_k, pos_v, wk1, wk2, wv1, wv2, _pair_block_diag(wk1), _pair_block_diag(wv1)]
    full = lambda a: pl.BlockSpec(a.shape, lambda b, i, pt: (0,) * a.ndim)
    o_spec = pl.BlockSpec((None, n, G * dh), lambda b, i, pt: (b, 0, 0))
    return pl.pallas_call(
        _cmp_sample_body,
        out_shape=[jax.ShapeDtypeStruct((Bs, n, G * dh), f32)] * 2,
        grid_spec=pltpu.PrefetchScalarGridSpec(
            num_scalar_prefetch=1,
            grid=(Bs, n_pages // PAGES_PER_STEP),
            in_specs=[page(k) for k in range(PAGES_PER_STEP)] + [full(a) for a in consts],
            out_specs=[o_spec, o_spec],
            scratch_shapes=[pltpu.VMEM((2 * G * dh // LANES, C_CMP_STRIDE, n, LANES), f32)],
        ),
        compiler_params=_params("parallel", "arbitrary"),
        name="nsa_compress_sample",
    )(page_table, *([cache_t] * PAGES_PER_STEP), *consts)


def _cmp_topk_sample_body(sl_ref, q_ref, kc_ref, vc_ref, oc_ref, idx_ref, *, t):
    G, dh = C_KV_HEADS, C_HEAD_DIM
    n = kc_ref.shape[0]
    n_slc = t // C_SEL_BLOCK + 1
    NS = -(-n_slc // LANES) * LANES
    cpos = lax.broadcasted_iota(jnp.int32, (1, n), 1) * C_CMP_STRIDE + (C_CMP_BLOCK - 1)
    mask = cpos <= t
    c_start = lax.broadcasted_iota(jnp.int32, (n, NS), 0) * C_CMP_STRIDE
    jb = lax.broadcasted_iota(jnp.int32, (n, NS), 1)
    cover = jnp.where(c_start < (jb + 1) * C_SEL_BLOCK,
                      jnp.where(c_start + C_CMP_BLOCK > jb * C_SEL_BLOCK, 1.0, 0.0), 0.0).astype(bf16)
    blk = lax.broadcasted_iota(jnp.int32, (1, NS), 1)
    cur = t // C_SEL_BLOCK
    forced = (blk == 0) | (blk == cur) | (blk == cur - 1)
    lane = lax.broadcasted_iota(jnp.int32, (1, LANES), 1)
    o_parts, p_rows = [], []
    for g in range(G):
        q8 = jnp.concatenate([q_ref[:, (g * C_HPG + hh) * dh:(g * C_HPG + hh + 1) * dh] for hh in range(C_HPG)]
                             + [jnp.zeros((8 - C_HPG, dh), f32)], axis=0)
        slope = jnp.concatenate([jnp.full((1, 1), sl_ref[g * C_HPG + hh], f32) for hh in range(C_HPG)]
                                + [jnp.zeros((8 - C_HPG, 1), f32)], axis=0)
        s = _dot_t(q8.astype(bf16), kc_ref[:, g * dh:(g + 1) * dh].astype(bf16)) * dh ** -0.5
        s = s - slope * (float(t) - cpos.astype(f32))
        s = jnp.where(mask, s, NEG_INF)
        m = jnp.max(s, axis=-1, keepdims=True)
        e = jnp.where(mask, jnp.exp(s - m), 0.0)
        p = e / jnp.maximum(jnp.sum(e, axis=-1, keepdims=True), TINY)
        o = jnp.dot(p.astype(bf16), vc_ref[:, g * dh:(g + 1) * dh].astype(bf16), preferred_element_type=f32)
        o_parts += [o[hh:hh + 1] for hh in range(C_HPG)]
        p_rows.append(jnp.sum(p[0:C_HPG], axis=0, keepdims=True))
    oc_ref[...] = jnp.concatenate(o_parts, axis=-1)
    rest = jnp.concatenate(p_rows + [jnp.zeros((8 - G, n), f32)], axis=0)
    imp = jnp.zeros((8, NS), f32)
    for _ in range(N_PIECES):
        piece = rest.astype(bf16)
        imp = imp + jnp.dot(piece, cover, preferred_element_type=f32)
        rest = rest - piece.astype(f32)
    row = lax.broadcasted_iota(jnp.int32, (8, 1), 0)
    score = jnp.where((blk * C_SEL_BLOCK <= t) & (row < G), jnp.where(forced, FORCE, imp), NEG_INF)
    rank = jnp.zeros((8, NS), f32)
    for k in range(1, NS):
        other = pltpu.roll(score, k, 1)
        ahead = jnp.where(blk >= k, jnp.where(other >= score, 1.0, 0.0), jnp.where(other > score, 1.0, 0.0))
        rank = rank + ahead
    blk_f = blk.astype(f32)
    idx = jnp.zeros((8, LANES), f32)
    for r in range(C_TOPK):
        pick = jnp.sum(jnp.where(rank == float(r), blk_f, 0.0), axis=-1, keepdims=True)
        idx = jnp.where(lane == r, pick, idx)
    idx_ref[...] = idx.astype(jnp.int32)


def nsa_cmp_topk_sample(proj, kc, vc, t):
    Bs = proj.shape[0]
    n, W = kc.shape[1:]
    HD = C_HEADS * C_HEAD_DIM
    return pl.pallas_call(
        functools.partial(_cmp_topk_sample_body, t=t),
        out_shape=[jax.ShapeDtypeStruct((Bs, 1, HD), f32), jax.ShapeDtypeStruct((Bs, 8, LANES), jnp.int32)],
        grid_spec=pltpu.PrefetchScalarGridSpec(
            num_scalar_prefetch=1,
            grid=(Bs,),
            in_specs=[pl.BlockSpec((None, 1, HD), lambda b, sl: (b, 0, 0)),
                      pl.BlockSpec((None, n, W), lambda b, sl: (b, 0, 0)),
                      pl.BlockSpec((None, n, W), lambda b, sl: (b, 0, 0))],
            out_specs=[pl.BlockSpec((None, 1, HD), lambda b, sl: (b, 0, 0)),
                       pl.BlockSpec((None, 8, LANES), lambda b, sl: (b, 0, 0))],
        ),
        compiler_params=_params("parallel"),
        name="nsa_cmp_topk_sample",
    )(jnp.asarray(SLOPES_C), proj, kc, vc)


def _sel_win_sample_body(pt_ref, ix_ref, sl_ref, *refs, t):
    blocks = refs[:C_TOPK]
    (win_ref, q_ref, ksn_ref, vsn_ref, kwn_ref, vwn_ref, gate_ref, bg_ref, oc_ref, o_ref) = refs[C_TOPK:]
    b = pl.program_id(0)
    g = pl.program_id(1)
    G, dh = C_KV_HEADS, C_HEAD_DIM
    scale = dh ** -0.5
    n_past = t // C_SEL_BLOCK
    per_page = PAGE_SIZE // C_SEL_BLOCK
    q8 = jnp.concatenate([q_ref[:, hh * dh:(hh + 1) * dh] for hh in range(C_HPG)]
                         + [jnp.zeros((8 - C_HPG, dh), f32)], axis=0)
    q8b = q8.astype(bf16)
    slope = jnp.concatenate([jnp.full((1, 1), sl_ref[g * C_HPG + hh], f32) for hh in range(C_HPG)]
                            + [jnp.zeros((8 - C_HPG, 1), f32)], axis=0)

    def own(ref):
        return functools.reduce(jnp.add, [jnp.where(g == gg, ref[:, gg * dh:(gg + 1) * dh], 0.0) for gg in range(G)])

    def softmax_pv(scores, values_t, s_new, v_new):
        m = functools.reduce(jnp.maximum, [jnp.max(s, axis=-1, keepdims=True) for s in scores] + [s_new])
        e_new = jnp.exp(s_new - m)
        den = e_new
        acc = e_new * v_new
        for s, vt in zip(scores, values_t):
            e = jnp.exp(s - m)
            den = den + jnp.sum(e, axis=-1, keepdims=True)
            acc = acc + _dot_t(e.astype(bf16), vt.astype(bf16))
        return acc / den

    lane = lax.broadcasted_iota(jnp.int32, (1, PAGE_SIZE), 1)
    scores, values_t = [], []
    for s_i in range(C_TOPK):
        j = ix_ref[b, g * C_TOPK + s_i]
        jc = jnp.minimum(j, n_past - 1)
        kpos = (jc // per_page) * PAGE_SIZE + lane
        sc = jnp.dot(q8b, blocks[s_i][0].astype(bf16), preferred_element_type=f32) * scale \
            - slope * (t - kpos).astype(f32)
        ok = ((lane >> C_SEL_SHIFT) == (jc % per_page)) & (j < n_past)
        scores.append(jnp.where(ok, sc, NEG_INF))
        values_t.append(blocks[s_i][1])
    s_new = jnp.sum(q8 * own(ksn_ref), axis=-1, keepdims=True) * scale
    o_s = softmax_pv(scores, values_t, s_new, own(vsn_ref))

    nw = win_ref.shape[-1]
    wdist = (nw - lax.broadcasted_iota(jnp.int32, (1, nw), 1)).astype(f32)
    sw = jnp.dot(q8b, win_ref[0].astype(bf16), preferred_element_type=f32) * scale - slope * wdist
    s_new = jnp.sum(q8 * own(kwn_ref), axis=-1, keepdims=True) * scale
    o_w = softmax_pv([sw], [win_ref[1]], s_new, own(vwn_ref))

    gs = jax.nn.sigmoid(gate_ref[...] + bg_ref[...])
    glane = lax.broadcasted_iota(jnp.int32, (1, LANES), 1)
    outs = []
    for hh in range(C_HPG):
        gate = [jnp.sum(jnp.where(glane == (g * C_HPG + hh) * 3 + r, gs, 0.0), axis=-1, keepdims=True)
                for r in range(3)]
        outs.append(gate[0] * oc_ref[:, hh * dh:(hh + 1) * dh] + gate[1] * o_s[hh:hh + 1] + gate[2] * o_w[hh:hh + 1])
    o_ref[...] = jnp.concatenate(outs, axis=-1).astype(o_ref.dtype)


def nsa_sel_win_sample(page_table, sel_idx, cache_t, win_t, layer, proj, b_gate_p, o_c, t):
    Bs = proj.shape[0]
    G, dh = C_KV_HEADS, C_HEAD_DIM
    GD = G * dh
    assert win_t.shape[-1] == C_WINDOW and t % C_SEL_BLOCK == 0
    n_past = t // C_SEL_BLOCK
    per_page = PAGE_SIZE // C_SEL_BLOCK

    def blk_spec(s_i):
        def imap(b, g, pt, ix, sl):
            j = jnp.minimum(ix[b, g * C_TOPK + s_i], n_past - 1)
            return (layer, pt[b, j // per_page], 1, g, 0, 0)
        return pl.BlockSpec((None, None, 2, None, dh, PAGE_SIZE), imap)

    row = lambda w, blk: pl.BlockSpec((None, 1, w), lambda b, g, pt, ix, sl, blk=blk: (b, 0, blk))
    base = C_HEADS * dh // GD
    gate_blk = (C_HEADS * dh + 6 * GD) // LANES
    return pl.pallas_call(
        functools.partial(_sel_win_sample_body, t=t),
        out_shape=jax.ShapeDtypeStruct((Bs, 1, C_HEADS * dh), bf16),
        grid_spec=pltpu.PrefetchScalarGridSpec(
            num_scalar_prefetch=3,
            grid=(Bs, G),
            in_specs=[blk_spec(s_i) for s_i in range(C_TOPK)] + [
                pl.BlockSpec((None, None, 2, None, dh, C_WINDOW), lambda b, g, pt, ix, sl: (layer, b, 0, g, 0, 0)),
                pl.BlockSpec((None, 1, C_HPG * dh), lambda b, g, pt, ix, sl: (b, 0, g)),
                row(GD, base + 2), row(GD, base + 3), row(GD, base + 4), row(GD, base + 5),
                row(LANES, gate_blk),
                pl.BlockSpec((1, LANES), lambda b, g, pt, ix, sl: (0, 0)),
                pl.BlockSpec((None, 1, C_HPG * dh), lambda b, g, pt, ix, sl: (b, 0, g)),
            ],
            out_specs=pl.BlockSpec((None, 1, C_HPG * dh), lambda b, g, pt, ix, sl: (b, 0, g)),
        ),
        compiler_params=_params("parallel", "parallel"),
        name="nsa_sel_win_sample",
    )(page_table, sel_idx, jnp.asarray(SLOPES_C), *([cache_t] * C_TOPK), win_t, proj, proj, proj, proj, proj,
      proj, b_gate_p, o_c)


def _norm_body(x_ref, g_ref, o_ref):
    o_ref[...] = _rms(x_ref[...], g_ref[...])


def rmsnorm_rows(x, g, *, tm=1024):
    M, D = x.shape
    tm = min(tm, M)
    return pl.pallas_call(
        _norm_body,
        out_shape=jax.ShapeDtypeStruct((M, D), f32),
        grid=(M // tm,),
        in_specs=[pl.BlockSpec((tm, D), lambda i: (i, 0)), pl.BlockSpec((1, D), lambda i: (0, 0))],
        out_specs=pl.BlockSpec((tm, D), lambda i: (i, 0)),
        compiler_params=_params("parallel"),
        name="rmsnorm",
    )(x, g.reshape(1, D))


def _position_minor(cache):
    return jnp.transpose(cache, (0, 1, 3, 4, 5, 2))


def kernel(x_prompt, x_sample, cache_dil_w128, cache_dil_w512, cache_dil_w2048, state_gla, cache_nsa_win,
           cache_nsa_kv, cache_mem_kv, page_table, mem_prompt, g_mix, g_cross, g_mem, g_ffn, g_final,
           w_a_qkv, w_a_o, w_b_in, w_b_gate2, b_b_gate, g_b_head, w_b_o, w_c_in, b_c_gate, c_pos_k, c_pos_v,
           w_c_k1, w_c_k2, w_c_v1, w_c_v2, w_c_o, w_x_q, w_x_kv, w_x_o, w_ffn_in, w_ffn_out):
    Bp, L, D = x_prompt.shape
    Bs = x_sample.shape[0]
    depth = g_mix.shape[0]
    n_pages = page_table.shape[1]
    t_s = n_pages * PAGE_SIZE
    G, dh = C_KV_HEADS, C_HEAD_DIM
    GD = G * dh
    HK, HV = B_HEADS * B_DK, B_HEADS * B_DV
    cast = lambda a: a.astype(bf16)

    wa_qkv, wa_o = cast(w_a_qkv), cast(w_a_o)
    nb = 2 * HK + HV
    wb_in = cast(jnp.concatenate(
        [w_b_in[..., :nb], w_b_in[..., nb + B_GATE_RANK:], w_b_in[..., nb:nb + B_GATE_RANK],
         jnp.zeros(w_b_in.shape[:2] + (LANES - B_GATE_RANK,), f32)], axis=-1))
    wb_gate2 = jnp.pad(w_b_gate2, ((0, 0), (0, LANES - B_GATE_RANK), (0, 0)))
    wb_o = cast(w_b_o)
    nc = -(-w_c_in.shape[-1] // LANES) * LANES
    wc_in = cast(jnp.pad(w_c_in, ((0, 0), (0, 0), (0, nc - w_c_in.shape[-1]))))
    bc_gate = jnp.pad(b_c_gate, ((0, 0), (0, LANES - b_c_gate.shape[-1])))
    half = C_CMP_STRIDE * dh
    two_chunk = lambda w: cast(jnp.concatenate([w[:, :half], w[:, half:]], axis=-1))
    wc_k1, wc_v1 = two_chunk(w_c_k1), two_chunk(w_c_v1)
    wc_k2, wc_v2, wc_o = cast(w_c_k2), cast(w_c_v2), cast(w_c_o)
    pos_k = c_pos_k.reshape(-1, 2, half)
    pos_v = c_pos_v.reshape(-1, 2, half)
    wx_q, wx_kv, wx_o = cast(w_x_q), cast(w_x_kv), cast(w_x_o)
    wf_in, wf_out = cast(w_ffn_in), cast(w_ffn_out)

    Mp = Bp * L
    x = x_prompt.reshape(Mp, D)
    mem2d = mem_prompt.reshape(-1, D)
    dil_p = [[] for _ in range(A_GROUPS)]
    gla_p, rows_p, win_p, mem_p = [], [], [], []
    for l in range(depth):
        kind, j = l % N_MIXERS, l // N_MIXERS
        if kind == 0:
            qkv = mm(x, (wa_qkv, j), g=g_mix[l]).reshape(Bp, L, -1)
            outs = [dilated_group(qkv, grp) for grp in range(A_GROUPS)]
            x = dilated_merge_out([o.reshape(Mp, A_OUT) for o, _ in outs],
                                  [s.reshape(Mp, A_OUT) for _, s in outs], (wa_o, j), x)
            sect = A_GROUPS * A_OUT
            for grp in range(A_GROUPS):
                w = min(A_WINDOWS[grp], L)
                tail = lambda t: qkv[:, L - w:, t * sect + grp * A_OUT:t * sect + (grp + 1) * A_OUT].reshape(
                    Bp, w, A_SLOTS, A_HEAD_DIM)
                dil_p[grp].append(jnp.stack([tail(1), tail(2)], axis=2))
        elif kind == 1:
            proj = mm(x, (wb_in, j), g=g_mix[l]).reshape(Bp, L, -1)
            y, S = gla_prompt(proj, wb_gate2[j], b_b_gate[j], g_b_head[j].reshape(-1),
                              jnp.zeros((Bp, B_HEADS, B_DK, B_DV), f32))
            gla_p.append(S)
            x = mm(y.reshape(Mp, HV), (wb_o, j), res=x)
        else:
            proj = mm(x, (wc_in, j), g=g_mix[l]).reshape(Bp, L, -1)
            q_w = C_HEADS * dh
            rows = proj[..., q_w:q_w + 6 * GD].reshape(Bp, L, 6, G, dh)
            chunks = lambda a: a.reshape(Bp, L // C_CMP_STRIDE, C_CMP_STRIDE, G, dh).transpose(
                0, 3, 1, 2, 4).reshape(Bp, G, L // C_CMP_STRIDE, half)
            kc, vc = nsa_compress(chunks(rows[:, :, 0]), chunks(rows[:, :, 1]), pos_k[j], pos_v[j],
                                  wc_k1[j], wc_k2[j], wc_v1[j], wc_v2[j])
            seqs = [cast(rows[:, :, r].transpose(0, 2, 1, 3)) for r in range(2, 6)]
            o = nsa_attention_prompt(proj, bc_gate[j:j + 1], kc, vc, *seqs)
            x = mm(o.reshape(Mp, q_w), (wc_o, j), res=x)
            rows_p.append(rows[:, :, :4])
            win_p.append(rows[:, L - min(C_WINDOW, L):, 4:])
        mem_kv = mm(mem2d, (wx_kv, l), g=g_mem[l]).reshape(Bp, -1, 2 * D)
        mem_p.append(mem_kv.reshape(Bp, -1, 2, MEM_HEADS, D // MEM_HEADS))
        x = xattn_prompt(x.reshape(Bp, L, D), g_cross[l], (wx_q, l), cast(mem_kv), (wx_o, l)).reshape(Mp, D)
        x = ffn(x, g_ffn[l], (wf_in, l), (wf_out, l))
    y_prompt = rmsnorm_rows(x, g_final).reshape(Bp, L, D)

    x = x_sample.reshape(Bs, D)
    dil_t = [_position_minor(c) for c in (cache_dil_w128, cache_dil_w512, cache_dil_w2048)]
    nsa_t = _position_minor(cache_nsa_kv)
    win_t = _position_minor(cache_nsa_win)
    dil_s = [[] for _ in range(A_GROUPS)]
    gla_s, rows_s, win_s = [], [], []
    for l in range(depth):
        kind, j = l % N_MIXERS, l // N_MIXERS
        if kind == 0:
            qkv = mm(x, (wa_qkv, j), g=g_mix[l]).reshape(Bs, 1, -1)
            o = dil_sample(qkv, dil_t, j)
            x = mm(o.transpose(0, 2, 1).reshape(Bs, A_OUT), (wa_o, j), res=x)
            kv = qkv.reshape(Bs, 1, 3, A_GROUPS, A_SLOTS, A_HEAD_DIM)
            for grp in range(A_GROUPS):
                dil_s[grp].append(jnp.stack([kv[:, :, 1, grp], kv[:, :, 2, grp]], axis=2))
        elif kind == 1:
            proj = mm(x, (wb_in, j), g=g_mix[l]).reshape(Bs, 1, -1)
            y, S = gla_step(proj, wb_gate2[j], b_b_gate[j], g_b_head[j].reshape(-1), state_gla, j)
            gla_s.append(S)
            x = mm(y.reshape(Bs, HV), (wb_o, j), res=x)
        else:
            proj = mm(x, (wc_in, j), g=g_mix[l]).reshape(Bs, 1, -1)
            q_w = C_HEADS * dh
            kc, vc = nsa_compress_sample(page_table, nsa_t, j, pos_k[j], pos_v[j],
                                         wc_k1[j], wc_k2[j], wc_v1[j], wc_v2[j])
            o_c, idx = nsa_cmp_topk_sample(proj, kc, vc, t_s)
            sel_idx = idx[:, :G, :C_TOPK].reshape(Bs, G * C_TOPK)
            o = nsa_sel_win_sample(page_table, sel_idx, nsa_t, win_t, j, proj, bc_gate[j:j + 1], o_c, t_s)
            x = mm(o.reshape(Bs, q_w), (wc_o, j), res=x)
            rows = proj[..., q_w:q_w + 6 * GD].reshape(Bs, 1, 6, G, dh)
            rows_s.append(rows[:, :, :4])
            win_s.append(rows[:, :, 4:])
        q = mm(x, (wx_q, l), g=g_cross[l]).reshape(Bs, 1, D)
        o = xattn_sample(q, cache_mem_kv, l)
        x = mm(o.reshape(Bs, D), (wx_o, l), res=x)
        x = ffn(x, g_ffn[l], (wf_in, l), (wf_out, l))
    y_sample = rmsnorm_rows(x, g_final).reshape(Bs, 1, D)

    st = jnp.stack
    return (y_prompt, y_sample, st(dil_p[0]), st(dil_s[0]), st(dil_p[1]), st(dil_s[1]), st(dil_p[2]), st(dil_s[2]),
            st(gla_p), st(gla_s), st(win_p), st(win_s), st(rows_p), st(rows_s), st(mem_p))
```

```python
import functools

import jax
import jax.numpy as jnp
import numpy as np
from jax import lax
from jax.experimental import pallas as pl
from jax.experimental.pallas import tpu as pltpu

f32 = jnp.float32
bf16 = jnp.bfloat16

N_MIXERS = 3
A_WINDOWS = (128, 512, 2048)
A_DILATIONS = (1, 4, 16)
A_GROUPS = 3
A_SLOTS = 8
A_HEAD_DIM = 64
A_OUT = A_SLOTS * A_HEAD_DIM
A_WIN_STEPS = 128

B_HEADS = 4
B_DK = 128
B_DV = 256
B_GATE_RANK = 16
B_GATE_TAU = 16.0

C_HEADS = 16
C_KV_HEADS = 4
C_HPG = C_HEADS // C_KV_HEADS
C_HEAD_DIM = 64
C_CMP_BLOCK = 32
C_CMP_STRIDE = 16
C_SEL_BLOCK = 64
C_SEL_SHIFT = 6
C_TOPK = 16
C_WINDOW = 512
C_CMP_HIDDEN = 128

MEM_HEADS = 4
PAGE_SIZE = 128

RMS_EPS = 1e-6
NEG_INF = -1e30
FORCE = 1e30
TINY = 1e-30
REMOVED = -3e38

LANES = 128
VMEM_LIMIT_BYTES = 56 * 1024 * 1024
HIGHEST = lax.Precision.HIGHEST


def _alibi_slopes(n):
    return np.asarray(2.0 ** (-8.0 * np.arange(1, n + 1) / n), dtype=np.float32)


def _bf16_pieces(x, n):
    out, rest = [], np.asarray(x, np.float64)
    for _ in range(n):
        p = rest.astype(bf16).astype(np.float64)
        out.append(p)
        rest = rest - p
    return np.stack(out, axis=-1).astype(np.float32)


LOG2E = 1.4426950408889634
SLOPES_A = _alibi_slopes(A_GROUPS * A_SLOTS).reshape(A_GROUPS, A_SLOTS)
SLOPES_C = _alibi_slopes(C_HEADS)
N_PIECES = 3
SLOPE_PIECES = 4
SLOPES_C_PIECES = _bf16_pieces(SLOPES_C.astype(np.float64) * LOG2E, SLOPE_PIECES).reshape(-1)
VAL_ROWS = C_HEAD_DIM + 16


def _params(*sem):
    return pltpu.CompilerParams(dimension_semantics=sem, vmem_limit_bytes=VMEM_LIMIT_BYTES)


def _rms(x, g):
    return x * lax.rsqrt(jnp.mean(x * x, axis=-1, keepdims=True) + RMS_EPS) * g


def _dot_t(a, b, precision=None):
    return lax.dot_general(a, b, (((1,), (1,)), ((), ())), precision=precision, preferred_element_type=f32)


def _hdot(a, b):
    return jnp.dot(a, b, precision=HIGHEST, preferred_element_type=f32)


def _weight_spec(w, block, index):
    if isinstance(w, tuple):
        stack, layer = w
        return stack, pl.BlockSpec((None,) + block, lambda *ids: (layer,) + index(*ids))
    return w, pl.BlockSpec(block, index)


def _weight_shape(w):
    return w[0].shape[1:] if isinstance(w, tuple) else w.shape


def _pick_tile(n, target):
    best = LANES
    for t in range(LANES, min(n, target) + 1, LANES):
        if n % t == 0:
            best = t
    return best


def _mm_body(*refs, norm, res):
    it = iter(refs)
    x_ref = next(it)
    g_ref = next(it) if norm else None
    w_ref = next(it)
    r_ref = next(it) if res else None
    o_ref = next(it)
    xn_ref = next(it)

    @pl.when(pl.program_id(1) == 0)
    def _():
        x = x_ref[...].astype(f32)
        if norm:
            x = _rms(x, g_ref[...])
        xn_ref[...] = x.astype(bf16)

    acc = jnp.dot(xn_ref[...], w_ref[...], preferred_element_type=f32)
    if res:
        acc = acc + r_ref[...]
    o_ref[...] = acc.astype(o_ref.dtype)


def mm(x, w, *, g=None, res=None, out_dtype=f32, tm=1024, tn=1024):
    M, K = x.shape
    N = _weight_shape(w)[1]
    tm = min(tm, M)
    tn = _pick_tile(N, tn)
    assert M % tm == 0 and N % tn == 0
    args = [x]
    specs = [pl.BlockSpec((tm, K), lambda i, j: (i, 0))]
    if g is not None:
        args.append(g.reshape(1, K))
        specs.append(pl.BlockSpec((1, K), lambda i, j: (0, 0)))
    w_arr, w_spec = _weight_spec(w, (K, tn), lambda i, j: (0, j))
    args.append(w_arr)
    specs.append(w_spec)
    if res is not None:
        args.append(res)
        specs.append(pl.BlockSpec((tm, tn), lambda i, j: (i, j)))
    return pl.pallas_call(
        functools.partial(_mm_body, norm=g is not None, res=res is not None),
        out_shape=jax.ShapeDtypeStruct((M, N), out_dtype),
        grid=(M // tm, N // tn),
        in_specs=specs,
        out_specs=pl.BlockSpec((tm, tn), lambda i, j: (i, j)),
        scratch_shapes=[pltpu.VMEM((tm, K), bf16)],
        compiler_params=_params("parallel", "arbitrary"),
        name="mm",
    )(*args)


def _ffn_body(x_ref, g_ref, wa_ref, wb_ref, wo_ref, o_ref, xn_ref, acc_ref):
    j = pl.program_id(1)

    @pl.when(j == 0)
    def _():
        xn_ref[...] = _rms(x_ref[...], g_ref[...]).astype(bf16)
        acc_ref[...] = jnp.zeros_like(acc_ref)

    xn = xn_ref[...]
    a = jnp.dot(xn, wa_ref[...], preferred_element_type=f32)
    b = jnp.dot(xn, wb_ref[...], preferred_element_type=f32)
    h = (a * jax.nn.sigmoid(a) * b).astype(bf16)
    acc_ref[...] += jnp.dot(h, wo_ref[...], preferred_element_type=f32)

    @pl.when(j == pl.num_programs(1) - 1)
    def _():
        o_ref[...] = x_ref[...] + acc_ref[...]


def ffn(x, g, w_in, w_out, *, tm=1024, th=256):
    M, D = x.shape
    F = _weight_shape(w_out)[0]
    tm = min(tm, M)
    th = _pick_tile(F, th)
    nh = F // th
    wa, wa_spec = _weight_spec(w_in, (D, th), lambda i, j: (0, j))
    wb, wb_spec = _weight_spec(w_in, (D, th), lambda i, j: (0, j + nh))
    wo, wo_spec = _weight_spec(w_out, (th, D), lambda i, j: (j, 0))
    return pl.pallas_call(
        _ffn_body,
        out_shape=jax.ShapeDtypeStruct((M, D), f32),
        grid=(M // tm, nh),
        in_specs=[
            pl.BlockSpec((tm, D), lambda i, j: (i, 0)),
            pl.BlockSpec((1, D), lambda i, j: (0, 0)),
            wa_spec, wb_spec, wo_spec,
        ],
        out_specs=pl.BlockSpec((tm, D), lambda i, j: (i, 0)),
        scratch_shapes=[pltpu.VMEM((tm, D), bf16), pltpu.VMEM((tm, D), f32)],
        compiler_params=_params("parallel", "arbitrary"),
        name="ffn",
    )(x, g.reshape(1, D), wa, wb, wo)


def _xattn_body(x_ref, g_ref, wq_ref, kv_ref, wo_ref, o_ref, *, heads):
    x = x_ref[...]
    D = x.shape[-1]
    dh = D // heads
    xn = _rms(x, g_ref[...]).astype(bf16)
    q = jnp.dot(xn, wq_ref[...], preferred_element_type=f32).astype(bf16)
    outs = []
    for h in range(heads):
        kh = kv_ref[:, h * dh:(h + 1) * dh]
        vh = kv_ref[:, D + h * dh:D + (h + 1) * dh]
        s = _dot_t(q[:, h * dh:(h + 1) * dh], kh) * dh ** -0.5
        m = jnp.max(s, axis=-1, keepdims=True)
        e = jnp.exp(s - m)
        outs.append(jnp.dot(e.astype(bf16), vh, preferred_element_type=f32) / jnp.sum(e, axis=-1, keepdims=True))
    o = jnp.concatenate(outs, axis=-1).astype(bf16)
    o_ref[...] = x + jnp.dot(o, wo_ref[...], preferred_element_type=f32)


def xattn_prompt(x, g, wq, kv, wo, *, tq=512):
    B, L, D = x.shape
    Mem = kv.shape[1]
    tq = min(tq, L)
    wq, wq_spec = _weight_spec(wq, (D, D), lambda b, i: (0, 0))
    wo, wo_spec = _weight_spec(wo, (D, D), lambda b, i: (0, 0))
    return pl.pallas_call(
        functools.partial(_xattn_body, heads=MEM_HEADS),
        out_shape=jax.ShapeDtypeStruct((B, L, D), f32),
        grid=(B, L // tq),
        in_specs=[
            pl.BlockSpec((None, tq, D), lambda b, i: (b, i, 0)),
            pl.BlockSpec((1, D), lambda b, i: (0, 0)),
            wq_spec,
            pl.BlockSpec((None, Mem, 2 * D), lambda b, i: (b, 0, 0)),
            wo_spec,
        ],
        out_specs=pl.BlockSpec((None, tq, D), lambda b, i: (b, i, 0)),
        compiler_params=_params("parallel", "parallel"),
        name="xattn_prompt",
    )(x, g.reshape(1, D), wq, kv, wo)


def _dil_body(sl_ref, q_ref, kc_ref, kp_ref, vc_ref, vp_ref, o_ref, lse_ref, *, grp, dil, tq):
    i = pl.program_id(1)
    c = pl.program_id(2)
    ws = A_WIN_STEPS
    scale = A_HEAD_DIM ** -0.5
    row = lax.broadcasted_iota(jnp.int32, (ws, 2 * ws), 0)
    col = lax.broadcasted_iota(jnp.int32, (ws, 2 * ws), 1)
    dist = row + ws - col
    in_band = (dist >= 0) & (dist <= ws)
    valid_first = in_band & ((col >= ws) | (i > 0))
    distf = (dist * dil).astype(f32)
    lane = lax.broadcasted_iota(jnp.int32, (1, LANES), 1)
    first = lane < A_HEAD_DIM
    slopes = [sl_ref[grp * A_SLOTS + 2 * c + half] for half in range(2)]

    def rows(ref, r, start, n):
        return ref[pl.ds(r + start * dil, n, stride=dil), :] if dil > 1 else ref[start:start + n, :]

    def residue(r, carry):
        for j in range(tq // ws):
            q = rows(q_ref, r, j * ws, ws).astype(bf16)
            if j == 0:
                k = jnp.concatenate([rows(kp_ref, r, 0, ws), rows(kc_ref, r, 0, ws)], axis=0)
                v = jnp.concatenate([rows(vp_ref, r, 0, ws), rows(vc_ref, r, 0, ws)], axis=0)
            else:
                k = rows(kc_ref, r, (j - 1) * ws, 2 * ws)
                v = rows(vc_ref, r, (j - 1) * ws, 2 * ws)
            k, v = k.astype(bf16), v.astype(bf16)
            valid = valid_first if j == 0 else in_band
            halves = []
            for half in range(2):
                qm = jnp.where(first == (half == 0), q, jnp.zeros_like(q))
                s = _dot_t(qm, k) * scale - slopes[half] * distf
                s = jnp.where(valid, s, NEG_INF)
                m = jnp.max(s, axis=-1, keepdims=True)
                e = jnp.exp(s - m)
                den = jnp.sum(e, axis=-1, keepdims=True)
                pv = jnp.dot(e.astype(bf16), v, preferred_element_type=f32) / den
                halves.append((pv, m + jnp.log(den)))
            o_t = jnp.where(first, halves[0][0], halves[1][0])
            l_t = jnp.where(first, halves[0][1], halves[1][1])
            if dil > 1:
                o_ref[pl.ds(r + j * ws * dil, ws, stride=dil), :] = o_t
                lse_ref[pl.ds(r + j * ws * dil, ws, stride=dil), :] = l_t
            else:
                o_ref[j * ws:(j + 1) * ws, :] = o_t
                lse_ref[j * ws:(j + 1) * ws, :] = l_t
        return carry

    unroll = min(dil, DIL_UNROLL)

    def trip(u, carry):
        for k in range(unroll):
            residue(u * unroll + k, carry)
        return carry

    if dil > unroll:
        lax.fori_loop(0, dil // unroll, trip, 0)
    else:
        trip(0, 0)


DIL_TQ = (1024, 256, 128)
DIL_UNROLL = 4


def dilated_group(qkv, grp):
    B, L, W3 = qkv.shape
    d = A_DILATIONS[grp]
    tq = min(DIL_TQ[grp], L // d)
    rows_blk = tq * d
    prev_blk = A_WIN_STEPS * d
    assert L % rows_blk == 0 and tq % A_WIN_STEPS == 0
    ratio = tq // A_WIN_STEPS
    tiles = A_OUT // LANES
    sect = A_GROUPS * tiles

    def spec(t, prev=False):
        col = lambda c: t * sect + grp * tiles + c
        if prev:
            return pl.BlockSpec((None, prev_blk, LANES),
                                lambda b, i, c, sl: (b, jnp.maximum(i * ratio - 1, 0), col(c)))
        return pl.BlockSpec((None, rows_blk, LANES), lambda b, i, c, sl: (b, i, col(c)))

    out_spec = pl.BlockSpec((None, rows_blk, LANES), lambda b, i, c, sl: (b, i, c))
    return pl.pallas_call(
        functools.partial(_dil_body, grp=grp, dil=d, tq=tq),
        out_shape=[jax.ShapeDtypeStruct((B, L, A_OUT), f32)] * 2,
        grid_spec=pltpu.PrefetchScalarGridSpec(
            num_scalar_prefetch=1,
            grid=(B, L // rows_blk, tiles),
            in_specs=[spec(0), spec(1), spec(1, True), spec(2), spec(2, True)],
            out_specs=[out_spec, out_spec],
        ),
        compiler_params=_params("parallel", "parallel", "parallel"),
        name=f"dilated_g{grp}",
    )(jnp.asarray(SLOPES_A.reshape(-1)), qkv, qkv, qkv, qkv, qkv)


def _dil_out_body(o0, o1, o2, l0, l1, l2, w_ref, r_ref, out_ref):
    ls = [l0[...], l1[...], l2[...]]
    m = jnp.maximum(jnp.maximum(ls[0], ls[1]), ls[2])
    es = [jnp.exp(l - m) for l in ls]
    den = es[0] + es[1] + es[2]
    o = (es[0] / den) * o0[...] + (es[1] / den) * o1[...] + (es[2] / den) * o2[...]
    out_ref[...] = r_ref[...] + jnp.dot(o.astype(bf16), w_ref[...], preferred_element_type=f32)


def dilated_merge_out(os, lses, w_o, res, *, tm=512):
    M, D = res.shape
    tm = min(tm, M)
    row = lambda n: pl.BlockSpec((tm, n), lambda i: (i, 0))
    w_o, w_spec = _weight_spec(w_o, (A_OUT, D), lambda i: (0, 0))
    return pl.pallas_call(
        _dil_out_body,
        out_shape=jax.ShapeDtypeStruct((M, D), f32),
        grid=(M // tm,),
        in_specs=[row(A_OUT)] * 6 + [w_spec, row(D)],
        out_specs=row(D),
        compiler_params=_params("parallel"),
        name="dilated_merge_out",
    )(*os, *lses, w_o, res)


GLA_SUB = 16


def _log_sigmoid(z):
    return -(jnp.maximum(-z, 0.0) + jnp.log1p(jnp.exp(-jnp.abs(z))))


def _gla_body(q_ref, k_ref, v_ref, r_ref, gl_ref, w2_ref, bg_ref, gh_ref, s0_ref, y_ref, s_ref, a_ref):
    C = q_ref.shape[0]
    c = GLA_SUB

    @pl.when(pl.program_id(1) == 0)
    def _():
        s_ref[...] = s0_ref[...]

    z = _hdot(gl_ref[...], w2_ref[...]) + bg_ref[...]
    la = _log_sigmoid(z) / B_GATE_TAU
    rowC = lax.broadcasted_iota(jnp.int32, (C, C), 0)
    colC = lax.broadcasted_iota(jnp.int32, (C, C), 1)
    tri = (rowC >= colC).astype(f32)
    b_all = _hdot(tri, la)
    row_k = lax.broadcasted_iota(jnp.int32, (C, B_DK), 0)
    col_c = lax.broadcasted_iota(jnp.int32, (c, C), 1)
    row_c = lax.broadcasted_iota(jnp.int32, (c, 1), 0)

    for h in range(B_HEADS):
        bh = b_all[:, h * B_DK:(h + 1) * B_DK]
        qh = q_ref[:, h * B_DK:(h + 1) * B_DK] * B_DK ** -0.5
        kh = k_ref[:, h * B_DK:(h + 1) * B_DK]
        vh = v_ref[:, h * B_DV:(h + 1) * B_DV].astype(bf16)
        S = s_ref[h]
        o = jnp.dot((qh * jnp.exp(bh)).astype(bf16), S.astype(bf16), preferred_element_type=f32)

        for I in range(C // c):
            r0 = I * c
            qI, kI, bI = qh[r0:r0 + c], kh[r0:r0 + c], bh[r0:r0 + c]
            if I == 0:
                A_I = jnp.zeros((c, C), f32)
            else:
                beta = bh[r0 - 1:r0]
                qt = qI * jnp.exp(bI - beta)
                kt = kh * jnp.exp(jnp.where(row_k < r0, beta - bh, 0.0))
                A_I = jnp.where(col_c < r0, _dot_t(qt.astype(bf16), kt.astype(bf16)), 0.0)
            for j in range(c):
                ex = jnp.exp(jnp.minimum(bI - bI[j:j + 1], 0.0))
                tj = jnp.sum(qI * kI[j:j + 1] * ex, axis=-1, keepdims=True)
                tj = jnp.where(row_c >= j, tj, 0.0)
                A_I = jnp.where(col_c == r0 + j, tj, A_I)
            a_ref[r0:r0 + c, :] = A_I
        o = o + jnp.dot(a_ref[...].astype(bf16), vh, preferred_element_type=f32)

        b_last = bh[C - 1:C]
        kdec = jnp.transpose(kh * jnp.exp(b_last - bh)).astype(bf16)
        decay = jnp.transpose(jnp.broadcast_to(jnp.exp(b_last), (8, B_DK)))[:, 0:1]
        s_ref[h] = decay * S + jnp.dot(kdec, vh, preferred_element_type=f32)

        on = _rms(o, gh_ref[:, h * B_DV:(h + 1) * B_DV])
        rh = r_ref[:, h * B_DV:(h + 1) * B_DV]
        y_ref[:, h * B_DV:(h + 1) * B_DV] = (on * (rh * jax.nn.sigmoid(rh))).astype(y_ref.dtype)


def gla_prompt(proj, w_gate2p, b_gate, g_head, s0, *, chunk=128):
    B, L, _ = proj.shape
    HK, HV = B_HEADS * B_DK, B_HEADS * B_DV
    C = min(chunk, L)
    assert L % C == 0 and C % GLA_SUB == 0
    return pl.pallas_call(
        _gla_body,
        out_shape=[jax.ShapeDtypeStruct((B, L, HV), bf16),
                   jax.ShapeDtypeStruct((B, B_HEADS, B_DK, B_DV), f32)],
        grid=(B, L // C),
        in_specs=[
            pl.BlockSpec((None, C, HK), lambda b, i: (b, i, 0)),
            pl.BlockSpec((None, C, HK), lambda b, i: (b, i, 1)),
            pl.BlockSpec((None, C, HV), lambda b, i: (b, i, 1)),
            pl.BlockSpec((None, C, HV), lambda b, i: (b, i, 2)),
            pl.BlockSpec((None, C, LANES), lambda b, i: (b, i, (2 * HK + 2 * HV) // LANES)),
            pl.BlockSpec((LANES, HK), lambda b, i: (0, 0)),
            pl.BlockSpec((1, HK), lambda b, i: (0, 0)),
            pl.BlockSpec((1, HV), lambda b, i: (0, 0)),
            pl.BlockSpec((None, B_HEADS, B_DK, B_DV), lambda b, i: (b, 0, 0, 0)),
        ],
        out_specs=[pl.BlockSpec((None, C, HV), lambda b, i: (b, i, 0)),
                   pl.BlockSpec((None, B_HEADS, B_DK, B_DV), lambda b, i: (b, 0, 0, 0))],
        scratch_shapes=[pltpu.VMEM((C, C), f32)],
        compiler_params=_params("parallel", "arbitrary"),
        name="gla_prompt",
    )(proj, proj, proj, proj, proj, w_gate2p, b_gate.reshape(1, HK), g_head.reshape(1, HV), s0)


def _cmp_mlp_tail(z, posb, w2_ref):
    n = z.shape[0]
    H = C_CMP_HIDDEN
    hid = z[:, :H] + pltpu.roll(z[:, H:], n - 1, 0) + posb
    return jnp.dot(jax.nn.gelu(hid).astype(bf16), w2_ref[...], preferred_element_type=f32)


def _cmp_pos_bias(p_ref, w1_ref):
    H = C_CMP_HIDDEN
    pz = jnp.dot(p_ref[...].astype(bf16), w1_ref[...], preferred_element_type=f32)
    return pz[0:1, :H] + pz[1:2, H:]


def _cmp_body(xk_ref, xv_ref, pk_ref, pv_ref, wk1_ref, wk2_ref, wv1_ref, wv2_ref, ok_ref, ov_ref):
    for x_ref, p_ref, w1_ref, w2_ref, o_ref in ((xk_ref, pk_ref, wk1_ref, wk2_ref, ok_ref),
                                                (xv_ref, pv_ref, wv1_ref, wv2_ref, ov_ref)):
        z = jnp.dot(x_ref[...].astype(bf16), w1_ref[...], preferred_element_type=f32)
        o_ref[...] = _cmp_mlp_tail(z, _cmp_pos_bias(p_ref, w1_ref), w2_ref).astype(o_ref.dtype)


def nsa_compress(xk, xv, pos_k, pos_v, wk1, wk2, wv1, wv2):
    B, G, n, W = xk.shape
    x_spec = pl.BlockSpec((None, None, n, W), lambda b, g: (b, g, 0, 0))
    full = lambda a: pl.BlockSpec(a.shape, lambda b, g: (0,) * a.ndim)
    o_spec = pl.BlockSpec((None, None, n, C_HEAD_DIM), lambda b, g: (b, g, 0, 0))
    consts = [pos_k, pos_v, wk1, wk2, wv1, wv2]
    return pl.pallas_call(
        _cmp_body,
        out_shape=[jax.ShapeDtypeStruct((B, G, n, C_HEAD_DIM), bf16)] * 2,
        grid=(B, G),
        in_specs=[x_spec, x_spec] + [full(a) for a in consts],
        out_specs=[o_spec, o_spec],
        compiler_params=_params("parallel", "parallel"),
        name="nsa_compress",
    )(xk, xv, *consts)


def _topk_mask_t(score_t, blk_t):
    keep = jnp.zeros(score_t.shape, f32)
    big = float(score_t.shape[0])
    for _ in range(C_TOPK):
        mx = jnp.max(score_t, axis=0, keepdims=True)
        first = jnp.min(jnp.where(score_t == mx, blk_t, big), axis=0, keepdims=True)
        hit = blk_t == first
        keep = jnp.where(hit, jnp.where(mx > 0.5 * NEG_INF, 1.0, 0.0), keep)
        score_t = jnp.where(hit, REMOVED, score_t)
    return keep


def _reduce_rows(x, op):
    slabs = [x[r:r + 8] for r in range(0, x.shape[0], 8)]
    while len(slabs) > 1:
        slabs = [op(slabs[k], slabs[k + 1]) if k + 1 < len(slabs) else slabs[k] for k in range(0, len(slabs), 2)]
    red = jnp.max if op is jnp.maximum else jnp.sum
    return red(slabs[0], axis=0, keepdims=True)


def _softmax2_rows_t(s, mask):
    e = jnp.where(mask, jnp.exp2(s - _reduce_rows(s, jnp.maximum)), 0.0)
    return e / jnp.maximum(_reduce_rows(e, jnp.add), TINY)


def _nsa_body(sp_ref, q_ref, gate_ref, bg_ref, kc_ref, vc_ref, ks_ref, vs_ref, kw_ref, vw_ref,
              o_ref, m_ref, acc_ref, gs_ref, sa_ref, sb_ref, *, tk):
    g = pl.program_id(1)
    i = pl.program_id(2)
    tq = q_ref.shape[0]
    R = C_HPG * tq
    L = ks_ref.shape[0]
    n_cmp = kc_ref.shape[0]
    NS = LANES
    dh = C_HEAD_DIM
    q0 = i * tq

    q_t = jnp.transpose(q_ref[...] * (dh ** -0.5 * LOG2E))
    prow = lax.broadcasted_iota(jnp.int32, (LANES - dh, 1), 0)
    cols = []
    for hh in range(C_HPG):
        pc = jnp.zeros((LANES - dh, 1), f32)
        for n in reversed(range(SLOPE_PIECES)):
            pc = jnp.where(prow < 2 * n + 2, sp_ref[(g * C_HPG + hh) * SLOPE_PIECES + n], pc)
        cols.append(jnp.concatenate([q_t[hh * dh:(hh + 1) * dh], jnp.broadcast_to(pc, (LANES - dh, tq))], axis=0))
    qa_t = jnp.concatenate(cols, axis=1).astype(bf16)
    t1 = q0 + lax.broadcasted_iota(jnp.int32, (1, tq), 1)
    t = jnp.concatenate([t1] * C_HPG, axis=1)

    W = C_WINDOW + tq
    w0 = pl.multiple_of(q0, tq)
    s = jnp.dot(kw_ref[pl.ds(w0, W), :], qa_t, preferred_element_type=f32)
    slabs = []
    for r0 in range(0, W, LANES):
        wpos = q0 - C_WINDOW + r0 + lax.broadcasted_iota(jnp.int32, (LANES, 1), 0)
        sl = s[r0:r0 + LANES]
        if r0 < tq - 1:
            sl = jnp.where(wpos >= t - C_WINDOW, sl, NEG_INF)
        if r0 + LANES - 1 > C_WINDOW:
            sl = jnp.where(wpos <= t, sl, NEG_INF)
        slabs.append(sl)
    s = jnp.concatenate(slabs, axis=0)
    e = jnp.exp2(s - _reduce_rows(s, jnp.maximum))
    o_w = jnp.dot(vw_ref[:, pl.ds(w0, W)], e.astype(bf16), preferred_element_type=f32)
    o_w = o_w[0:dh] / o_w[dh:dh + 1]

    cpos = lax.broadcasted_iota(jnp.int32, (n_cmp, 1), 0) * C_CMP_STRIDE + (C_CMP_BLOCK - 1)
    mask = cpos <= t
    s = jnp.where(mask, jnp.dot(kc_ref[...], qa_t, preferred_element_type=f32), NEG_INF)
    p = _softmax2_rows_t(s, mask)
    o_c = jnp.dot(vc_ref[...], p.astype(bf16), preferred_element_type=f32)

    psum = p[:, 0:tq]
    for hh in range(1, C_HPG):
        psum = psum + p[:, hh * tq:(hh + 1) * tq]
    n_blk = -(-(L // C_SEL_BLOCK) // 8) * 8
    jb = lax.broadcasted_iota(jnp.int32, (n_blk, n_cmp), 0)
    c_start = lax.broadcasted_iota(jnp.int32, (n_blk, n_cmp), 1) * C_CMP_STRIDE
    cover = jnp.where(c_start < (jb + 1) * C_SEL_BLOCK,
                      jnp.where(c_start + C_CMP_BLOCK > jb * C_SEL_BLOCK, 1.0, 0.0), 0.0).astype(bf16)
    imp = jnp.zeros((n_blk, tq), f32)
    rest = psum
    for _ in range(N_PIECES):
        piece = rest.astype(bf16)
        imp = imp + jnp.dot(cover, piece, preferred_element_type=f32)
        rest = rest - piece.astype(f32)
    blk = lax.broadcasted_iota(jnp.int32, (n_blk, tq), 0)
    cur = t1 >> C_SEL_SHIFT
    forced = (blk == 0) | (blk == cur) | (blk == cur - 1)
    score = jnp.where(blk * C_SEL_BLOCK <= t1, jnp.where(forced, FORCE, imp), NEG_INF)
    keep = _topk_mask_t(score, blk.astype(f32))
    neg1 = jnp.concatenate([jnp.where(keep > 0.5, jnp.where(blk == cur, NEG_INF, 0.0), NEG_INF),
                            jnp.full((NS - n_blk, tq), NEG_INF, f32)], axis=0).astype(bf16)
    qa2_t = jnp.concatenate([qa_t, jnp.concatenate([neg1] * C_HPG, axis=1)], axis=0)

    d0 = pl.multiple_of(q0, tq)
    kpos = q0 + lax.broadcasted_iota(jnp.int32, (tq, 1), 0)
    sc = jnp.dot(ks_ref[pl.ds(d0, tq), 0:LANES], qa_t, preferred_element_type=f32)
    sc = jnp.where(kpos <= t, jnp.where((kpos >> C_SEL_SHIFT) == (t >> C_SEL_SHIFT), sc, NEG_INF), NEG_INF)
    m0 = _reduce_rows(sc, jnp.maximum)
    m_ref[...] = m0
    acc_ref[...] = jnp.dot(vs_ref[:, pl.ds(d0, tq)], jnp.exp2(sc - m0).astype(bf16), preferred_element_type=f32)

    n_tiles = q0 // tk + 1

    def scores(kt, s_ref):
        k0 = pl.multiple_of(jnp.minimum(kt, n_tiles - 1) * tk, tk)
        s_ref[...] = jnp.dot(ks_ref[pl.ds(k0, tk), :], qa2_t, preferred_element_type=f32)

    def absorb(kt, s_ref):
        k0 = pl.multiple_of(kt * tk, tk)
        sc = s_ref[...]
        m_old = m_ref[...]
        m_new = jnp.maximum(m_old, _reduce_rows(sc, jnp.maximum))
        ee = jnp.exp2(sc - m_new).astype(bf16)
        acc_ref[...] = jnp.exp2(m_old - m_new) * acc_ref[...] + jnp.dot(vs_ref[:, pl.ds(k0, tk)], ee,
                                                                         preferred_element_type=f32)
        m_ref[...] = m_new

    scores(0, sa_ref)

    def pair(j, carry):
        scores(2 * j + 1, sb_ref)
        absorb(2 * j, sa_ref)
        scores(2 * j + 2, sa_ref)
        absorb(2 * j + 1, sb_ref)
        return carry

    lax.fori_loop(0, n_tiles // 2, pair, 0)

    @pl.when(n_tiles % 2 == 1)
    def _():
        absorb(n_tiles - 1, sa_ref)

    o_s = acc_ref[0:dh, :] / acc_ref[dh:dh + 1, :]

    gs_ref[...] = jnp.transpose(jax.nn.sigmoid(gate_ref[...] + bg_ref[...]))

    def gate(r):
        return jnp.concatenate([gs_ref[pl.ds((g * C_HPG + hh) * 3 + r, 1), :] for hh in range(C_HPG)], axis=1)

    o = gate(0) * o_c + gate(1) * o_s + gate(2) * o_w
    o = jnp.concatenate([o[:, hh * tq:(hh + 1) * tq] for hh in range(C_HPG)], axis=0)
    o_ref[...] = jnp.transpose(o).astype(o_ref.dtype)


def _aug_keys(k, pos, one_hot_blocks):
    B, G, n, dh = k.shape
    hi = (pos // C_SEL_BLOCK) * C_SEL_BLOCK
    cols = np.zeros((n, LANES - dh), np.float32)
    for j in range(SLOPE_PIECES):
        cols[:, 2 * j] = hi
        cols[:, 2 * j + 1] = pos - hi
    parts = [k, jnp.broadcast_to(jnp.asarray(cols, bf16), (B, G, n, LANES - dh))]
    if one_hot_blocks:
        oh = (pos[:, None] // C_SEL_BLOCK == np.arange(LANES)[None, :]).astype(np.float32)
        parts.append(jnp.broadcast_to(jnp.asarray(oh, bf16), (B, G, n, LANES)))
    return jnp.concatenate(parts, axis=-1)


def nsa_attention_prompt(proj, b_gate_p, kc, vc, ks, vs, kw, vw, *, tq=256, tk=256):
    B, L, _ = proj.shape
    G, dh = C_KV_HEADS, C_HEAD_DIM
    tq = min(tq, L)
    tk = min(tk, L)
    assert tk % tq == 0 and L % tk == 0 and L // C_SEL_BLOCK <= LANES
    R = C_HPG * tq
    n_cmp = kc.shape[2]
    kc_a = _aug_keys(kc, np.arange(n_cmp) * C_CMP_STRIDE + (C_CMP_BLOCK - 1), False)
    ks_a = _aug_keys(ks, np.arange(L), True)
    kw_a = _aug_keys(kw, np.arange(L), False)
    lead = np.zeros((C_WINDOW, LANES), np.float32)
    lead[:, dh:dh + 2 * SLOPE_PIECES:2] = NEG_INF
    kw_a = jnp.concatenate([jnp.broadcast_to(jnp.asarray(lead, bf16), (B, G, C_WINDOW, LANES)), kw_a], axis=2)
    gate_blk = (C_HEADS * dh + 6 * G * dh) // LANES
    seq = lambda n, w: pl.BlockSpec((None, None, n, w), lambda b, g, i, sp: (b, g, 0, 0))
    tr = lambda a: jnp.swapaxes(a, 2, 3)

    def with_ones(a):
        n = a.shape[2]
        tail = np.zeros((VAL_ROWS - dh, n), np.float32)
        tail[0] = 1.0
        return jnp.concatenate([tr(a), jnp.broadcast_to(jnp.asarray(tail, bf16), (B, G, VAL_ROWS - dh, n))], axis=2)

    return pl.pallas_call(
        functools.partial(_nsa_body, tk=tk),
        out_shape=jax.ShapeDtypeStruct((B, L, C_HEADS * dh), bf16),
        grid_spec=pltpu.PrefetchScalarGridSpec(
            num_scalar_prefetch=1,
            grid=(B, G, L // tq),
            in_specs=[
                pl.BlockSpec((None, tq, C_HPG * dh), lambda b, g, i, sp: (b, i, g)),
                pl.BlockSpec((None, tq, LANES), lambda b, g, i, sp: (b, i, gate_blk)),
                pl.BlockSpec((1, LANES), lambda b, g, i, sp: (0, 0)),
                seq(n_cmp, LANES), seq(dh, n_cmp), seq(L, 2 * LANES), seq(VAL_ROWS, L), seq(L + C_WINDOW, LANES),
                seq(VAL_ROWS, L + C_WINDOW),
            ],
            out_specs=pl.BlockSpec((None, tq, C_HPG * dh), lambda b, g, i, sp: (b, i, g)),
            scratch_shapes=[pltpu.VMEM((1, R), f32), pltpu.VMEM((VAL_ROWS, R), f32),
                            pltpu.VMEM((LANES, tq), f32), pltpu.VMEM((tk, R), f32), pltpu.VMEM((tk, R), f32)],
        ),
        compiler_params=_params("parallel", "parallel", "parallel"),
        name="nsa_attention_prompt",
    )(jnp.asarray(SLOPES_C_PIECES), proj, proj, b_gate_p, kc_a, tr(vc), ks_a, with_ones(vs), kw_a,
      jnp.pad(with_ones(vw), ((0, 0), (0, 0), (0, 0), (C_WINDOW, 0))))


def _slot_rows(ref, off, n, w):
    rows = jnp.concatenate([ref[:, off + s * w:off + (s + 1) * w] for s in range(n)]
                           + ([jnp.zeros((8 - n, w), f32)] if n < 8 else []), axis=0)
    return jnp.concatenate([rows, jnp.zeros((8, LANES - w), f32)], axis=-1) if w < LANES else rows


def _dil_sample_body(sl_ref, qkv_ref, c0_ref, c1_ref, c2_ref, o_ref):
    dh = A_HEAD_DIM
    scale = dh ** -0.5
    sect = A_GROUPS * A_OUT
    scores, news, v_cols = [], [], []
    for grp, c_ref in enumerate((c0_ref, c1_ref, c2_ref)):
        W = c_ref.shape[-1]
        d = A_DILATIONS[grp]
        q8 = _slot_rows(qkv_ref, grp * A_OUT, A_SLOTS, dh)
        kn8 = _slot_rows(qkv_ref, sect + grp * A_OUT, A_SLOTS, dh)
        vn8 = _slot_rows(qkv_ref, 2 * sect + grp * A_OUT, A_SLOTS, dh)
        q_cols = jnp.transpose(q8)[:dh]
        v_cols.append(jnp.transpose(vn8)[:dh])
        news.append(jnp.sum(q8 * kn8, axis=-1, keepdims=True) * scale)
        pos = lax.broadcasted_iota(jnp.int32, (1, W), 1)
        rows = [jnp.sum(c_ref[0, s] * q_cols[:, s:s + 1], axis=0, keepdims=True) for s in range(A_SLOTS)]
        slope = jnp.concatenate([jnp.full((1, 1), sl_ref[grp * A_SLOTS + s], f32) for s in range(A_SLOTS)], axis=0)
        sc = jnp.concatenate(rows, axis=0) * scale - slope * (W - pos).astype(f32)
        scores.append(jnp.where((pos & (d - 1)) == 0, sc, NEG_INF))
    m = functools.reduce(jnp.maximum, [jnp.max(s, axis=-1, keepdims=True) for s in scores] + news)
    den = jnp.zeros((A_SLOTS, 1), f32)
    acc = [jnp.zeros((dh, 1), f32) for _ in range(A_SLOTS)]
    for grp, c_ref in enumerate((c0_ref, c1_ref, c2_ref)):
        e = jnp.exp(scores[grp] - m)
        en = jnp.exp(news[grp] - m)
        den = den + jnp.sum(e, axis=-1, keepdims=True) + en
        for s in range(A_SLOTS):
            acc[s] = acc[s] + jnp.sum(c_ref[1, s] * e[s:s + 1, :], axis=-1, keepdims=True) \
                + en[s:s + 1, :] * v_cols[grp][:, s:s + 1]
    o_ref[...] = jnp.concatenate([acc[s] / den[s:s + 1, :] for s in range(A_SLOTS)], axis=-1)


def dil_sample(qkv, caches_t, layer):
    Bs = qkv.shape[0]
    for grp, c in enumerate(caches_t):
        assert c.shape[-1] == A_WINDOWS[grp]
    c_spec = lambda c: pl.BlockSpec((None, None) + c.shape[2:], lambda b, sl: (layer, b, 0, 0, 0, 0))
    return pl.pallas_call(
        _dil_sample_body,
        out_shape=jax.ShapeDtypeStruct((Bs, A_HEAD_DIM, A_SLOTS), f32),
        grid_spec=pltpu.PrefetchScalarGridSpec(
            num_scalar_prefetch=1,
            grid=(Bs,),
            in_specs=[pl.BlockSpec((None, 1, qkv.shape[-1]), lambda b, sl: (b, 0, 0))]
            + [c_spec(c) for c in caches_t],
            out_specs=pl.BlockSpec((None, A_HEAD_DIM, A_SLOTS), lambda b, sl: (b, 0, 0)),
        ),
        compiler_params=_params("parallel"),
        name="dil_sample",
    )(jnp.asarray(SLOPES_A.reshape(-1)), qkv, *caches_t)


def _xattn_sample_body(q_ref, c_ref, o_ref):
    dh = q_ref.shape[-1] // MEM_HEADS
    outs = []
    for h in range(MEM_HEADS):
        qh = q_ref[:, h * dh:(h + 1) * dh]
        s = jnp.sum(c_ref[:, 0, h, :] * qh, axis=-1, keepdims=True) * dh ** -0.5
        e = jnp.exp(s - jnp.max(s, axis=0, keepdims=True))
        outs.append(jnp.sum(e * c_ref[:, 1, h, :], axis=0, keepdims=True) / jnp.sum(e, axis=0, keepdims=True))
    o_ref[...] = jnp.concatenate(outs, axis=-1)


def xattn_sample(q, cache, layer):
    Bs, _, D = q.shape
    return pl.pallas_call(
        _xattn_sample_body,
        out_shape=jax.ShapeDtypeStruct((Bs, 1, D), f32),
        grid=(Bs,),
        in_specs=[pl.BlockSpec((None, 1, D), lambda b: (b, 0, 0)),
                  pl.BlockSpec((None, None) + cache.shape[2:], lambda b: (layer, b, 0, 0, 0, 0))],
        out_specs=pl.BlockSpec((None, 1, D), lambda b: (b, 0, 0)),
        compiler_params=_params("parallel"),
        name="xattn_sample",
    )(q, cache)


def _gla_step_body(q_ref, k_ref, v_ref, r_ref, gl_ref, w2_ref, bg_ref, gh_ref, s0_ref, y_ref, s_ref):
    z = _hdot(gl_ref[...], w2_ref[...]) + bg_ref[...]
    a = jnp.exp(_log_sigmoid(z) / B_GATE_TAU)
    pad = jnp.zeros((5, B_DK), f32)
    for h in range(B_HEADS):
        ks_ = slice(h * B_DK, (h + 1) * B_DK)
        vs_ = slice(h * B_DV, (h + 1) * B_DV)
        rows = jnp.concatenate([a[:, ks_], k_ref[:, ks_], q_ref[:, ks_] * B_DK ** -0.5, pad], axis=0)
        cols = jnp.transpose(rows)
        S = cols[:, 0:1] * s0_ref[h] + cols[:, 1:2] * v_ref[:, vs_]
        s_ref[h] = S
        o = jnp.sum(cols[:, 2:3] * S, axis=0, keepdims=True)
        rh = r_ref[:, vs_]
        y_ref[:, vs_] = (_rms(o, gh_ref[:, vs_]) * (rh * jax.nn.sigmoid(rh))).astype(y_ref.dtype)


def gla_step(proj, w_gate2p, b_gate, g_head, state, layer):
    Bs = proj.shape[0]
    HK, HV = B_HEADS * B_DK, B_HEADS * B_DV
    return pl.pallas_call(
        _gla_step_body,
        out_shape=[jax.ShapeDtypeStruct((Bs, 1, HV), bf16),
                   jax.ShapeDtypeStruct((Bs, B_HEADS, B_DK, B_DV), f32)],
        grid=(Bs,),
        in_specs=[
            pl.BlockSpec((None, 1, HK), lambda b: (b, 0, 0)),
            pl.BlockSpec((None, 1, HK), lambda b: (b, 0, 1)),
            pl.BlockSpec((None, 1, HV), lambda b: (b, 0, 1)),
            pl.BlockSpec((None, 1, HV), lambda b: (b, 0, 2)),
            pl.BlockSpec((None, 1, LANES), lambda b: (b, 0, (2 * HK + 2 * HV) // LANES)),
            pl.BlockSpec((LANES, HK), lambda b: (0, 0)),
            pl.BlockSpec((1, HK), lambda b: (0, 0)),
            pl.BlockSpec((1, HV), lambda b: (0, 0)),
            pl.BlockSpec((None, None, B_HEADS, B_DK, B_DV), lambda b: (layer, b, 0, 0, 0)),
        ],
        out_specs=[pl.BlockSpec((None, 1, HV), lambda b: (b, 0, 0)),
                   pl.BlockSpec((None, B_HEADS, B_DK, B_DV), lambda b: (b, 0, 0, 0))],
        compiler_params=_params("parallel"),
        name="gla_step",
    )(proj, proj, proj, proj, proj, w_gate2p, b_gate.reshape(1, HK), g_head.reshape(1, HV), state)


PAGES_PER_STEP = 8


def _cmp_sample_body(pt_ref, *refs):
    pages = refs[:PAGES_PER_STEP]
    (pk_ref, pv_ref, wk1_ref, wk2_ref, wv1_ref, wv2_ref, wkp_ref, wvp_ref,
     ok_ref, ov_ref, x_ref) = refs[PAGES_PER_STEP:]
    i = pl.program_id(1)
    H = C_CMP_HIDDEN
    n_tiles = x_ref.shape[0]
    per_kind = n_tiles // 2
    per_page = PAGE_SIZE // C_CMP_STRIDE
    out_row = lax.broadcasted_iota(jnp.int32, (PAGE_SIZE, PAGE_SIZE), 0)
    in_row = lax.broadcasted_iota(jnp.int32, (PAGE_SIZE, PAGE_SIZE), 1)
    shift = per_page.bit_length() - 1
    perm = (in_row == (out_row & (per_page - 1)) * C_CMP_STRIDE + (out_row >> shift)).astype(bf16)
    for k in range(PAGES_PER_STEP):
        c0 = pl.multiple_of((i * PAGES_PER_STEP + k) * per_page, per_page)
        for kind in range(2):
            for tile in range(per_kind):
                xt = pages[k][kind, 2 * tile:2 * tile + 2].reshape(LANES, PAGE_SIZE).astype(bf16)
                xs = _dot_t(perm, xt)
                for p in range(C_CMP_STRIDE):
                    x_ref[kind * per_kind + tile, p, pl.ds(c0, per_page), :] = xs[p * per_page:(p + 1) * per_page]

    @pl.when(i == pl.num_programs(1) - 1)
    def _():
        n = x_ref.shape[2]
        for kind, (p_ref, w1_ref, w2_ref, wp_ref, o_ref) in enumerate(
                ((pk_ref, wk1_ref, wk2_ref, wkp_ref, ok_ref), (pv_ref, wv1_ref, wv2_ref, wvp_ref, ov_ref))):
            posb = _cmp_pos_bias(p_ref, w1_ref)
            outs = []
            for tile in range(per_kind):
                ct = kind * per_kind + tile
                z2 = jnp.zeros((n, 4 * H), f32)
                for pp in range(C_CMP_STRIDE // 2):
                    xp = jnp.concatenate([x_ref[ct, 2 * pp], x_ref[ct, 2 * pp + 1]], axis=-1).astype(bf16)
                    z2 = z2 + jnp.dot(xp, wp_ref[pp], preferred_element_type=f32)
                outs += [_cmp_mlp_tail(z, posb, w2_ref) for z in (z2[:, :2 * H], z2[:, 2 * H:])]
            o_ref[...] = jnp.concatenate(outs, axis=-1)


def _pair_block_diag(w1):
    dh = C_HEAD_DIM
    w = w1.reshape(C_CMP_STRIDE, dh, w1.shape[1])
    z = jnp.zeros_like(w)
    bd = jnp.concatenate([jnp.concatenate([w, z], axis=-1), jnp.concatenate([z, w], axis=-1)], axis=1)
    return bd.reshape(C_CMP_STRIDE // 2, 4 * dh, 2 * w1.shape[1])


def nsa_compress_sample(page_table, cache_t, layer, pos_k, pos_v, wk1, wk2, wv1, wv2):
    Bs, n_pages = page_table.shape
    G, dh = C_KV_HEADS, C_HEAD_DIM
    assert n_pages % PAGES_PER_STEP == 0 and cache_t.shape[2:] == (4, G, dh, PAGE_SIZE) and 2 * dh == LANES
    n_rows = n_pages * PAGE_SIZE
    n = n_rows // C_CMP_STRIDE
    page = lambda k: pl.BlockSpec((None, None, 2, G, dh, PAGE_SIZE),
                                  lambda b, i, pt, k=k: (layer, pt[b, i * PAGES_PER_STEP + k], 0, 0, 0, 0))
    consts = [pos_k, pos_v, wk1, wk2, wv1, wv2, _pair_block_diag(wk1), _pair_block_diag(wv1)]
    full = lambda a: pl.BlockSpec(a.shape, lambda b, i, pt: (0,) * a.ndim)
    o_spec = pl.BlockSpec((None, n, G * dh), lambda b, i, pt: (b, 0, 0))
    return pl.pallas_call(
        _cmp_sample_body,
        out_shape=[jax.ShapeDtypeStruct((Bs, n, G * dh), f32)] * 2,
        grid_spec=pltpu.PrefetchScalarGridSpec(
            num_scalar_prefetch=1,
            grid=(Bs, n_pages // PAGES_PER_STEP),
            in_specs=[page(k) for k in range(PAGES_PER_STEP)] + [full(a) for a in consts],
            out_specs=[o_spec, o_spec],
            scratch_shapes=[pltpu.VMEM((2 * G * dh // LANES, C_CMP_STRIDE, n, LANES), f32)],
        ),
        compiler_params=_params("parallel", "arbitrary"),
        name="nsa_compress_sample",
    )(page_table, *([cache_t] * PAGES_PER_STEP), *consts)


def _cmp_topk_sample_body(sl_ref, q_ref, kc_ref, vc_ref, oc_ref, idx_ref, *, t):
    G, dh = C_KV_HEADS, C_HEAD_DIM
    n = kc_ref.shape[0]
    n_slc = t // C_SEL_BLOCK + 1
    NS = -(-n_slc // LANES) * LANES
    cpos = lax.broadcasted_iota(jnp.int32, (1, n), 1) * C_CMP_STRIDE + (C_CMP_BLOCK - 1)
    mask = cpos <= t
    c_start = lax.broadcasted_iota(jnp.int32, (n, NS), 0) * C_CMP_STRIDE
    jb = lax.broadcasted_iota(jnp.int32, (n, NS), 1)
    cover = jnp.where(c_start < (jb + 1) * C_SEL_BLOCK,
                      jnp.where(c_start + C_CMP_BLOCK > jb * C_SEL_BLOCK, 1.0, 0.0), 0.0).astype(bf16)
    blk = lax.broadcasted_iota(jnp.int32, (1, NS), 1)
    cur = t // C_SEL_BLOCK
    forced = (blk == 0) | (blk == cur) | (blk == cur - 1)
    lane = lax.broadcasted_iota(jnp.int32, (1, LANES), 1)
    o_parts, p_rows = [], []
    for g in range(G):
        q8 = jnp.concatenate([q_ref[:, (g * C_HPG + hh) * dh:(g * C_HPG + hh + 1) * dh] for hh in range(C_HPG)]
                             + [jnp.zeros((8 - C_HPG, dh), f32)], axis=0)
        slope = jnp.concatenate([jnp.full((1, 1), sl_ref[g * C_HPG + hh], f32) for hh in range(C_HPG)]
                                + [jnp.zeros((8 - C_HPG, 1), f32)], axis=0)
        s = _dot_t(q8.astype(bf16), kc_ref[:, g * dh:(g + 1) * dh].astype(bf16)) * dh ** -0.5
        s = s - slope * (float(t) - cpos.astype(f32))
        s = jnp.where(mask, s, NEG_INF)
        m = jnp.max(s, axis=-1, keepdims=True)
        e = jnp.where(mask, jnp.exp(s - m), 0.0)
        p = e / jnp.maximum(jnp.sum(e, axis=-1, keepdims=True), TINY)
        o = jnp.dot(p.astype(bf16), vc_ref[:, g * dh:(g + 1) * dh].astype(bf16), preferred_element_type=f32)
        o_parts += [o[hh:hh + 1] for hh in range(C_HPG)]
        p_rows.append(jnp.sum(p[0:C_HPG], axis=0, keepdims=True))
    oc_ref[...] = jnp.concatenate(o_parts, axis=-1)
    rest = jnp.concatenate(p_rows + [jnp.zeros((8 - G, n), f32)], axis=0)
    imp = jnp.zeros((8, NS), f32)
    for _ in range(N_PIECES):
        piece = rest.astype(bf16)
        imp = imp + jnp.dot(piece, cover, preferred_element_type=f32)
        rest = rest - piece.astype(f32)
    row = lax.broadcasted_iota(jnp.int32, (8, 1), 0)
    score = jnp.where((blk * C_SEL_BLOCK <= t) & (row < G), jnp.where(forced, FORCE, imp), NEG_INF)
    rank = jnp.zeros((8, NS), f32)
    for k in range(1, NS):
        other = pltpu.roll(score, k, 1)
        ahead = jnp.where(blk >= k, jnp.where(other >= score, 1.0, 0.0), jnp.where(other > score, 1.0, 0.0))
        rank = rank + ahead
    blk_f = blk.astype(f32)
    idx = jnp.zeros((8, LANES), f32)
    for r in range(C_TOPK):
        pick = jnp.sum(jnp.where(rank == float(r), blk_f, 0.0), axis=-1, keepdims=True)
        idx = jnp.where(lane == r, pick, idx)
    idx_ref[...] = idx.astype(jnp.int32)


def nsa_cmp_topk_sample(proj, kc, vc, t):
    Bs = proj.shape[0]
    n, W = kc.shape[1:]
    HD = C_HEADS * C_HEAD_DIM
    return pl.pallas_call(
        functools.partial(_cmp_topk_sample_body, t=t),
        out_shape=[jax.ShapeDtypeStruct((Bs, 1, HD), f32), jax.ShapeDtypeStruct((Bs, 8, LANES), jnp.int32)],
        grid_spec=pltpu.PrefetchScalarGridSpec(
            num_scalar_prefetch=1,
            grid=(Bs,),
            in_specs=[pl.BlockSpec((None, 1, HD), lambda b, sl: (b, 0, 0)),
                      pl.BlockSpec((None, n, W), lambda b, sl: (b, 0, 0)),
                      pl.BlockSpec((None, n, W), lambda b, sl: (b, 0, 0))],
            out_specs=[pl.BlockSpec((None, 1, HD), lambda b, sl: (b, 0, 0)),
                       pl.BlockSpec((None, 8, LANES), lambda b, sl: (b, 0, 0))],
        ),
        compiler_params=_params("parallel"),
        name="nsa_cmp_topk_sample",
    )(jnp.asarray(SLOPES_C), proj, kc, vc)


def _sel_win_sample_body(pt_ref, ix_ref, sl_ref, *refs, t):
    blocks = refs[:C_TOPK]
    (win_ref, q_ref, ksn_ref, vsn_ref, kwn_ref, vwn_ref, gate_ref, bg_ref, oc_ref, o_ref) = refs[C_TOPK:]
    b = pl.program_id(0)
    g = pl.program_id(1)
    G, dh = C_KV_HEADS, C_HEAD_DIM
    scale = dh ** -0.5
    n_past = t // C_SEL_BLOCK
    per_page = PAGE_SIZE // C_SEL_BLOCK
    q8 = jnp.concatenate([q_ref[:, hh * dh:(hh + 1) * dh] for hh in range(C_HPG)]
                         + [jnp.zeros((8 - C_HPG, dh), f32)], axis=0)
    q8b = q8.astype(bf16)
    slope = jnp.concatenate([jnp.full((1, 1), sl_ref[g * C_HPG + hh], f32) for hh in range(C_HPG)]
                            + [jnp.zeros((8 - C_HPG, 1), f32)], axis=0)

    def own(ref):
        return functools.reduce(jnp.add, [jnp.where(g == gg, ref[:, gg * dh:(gg + 1) * dh], 0.0) for gg in range(G)])

    def softmax_pv(scores, values_t, s_new, v_new):
        m = functools.reduce(jnp.maximum, [jnp.max(s, axis=-1, keepdims=True) for s in scores] + [s_new])
        e_new = jnp.exp(s_new - m)
        den = e_new
        acc = e_new * v_new
        for s, vt in zip(scores, values_t):
            e = jnp.exp(s - m)
            den = den + jnp.sum(e, axis=-1, keepdims=True)
            acc = acc + _dot_t(e.astype(bf16), vt.astype(bf16))
        return acc / den

    lane = lax.broadcasted_iota(jnp.int32, (1, PAGE_SIZE), 1)
    scores, values_t = [], []
    for s_i in range(C_TOPK):
        j = ix_ref[b, g * C_TOPK + s_i]
        jc = jnp.minimum(j, n_past - 1)
        kpos = (jc // per_page) * PAGE_SIZE + lane
        sc = jnp.dot(q8b, blocks[s_i][0].astype(bf16), preferred_element_type=f32) * scale \
            - slope * (t - kpos).astype(f32)
        ok = ((lane >> C_SEL_SHIFT) == (jc % per_page)) & (j < n_past)
        scores.append(jnp.where(ok, sc, NEG_INF))
        values_t.append(blocks[s_i][1])
    s_new = jnp.sum(q8 * own(ksn_ref), axis=-1, keepdims=True) * scale
    o_s = softmax_pv(scores, values_t, s_new, own(vsn_ref))

    nw = win_ref.shape[-1]
    wdist = (nw - lax.broadcasted_iota(jnp.int32, (1, nw), 1)).astype(f32)
    sw = jnp.dot(q8b, win_ref[0].astype(bf16), preferred_element_type=f32) * scale - slope * wdist
    s_new = jnp.sum(q8 * own(kwn_ref), axis=-1, keepdims=True) * scale
    o_w = softmax_pv([sw], [win_ref[1]], s_new, own(vwn_ref))

    gs = jax.nn.sigmoid(gate_ref[...] + bg_ref[...])
    glane = lax.broadcasted_iota(jnp.int32, (1, LANES), 1)
    outs = []
    for hh in range(C_HPG):
        gate = [jnp.sum(jnp.where(glane == (g * C_HPG + hh) * 3 + r, gs, 0.0), axis=-1, keepdims=True)
                for r in range(3)]
        outs.append(gate[0] * oc_ref[:, hh * dh:(hh + 1) * dh] + gate[1] * o_s[hh:hh + 1] + gate[2] * o_w[hh:hh + 1])
    o_ref[...] = jnp.concatenate(outs, axis=-1).astype(o_ref.dtype)


def nsa_sel_win_sample(page_table, sel_idx, cache_t, win_t, layer, proj, b_gate_p, o_c, t):
    Bs = proj.shape[0]
    G, dh = C_KV_HEADS, C_HEAD_DIM
    GD = G * dh
    assert win_t.shape[-1] == C_WINDOW and t % C_SEL_BLOCK == 0
    n_past = t // C_SEL_BLOCK
    per_page = PAGE_SIZE // C_SEL_BLOCK

    def blk_spec(s_i):
        def imap(b, g, pt, ix, sl):
            j = jnp.minimum(ix[b, g * C_TOPK + s_i], n_past - 1)
            return (layer, pt[b, j // per_page], 1, g, 0, 0)
        return pl.BlockSpec((None, None, 2, None, dh, PAGE_SIZE), imap)

    row = lambda w, blk: pl.BlockSpec((None, 1, w), lambda b, g, pt, ix, sl, blk=blk: (b, 0, blk))
    base = C_HEADS * dh // GD
    gate_blk = (C_HEADS * dh + 6 * GD) // LANES
    return pl.pallas_call(
        functools.partial(_sel_win_sample_body, t=t),
        out_shape=jax.ShapeDtypeStruct((Bs, 1, C_HEADS * dh), bf16),
        grid_spec=pltpu.PrefetchScalarGridSpec(
            num_scalar_prefetch=3,
            grid=(Bs, G),
            in_specs=[blk_spec(s_i) for s_i in range(C_TOPK)] + [
                pl.BlockSpec((None, None, 2, None, dh, C_WINDOW), lambda b, g, pt, ix, sl: (layer, b, 0, g, 0, 0)),
                pl.BlockSpec((None, 1, C_HPG * dh), lambda b, g, pt, ix, sl: (b, 0, g)),
                row(GD, base + 2), row(GD, base + 3), row(GD, base + 4), row(GD, base + 5),
                row(LANES, gate_blk),
                pl.BlockSpec((1, LANES), lambda b, g, pt, ix, sl: (0, 0)),
                pl.BlockSpec((None, 1, C_HPG * dh), lambda b, g, pt, ix, sl: (b, 0, g)),
            ],
            out_specs=pl.BlockSpec((None, 1, C_HPG * dh), lambda b, g, pt, ix, sl: (b, 0, g)),
        ),
        compiler_params=_params("parallel", "parallel"),
        name="nsa_sel_win_sample",
    )(page_table, sel_idx, jnp.asarray(SLOPES_C), *([cache_t] * C_TOPK), win_t, proj, proj, proj, proj, proj,
      proj, b_gate_p, o_c)


def _norm_body(x_ref, g_ref, o_ref):
    o_ref[...] = _rms(x_ref[...], g_ref[...])


def rmsnorm_rows(x, g, *, tm=1024):
    M, D = x.shape
    tm = min(tm, M)
    return pl.pallas_call(
        _norm_body,
        out_shape=jax.ShapeDtypeStruct((M, D), f32),
        grid=(M // tm,),
        in_specs=[pl.BlockSpec((tm, D), lambda i: (i, 0)), pl.BlockSpec((1, D), lambda i: (0, 0))],
        out_specs=pl.BlockSpec((tm, D), lambda i: (i, 0)),
        compiler_params=_params("parallel"),
        name="rmsnorm",
    )(x, g.reshape(1, D))


def _position_minor(cache):
    return jnp.transpose(cache, (0, 1, 3, 4, 5, 2))


def kernel(x_prompt, x_sample, cache_dil_w128, cache_dil_w512, cache_dil_w2048, state_gla, cache_nsa_win,
           cache_nsa_kv, cache_mem_kv, page_table, mem_prompt, g_mix, g_cross, g_mem, g_ffn, g_final,
           w_a_qkv, w_a_o, w_b_in, w_b_gate2, b_b_gate, g_b_head, w_b_o, w_c_in, b_c_gate, c_pos_k, c_pos_v,
           w_c_k1, w_c_k2, w_c_v1, w_c_v2, w_c_o, w_x_q, w_x_kv, w_x_o, w_ffn_in, w_ffn_out):
    Bp, L, D = x_prompt.shape
    Bs = x_sample.shape[0]
    depth = g_mix.shape[0]
    n_pages = page_table.shape[1]
    t_s = n_pages * PAGE_SIZE
    G, dh = C_KV_HEADS, C_HEAD_DIM
    GD = G * dh
    HK, HV = B_HEADS * B_DK, B_HEADS * B_DV
    cast = lambda a: a.astype(bf16)

    wa_qkv, wa_o = cast(w_a_qkv), cast(w_a_o)
    nb = 2 * HK + HV
    wb_in = cast(jnp.concatenate(
        [w_b_in[..., :nb], w_b_in[..., nb + B_GATE_RANK:], w_b_in[..., nb:nb + B_GATE_RANK],
         jnp.zeros(w_b_in.shape[:2] + (LANES - B_GATE_RANK,), f32)], axis=-1))
    wb_gate2 = jnp.pad(w_b_gate2, ((0, 0), (0, LANES - B_GATE_RANK), (0, 0)))
    wb_o = cast(w_b_o)
    nc = -(-w_c_in.shape[-1] // LANES) * LANES
    wc_in = cast(jnp.pad(w_c_in, ((0, 0), (0, 0), (0, nc - w_c_in.shape[-1]))))
    bc_gate = jnp.pad(b_c_gate, ((0, 0), (0, LANES - b_c_gate.shape[-1])))
    half = C_CMP_STRIDE * dh
    two_chunk = lambda w: cast(jnp.concatenate([w[:, :half], w[:, half:]], axis=-1))
    wc_k1, wc_v1 = two_chunk(w_c_k1), two_chunk(w_c_v1)
    wc_k2, wc_v2, wc_o = cast(w_c_k2), cast(w_c_v2), cast(w_c_o)
    pos_k = c_pos_k.reshape(-1, 2, half)
    pos_v = c_pos_v.reshape(-1, 2, half)
    wx_q, wx_kv, wx_o = cast(w_x_q), cast(w_x_kv), cast(w_x_o)
    wf_in, wf_out = cast(w_ffn_in), cast(w_ffn_out)

    Mp = Bp * L
    x = x_prompt.reshape(Mp, D)
    mem2d = mem_prompt.reshape(-1, D)
    dil_p = [[] for _ in range(A_GROUPS)]
    gla_p, rows_p, win_p, mem_p = [], [], [], []
    for l in range(depth):
        kind, j = l % N_MIXERS, l // N_MIXERS
        if kind == 0:
            qkv = mm(x, (wa_qkv, j), g=g_mix[l]).reshape(Bp, L, -1)
            outs = [dilated_group(qkv, grp) for grp in range(A_GROUPS)]
            x = dilated_merge_out([o.reshape(Mp, A_OUT) for o, _ in outs],
                                  [s.reshape(Mp, A_OUT) for _, s in outs], (wa_o, j), x)
            sect = A_GROUPS * A_OUT
            for grp in range(A_GROUPS):
                w = min(A_WINDOWS[grp], L)
                tail = lambda t: qkv[:, L - w:, t * sect + grp * A_OUT:t * sect + (grp + 1) * A_OUT].reshape(
                    Bp, w, A_SLOTS, A_HEAD_DIM)
                dil_p[grp].append(jnp.stack([tail(1), tail(2)], axis=2))
        elif kind == 1:
            proj = mm(x, (wb_in, j), g=g_mix[l]).reshape(Bp, L, -1)
            y, S = gla_prompt(proj, wb_gate2[j], b_b_gate[j], g_b_head[j].reshape(-1),
                              jnp.zeros((Bp, B_HEADS, B_DK, B_DV), f32))
            gla_p.append(S)
            x = mm(y.reshape(Mp, HV), (wb_o, j), res=x)
        else:
            proj = mm(x, (wc_in, j), g=g_mix[l]).reshape(Bp, L, -1)
            q_w = C_HEADS * dh
            rows = proj[..., q_w:q_w + 6 * GD].reshape(Bp, L, 6, G, dh)
            chunks = lambda a: a.reshape(Bp, L // C_CMP_STRIDE, C_CMP_STRIDE, G, dh).transpose(
                0, 3, 1, 2, 4).reshape(Bp, G, L // C_CMP_STRIDE, half)
            kc, vc = nsa_compress(chunks(rows[:, :, 0]), chunks(rows[:, :, 1]), pos_k[j], pos_v[j],
                                  wc_k1[j], wc_k2[j], wc_v1[j], wc_v2[j])
            seqs = [cast(rows[:, :, r].transpose(0, 2, 1, 3)) for r in range(2, 6)]
            o = nsa_attention_prompt(proj, bc_gate[j:j + 1], kc, vc, *seqs)
            x = mm(o.reshape(Mp, q_w), (wc_o, j), res=x)
            rows_p.append(rows[:, :, :4])
            win_p.append(rows[:, L - min(C_WINDOW, L):, 4:])
        mem_kv = mm(mem2d, (wx_kv, l), g=g_mem[l]).reshape(Bp, -1, 2 * D)
        mem_p.append(mem_kv.reshape(Bp, -1, 2, MEM_HEADS, D // MEM_HEADS))
        x = xattn_prompt(x.reshape(Bp, L, D), g_cross[l], (wx_q, l), cast(mem_kv), (wx_o, l)).reshape(Mp, D)
        x = ffn(x, g_ffn[l], (wf_in, l), (wf_out, l))
    y_prompt = rmsnorm_rows(x, g_final).reshape(Bp, L, D)

    x = x_sample.reshape(Bs, D)
    dil_t = [_position_minor(c) for c in (cache_dil_w128, cache_dil_w512, cache_dil_w2048)]
    nsa_t = _position_minor(cache_nsa_kv)
    win_t = _position_minor(cache_nsa_win)
    dil_s = [[] for _ in range(A_GROUPS)]
    gla_s, rows_s, win_s = [], [], []
    for l in range(depth):
        kind, j = l % N_MIXERS, l // N_MIXERS
        if kind == 0:
            qkv = mm(x, (wa_qkv, j), g=g_mix[l]).reshape(Bs, 1, -1)
            o = dil_sample(qkv, dil_t, j)
            x = mm(o.transpose(0, 2, 1).reshape(Bs, A_OUT), (wa_o, j), res=x)
            kv = qkv.reshape(Bs, 1, 3, A_GROUPS, A_SLOTS, A_HEAD_DIM)
            for grp in range(A_GROUPS):
                dil_s[grp].append(jnp.stack([kv[:, :, 1, grp], kv[:, :, 2, grp]], axis=2))
        elif kind == 1:
            proj = mm(x, (wb_in, j), g=g_mix[l]).reshape(Bs, 1, -1)
            y, S = gla_step(proj, wb_gate2[j], b_b_gate[j], g_b_head[j].reshape(-1), state_gla, j)
            gla_s.append(S)
            x = mm(y.reshape(Bs, HV), (wb_o, j), res=x)
        else:
            proj = mm(x, (wc_in, j), g=g_mix[l]).reshape(Bs, 1, -1)
            q_w = C_HEADS * dh
            kc, vc = nsa_compress_sample(page_table, nsa_t, j, pos_k[j], pos_v[j],
                                         wc_k1[j], wc_k2[j], wc_v1[j], wc_v2[j])
            o_c, idx = nsa_cmp_topk_sample(proj, kc, vc, t_s)
            sel_idx = idx[:, :G, :C_TOPK].reshape(Bs, G * C_TOPK)
            o = nsa_sel_win_sample(page_table, sel_idx, nsa_t, win_t, j, proj, bc_gate[j:j + 1], o_c, t_s)
            x = mm(o.reshape(Bs, q_w), (wc_o, j), res=x)
            rows = proj[..., q_w:q_w + 6 * GD].reshape(Bs, 1, 6, G, dh)
            rows_s.append(rows[:, :, :4])
            win_s.append(rows[:, :, 4:])
        q = mm(x, (wx_q, l), g=g_cross[l]).reshape(Bs, 1, D)
        o = xattn_sample(q, cache_mem_kv, l)
        x = mm(o.reshape(Bs, D), (wx_o, l), res=x)
        x = ffn(x, g_ffn[l], (wf_in, l), (wf_out, l))
    y_sample = rmsnorm_rows(x, g_final).reshape(Bs, 1, D)

    st = jnp.stack
    return (y_prompt, y_sample, st(dil_p[0]), st(dil_s[0]), st(dil_p[1]), st(dil_s[1]), st(dil_p[2]), st(dil_s[2]),
            st(gla_p), st(gla_s), st(win_p), st(win_s), st(rows_p), st(rows_s), st(mem_p))
```

```python
import functools

import jax
import jax.numpy as jnp
import numpy as np
from jax import lax
from jax.experimental import pallas as pl
from jax.experimental.pallas import tpu as pltpu

f32 = jnp.float32
bf16 = jnp.bfloat16

N_MIXERS = 3
A_WINDOWS = (128, 512, 2048)
A_DILATIONS = (1, 4, 16)
A_GROUPS = 3
A_SLOTS = 8
A_HEAD_DIM = 64
A_OUT = A_SLOTS * A_HEAD_DIM
A_WIN_STEPS = 128

B_HEADS = 4
B_DK = 128
B_DV = 256
B_GATE_RANK = 16
B_GATE_TAU = 16.0

C_HEADS = 16
C_KV_HEADS = 4
C_HPG = C_HEADS // C_KV_HEADS
C_HEAD_DIM = 64
C_CMP_BLOCK = 32
C_CMP_STRIDE = 16
C_SEL_BLOCK = 64
C_SEL_SHIFT = 6
C_TOPK = 16
C_WINDOW = 512
C_CMP_HIDDEN = 128

MEM_HEADS = 4
PAGE_SIZE = 128

RMS_EPS = 1e-6
NEG_INF = -1e30
FORCE = 1e30
TINY = 1e-30
REMOVED = -3e38

LANES = 128
VMEM_LIMIT_BYTES = 56 * 1024 * 1024
HIGHEST = lax.Precision.HIGHEST


def _alibi_slopes(n):
    return np.asarray(2.0 ** (-8.0 * np.arange(1, n + 1) / n), dtype=np.float32)


def _bf16_pieces(x, n):
    out, rest = [], np.asarray(x, np.float64)
    for _ in range(n):
        p = rest.astype(bf16).astype(np.float64)
        out.append(p)
        rest = rest - p
    return np.stack(out, axis=-1).astype(np.float32)


LOG2E = 1.4426950408889634
SLOPES_A = _alibi_slopes(A_GROUPS * A_SLOTS).reshape(A_GROUPS, A_SLOTS)
SLOPES_C = _alibi_slopes(C_HEADS)
N_PIECES = 3
SLOPE_PIECES = 4
SLOPES_C_PIECES = _bf16_pieces(SLOPES_C.astype(np.float64) * LOG2E, SLOPE_PIECES).reshape(-1)
VAL_ROWS = C_HEAD_DIM + 16


def _params(*sem):
    return pltpu.CompilerParams(dimension_semantics=sem, vmem_limit_bytes=VMEM_LIMIT_BYTES)


def _rms(x, g):
    return x * lax.rsqrt(jnp.mean(x * x, axis=-1, keepdims=True) + RMS_EPS) * g


def _dot_t(a, b, precision=None):
    return lax.dot_general(a, b, (((1,), (1,)), ((), ())), precision=precision, preferred_element_type=f32)


def _hdot(a, b):
    return jnp.dot(a, b, precision=HIGHEST, preferred_element_type=f32)


def _weight_spec(w, block, index):
    if isinstance(w, tuple):
        stack, layer = w
        return stack, pl.BlockSpec((None,) + block, lambda *ids: (layer,) + index(*ids))
    return w, pl.BlockSpec(block, index)


def _weight_shape(w):
    return w[0].shape[1:] if isinstance(w, tuple) else w.shape


def _pick_tile(n, target):
    best = LANES
    for t in range(LANES, min(n, target) + 1, LANES):
        if n % t == 0:
            best = t
    return best


def _mm_body(*refs, norm, res):
    it = iter(refs)
    x_ref = next(it)
    g_ref = next(it) if norm else None
    w_ref = next(it)
    r_ref = next(it) if res else None
    o_ref = next(it)
    xn_ref = next(it)

    @pl.when(pl.program_id(1) == 0)
    def _():
        x = x_ref[...].astype(f32)
        if norm:
            x = _rms(x, g_ref[...])
        xn_ref[...] = x.astype(bf16)

    acc = jnp.dot(xn_ref[...], w_ref[...], preferred_element_type=f32)
    if res:
        acc = acc + r_ref[...]
    o_ref[...] = acc.astype(o_ref.dtype)


def mm(x, w, *, g=None, res=None, out_dtype=f32, tm=1024, tn=1024):
    M, K = x.shape
    N = _weight_shape(w)[1]
    tm = min(tm, M)
    tn = _pick_tile(N, tn)
    assert M % tm == 0 and N % tn == 0
    args = [x]
    specs = [pl.BlockSpec((tm, K), lambda i, j: (i, 0))]
    if g is not None:
        args.append(g.reshape(1, K))
        specs.append(pl.BlockSpec((1, K), lambda i, j: (0, 0)))
    w_arr, w_spec = _weight_spec(w, (K, tn), lambda i, j: (0, j))
    args.append(w_arr)
    specs.append(w_spec)
    if res is not None:
        args.append(res)
        specs.append(pl.BlockSpec((tm, tn), lambda i, j: (i, j)))
    return pl.pallas_call(
        functools.partial(_mm_body, norm=g is not None, res=res is not None),
        out_shape=jax.ShapeDtypeStruct((M, N), out_dtype),
        grid=(M // tm, N // tn),
        in_specs=specs,
        out_specs=pl.BlockSpec((tm, tn), lambda i, j: (i, j)),
        scratch_shapes=[pltpu.VMEM((tm, K), bf16)],
        compiler_params=_params("parallel", "arbitrary"),
        name="mm",
    )(*args)


def _ffn_body(x_ref, g_ref, wa_ref, wb_ref, wo_ref, o_ref, xn_ref, acc_ref):
    j = pl.program_id(1)

    @pl.when(j == 0)
    def _():
        xn_ref[...] = _rms(x_ref[...], g_ref[...]).astype(bf16)
        acc_ref[...] = jnp.zeros_like(acc_ref)

    xn = xn_ref[...]
    a = jnp.dot(xn, wa_ref[...], preferred_element_type=f32)
    b = jnp.dot(xn, wb_ref[...], preferred_element_type=f32)
    h = (a * jax.nn.sigmoid(a) * b).astype(bf16)
    acc_ref[...] += jnp.dot(h, wo_ref[...], preferred_element_type=f32)

    @pl.when(j == pl.num_programs(1) - 1)
    def _():
        o_ref[...] = x_ref[...] + acc_ref[...]


def ffn(x, g, w_in, w_out, *, tm=1024, th=256):
    M, D = x.shape
    F = _weight_shape(w_out)[0]
    tm = min(tm, M)
    th = _pick_tile(F, th)
    nh = F // th
    wa, wa_spec = _weight_spec(w_in, (D, th), lambda i, j: (0, j))
    wb, wb_spec = _weight_spec(w_in, (D, th), lambda i, j: (0, j + nh))
    wo, wo_spec = _weight_spec(w_out, (th, D), lambda i, j: (j, 0))
    return pl.pallas_call(
        _ffn_body,
        out_shape=jax.ShapeDtypeStruct((M, D), f32),
        grid=(M // tm, nh),
        in_specs=[
            pl.BlockSpec((tm, D), lambda i, j: (i, 0)),
            pl.BlockSpec((1, D), lambda i, j: (0, 0)),
            wa_spec, wb_spec, wo_spec,
        ],
        out_specs=pl.BlockSpec((tm, D), lambda i, j: (i, 0)),
        scratch_shapes=[pltpu.VMEM((tm, D), bf16), pltpu.VMEM((tm, D), f32)],
        compiler_params=_params("parallel", "arbitrary"),
        name="ffn",
    )(x, g.reshape(1, D), wa, wb, wo)


def _xattn_body(x_ref, g_ref, wq_ref, kv_ref, wo_ref, o_ref, *, heads):
    x = x_ref[...]
    D = x.shape[-1]
    dh = D // heads
    xn = _rms(x, g_ref[...]).astype(bf16)
    q = jnp.dot(xn, wq_ref[...], preferred_element_type=f32).astype(bf16)
    outs = []
    for h in range(heads):
        kh = kv_ref[:, h * dh:(h + 1) * dh]
        vh = kv_ref[:, D + h * dh:D + (h + 1) * dh]
        s = _dot_t(q[:, h * dh:(h + 1) * dh], kh) * dh ** -0.5
        m = jnp.max(s, axis=-1, keepdims=True)
        e = jnp.exp(s - m)
        outs.append(jnp.dot(e.astype(bf16), vh, preferred_element_type=f32) / jnp.sum(e, axis=-1, keepdims=True))
    o = jnp.concatenate(outs, axis=-1).astype(bf16)
    o_ref[...] = x + jnp.dot(o, wo_ref[...], preferred_element_type=f32)


def xattn_prompt(x, g, wq, kv, wo, *, tq=512):
    B, L, D = x.shape
    Mem = kv.shape[1]
    tq = min(tq, L)
    wq, wq_spec = _weight_spec(wq, (D, D), lambda b, i: (0, 0))
    wo, wo_spec = _weight_spec(wo, (D, D), lambda b, i: (0, 0))
    return pl.pallas_call(
        functools.partial(_xattn_body, heads=MEM_HEADS),
        out_shape=jax.ShapeDtypeStruct((B, L, D), f32),
        grid=(B, L // tq),
        in_specs=[
            pl.BlockSpec((None, tq, D), lambda b, i: (b, i, 0)),
            pl.BlockSpec((1, D), lambda b, i: (0, 0)),
            wq_spec,
            pl.BlockSpec((None, Mem, 2 * D), lambda b, i: (b, 0, 0)),
            wo_spec,
        ],
        out_specs=pl.BlockSpec((None, tq, D), lambda b, i: (b, i, 0)),
        compiler_params=_params("parallel", "parallel"),
        name="xattn_prompt",
    )(x, g.reshape(1, D), wq, kv, wo)


def _dil_body(sl_ref, q_ref, kc_ref, kp_ref, vc_ref, vp_ref, o_ref, lse_ref, *, grp, dil, tq):
    i = pl.program_id(1)
    c = pl.program_id(2)
    ws = A_WIN_STEPS
    scale = A_HEAD_DIM ** -0.5
    row = lax.broadcasted_iota(jnp.int32, (ws, 2 * ws), 0)
    col = lax.broadcasted_iota(jnp.int32, (ws, 2 * ws), 1)
    dist = row + ws - col
    in_band = (dist >= 0) & (dist <= ws)
    valid_first = in_band & ((col >= ws) | (i > 0))
    distf = (dist * dil).astype(f32)
    lane = lax.broadcasted_iota(jnp.int32, (1, LANES), 1)
    first = lane < A_HEAD_DIM
    slopes = [sl_ref[grp * A_SLOTS + 2 * c + half] for half in range(2)]

    def rows(ref, r, start, n):
        return ref[pl.ds(r + start * dil, n, stride=dil), :] if dil > 1 else ref[start:start + n, :]

    def residue(r, carry):
        for j in range(tq // ws):
            q = rows(q_ref, r, j * ws, ws).astype(bf16)
            if j == 0:
                k = jnp.concatenate([rows(kp_ref, r, 0, ws), rows(kc_ref, r, 0, ws)], axis=0)
                v = jnp.concatenate([rows(vp_ref, r, 0, ws), rows(vc_ref, r, 0, ws)], axis=0)
            else:
                k = rows(kc_ref, r, (j - 1) * ws, 2 * ws)
                v = rows(vc_ref, r, (j - 1) * ws, 2 * ws)
            k, v = k.astype(bf16), v.astype(bf16)
            valid = valid_first if j == 0 else in_band
            halves = []
            for half in range(2):
                qm = jnp.where(first == (half == 0), q, jnp.zeros_like(q))
                s = _dot_t(qm, k) * scale - slopes[half] * distf
                s = jnp.where(valid, s, NEG_INF)
                m = jnp.max(s, axis=-1, keepdims=True)
                e = jnp.exp(s - m)
                den = jnp.sum(e, axis=-1, keepdims=True)
                pv = jnp.dot(e.astype(bf16), v, preferred_element_type=f32) / den
                halves.append((pv, m + jnp.log(den)))
            o_t = jnp.where(first, halves[0][0], halves[1][0])
            l_t = jnp.where(first, halves[0][1], halves[1][1])
            if dil > 1:
                o_ref[pl.ds(r + j * ws * dil, ws, stride=dil), :] = o_t
                lse_ref[pl.ds(r + j * ws * dil, ws, stride=dil), :] = l_t
            else:
                o_ref[j * ws:(j + 1) * ws, :] = o_t
                lse_ref[j * ws:(j + 1) * ws, :] = l_t
        return carry

    unroll = min(dil, DIL_UNROLL)

    def trip(u, carry):
        for k in range(unroll):
            residue(u * unroll + k, carry)
        return carry

    if dil > unroll:
        lax.fori_loop(0, dil // unroll, trip, 0)
    else:
        trip(0, 0)


DIL_TQ = (1024, 256, 128)
DIL_UNROLL = 8


def dilated_group(qkv, grp):
    B, L, W3 = qkv.shape
    d = A_DILATIONS[grp]
    tq = min(DIL_TQ[grp], L // d)
    rows_blk = tq * d
    prev_blk = A_WIN_STEPS * d
    assert L % rows_blk == 0 and tq % A_WIN_STEPS == 0
    ratio = tq // A_WIN_STEPS
    tiles = A_OUT // LANES
    sect = A_GROUPS * tiles

    def spec(t, prev=False):
        col = lambda c: t * sect + grp * tiles + c
        if prev:
            return pl.BlockSpec((None, prev_blk, LANES),
                                lambda b, i, c, sl: (b, jnp.maximum(i * ratio - 1, 0), col(c)))
        return pl.BlockSpec((None, rows_blk, LANES), lambda b, i, c, sl: (b, i, col(c)))

    out_spec = pl.BlockSpec((None, rows_blk, LANES), lambda b, i, c, sl: (b, i, c))
    return pl.pallas_call(
        functools.partial(_dil_body, grp=grp, dil=d, tq=tq),
        out_shape=[jax.ShapeDtypeStruct((B, L, A_OUT), f32)] * 2,
        grid_spec=pltpu.PrefetchScalarGridSpec(
            num_scalar_prefetch=1,
            grid=(B, L // rows_blk, tiles),
            in_specs=[spec(0), spec(1), spec(1, True), spec(2), spec(2, True)],
            out_specs=[out_spec, out_spec],
        ),
        compiler_params=_params("parallel", "parallel", "parallel"),
        name=f"dilated_g{grp}",
    )(jnp.asarray(SLOPES_A.reshape(-1)), qkv, qkv, qkv, qkv, qkv)


def _dil_out_body(o0, o1, o2, l0, l1, l2, w_ref, r_ref, out_ref):
    ls = [l0[...], l1[...], l2[...]]
    m = jnp.maximum(jnp.maximum(ls[0], ls[1]), ls[2])
    es = [jnp.exp(l - m) for l in ls]
    den = es[0] + es[1] + es[2]
    o = (es[0] / den) * o0[...] + (es[1] / den) * o1[...] + (es[2] / den) * o2[...]
    out_ref[...] = r_ref[...] + jnp.dot(o.astype(bf16), w_ref[...], preferred_element_type=f32)


def dilated_merge_out(os, lses, w_o, res, *, tm=512):
    M, D = res.shape
    tm = min(tm, M)
    row = lambda n: pl.BlockSpec((tm, n), lambda i: (i, 0))
    w_o, w_spec = _weight_spec(w_o, (A_OUT, D), lambda i: (0, 0))
    return pl.pallas_call(
        _dil_out_body,
        out_shape=jax.ShapeDtypeStruct((M, D), f32),
        grid=(M // tm,),
        in_specs=[row(A_OUT)] * 6 + [w_spec, row(D)],
        out_specs=row(D),
        compiler_params=_params("parallel"),
        name="dilated_merge_out",
    )(*os, *lses, w_o, res)


GLA_SUB = 16


def _log_sigmoid(z):
    return -(jnp.maximum(-z, 0.0) + jnp.log1p(jnp.exp(-jnp.abs(z))))


def _gla_body(q_ref, k_ref, v_ref, r_ref, gl_ref, w2_ref, bg_ref, gh_ref, s0_ref, y_ref, s_ref, a_ref):
    C = q_ref.shape[0]
    c = GLA_SUB

    @pl.when(pl.program_id(1) == 0)
    def _():
        s_ref[...] = s0_ref[...]

    z = _hdot(gl_ref[...], w2_ref[...]) + bg_ref[...]
    la = _log_sigmoid(z) / B_GATE_TAU
    rowC = lax.broadcasted_iota(jnp.int32, (C, C), 0)
    colC = lax.broadcasted_iota(jnp.int32, (C, C), 1)
    tri = (rowC >= colC).astype(f32)
    b_all = _hdot(tri, la)
    row_k = lax.broadcasted_iota(jnp.int32, (C, B_DK), 0)
    col_c = lax.broadcasted_iota(jnp.int32, (c, C), 1)
    row_c = lax.broadcasted_iota(jnp.int32, (c, 1), 0)

    for h in range(B_HEADS):
        bh = b_all[:, h * B_DK:(h + 1) * B_DK]
        qh = q_ref[:, h * B_DK:(h + 1) * B_DK] * B_DK ** -0.5
        kh = k_ref[:, h * B_DK:(h + 1) * B_DK]
        vh = v_ref[:, h * B_DV:(h + 1) * B_DV].astype(bf16)
        S = s_ref[h]
        o = jnp.dot((qh * jnp.exp(bh)).astype(bf16), S.astype(bf16), preferred_element_type=f32)

        for I in range(C // c):
            r0 = I * c
            qI, kI, bI = qh[r0:r0 + c], kh[r0:r0 + c], bh[r0:r0 + c]
            if I == 0:
                A_I = jnp.zeros((c, C), f32)
            else:
                beta = bh[r0 - 1:r0]
                qt = qI * jnp.exp(bI - beta)
                kt = kh * jnp.exp(jnp.where(row_k < r0, beta - bh, 0.0))
                A_I = jnp.where(col_c < r0, _dot_t(qt.astype(bf16), kt.astype(bf16)), 0.0)
            for j in range(c):
                ex = jnp.exp(jnp.minimum(bI - bI[j:j + 1], 0.0))
                tj = jnp.sum(qI * kI[j:j + 1] * ex, axis=-1, keepdims=True)
                tj = jnp.where(row_c >= j, tj, 0.0)
                A_I = jnp.where(col_c == r0 + j, tj, A_I)
            a_ref[r0:r0 + c, :] = A_I
        o = o + jnp.dot(a_ref[...].astype(bf16), vh, preferred_element_type=f32)

        b_last = bh[C - 1:C]
        kdec = jnp.transpose(kh * jnp.exp(b_last - bh)).astype(bf16)
        decay = jnp.transpose(jnp.broadcast_to(jnp.exp(b_last), (8, B_DK)))[:, 0:1]
        s_ref[h] = decay * S + jnp.dot(kdec, vh, preferred_element_type=f32)

        on = _rms(o, gh_ref[:, h * B_DV:(h + 1) * B_DV])
        rh = r_ref[:, h * B_DV:(h + 1) * B_DV]
        y_ref[:, h * B_DV:(h + 1) * B_DV] = (on * (rh * jax.nn.sigmoid(rh))).astype(y_ref.dtype)


def gla_prompt(proj, w_gate2p, b_gate, g_head, s0, *, chunk=128):
    B, L, _ = proj.shape
    HK, HV = B_HEADS * B_DK, B_HEADS * B_DV
    C = min(chunk, L)
    assert L % C == 0 and C % GLA_SUB == 0
    return pl.pallas_call(
        _gla_body,
        out_shape=[jax.ShapeDtypeStruct((B, L, HV), bf16),
                   jax.ShapeDtypeStruct((B, B_HEADS, B_DK, B_DV), f32)],
        grid=(B, L // C),
        in_specs=[
            pl.BlockSpec((None, C, HK), lambda b, i: (b, i, 0)),
            pl.BlockSpec((None, C, HK), lambda b, i: (b, i, 1)),
            pl.BlockSpec((None, C, HV), lambda b, i: (b, i, 1)),
            pl.BlockSpec((None, C, HV), lambda b, i: (b, i, 2)),
            pl.BlockSpec((None, C, LANES), lambda b, i: (b, i, (2 * HK + 2 * HV) // LANES)),
            pl.BlockSpec((LANES, HK), lambda b, i: (0, 0)),
            pl.BlockSpec((1, HK), lambda b, i: (0, 0)),
            pl.BlockSpec((1, HV), lambda b, i: (0, 0)),
            pl.BlockSpec((None, B_HEADS, B_DK, B_DV), lambda b, i: (b, 0, 0, 0)),
        ],
        out_specs=[pl.BlockSpec((None, C, HV), lambda b, i: (b, i, 0)),
                   pl.BlockSpec((None, B_HEADS, B_DK, B_DV), lambda b, i: (b, 0, 0, 0))],
        scratch_shapes=[pltpu.VMEM((C, C), f32)],
        compiler_params=_params("parallel", "arbitrary"),
        name="gla_prompt",
    )(proj, proj, proj, proj, proj, w_gate2p, b_gate.reshape(1, HK), g_head.reshape(1, HV), s0)


def _cmp_mlp_tail(z, posb, w2_ref):
    n = z.shape[0]
    H = C_CMP_HIDDEN
    hid = z[:, :H] + pltpu.roll(z[:, H:], n - 1, 0) + posb
    return jnp.dot(jax.nn.gelu(hid).astype(bf16), w2_ref[...], preferred_element_type=f32)


def _cmp_pos_bias(p_ref, w1_ref):
    H = C_CMP_HIDDEN
    pz = jnp.dot(p_ref[...].astype(bf16), w1_ref[...], preferred_element_type=f32)
    return pz[0:1, :H] + pz[1:2, H:]


def _cmp_body(xk_ref, xv_ref, pk_ref, pv_ref, wk1_ref, wk2_ref, wv1_ref, wv2_ref, ok_ref, ov_ref):
    for x_ref, p_ref, w1_ref, w2_ref, o_ref in ((xk_ref, pk_ref, wk1_ref, wk2_ref, ok_ref),
                                                (xv_ref, pv_ref, wv1_ref, wv2_ref, ov_ref)):
        z = jnp.dot(x_ref[...].astype(bf16), w1_ref[...], preferred_element_type=f32)
        o_ref[...] = _cmp_mlp_tail(z, _cmp_pos_bias(p_ref, w1_ref), w2_ref).astype(o_ref.dtype)


def nsa_compress(xk, xv, pos_k, pos_v, wk1, wk2, wv1, wv2):
    B, G, n, W = xk.shape
    x_spec = pl.BlockSpec((None, None, n, W), lambda b, g: (b, g, 0, 0))
    full = lambda a: pl.BlockSpec(a.shape, lambda b, g: (0,) * a.ndim)
    o_spec = pl.BlockSpec((None, None, n, C_HEAD_DIM), lambda b, g: (b, g, 0, 0))
    consts = [pos_k, pos_v, wk1, wk2, wv1, wv2]
    return pl.pallas_call(
        _cmp_body,
        out_shape=[jax.ShapeDtypeStruct((B, G, n, C_HEAD_DIM), bf16)] * 2,
        grid=(B, G),
        in_specs=[x_spec, x_spec] + [full(a) for a in consts],
        out_specs=[o_spec, o_spec],
        compiler_params=_params("parallel", "parallel"),
        name="nsa_compress",
    )(xk, xv, *consts)


def _topk_mask_t(score_t, blk_t):
    keep = jnp.zeros(score_t.shape, f32)
    big = float(score_t.shape[0])
    for _ in range(C_TOPK):
        mx = jnp.max(score_t, axis=0, keepdims=True)
        first = jnp.min(jnp.where(score_t == mx, blk_t, big), axis=0, keepdims=True)
        hit = blk_t == first
        keep = jnp.where(hit, jnp.where(mx > 0.5 * NEG_INF, 1.0, 0.0), keep)
        score_t = jnp.where(hit, REMOVED, score_t)
    return keep


def _reduce_rows(x, op):
    slabs = [x[r:r + 8] for r in range(0, x.shape[0], 8)]
    while len(slabs) > 1:
        slabs = [op(slabs[k], slabs[k + 1]) if k + 1 < len(slabs) else slabs[k] for k in range(0, len(slabs), 2)]
    red = jnp.max if op is jnp.maximum else jnp.sum
    return red(slabs[0], axis=0, keepdims=True)


def _softmax2_rows_t(s, mask):
    e = jnp.where(mask, jnp.exp2(s - _reduce_rows(s, jnp.maximum)), 0.0)
    return e / jnp.maximum(_reduce_rows(e, jnp.add), TINY)


def _nsa_body(sp_ref, q_ref, gate_ref, bg_ref, kc_ref, vc_ref, ks_ref, vs_ref, kw_ref, vw_ref,
              o_ref, m_ref, acc_ref, gs_ref, sa_ref, sb_ref, *, tk):
    g = pl.program_id(1)
    i = pl.program_id(2)
    tq = q_ref.shape[0]
    R = C_HPG * tq
    L = ks_ref.shape[0]
    n_cmp = kc_ref.shape[0]
    NS = LANES
    dh = C_HEAD_DIM
    q0 = i * tq

    q_t = jnp.transpose(q_ref[...] * (dh ** -0.5 * LOG2E))
    prow = lax.broadcasted_iota(jnp.int32, (LANES - dh, 1), 0)
    cols = []
    for hh in range(C_HPG):
        pc = jnp.zeros((LANES - dh, 1), f32)
        for n in reversed(range(SLOPE_PIECES)):
            pc = jnp.where(prow < 2 * n + 2, sp_ref[(g * C_HPG + hh) * SLOPE_PIECES + n], pc)
        cols.append(jnp.concatenate([q_t[hh * dh:(hh + 1) * dh], jnp.broadcast_to(pc, (LANES - dh, tq))], axis=0))
    qa_t = jnp.concatenate(cols, axis=1).astype(bf16)
    t1 = q0 + lax.broadcasted_iota(jnp.int32, (1, tq), 1)
    t = jnp.concatenate([t1] * C_HPG, axis=1)

    W = C_WINDOW + tq
    w0 = pl.multiple_of(q0, tq)
    s = jnp.dot(kw_ref[pl.ds(w0, W), :], qa_t, preferred_element_type=f32)
    slabs = []
    for r0 in range(0, W, LANES):
        wpos = q0 - C_WINDOW + r0 + lax.broadcasted_iota(jnp.int32, (LANES, 1), 0)
        sl = s[r0:r0 + LANES]
        if r0 < tq - 1:
            sl = jnp.where(wpos >= t - C_WINDOW, sl, NEG_INF)
        if r0 + LANES - 1 > C_WINDOW:
            sl = jnp.where(wpos <= t, sl, NEG_INF)
        slabs.append(sl)
    s = jnp.concatenate(slabs, axis=0)
    e = jnp.exp2(s - _reduce_rows(s, jnp.maximum))
    o_w = jnp.dot(vw_ref[:, pl.ds(w0, W)], e.astype(bf16), preferred_element_type=f32)
    o_w = o_w[0:dh] / o_w[dh:dh + 1]

    cpos = lax.broadcasted_iota(jnp.int32, (n_cmp, 1), 0) * C_CMP_STRIDE + (C_CMP_BLOCK - 1)
    mask = cpos <= t
    s = jnp.where(mask, jnp.dot(kc_ref[...], qa_t, preferred_element_type=f32), NEG_INF)
    p = _softmax2_rows_t(s, mask)
    o_c = jnp.dot(vc_ref[...], p.astype(bf16), preferred_element_type=f32)

    psum = p[:, 0:tq]
    for hh in range(1, C_HPG):
        psum = psum + p[:, hh * tq:(hh + 1) * tq]
    n_blk = -(-(L // C_SEL_BLOCK) // 8) * 8
    jb = lax.broadcasted_iota(jnp.int32, (n_blk, n_cmp), 0)
    c_start = lax.broadcasted_iota(jnp.int32, (n_blk, n_cmp), 1) * C_CMP_STRIDE
    cover = jnp.where(c_start < (jb + 1) * C_SEL_BLOCK,
                      jnp.where(c_start + C_CMP_BLOCK > jb * C_SEL_BLOCK, 1.0, 0.0), 0.0).astype(bf16)
    imp = jnp.zeros((n_blk, tq), f32)
    rest = psum
    for _ in range(N_PIECES):
        piece = rest.astype(bf16)
        imp = imp + jnp.dot(cover, piece, preferred_element_type=f32)
        rest = rest - piece.astype(f32)
    blk = lax.broadcasted_iota(jnp.int32, (n_blk, tq), 0)
    cur = t1 >> C_SEL_SHIFT
    forced = (blk == 0) | (blk == cur) | (blk == cur - 1)
    score = jnp.where(blk * C_SEL_BLOCK <= t1, jnp.where(forced, FORCE, imp), NEG_INF)
    keep = _topk_mask_t(score, blk.astype(f32))
    neg1 = jnp.concatenate([jnp.where(keep > 0.5, jnp.where(blk == cur, NEG_INF, 0.0), NEG_INF),
                            jnp.full((NS - n_blk, tq), NEG_INF, f32)], axis=0).astype(bf16)
    qa2_t = jnp.concatenate([qa_t, jnp.concatenate([neg1] * C_HPG, axis=1)], axis=0)

    d0 = pl.multiple_of(q0, tq)
    kpos = q0 + lax.broadcasted_iota(jnp.int32, (tq, 1), 0)
    sc = jnp.dot(ks_ref[pl.ds(d0, tq), 0:LANES], qa_t, preferred_element_type=f32)
    sc = jnp.where(kpos <= t, jnp.where((kpos >> C_SEL_SHIFT) == (t >> C_SEL_SHIFT), sc, NEG_INF), NEG_INF)
    m0 = _reduce_rows(sc, jnp.maximum)
    m_ref[...] = m0
    acc_ref[...] = jnp.dot(vs_ref[:, pl.ds(d0, tq)], jnp.exp2(sc - m0).astype(bf16), preferred_element_type=f32)

    n_tiles = q0 // tk + 1

    def scores(kt, s_ref):
        k0 = pl.multiple_of(jnp.minimum(kt, n_tiles - 1) * tk, tk)
        s_ref[...] = jnp.dot(ks_ref[pl.ds(k0, tk), :], qa2_t, preferred_element_type=f32)

    def absorb(kt, s_ref):
        k0 = pl.multiple_of(kt * tk, tk)
        sc = s_ref[...]
        m_old = m_ref[...]
        m_new = jnp.maximum(m_old, _reduce_rows(sc, jnp.maximum))
        ee = jnp.exp2(sc - m_new).astype(bf16)
        acc_ref[...] = jnp.exp2(m_old - m_new) * acc_ref[...] + jnp.dot(vs_ref[:, pl.ds(k0, tk)], ee,
                                                                         preferred_element_type=f32)
        m_ref[...] = m_new

    scores(0, sa_ref)

    def pair(j, carry):
        scores(2 * j + 1, sb_ref)
        absorb(2 * j, sa_ref)
        scores(2 * j + 2, sa_ref)
        absorb(2 * j + 1, sb_ref)
        return carry

    lax.fori_loop(0, n_tiles // 2, pair, 0)

    @pl.when(n_tiles % 2 == 1)
    def _():
        absorb(n_tiles - 1, sa_ref)

    o_s = acc_ref[0:dh, :] / acc_ref[dh:dh + 1, :]

    gs_ref[...] = jnp.transpose(jax.nn.sigmoid(gate_ref[...] + bg_ref[...]))

    def gate(r):
        return jnp.concatenate([gs_ref[pl.ds((g * C_HPG + hh) * 3 + r, 1), :] for hh in range(C_HPG)], axis=1)

    o = gate(0) * o_c + gate(1) * o_s + gate(2) * o_w
    o = jnp.concatenate([o[:, hh * tq:(hh + 1) * tq] for hh in range(C_HPG)], axis=0)
    o_ref[...] = jnp.transpose(o).astype(o_ref.dtype)


def _aug_keys(k, pos, one_hot_blocks):
    B, G, n, dh = k.shape
    hi = (pos // C_SEL_BLOCK) * C_SEL_BLOCK
    cols = np.zeros((n, LANES - dh), np.float32)
    for j in range(SLOPE_PIECES):
        cols[:, 2 * j] = hi
        cols[:, 2 * j + 1] = pos - hi
    parts = [k, jnp.broadcast_to(jnp.asarray(cols, bf16), (B, G, n, LANES - dh))]
    if one_hot_blocks:
        oh = (pos[:, None] // C_SEL_BLOCK == np.arange(LANES)[None, :]).astype(np.float32)
        parts.append(jnp.broadcast_to(jnp.asarray(oh, bf16), (B, G, n, LANES)))
    return jnp.concatenate(parts, axis=-1)


def nsa_attention_prompt(proj, b_gate_p, kc, vc, ks, vs, kw, vw, *, tq=256, tk=256):
    B, L, _ = proj.shape
    G, dh = C_KV_HEADS, C_HEAD_DIM
    tq = min(tq, L)
    tk = min(tk, L)
    assert tk % tq == 0 and L % tk == 0 and L // C_SEL_BLOCK <= LANES
    R = C_HPG * tq
    n_cmp = kc.shape[2]
    kc_a = _aug_keys(kc, np.arange(n_cmp) * C_CMP_STRIDE + (C_CMP_BLOCK - 1), False)
    ks_a = _aug_keys(ks, np.arange(L), True)
    kw_a = _aug_keys(kw, np.arange(L), False)
    lead = np.zeros((C_WINDOW, LANES), np.float32)
    lead[:, dh:dh + 2 * SLOPE_PIECES:2] = NEG_INF
    kw_a = jnp.concatenate([jnp.broadcast_to(jnp.asarray(lead, bf16), (B, G, C_WINDOW, LANES)), kw_a], axis=2)
    gate_blk = (C_HEADS * dh + 6 * G * dh) // LANES
    seq = lambda n, w: pl.BlockSpec((None, None, n, w), lambda b, g, i, sp: (b, g, 0, 0))
    tr = lambda a: jnp.swapaxes(a, 2, 3)

    def with_ones(a):
        n = a.shape[2]
        tail = np.zeros((VAL_ROWS - dh, n), np.float32)
        tail[0] = 1.0
        return jnp.concatenate([tr(a), jnp.broadcast_to(jnp.asarray(tail, bf16), (B, G, VAL_ROWS - dh, n))], axis=2)

    return pl.pallas_call(
        functools.partial(_nsa_body, tk=tk),
        out_shape=jax.ShapeDtypeStruct((B, L, C_HEADS * dh), bf16),
        grid_spec=pltpu.PrefetchScalarGridSpec(
            num_scalar_prefetch=1,
            grid=(B, G, L // tq),
            in_specs=[
                pl.BlockSpec((None, tq, C_HPG * dh), lambda b, g, i, sp: (b, i, g)),
                pl.BlockSpec((None, tq, LANES), lambda b, g, i, sp: (b, i, gate_blk)),
                pl.BlockSpec((1, LANES), lambda b, g, i, sp: (0, 0)),
                seq(n_cmp, LANES), seq(dh, n_cmp), seq(L, 2 * LANES), seq(VAL_ROWS, L), seq(L + C_WINDOW, LANES),
                seq(VAL_ROWS, L + C_WINDOW),
            ],
            out_specs=pl.BlockSpec((None, tq, C_HPG * dh), lambda b, g, i, sp: (b, i, g)),
            scratch_shapes=[pltpu.VMEM((1, R), f32), pltpu.VMEM((VAL_ROWS, R), f32),
                            pltpu.VMEM((LANES, tq), f32), pltpu.VMEM((tk, R), f32), pltpu.VMEM((tk, R), f32)],
        ),
        compiler_params=_params("parallel", "parallel", "parallel"),
        name="nsa_attention_prompt",
    )(jnp.asarray(SLOPES_C_PIECES), proj, proj, b_gate_p, kc_a, tr(vc), ks_a, with_ones(vs), kw_a,
      jnp.pad(with_ones(vw), ((0, 0), (0, 0), (0, 0), (C_WINDOW, 0))))


def _slot_rows(ref, off, n, w):
    rows = jnp.concatenate([ref[:, off + s * w:off + (s + 1) * w] for s in range(n)]
                           + ([jnp.zeros((8 - n, w), f32)] if n < 8 else []), axis=0)
    return jnp.concatenate([rows, jnp.zeros((8, LANES - w), f32)], axis=-1) if w < LANES else rows


def _dil_sample_body(sl_ref, qkv_ref, c0_ref, c1_ref, c2_ref, o_ref):
    dh = A_HEAD_DIM
    scale = dh ** -0.5
    sect = A_GROUPS * A_OUT
    scores, news, v_cols = [], [], []
    for grp, c_ref in enumerate((c0_ref, c1_ref, c2_ref)):
        W = c_ref.shape[-1]
        d = A_DILATIONS[grp]
        q8 = _slot_rows(qkv_ref, grp * A_OUT, A_SLOTS, dh)
        kn8 = _slot_rows(qkv_ref, sect + grp * A_OUT, A_SLOTS, dh)
        vn8 = _slot_rows(qkv_ref, 2 * sect + grp * A_OUT, A_SLOTS, dh)
        q_cols = jnp.transpose(q8)[:dh]
        v_cols.append(jnp.transpose(vn8)[:dh])
        news.append(jnp.sum(q8 * kn8, axis=-1, keepdims=True) * scale)
        pos = lax.broadcasted_iota(jnp.int32, (1, W), 1)
        rows = [jnp.sum(c_ref[0, s] * q_cols[:, s:s + 1], axis=0, keepdims=True) for s in range(A_SLOTS)]
        slope = jnp.concatenate([jnp.full((1, 1), sl_ref[grp * A_SLOTS + s], f32) for s in range(A_SLOTS)], axis=0)
        sc = jnp.concatenate(rows, axis=0) * scale - slope * (W - pos).astype(f32)
        scores.append(jnp.where((pos & (d - 1)) == 0, sc, NEG_INF))
    m = functools.reduce(jnp.maximum, [jnp.max(s, axis=-1, keepdims=True) for s in scores] + news)
    den = jnp.zeros((A_SLOTS, 1), f32)
    acc = [jnp.zeros((dh, 1), f32) for _ in range(A_SLOTS)]
    for grp, c_ref in enumerate((c0_ref, c1_ref, c2_ref)):
        e = jnp.exp(scores[grp] - m)
        en = jnp.exp(news[grp] - m)
        den = den + jnp.sum(e, axis=-1, keepdims=True) + en
        for s in range(A_SLOTS):
            acc[s] = acc[s] + jnp.sum(c_ref[1, s] * e[s:s + 1, :], axis=-1, keepdims=True) \
                + en[s:s + 1, :] * v_cols[grp][:, s:s + 1]
    o_ref[...] = jnp.concatenate([acc[s] / den[s:s + 1, :] for s in range(A_SLOTS)], axis=-1)


def dil_sample(qkv, caches_t, layer):
    Bs = qkv.shape[0]
    for grp, c in enumerate(caches_t):
        assert c.shape[-1] == A_WINDOWS[grp]
    c_spec = lambda c: pl.BlockSpec((None, None) + c.shape[2:], lambda b, sl: (layer, b, 0, 0, 0, 0))
    return pl.pallas_call(
        _dil_sample_body,
        out_shape=jax.ShapeDtypeStruct((Bs, A_HEAD_DIM, A_SLOTS), f32),
        grid_spec=pltpu.PrefetchScalarGridSpec(
            num_scalar_prefetch=1,
            grid=(Bs,),
            in_specs=[pl.BlockSpec((None, 1, qkv.shape[-1]), lambda b, sl: (b, 0, 0))]
            + [c_spec(c) for c in caches_t],
            out_specs=pl.BlockSpec((None, A_HEAD_DIM, A_SLOTS), lambda b, sl: (b, 0, 0)),
        ),
        compiler_params=_params("parallel"),
        name="dil_sample",
    )(jnp.asarray(SLOPES_A.reshape(-1)), qkv, *caches_t)


def _xattn_sample_body(q_ref, c_ref, o_ref):
    dh = q_ref.shape[-1] // MEM_HEADS
    outs = []
    for h in range(MEM_HEADS):
        qh = q_ref[:, h * dh:(h + 1) * dh]
        s = jnp.sum(c_ref[:, 0, h, :] * qh, axis=-1, keepdims=True) * dh ** -0.5
        e = jnp.exp(s - jnp.max(s, axis=0, keepdims=True))
        outs.append(jnp.sum(e * c_ref[:, 1, h, :], axis=0, keepdims=True) / jnp.sum(e, axis=0, keepdims=True))
    o_ref[...] = jnp.concatenate(outs, axis=-1)


def xattn_sample(q, cache, layer):
    Bs, _, D = q.shape
    return pl.pallas_call(
        _xattn_sample_body,
        out_shape=jax.ShapeDtypeStruct((Bs, 1, D), f32),
        grid=(Bs,),
        in_specs=[pl.BlockSpec((None, 1, D), lambda b: (b, 0, 0)),
                  pl.BlockSpec((None, None) + cache.shape[2:], lambda b: (layer, b, 0, 0, 0, 0))],
        out_specs=pl.BlockSpec((None, 1, D), lambda b: (b, 0, 0)),
        compiler_params=_params("parallel"),
        name="xattn_sample",
    )(q, cache)


def _gla_step_body(q_ref, k_ref, v_ref, r_ref, gl_ref, w2_ref, bg_ref, gh_ref, s0_ref, y_ref, s_ref):
    z = _hdot(gl_ref[...], w2_ref[...]) + bg_ref[...]
    a = jnp.exp(_log_sigmoid(z) / B_GATE_TAU)
    pad = jnp.zeros((5, B_DK), f32)
    for h in range(B_HEADS):
        ks_ = slice(h * B_DK, (h + 1) * B_DK)
        vs_ = slice(h * B_DV, (h + 1) * B_DV)
        rows = jnp.concatenate([a[:, ks_], k_ref[:, ks_], q_ref[:, ks_] * B_DK ** -0.5, pad], axis=0)
        cols = jnp.transpose(rows)
        S = cols[:, 0:1] * s0_ref[h] + cols[:, 1:2] * v_ref[:, vs_]
        s_ref[h] = S
        o = jnp.sum(cols[:, 2:3] * S, axis=0, keepdims=True)
        rh = r_ref[:, vs_]
        y_ref[:, vs_] = (_rms(o, gh_ref[:, vs_]) * (rh * jax.nn.sigmoid(rh))).astype(y_ref.dtype)


def gla_step(proj, w_gate2p, b_gate, g_head, state, layer):
    Bs = proj.shape[0]
    HK, HV = B_HEADS * B_DK, B_HEADS * B_DV
    return pl.pallas_call(
        _gla_step_body,
        out_shape=[jax.ShapeDtypeStruct((Bs, 1, HV), bf16),
                   jax.ShapeDtypeStruct((Bs, B_HEADS, B_DK, B_DV), f32)],
        grid=(Bs,),
        in_specs=[
            pl.BlockSpec((None, 1, HK), lambda b: (b, 0, 0)),
            pl.BlockSpec((None, 1, HK), lambda b: (b, 0, 1)),
            pl.BlockSpec((None, 1, HV), lambda b: (b, 0, 1)),
            pl.BlockSpec((None, 1, HV), lambda b: (b, 0, 2)),
            pl.BlockSpec((None, 1, LANES), lambda b: (b, 0, (2 * HK + 2 * HV) // LANES)),
            pl.BlockSpec((LANES, HK), lambda b: (0, 0)),
            pl.BlockSpec((1, HK), lambda b: (0, 0)),
            pl.BlockSpec((1, HV), lambda b: (0, 0)),
            pl.BlockSpec((None, None, B_HEADS, B_DK, B_DV), lambda b: (layer, b, 0, 0, 0)),
        ],
        out_specs=[pl.BlockSpec((None, 1, HV), lambda b: (b, 0, 0)),
                   pl.BlockSpec((None, B_HEADS, B_DK, B_DV), lambda b: (b, 0, 0, 0))],
        compiler_params=_params("parallel"),
        name="gla_step",
    )(proj, proj, proj, proj, proj, w_gate2p, b_gate.reshape(1, HK), g_head.reshape(1, HV), state)


PAGES_PER_STEP = 8


def _cmp_sample_body(pt_ref, *refs):
    pages = refs[:PAGES_PER_STEP]
    (pk_ref, pv_ref, wk1_ref, wk2_ref, wv1_ref, wv2_ref, wkp_ref, wvp_ref,
     ok_ref, ov_ref, x_ref) = refs[PAGES_PER_STEP:]
    i = pl.program_id(1)
    H = C_CMP_HIDDEN
    n_tiles = x_ref.shape[0]
    per_kind = n_tiles // 2
    per_page = PAGE_SIZE // C_CMP_STRIDE
    out_row = lax.broadcasted_iota(jnp.int32, (PAGE_SIZE, PAGE_SIZE), 0)
    in_row = lax.broadcasted_iota(jnp.int32, (PAGE_SIZE, PAGE_SIZE), 1)
    shift = per_page.bit_length() - 1
    perm = (in_row == (out_row & (per_page - 1)) * C_CMP_STRIDE + (out_row >> shift)).astype(bf16)
    for k in range(PAGES_PER_STEP):
        c0 = pl.multiple_of((i * PAGES_PER_STEP + k) * per_page, per_page)
        for kind in range(2):
            for tile in range(per_kind):
                xt = pages[k][kind, 2 * tile:2 * tile + 2].reshape(LANES, PAGE_SIZE).astype(bf16)
                xs = _dot_t(perm, xt)
                for p in range(C_CMP_STRIDE):
                    x_ref[kind * per_kind + tile, p, pl.ds(c0, per_page), :] = xs[p * per_page:(p + 1) * per_page]

    @pl.when(i == pl.num_programs(1) - 1)
    def _():
        n = x_ref.shape[2]
        for kind, (p_ref, w1_ref, w2_ref, wp_ref, o_ref) in enumerate(
                ((pk_ref, wk1_ref, wk2_ref, wkp_ref, ok_ref), (pv_ref, wv1_ref, wv2_ref, wvp_ref, ov_ref))):
            posb = _cmp_pos_bias(p_ref, w1_ref)
            outs = []
            for tile in range(per_kind):
                ct = kind * per_kind + tile
                z2 = jnp.zeros((n, 4 * H), f32)
                for pp in range(C_CMP_STRIDE // 2):
                    xp = jnp.concatenate([x_ref[ct, 2 * pp], x_ref[ct, 2 * pp + 1]], axis=-1).astype(bf16)
                    z2 = z2 + jnp.dot(xp, wp_ref[pp], preferred_element_type=f32)
                outs += [_cmp_mlp_tail(z, posb, w2_ref) for z in (z2[:, :2 * H], z2[:, 2 * H:])]
            o_ref[...] = jnp.concatenate(outs, axis=-1)


def _pair_block_diag(w1):
    dh = C_HEAD_DIM
    w = w1.reshape(C_CMP_STRIDE, dh, w1.shape[1])
    z = jnp.zeros_like(w)
    bd = jnp.concatenate([jnp.concatenate([w, z], axis=-1), jnp.concatenate([z, w], axis=-1)], axis=1)
    return bd.reshape(C_CMP_STRIDE // 2, 4 * dh, 2 * w1.shape[1])


def nsa_compress_sample(page_table, cache_t, layer, pos_k, pos_v, wk1, wk2, wv1, wv2):
    Bs, n_pages = page_table.shape
    G, dh = C_KV_HEADS, C_HEAD_DIM
    assert n_pages % PAGES_PER_STEP == 0 and cache_t.shape[2:] == (4, G, dh, PAGE_SIZE) and 2 * dh == LANES
    n_rows = n_pages * PAGE_SIZE
    n = n_rows // C_CMP_STRIDE
    page = lambda k: pl.BlockSpec((None, None, 2, G, dh, PAGE_SIZE),
                                  lambda b, i, pt, k=k: (layer, pt[b, i * PAGES_PER_STEP + k], 0, 0, 0, 0))
    consts = [pos_k, pos_v, wk1, wk2, wv1, wv2, _pair_block_diag(wk1), _pair_block_diag(wv1)]
    full = lambda a: pl.BlockSpec(a.shape, lambda b, i, pt: (0,) * a.ndim)
    o_spec = pl.BlockSpec((None, n, G * dh), lambda b, i, pt: (b, 0, 0))
    return pl.pallas_call(
        _cmp_sample_body,
        out_shape=[jax.ShapeDtypeStruct((Bs, n, G * dh), f32)] * 2,
        grid_spec=pltpu.PrefetchScalarGridSpec(
            num_scalar_prefetch=1,
            grid=(Bs, n_pages // PAGES_PER_STEP),
            in_specs=[page(k) for k in range(PAGES_PER_STEP)] + [full(a) for a in consts],
            out_specs=[o_spec, o_spec],
            scratch_shapes=[pltpu.VMEM((2 * G * dh // LANES, C_CMP_STRIDE, n, LANES), f32)],
        ),
        compiler_params=_params("parallel", "arbitrary"),
        name="nsa_compress_sample",
    )(page_table, *([cache_t] * PAGES_PER_STEP), *consts)


def _cmp_topk_sample_body(sl_ref, q_ref, kc_ref, vc_ref, oc_ref, idx_ref, *, t):
    G, dh = C_KV_HEADS, C_HEAD_DIM
    n = kc_ref.shape[0]
    n_slc = t // C_SEL_BLOCK + 1
    NS = -(-n_slc // LANES) * LANES
    cpos = lax.broadcasted_iota(jnp.int32, (1, n), 1) * C_CMP_STRIDE + (C_CMP_BLOCK - 1)
    mask = cpos <= t
    c_start = lax.broadcasted_iota(jnp.int32, (n, NS), 0) * C_CMP_STRIDE
    jb = lax.broadcasted_iota(jnp.int32, (n, NS), 1)
    cover = jnp.where(c_start < (jb + 1) * C_SEL_BLOCK,
                      jnp.where(c_start + C_CMP_BLOCK > jb * C_SEL_BLOCK, 1.0, 0.0), 0.0).astype(bf16)
    blk = lax.broadcasted_iota(jnp.int32, (1, NS), 1)
    cur = t // C_SEL_BLOCK
    forced = (blk == 0) | (blk == cur) | (blk == cur - 1)
    lane = lax.broadcasted_iota(jnp.int32, (1, LANES), 1)
    o_parts, p_rows = [], []
    for g in range(G):
        q8 = jnp.concatenate([q_ref[:, (g * C_HPG + hh) * dh:(g * C_HPG + hh + 1) * dh] for hh in range(C_HPG)]
                             + [jnp.zeros((8 - C_HPG, dh), f32)], axis=0)
        slope = jnp.concatenate([jnp.full((1, 1), sl_ref[g * C_HPG + hh], f32) for hh in range(C_HPG)]
                                + [jnp.zeros((8 - C_HPG, 1), f32)], axis=0)
        s = _dot_t(q8.astype(bf16), kc_ref[:, g * dh:(g + 1) * dh].astype(bf16)) * dh ** -0.5
        s = s - slope * (float(t) - cpos.astype(f32))
        s = jnp.where(mask, s, NEG_INF)
        m = jnp.max(s, axis=-1, keepdims=True)
        e = jnp.where(mask, jnp.exp(s - m), 0.0)
        p = e / jnp.maximum(jnp.sum(e, axis=-1, keepdims=True), TINY)
        o = jnp.dot(p.astype(bf16), vc_ref[:, g * dh:(g + 1) * dh].astype(bf16), preferred_element_type=f32)
        o_parts += [o[hh:hh + 1] for hh in range(C_HPG)]
        p_rows.append(jnp.sum(p[0:C_HPG], axis=0, keepdims=True))
    oc_ref[...] = jnp.concatenate(o_parts, axis=-1)
    rest = jnp.concatenate(p_rows + [jnp.zeros((8 - G, n), f32)], axis=0)
    imp = jnp.zeros((8, NS), f32)
    for _ in range(N_PIECES):
        piece = rest.astype(bf16)
        imp = imp + jnp.dot(piece, cover, preferred_element_type=f32)
        rest = rest - piece.astype(f32)
    row = lax.broadcasted_iota(jnp.int32, (8, 1), 0)
    score = jnp.where((blk * C_SEL_BLOCK <= t) & (row < G), jnp.where(forced, FORCE, imp), NEG_INF)
    rank = jnp.zeros((8, NS), f32)
    for k in range(1, NS):
        other = pltpu.roll(score, k, 1)
        ahead = jnp.where(blk >= k, jnp.where(other >= score, 1.0, 0.0), jnp.where(other > score, 1.0, 0.0))
        rank = rank + ahead
    blk_f = blk.astype(f32)
    idx = jnp.zeros((8, LANES), f32)
    for r in range(C_TOPK):
        pick = jnp.sum(jnp.where(rank == float(r), blk_f, 0.0), axis=-1, keepdims=True)
        idx = jnp.where(lane == r, pick, idx)
    idx_ref[...] = idx.astype(jnp.int32)


def nsa_cmp_topk_sample(proj, kc, vc, t):
    Bs = proj.shape[0]
    n, W = kc.shape[1:]
    HD = C_HEADS * C_HEAD_DIM
    return pl.pallas_call(
        functools.partial(_cmp_topk_sample_body, t=t),
        out_shape=[jax.ShapeDtypeStruct((Bs, 1, HD), f32), jax.ShapeDtypeStruct((Bs, 8, LANES), jnp.int32)],
        grid_spec=pltpu.PrefetchScalarGridSpec(
            num_scalar_prefetch=1,
            grid=(Bs,),
            in_specs=[pl.BlockSpec((None, 1, HD), lambda b, sl: (b, 0, 0)),
                      pl.BlockSpec((None, n, W), lambda b, sl: (b, 0, 0)),
                      pl.BlockSpec((None, n, W), lambda b, sl: (b, 0, 0))],
            out_specs=[pl.BlockSpec((None, 1, HD), lambda b, sl: (b, 0, 0)),
                       pl.BlockSpec((None, 8, LANES), lambda b, sl: (b, 0, 0))],
        ),
        compiler_params=_params("parallel"),
        name="nsa_cmp_topk_sample",
    )(jnp.asarray(SLOPES_C), proj, kc, vc)


def _sel_win_sample_body(pt_ref, ix_ref, sl_ref, *refs, t):
    blocks = refs[:C_TOPK]
    (win_ref, q_ref, ksn_ref, vsn_ref, kwn_ref, vwn_ref, gate_ref, bg_ref, oc_ref, o_ref) = refs[C_TOPK:]
    b = pl.program_id(0)
    g = pl.program_id(1)
    G, dh = C_KV_HEADS, C_HEAD_DIM
    scale = dh ** -0.5
    n_past = t // C_SEL_BLOCK
    per_page = PAGE_SIZE // C_SEL_BLOCK
    q8 = jnp.concatenate([q_ref[:, hh * dh:(hh + 1) * dh] for hh in range(C_HPG)]
                         + [jnp.zeros((8 - C_HPG, dh), f32)], axis=0)
    q8b = q8.astype(bf16)
    slope = jnp.concatenate([jnp.full((1, 1), sl_ref[g * C_HPG + hh], f32) for hh in range(C_HPG)]
                            + [jnp.zeros((8 - C_HPG, 1), f32)], axis=0)

    def own(ref):
        return functools.reduce(jnp.add, [jnp.where(g == gg, ref[:, gg * dh:(gg + 1) * dh], 0.0) for gg in range(G)])

    def softmax_pv(scores, values_t, s_new, v_new):
        m = functools.reduce(jnp.maximum, [jnp.max(s, axis=-1, keepdims=True) for s in scores] + [s_new])
        e_new = jnp.exp(s_new - m)
        den = e_new
        acc = e_new * v_new
        for s, vt in zip(scores, values_t):
            e = jnp.exp(s - m)
            den = den + jnp.sum(e, axis=-1, keepdims=True)
            acc = acc + _dot_t(e.astype(bf16), vt.astype(bf16))
        return acc / den

    lane = lax.broadcasted_iota(jnp.int32, (1, PAGE_SIZE), 1)
    scores, values_t = [], []
    for s_i in range(C_TOPK):
        j = ix_ref[b, g * C_TOPK + s_i]
        jc = jnp.minimum(j, n_past - 1)
        kpos = (jc // per_page) * PAGE_SIZE + lane
        sc = jnp.dot(q8b, blocks[s_i][0].astype(bf16), preferred_element_type=f32) * scale \
            - slope * (t - kpos).astype(f32)
        ok = ((lane >> C_SEL_SHIFT) == (jc % per_page)) & (j < n_past)
        scores.append(jnp.where(ok, sc, NEG_INF))
        values_t.append(blocks[s_i][1])
    s_new = jnp.sum(q8 * own(ksn_ref), axis=-1, keepdims=True) * scale
    o_s = softmax_pv(scores, values_t, s_new, own(vsn_ref))

    nw = win_ref.shape[-1]
    wdist = (nw - lax.broadcasted_iota(jnp.int32, (1, nw), 1)).astype(f32)
    sw = jnp.dot(q8b, win_ref[0].astype(bf16), preferred_element_type=f32) * scale - slope * wdist
    s_new = jnp.sum(q8 * own(kwn_ref), axis=-1, keepdims=True) * scale
    o_w = softmax_pv([sw], [win_ref[1]], s_new, own(vwn_ref))

    gs = jax.nn.sigmoid(gate_ref[...] + bg_ref[...])
    glane = lax.broadcasted_iota(jnp.int32, (1, LANES), 1)
    outs = []
    for hh in range(C_HPG):
        gate = [jnp.sum(jnp.where(glane == (g * C_HPG + hh) * 3 + r, gs, 0.0), axis=-1, keepdims=True)
                for r in range(3)]
        outs.append(gate[0] * oc_ref[:, hh * dh:(hh + 1) * dh] + gate[1] * o_s[hh:hh + 1] + gate[2] * o_w[hh:hh + 1])
    o_ref[...] = jnp.concatenate(outs, axis=-1).astype(o_ref.dtype)


def nsa_sel_win_sample(page_table, sel_idx, cache_t, win_t, layer, proj, b_gate_p, o_c, t):
    Bs = proj.shape[0]
    G, dh = C_KV_HEADS, C_HEAD_DIM
    GD = G * dh
    assert win_t.shape[-1] == C_WINDOW and t % C_SEL_BLOCK == 0
    n_past = t // C_SEL_BLOCK
    per_page = PAGE_SIZE // C_SEL_BLOCK

    def blk_spec(s_i):
        def imap(b, g, pt, ix, sl):
            j = jnp.minimum(ix[b, g * C_TOPK + s_i], n_past - 1)
            return (layer, pt[b, j // per_page], 1, g, 0, 0)
        return pl.BlockSpec((None, None, 2, None, dh, PAGE_SIZE), imap)

    row = lambda w, blk: pl.BlockSpec((None, 1, w), lambda b, g, pt, ix, sl, blk=blk: (b, 0, blk))
    base = C_HEADS * dh // GD
    gate_blk = (C_HEADS * dh + 6 * GD) // LANES
    return pl.pallas_call(
        functools.partial(_sel_win_sample_body, t=t),
        out_shape=jax.ShapeDtypeStruct((Bs, 1, C_HEADS * dh), bf16),
        grid_spec=pltpu.PrefetchScalarGridSpec(
            num_scalar_prefetch=3,
            grid=(Bs, G),
            in_specs=[blk_spec(s_i) for s_i in range(C_TOPK)] + [
                pl.BlockSpec((None, None, 2, None, dh, C_WINDOW), lambda b, g, pt, ix, sl: (layer, b, 0, g, 0, 0)),
                pl.BlockSpec((None, 1, C_HPG * dh), lambda b, g, pt, ix, sl: (b, 0, g)),
                row(GD, base + 2), row(GD, base + 3), row(GD, base + 4), row(GD, base + 5),
                row(LANES, gate_blk),
                pl.BlockSpec((1, LANES), lambda b, g, pt, ix, sl: (0, 0)),
                pl.BlockSpec((None, 1, C_HPG * dh), lambda b, g, pt, ix, sl: (b, 0, g)),
            ],
            out_specs=pl.BlockSpec((None, 1, C_HPG * dh), lambda b, g, pt, ix, sl: (b, 0, g)),
        ),
        compiler_params=_params("parallel", "parallel"),
        name="nsa_sel_win_sample",
    )(page_table, sel_idx, jnp.asarray(SLOPES_C), *([cache_t] * C_TOPK), win_t, proj, proj, proj, proj, proj,
      proj, b_gate_p, o_c)


def _norm_body(x_ref, g_ref, o_ref):
    o_ref[...] = _rms(x_ref[...], g_ref[...])


def rmsnorm_rows(x, g, *, tm=1024):
    M, D = x.shape
    tm = min(tm, M)
    return pl.pallas_call(
        _norm_body,
        out_shape=jax.ShapeDtypeStruct((M, D), f32),
        grid=(M // tm,),
        in_specs=[pl.BlockSpec((tm, D), lambda i: (i, 0)), pl.BlockSpec((1, D), lambda i: (0, 0))],
        out_specs=pl.BlockSpec((tm, D), lambda i: (i, 0)),
        compiler_params=_params("parallel"),
        name="rmsnorm",
    )(x, g.reshape(1, D))


def _position_minor(cache):
    return jnp.transpose(cache, (0, 1, 3, 4, 5, 2))


def kernel(x_prompt, x_sample, cache_dil_w128, cache_dil_w512, cache_dil_w2048, state_gla, cache_nsa_win,
           cache_nsa_kv, cache_mem_kv, page_table, mem_prompt, g_mix, g_cross, g_mem, g_ffn, g_final,
           w_a_qkv, w_a_o, w_b_in, w_b_gate2, b_b_gate, g_b_head, w_b_o, w_c_in, b_c_gate, c_pos_k, c_pos_v,
           w_c_k1, w_c_k2, w_c_v1, w_c_v2, w_c_o, w_x_q, w_x_kv, w_x_o, w_ffn_in, w_ffn_out):
    Bp, L, D = x_prompt.shape
    Bs = x_sample.shape[0]
    depth = g_mix.shape[0]
    n_pages = page_table.shape[1]
    t_s = n_pages * PAGE_SIZE
    G, dh = C_KV_HEADS, C_HEAD_DIM
    GD = G * dh
    HK, HV = B_HEADS * B_DK, B_HEADS * B_DV
    cast = lambda a: a.astype(bf16)

    wa_qkv, wa_o = cast(w_a_qkv), cast(w_a_o)
    nb = 2 * HK + HV
    wb_in = cast(jnp.concatenate(
        [w_b_in[..., :nb], w_b_in[..., nb + B_GATE_RANK:], w_b_in[..., nb:nb + B_GATE_RANK],
         jnp.zeros(w_b_in.shape[:2] + (LANES - B_GATE_RANK,), f32)], axis=-1))
    wb_gate2 = jnp.pad(w_b_gate2, ((0, 0), (0, LANES - B_GATE_RANK), (0, 0)))
    wb_o = cast(w_b_o)
    nc = -(-w_c_in.shape[-1] // LANES) * LANES
    wc_in = cast(jnp.pad(w_c_in, ((0, 0), (0, 0), (0, nc - w_c_in.shape[-1]))))
    bc_gate = jnp.pad(b_c_gate, ((0, 0), (0, LANES - b_c_gate.shape[-1])))
    half = C_CMP_STRIDE * dh
    two_chunk = lambda w: cast(jnp.concatenate([w[:, :half], w[:, half:]], axis=-1))
    wc_k1, wc_v1 = two_chunk(w_c_k1), two_chunk(w_c_v1)
    wc_k2, wc_v2, wc_o = cast(w_c_k2), cast(w_c_v2), cast(w_c_o)
    pos_k = c_pos_k.reshape(-1, 2, half)
    pos_v = c_pos_v.reshape(-1, 2, half)
    wx_q, wx_kv, wx_o = cast(w_x_q), cast(w_x_kv), cast(w_x_o)
    wf_in, wf_out = cast(w_ffn_in), cast(w_ffn_out)

    Mp = Bp * L
    x = x_prompt.reshape(Mp, D)
    mem2d = mem_prompt.reshape(-1, D)
    dil_p = [[] for _ in range(A_GROUPS)]
    gla_p, rows_p, win_p, mem_p = [], [], [], []
    for l in range(depth):
        kind, j = l % N_MIXERS, l // N_MIXERS
        if kind == 0:
            qkv = mm(x, (wa_qkv, j), g=g_mix[l]).reshape(Bp, L, -1)
            outs = [dilated_group(qkv, grp) for grp in range(A_GROUPS)]
            x = dilated_merge_out([o.reshape(Mp, A_OUT) for o, _ in outs],
                                  [s.reshape(Mp, A_OUT) for _, s in outs], (wa_o, j), x)
            sect = A_GROUPS * A_OUT
            for grp in range(A_GROUPS):
                w = min(A_WINDOWS[grp], L)
                tail = lambda t: qkv[:, L - w:, t * sect + grp * A_OUT:t * sect + (grp + 1) * A_OUT].reshape(
                    Bp, w, A_SLOTS, A_HEAD_DIM)
                dil_p[grp].append(jnp.stack([tail(1), tail(2)], axis=2))
        elif kind == 1:
            proj = mm(x, (wb_in, j), g=g_mix[l]).reshape(Bp, L, -1)
            y, S = gla_prompt(proj, wb_gate2[j], b_b_gate[j], g_b_head[j].reshape(-1),
                              jnp.zeros((Bp, B_HEADS, B_DK, B_DV), f32))
            gla_p.append(S)
            x = mm(y.reshape(Mp, HV), (wb_o, j), res=x)
        else:
            proj = mm(x, (wc_in, j), g=g_mix[l]).reshape(Bp, L, -1)
            q_w = C_HEADS * dh
            rows = proj[..., q_w:q_w + 6 * GD].reshape(Bp, L, 6, G, dh)
            chunks = lambda a: a.reshape(Bp, L // C_CMP_STRIDE, C_CMP_STRIDE, G, dh).transpose(
                0, 3, 1, 2, 4).reshape(Bp, G, L // C_CMP_STRIDE, half)
            kc, vc = nsa_compress(chunks(rows[:, :, 0]), chunks(rows[:, :, 1]), pos_k[j], pos_v[j],
                                  wc_k1[j], wc_k2[j], wc_v1[j], wc_v2[j])
            seqs = [cast(rows[:, :, r].transpose(0, 2, 1, 3)) for r in range(2, 6)]
            o = nsa_attention_prompt(proj, bc_gate[j:j + 1], kc, vc, *seqs)
            x = mm(o.reshape(Mp, q_w), (wc_o, j), res=x)
            rows_p.append(rows[:, :, :4])
            win_p.append(rows[:, L - min(C_WINDOW, L):, 4:])
        mem_kv = mm(mem2d, (wx_kv, l), g=g_mem[l]).reshape(Bp, -1, 2 * D)
        mem_p.append(mem_kv.reshape(Bp, -1, 2, MEM_HEADS, D // MEM_HEADS))
        x = xattn_prompt(x.reshape(Bp, L, D), g_cross[l], (wx_q, l), cast(mem_kv), (wx_o, l)).reshape(Mp, D)
        x = ffn(x, g_ffn[l], (wf_in, l), (wf_out, l))
    y_prompt = rmsnorm_rows(x, g_final).reshape(Bp, L, D)

    x = x_sample.reshape(Bs, D)
    dil_t = [_position_minor(c) for c in (cache_dil_w128, cache_dil_w512, cache_dil_w2048)]
    nsa_t = _position_minor(cache_nsa_kv)
    win_t = _position_minor(cache_nsa_win)
    dil_s = [[] for _ in range(A_GROUPS)]
    gla_s, rows_s, win_s = [], [], []
    for l in range(depth):
        kind, j = l % N_MIXERS, l // N_MIXERS
        if kind == 0:
            qkv = mm(x, (wa_qkv, j), g=g_mix[l]).reshape(Bs, 1, -1)
            o = dil_sample(qkv, dil_t, j)
            x = mm(o.transpose(0, 2, 1).reshape(Bs, A_OUT), (wa_o, j), res=x)
            kv = qkv.reshape(Bs, 1, 3, A_GROUPS, A_SLOTS, A_HEAD_DIM)
            for grp in range(A_GROUPS):
                dil_s[grp].append(jnp.stack([kv[:, :, 1, grp], kv[:, :, 2, grp]], axis=2))
        elif kind == 1:
            proj = mm(x, (wb_in, j), g=g_mix[l]).reshape(Bs, 1, -1)
            y, S = gla_step(proj, wb_gate2[j], b_b_gate[j], g_b_head[j].reshape(-1), state_gla, j)
            gla_s.append(S)
            x = mm(y.reshape(Bs, HV), (wb_o, j), res=x)
        else:
            proj = mm(x, (wc_in, j), g=g_mix[l]).reshape(Bs, 1, -1)
            q_w = C_HEADS * dh
            kc, vc = nsa_compress_sample(page_table, nsa_t, j, pos_k[j], pos_v[j],
                                         wc_k1[j], wc_k2[j], wc_v1[j], wc_v2[j])
            o_c, idx = nsa_cmp_topk_sample(proj, kc, vc, t_s)
            sel_idx = idx[:, :G, :C_TOPK].reshape(Bs, G * C_TOPK)
            o = nsa_sel_win_sample(page_table, sel_idx, nsa_t, win_t, j, proj, bc_gate[j:j + 1], o_c, t_s)
            x = mm(o.reshape(Bs, q_w), (wc_o, j), res=x)
            rows = proj[..., q_w:q_w + 6 * GD].reshape(Bs, 1, 6, G, dh)
            rows_s.append(rows[:, :, :4])
            win_s.append(rows[:, :, 4:])
        q = mm(x, (wx_q, l), g=g_cross[l]).reshape(Bs, 1, D)
        o = xattn_sample(q, cache_mem_kv, l)
        x = mm(o.reshape(Bs, D), (wx_o, l), res=x)
        x = ffn(x, g_ffn[l], (wf_in, l), (wf_out, l))
    y_sample = rmsnorm_rows(x, g_final).reshape(Bs, 1, D)

    st = jnp.stack
    return (y_prompt, y_sample, st(dil_p[0]), st(dil_s[0]), st(dil_p[1]), st(dil_s[1]), st(dil_p[2]), st(dil_s[2]),
            st(gla_p), st(gla_s), st(win_p), st(win_s), st(rows_p), st(rows_s), st(mem_p))
```

```python
import functools

import jax
import jax.numpy as jnp
import numpy as np
from jax import lax
from jax.experimental import pallas as pl
from jax.experimental.pallas import tpu as pltpu

f32 = jnp.float32
bf16 = jnp.bfloat16

N_MIXERS = 3
A_WINDOWS = (128, 512, 2048)
A_DILATIONS = (1, 4, 16)
A_GROUPS = 3
A_SLOTS = 8
A_HEAD_DIM = 64
A_OUT = A_SLOTS * A_HEAD_DIM
A_WIN_STEPS = 128

B_HEADS = 4
B_DK = 128
B_DV = 256
B_GATE_RANK = 16
B_GATE_TAU = 16.0

C_HEADS = 16
C_KV_HEADS = 4
C_HPG = C_HEADS // C_KV_HEADS
C_HEAD_DIM = 64
C_CMP_BLOCK = 32
C_CMP_STRIDE = 16
C_SEL_BLOCK = 64
C_SEL_SHIFT = 6
C_TOPK = 16
C_WINDOW = 512
C_CMP_HIDDEN = 128

MEM_HEADS = 4
PAGE_SIZE = 128

RMS_EPS = 1e-6
NEG_INF = -1e30
FORCE = 1e30
TINY = 1e-30
REMOVED = -3e38

LANES = 128
VMEM_LIMIT_BYTES = 56 * 1024 * 1024
HIGHEST = lax.Precision.HIGHEST


def _alibi_slopes(n):
    return np.asarray(2.0 ** (-8.0 * np.arange(1, n + 1) / n), dtype=np.float32)


def _bf16_pieces(x, n):
    out, rest = [], np.asarray(x, np.float64)
    for _ in range(n):
        p = rest.astype(bf16).astype(np.float64)
        out.append(p)
        rest = rest - p
    return np.stack(out, axis=-1).astype(np.float32)


LOG2E = 1.4426950408889634
SLOPES_A = _alibi_slopes(A_GROUPS * A_SLOTS).reshape(A_GROUPS, A_SLOTS)
SLOPES_C = _alibi_slopes(C_HEADS)
N_PIECES = 3
SLOPE_PIECES = 4
SLOPES_C_PIECES = _bf16_pieces(SLOPES_C.astype(np.float64) * LOG2E, SLOPE_PIECES).reshape(-1)
VAL_ROWS = C_HEAD_DIM + 16


def _params(*sem):
    return pltpu.CompilerParams(dimension_semantics=sem, vmem_limit_bytes=VMEM_LIMIT_BYTES)


def _rms(x, g):
    return x * lax.rsqrt(jnp.mean(x * x, axis=-1, keepdims=True) + RMS_EPS) * g


def _dot_t(a, b, precision=None):
    return lax.dot_general(a, b, (((1,), (1,)), ((), ())), precision=precision, preferred_element_type=f32)


def _hdot(a, b):
    return jnp.dot(a, b, precision=HIGHEST, preferred_element_type=f32)


def _weight_spec(w, block, index):
    if isinstance(w, tuple):
        stack, layer = w
        return stack, pl.BlockSpec((None,) + block, lambda *ids: (layer,) + index(*ids))
    return w, pl.BlockSpec(block, index)


def _weight_shape(w):
    return w[0].shape[1:] if isinstance(w, tuple) else w.shape


def _pick_tile(n, target):
    best = LANES
    for t in range(LANES, min(n, target) + 1, LANES):
        if n % t == 0:
            best = t
    return best


def _mm_body(*refs, norm, res):
    it = iter(refs)
    x_ref = next(it)
    g_ref = next(it) if norm else None
    w_ref = next(it)
    r_ref = next(it) if res else None
    o_ref = next(it)
    xn_ref = next(it)

    @pl.when(pl.program_id(1) == 0)
    def _():
        x = x_ref[...].astype(f32)
        if norm:
            x = _rms(x, g_ref[...])
        xn_ref[...] = x.astype(bf16)

    acc = jnp.dot(xn_ref[...], w_ref[...], preferred_element_type=f32)
    if res:
        acc = acc + r_ref[...]
    o_ref[...] = acc.astype(o_ref.dtype)


def mm(x, w, *, g=None, res=None, out_dtype=f32, tm=1024, tn=1024):
    M, K = x.shape
    N = _weight_shape(w)[1]
    tm = min(tm, M)
    tn = _pick_tile(N, tn)
    assert M % tm == 0 and N % tn == 0
    args = [x]
    specs = [pl.BlockSpec((tm, K), lambda i, j: (i, 0))]
    if g is not None:
        args.append(g.reshape(1, K))
        specs.append(pl.BlockSpec((1, K), lambda i, j: (0, 0)))
    w_arr, w_spec = _weight_spec(w, (K, tn), lambda i, j: (0, j))
    args.append(w_arr)
    specs.append(w_spec)
    if res is not None:
        args.append(res)
        specs.append(pl.BlockSpec((tm, tn), lambda i, j: (i, j)))
    return pl.pallas_call(
        functools.partial(_mm_body, norm=g is not None, res=res is not None),
        out_shape=jax.ShapeDtypeStruct((M, N), out_dtype),
        grid=(M // tm, N // tn),
        in_specs=specs,
        out_specs=pl.BlockSpec((tm, tn), lambda i, j: (i, j)),
        scratch_shapes=[pltpu.VMEM((tm, K), bf16)],
        compiler_params=_params("parallel", "arbitrary"),
        name="mm",
    )(*args)


def _ffn_body(x_ref, g_ref, gf_ref, wa_ref, wb_ref, wo_ref, o_ref, xn_ref, acc_ref, *, final_norm):
    j = pl.program_id(1)

    @pl.when(j == 0)
    def _():
        xn_ref[...] = _rms(x_ref[...], g_ref[...]).astype(bf16)
        acc_ref[...] = jnp.zeros_like(acc_ref)

    xn = xn_ref[...]
    a = jnp.dot(xn, wa_ref[...], preferred_element_type=f32)
    b = jnp.dot(xn, wb_ref[...], preferred_element_type=f32)
    h = (a * jax.nn.sigmoid(a) * b).astype(bf16)
    acc_ref[...] += jnp.dot(h, wo_ref[...], preferred_element_type=f32)

    @pl.when(j == pl.num_programs(1) - 1)
    def _():
        y = x_ref[...] + acc_ref[...]
        o_ref[...] = _rms(y, gf_ref[...]) if final_norm else y


def ffn(x, g, w_in, w_out, *, g_final=None, tm=1024, th=256):
    M, D = x.shape
    gf = g if g_final is None else g_final
    F = _weight_shape(w_out)[0]
    tm = min(tm, M)
    th = _pick_tile(F, th)
    nh = F // th
    wa, wa_spec = _weight_spec(w_in, (D, th), lambda i, j: (0, j))
    wb, wb_spec = _weight_spec(w_in, (D, th), lambda i, j: (0, j + nh))
    wo, wo_spec = _weight_spec(w_out, (th, D), lambda i, j: (j, 0))
    return pl.pallas_call(
        functools.partial(_ffn_body, final_norm=g_final is not None),
        out_shape=jax.ShapeDtypeStruct((M, D), f32),
        grid=(M // tm, nh),
        in_specs=[
            pl.BlockSpec((tm, D), lambda i, j: (i, 0)),
            pl.BlockSpec((1, D), lambda i, j: (0, 0)),
            pl.BlockSpec((1, D), lambda i, j: (0, 0)),
            wa_spec, wb_spec, wo_spec,
        ],
        out_specs=pl.BlockSpec((tm, D), lambda i, j: (i, 0)),
        scratch_shapes=[pltpu.VMEM((tm, D), bf16), pltpu.VMEM((tm, D), f32)],
        compiler_params=_params("parallel", "arbitrary"),
        name="ffn",
    )(x, g.reshape(1, D), gf.reshape(1, D), wa, wb, wo)


def _xattn_body(x_ref, g_ref, wq_ref, kv_ref, wo_ref, o_ref, *, heads):
    x = x_ref[...]
    D = x.shape[-1]
    dh = D // heads
    xn = _rms(x, g_ref[...]).astype(bf16)
    q = jnp.dot(xn, wq_ref[...], preferred_element_type=f32).astype(bf16)
    outs = []
    for h in range(heads):
        kh = kv_ref[:, h * dh:(h + 1) * dh]
        vh = kv_ref[:, D + h * dh:D + (h + 1) * dh]
        s = _dot_t(q[:, h * dh:(h + 1) * dh], kh) * dh ** -0.5
        m = jnp.max(s, axis=-1, keepdims=True)
        e = jnp.exp(s - m)
        outs.append(jnp.dot(e.astype(bf16), vh, preferred_element_type=f32) / jnp.sum(e, axis=-1, keepdims=True))
    o = jnp.concatenate(outs, axis=-1).astype(bf16)
    o_ref[...] = x + jnp.dot(o, wo_ref[...], preferred_element_type=f32)


def xattn_prompt(x, g, wq, kv, wo, *, tq=512):
    B, L, D = x.shape
    Mem = kv.shape[1]
    tq = min(tq, L)
    wq, wq_spec = _weight_spec(wq, (D, D), lambda b, i: (0, 0))
    wo, wo_spec = _weight_spec(wo, (D, D), lambda b, i: (0, 0))
    return pl.pallas_call(
        functools.partial(_xattn_body, heads=MEM_HEADS),
        out_shape=jax.ShapeDtypeStruct((B, L, D), f32),
        grid=(B, L // tq),
        in_specs=[
            pl.BlockSpec((None, tq, D), lambda b, i: (b, i, 0)),
            pl.BlockSpec((1, D), lambda b, i: (0, 0)),
            wq_spec,
            pl.BlockSpec((None, Mem, 2 * D), lambda b, i: (b, 0, 0)),
            wo_spec,
        ],
        out_specs=pl.BlockSpec((None, tq, D), lambda b, i: (b, i, 0)),
        compiler_params=_params("parallel", "parallel"),
        name="xattn_prompt",
    )(x, g.reshape(1, D), wq, kv, wo)


def _dil_body(sl_ref, q_ref, kc_ref, kp_ref, vc_ref, vp_ref, o_ref, lse_ref, *, grp, dil, tq):
    i = pl.program_id(1)
    c = pl.program_id(2)
    ws = A_WIN_STEPS
    scale = A_HEAD_DIM ** -0.5
    row = lax.broadcasted_iota(jnp.int32, (ws, 2 * ws), 0)
    col = lax.broadcasted_iota(jnp.int32, (ws, 2 * ws), 1)
    dist = row + ws - col
    in_band = (dist >= 0) & (dist <= ws)
    valid_first = in_band & ((col >= ws) | (i > 0))
    distf = (dist * dil).astype(f32)
    lane = lax.broadcasted_iota(jnp.int32, (1, LANES), 1)
    first = lane < A_HEAD_DIM
    slopes = [sl_ref[grp * A_SLOTS + 2 * c + half] for half in range(2)]

    def rows(ref, r, start, n):
        return ref[pl.ds(r + start * dil, n, stride=dil), :] if dil > 1 else ref[start:start + n, :]

    def residue(r, carry):
        for j in range(tq // ws):
            q = rows(q_ref, r, j * ws, ws).astype(bf16)
            if j == 0:
                k = jnp.concatenate([rows(kp_ref, r, 0, ws), rows(kc_ref, r, 0, ws)], axis=0)
                v = jnp.concatenate([rows(vp_ref, r, 0, ws), rows(vc_ref, r, 0, ws)], axis=0)
            else:
                k = rows(kc_ref, r, (j - 1) * ws, 2 * ws)
                v = rows(vc_ref, r, (j - 1) * ws, 2 * ws)
            k, v = k.astype(bf16), v.astype(bf16)
            valid = valid_first if j == 0 else in_band
            halves = []
            for half in range(2):
                qm = jnp.where(first == (half == 0), q, jnp.zeros_like(q))
                s = _dot_t(qm, k) * scale - slopes[half] * distf
                s = jnp.where(valid, s, NEG_INF)
                m = jnp.max(s, axis=-1, keepdims=True)
                e = jnp.exp(s - m)
                den = jnp.sum(e, axis=-1, keepdims=True)
                pv = jnp.dot(e.astype(bf16), v, preferred_element_type=f32) / den
                halves.append((pv, m + jnp.log(den)))
            o_t = jnp.where(first, halves[0][0], halves[1][0])
            l_t = jnp.where(first, halves[0][1], halves[1][1])
            if dil > 1:
                o_ref[pl.ds(r + j * ws * dil, ws, stride=dil), :] = o_t
                lse_ref[pl.ds(r + j * ws * dil, ws, stride=dil), :] = l_t
            else:
                o_ref[j * ws:(j + 1) * ws, :] = o_t
                lse_ref[j * ws:(j + 1) * ws, :] = l_t
        return carry

    unroll = min(dil, DIL_UNROLL)

    def trip(u, carry):
        for k in range(unroll):
            residue(u * unroll + k, carry)
        return carry

    if dil > unroll:
        lax.fori_loop(0, dil // unroll, trip, 0)
    else:
        trip(0, 0)


DIL_TQ = (1024, 256, 128)
DIL_UNROLL = 4


def dilated_group(qkv, grp):
    B, L, W3 = qkv.shape
    d = A_DILATIONS[grp]
    tq = min(DIL_TQ[grp], L // d)
    rows_blk = tq * d
    prev_blk = A_WIN_STEPS * d
    assert L % rows_blk == 0 and tq % A_WIN_STEPS == 0
    ratio = tq // A_WIN_STEPS
    tiles = A_OUT // LANES
    sect = A_GROUPS * tiles

    def spec(t, prev=False):
        col = lambda c: t * sect + grp * tiles + c
        if prev:
            return pl.BlockSpec((None, prev_blk, LANES),
                                lambda b, i, c, sl: (b, jnp.maximum(i * ratio - 1, 0), col(c)))
        return pl.BlockSpec((None, rows_blk, LANES), lambda b, i, c, sl: (b, i, col(c)))

    out_spec = pl.BlockSpec((None, rows_blk, LANES), lambda b, i, c, sl: (b, i, c))
    return pl.pallas_call(
        functools.partial(_dil_body, grp=grp, dil=d, tq=tq),
        out_shape=[jax.ShapeDtypeStruct((B, L, A_OUT), f32)] * 2,
        grid_spec=pltpu.PrefetchScalarGridSpec(
            num_scalar_prefetch=1,
            grid=(B, L // rows_blk, tiles),
            in_specs=[spec(0), spec(1), spec(1, True), spec(2), spec(2, True)],
            out_specs=[out_spec, out_spec],
        ),
        compiler_params=_params("parallel", "parallel", "parallel"),
        name=f"dilated_g{grp}",
    )(jnp.asarray(SLOPES_A.reshape(-1)), qkv, qkv, qkv, qkv, qkv)


def _dil_out_body(o0, o1, o2, l0, l1, l2, w_ref, r_ref, out_ref):
    ls = [l0[...], l1[...], l2[...]]
    m = jnp.maximum(jnp.maximum(ls[0], ls[1]), ls[2])
    es = [jnp.exp(l - m) for l in ls]
    den = es[0] + es[1] + es[2]
    o = (es[0] / den) * o0[...] + (es[1] / den) * o1[...] + (es[2] / den) * o2[...]
    out_ref[...] = r_ref[...] + jnp.dot(o.astype(bf16), w_ref[...], preferred_element_type=f32)


def dilated_merge_out(os, lses, w_o, res, *, tm=512):
    M, D = res.shape
    tm = min(tm, M)
    row = lambda n: pl.BlockSpec((tm, n), lambda i: (i, 0))
    w_o, w_spec = _weight_spec(w_o, (A_OUT, D), lambda i: (0, 0))
    return pl.pallas_call(
        _dil_out_body,
        out_shape=jax.ShapeDtypeStruct((M, D), f32),
        grid=(M // tm,),
        in_specs=[row(A_OUT)] * 6 + [w_spec, row(D)],
        out_specs=row(D),
        compiler_params=_params("parallel"),
        name="dilated_merge_out",
    )(*os, *lses, w_o, res)


GLA_SUB = 16


def _log_sigmoid(z):
    return -(jnp.maximum(-z, 0.0) + jnp.log1p(jnp.exp(-jnp.abs(z))))


def _gla_body(q_ref, k_ref, v_ref, r_ref, gl_ref, w2_ref, bg_ref, gh_ref, s0_ref, y_ref, s_ref, a_ref):
    C = q_ref.shape[0]
    c = GLA_SUB

    @pl.when(pl.program_id(1) == 0)
    def _():
        s_ref[...] = s0_ref[...]

    z = _hdot(gl_ref[...], w2_ref[...]) + bg_ref[...]
    la = _log_sigmoid(z) / B_GATE_TAU
    rowC = lax.broadcasted_iota(jnp.int32, (C, C), 0)
    colC = lax.broadcasted_iota(jnp.int32, (C, C), 1)
    tri = (rowC >= colC).astype(f32)
    b_all = _hdot(tri, la)
    row_k = lax.broadcasted_iota(jnp.int32, (C, B_DK), 0)
    col_c = lax.broadcasted_iota(jnp.int32, (c, C), 1)
    row_c = lax.broadcasted_iota(jnp.int32, (c, 1), 0)

    for h in range(B_HEADS):
        bh = b_all[:, h * B_DK:(h + 1) * B_DK]
        qh = q_ref[:, h * B_DK:(h + 1) * B_DK] * B_DK ** -0.5
        kh = k_ref[:, h * B_DK:(h + 1) * B_DK]
        vh = v_ref[:, h * B_DV:(h + 1) * B_DV].astype(bf16)
        S = s_ref[h]
        o = jnp.dot((qh * jnp.exp(bh)).astype(bf16), S.astype(bf16), preferred_element_type=f32)

        for I in range(C // c):
            r0 = I * c
            qI, kI, bI = qh[r0:r0 + c], kh[r0:r0 + c], bh[r0:r0 + c]
            if I == 0:
                A_I = jnp.zeros((c, C), f32)
            else:
                beta = bh[r0 - 1:r0]
                qt = qI * jnp.exp(bI - beta)
                kt = kh * jnp.exp(jnp.where(row_k < r0, beta - bh, 0.0))
                A_I = jnp.where(col_c < r0, _dot_t(qt.astype(bf16), kt.astype(bf16)), 0.0)
            for j in range(c):
                ex = jnp.exp(jnp.minimum(bI - bI[j:j + 1], 0.0))
                tj = jnp.sum(qI * kI[j:j + 1] * ex, axis=-1, keepdims=True)
                tj = jnp.where(row_c >= j, tj, 0.0)
                A_I = jnp.where(col_c == r0 + j, tj, A_I)
            a_ref[r0:r0 + c, :] = A_I
        o = o + jnp.dot(a_ref[...].astype(bf16), vh, preferred_element_type=f32)

        b_last = bh[C - 1:C]
        kdec = jnp.transpose(kh * jnp.exp(b_last - bh)).astype(bf16)
        decay = jnp.transpose(jnp.broadcast_to(jnp.exp(b_last), (8, B_DK)))[:, 0:1]
        s_ref[h] = decay * S + jnp.dot(kdec, vh, preferred_element_type=f32)

        on = _rms(o, gh_ref[:, h * B_DV:(h + 1) * B_DV])
        rh = r_ref[:, h * B_DV:(h + 1) * B_DV]
        y_ref[:, h * B_DV:(h + 1) * B_DV] = (on * (rh * jax.nn.sigmoid(rh))).astype(y_ref.dtype)


def gla_prompt(proj, w_gate2p, b_gate, g_head, s0, *, chunk=128):
    B, L, _ = proj.shape
    HK, HV = B_HEADS * B_DK, B_HEADS * B_DV
    C = min(chunk, L)
    assert L % C == 0 and C % GLA_SUB == 0
    return pl.pallas_call(
        _gla_body,
        out_shape=[jax.ShapeDtypeStruct((B, L, HV), bf16),
                   jax.ShapeDtypeStruct((B, B_HEADS, B_DK, B_DV), f32)],
        grid=(B, L // C),
        in_specs=[
            pl.BlockSpec((None, C, HK), lambda b, i: (b, i, 0)),
            pl.BlockSpec((None, C, HK), lambda b, i: (b, i, 1)),
            pl.BlockSpec((None, C, HV), lambda b, i: (b, i, 1)),
            pl.BlockSpec((None, C, HV), lambda b, i: (b, i, 2)),
            pl.BlockSpec((None, C, LANES), lambda b, i: (b, i, (2 * HK + 2 * HV) // LANES)),
            pl.BlockSpec((LANES, HK), lambda b, i: (0, 0)),
            pl.BlockSpec((1, HK), lambda b, i: (0, 0)),
            pl.BlockSpec((1, HV), lambda b, i: (0, 0)),
            pl.BlockSpec((None, B_HEADS, B_DK, B_DV), lambda b, i: (b, 0, 0, 0)),
        ],
        out_specs=[pl.BlockSpec((None, C, HV), lambda b, i: (b, i, 0)),
                   pl.BlockSpec((None, B_HEADS, B_DK, B_DV), lambda b, i: (b, 0, 0, 0))],
        scratch_shapes=[pltpu.VMEM((C, C), f32)],
        compiler_params=_params("parallel", "arbitrary"),
        name="gla_prompt",
    )(proj, proj, proj, proj, proj, w_gate2p, b_gate.reshape(1, HK), g_head.reshape(1, HV), s0)


def _cmp_mlp_tail(z, posb, w2_ref):
    n = z.shape[0]
    H = C_CMP_HIDDEN
    hid = z[:, :H] + pltpu.roll(z[:, H:], n - 1, 0) + posb
    return jnp.dot(jax.nn.gelu(hid).astype(bf16), w2_ref[...], preferred_element_type=f32)


def _cmp_pos_bias(p_ref, w1_ref):
    H = C_CMP_HIDDEN
    pz = jnp.dot(p_ref[...].astype(bf16), w1_ref[...], preferred_element_type=f32)
    return pz[0:1, :H] + pz[1:2, H:]


def _cmp_body(xk_ref, xv_ref, pk_ref, pv_ref, wk1_ref, wk2_ref, wv1_ref, wv2_ref, ok_ref, ov_ref):
    for x_ref, p_ref, w1_ref, w2_ref, o_ref in ((xk_ref, pk_ref, wk1_ref, wk2_ref, ok_ref),
                                                (xv_ref, pv_ref, wv1_ref, wv2_ref, ov_ref)):
        z = jnp.dot(x_ref[...].astype(bf16), w1_ref[...], preferred_element_type=f32)
        o_ref[...] = _cmp_mlp_tail(z, _cmp_pos_bias(p_ref, w1_ref), w2_ref).astype(o_ref.dtype)


def nsa_compress(xk, xv, pos_k, pos_v, wk1, wk2, wv1, wv2):
    B, G, n, W = xk.shape
    x_spec = pl.BlockSpec((None, None, n, W), lambda b, g: (b, g, 0, 0))
    full = lambda a: pl.BlockSpec(a.shape, lambda b, g: (0,) * a.ndim)
    o_spec = pl.BlockSpec((None, None, n, C_HEAD_DIM), lambda b, g: (b, g, 0, 0))
    consts = [pos_k, pos_v, wk1, wk2, wv1, wv2]
    return pl.pallas_call(
        _cmp_body,
        out_shape=[jax.ShapeDtypeStruct((B, G, n, C_HEAD_DIM), bf16)] * 2,
        grid=(B, G),
        in_specs=[x_spec, x_spec] + [full(a) for a in consts],
        out_specs=[o_spec, o_spec],
        compiler_params=_params("parallel", "parallel"),
        name="nsa_compress",
    )(xk, xv, *consts)


def _topk_mask_t(score_t, blk_t):
    keep = jnp.zeros(score_t.shape, f32)
    big = float(score_t.shape[0])
    for _ in range(C_TOPK):
        mx = jnp.max(score_t, axis=0, keepdims=True)
        first = jnp.min(jnp.where(score_t == mx, blk_t, big), axis=0, keepdims=True)
        hit = blk_t == first
        keep = jnp.where(hit, jnp.where(mx > 0.5 * NEG_INF, 1.0, 0.0), keep)
        score_t = jnp.where(hit, REMOVED, score_t)
    return keep


def _reduce_rows(x, op):
    slabs = [x[r:r + 8] for r in range(0, x.shape[0], 8)]
    while len(slabs) > 1:
        slabs = [op(slabs[k], slabs[k + 1]) if k + 1 < len(slabs) else slabs[k] for k in range(0, len(slabs), 2)]
    red = jnp.max if op is jnp.maximum else jnp.sum
    return red(slabs[0], axis=0, keepdims=True)


def _softmax2_rows_t(s, mask):
    e = jnp.where(mask, jnp.exp2(s - _reduce_rows(s, jnp.maximum)), 0.0)
    return e / jnp.maximum(_reduce_rows(e, jnp.add), TINY)


def _nsa_body(sp_ref, q_ref, gate_ref, bg_ref, kc_ref, vc_ref, ks_ref, vs_ref, kw_ref, vw_ref,
              o_ref, m_ref, acc_ref, gs_ref, sa_ref, sb_ref, *, tk):
    g = pl.program_id(1)
    i = pl.program_id(2)
    tq = q_ref.shape[0]
    R = C_HPG * tq
    L = ks_ref.shape[0]
    n_cmp = kc_ref.shape[0]
    NS = LANES
    dh = C_HEAD_DIM
    q0 = i * tq

    q_t = jnp.transpose(q_ref[...] * (dh ** -0.5 * LOG2E))
    prow = lax.broadcasted_iota(jnp.int32, (LANES - dh, 1), 0)
    cols = []
    for hh in range(C_HPG):
        pc = jnp.zeros((LANES - dh, 1), f32)
        for n in reversed(range(SLOPE_PIECES)):
            pc = jnp.where(prow < 2 * n + 2, sp_ref[(g * C_HPG + hh) * SLOPE_PIECES + n], pc)
        cols.append(jnp.concatenate([q_t[hh * dh:(hh + 1) * dh], jnp.broadcast_to(pc, (LANES - dh, tq))], axis=0))
    qa_t = jnp.concatenate(cols, axis=1).astype(bf16)
    t1 = q0 + lax.broadcasted_iota(jnp.int32, (1, tq), 1)
    t = jnp.concatenate([t1] * C_HPG, axis=1)

    W = C_WINDOW + tq
    w0 = pl.multiple_of(q0, tq)
    s = jnp.dot(kw_ref[pl.ds(w0, W), :], qa_t, preferred_element_type=f32)
    slabs = []
    for r0 in range(0, W, LANES):
        wpos = q0 - C_WINDOW + r0 + lax.broadcasted_iota(jnp.int32, (LANES, 1), 0)
        sl = s[r0:r0 + LANES]
        if r0 < tq - 1:
            sl = jnp.where(wpos >= t - C_WINDOW, sl, NEG_INF)
        if r0 + LANES - 1 > C_WINDOW:
            sl = jnp.where(wpos <= t, sl, NEG_INF)
        slabs.append(sl)
    s = jnp.concatenate(slabs, axis=0)
    e = jnp.exp2(s - _reduce_rows(s, jnp.maximum))
    o_w = jnp.dot(vw_ref[:, pl.ds(w0, W)], e.astype(bf16), preferred_element_type=f32)
    o_w = o_w[0:dh] / o_w[dh:dh + 1]

    cpos = lax.broadcasted_iota(jnp.int32, (n_cmp, 1), 0) * C_CMP_STRIDE + (C_CMP_BLOCK - 1)
    mask = cpos <= t
    s = jnp.where(mask, jnp.dot(kc_ref[...], qa_t, preferred_element_type=f32), NEG_INF)
    p = _softmax2_rows_t(s, mask)
    o_c = jnp.dot(vc_ref[...], p.astype(bf16), preferred_element_type=f32)

    psum = p[:, 0:tq]
    for hh in range(1, C_HPG):
        psum = psum + p[:, hh * tq:(hh + 1) * tq]
    n_blk = -(-(L // C_SEL_BLOCK) // 8) * 8
    jb = lax.broadcasted_iota(jnp.int32, (n_blk, n_cmp), 0)
    c_start = lax.broadcasted_iota(jnp.int32, (n_blk, n_cmp), 1) * C_CMP_STRIDE
    cover = jnp.where(c_start < (jb + 1) * C_SEL_BLOCK,
                      jnp.where(c_start + C_CMP_BLOCK > jb * C_SEL_BLOCK, 1.0, 0.0), 0.0).astype(bf16)
    imp = jnp.zeros((n_blk, tq), f32)
    rest = psum
    for _ in range(N_PIECES):
        piece = rest.astype(bf16)
        imp = imp + jnp.dot(cover, piece, preferred_element_type=f32)
        rest = rest - piece.astype(f32)
    blk = lax.broadcasted_iota(jnp.int32, (n_blk, tq), 0)
    cur = t1 >> C_SEL_SHIFT
    forced = (blk == 0) | (blk == cur) | (blk == cur - 1)
    score = jnp.where(blk * C_SEL_BLOCK <= t1, jnp.where(forced, FORCE, imp), NEG_INF)
    keep = _topk_mask_t(score, blk.astype(f32))
    neg1 = jnp.concatenate([jnp.where(keep > 0.5, jnp.where(blk == cur, NEG_INF, 0.0), NEG_INF),
                            jnp.full((NS - n_blk, tq), NEG_INF, f32)], axis=0).astype(bf16)
    qa2_t = jnp.concatenate([qa_t, jnp.concatenate([neg1] * C_HPG, axis=1)], axis=0)

    d0 = pl.multiple_of(q0, tq)
    kpos = q0 + lax.broadcasted_iota(jnp.int32, (tq, 1), 0)
    sc = jnp.dot(ks_ref[pl.ds(d0, tq), 0:LANES], qa_t, preferred_element_type=f32)
    sc = jnp.where(kpos <= t, jnp.where((kpos >> C_SEL_SHIFT) == (t >> C_SEL_SHIFT), sc, NEG_INF), NEG_INF)
    m0 = _reduce_rows(sc, jnp.maximum)
    m_ref[...] = m0
    acc_ref[...] = jnp.dot(vs_ref[:, pl.ds(d0, tq)], jnp.exp2(sc - m0).astype(bf16), preferred_element_type=f32)

    n_tiles = q0 // tk + 1

    def scores(kt, s_ref):
        k0 = pl.multiple_of(jnp.minimum(kt, n_tiles - 1) * tk, tk)
        s_ref[...] = jnp.dot(ks_ref[pl.ds(k0, tk), :], qa2_t, preferred_element_type=f32)

    def absorb(kt, s_ref):
        k0 = pl.multiple_of(kt * tk, tk)
        sc = s_ref[...]
        m_old = m_ref[...]
        m_new = jnp.maximum(m_old, _reduce_rows(sc, jnp.maximum))
        ee = jnp.exp2(sc - m_new).astype(bf16)
        acc_ref[...] = jnp.exp2(m_old - m_new) * acc_ref[...] + jnp.dot(vs_ref[:, pl.ds(k0, tk)], ee,
                                                                         preferred_element_type=f32)
        m_ref[...] = m_new

    scores(0, sa_ref)

    def pair(j, carry):
        scores(2 * j + 1, sb_ref)
        absorb(2 * j, sa_ref)
        scores(2 * j + 2, sa_ref)
        absorb(2 * j + 1, sb_ref)
        return carry

    lax.fori_loop(0, n_tiles // 2, pair, 0)

    @pl.when(n_tiles % 2 == 1)
    def _():
        absorb(n_tiles - 1, sa_ref)

    o_s = acc_ref[0:dh, :] / acc_ref[dh:dh + 1, :]

    gs_ref[...] = jnp.transpose(jax.nn.sigmoid(gate_ref[...] + bg_ref[...]))

    def gate(r):
        return jnp.concatenate([gs_ref[pl.ds((g * C_HPG + hh) * 3 + r, 1), :] for hh in range(C_HPG)], axis=1)

    o = gate(0) * o_c + gate(1) * o_s + gate(2) * o_w
    o = jnp.concatenate([o[:, hh * tq:(hh + 1) * tq] for hh in range(C_HPG)], axis=0)
    o_ref[...] = jnp.transpose(o).astype(o_ref.dtype)


def _aug_keys(k, pos, one_hot_blocks):
    B, G, n, dh = k.shape
    hi = (pos // C_SEL_BLOCK) * C_SEL_BLOCK
    cols = np.zeros((n, LANES - dh), np.float32)
    for j in range(SLOPE_PIECES):
        cols[:, 2 * j] = hi
        cols[:, 2 * j + 1] = pos - hi
    parts = [k, jnp.broadcast_to(jnp.asarray(cols, bf16), (B, G, n, LANES - dh))]
    if one_hot_blocks:
        oh = (pos[:, None] // C_SEL_BLOCK == np.arange(LANES)[None, :]).astype(np.float32)
        parts.append(jnp.broadcast_to(jnp.asarray(oh, bf16), (B, G, n, LANES)))
    return jnp.concatenate(parts, axis=-1)


def nsa_attention_prompt(proj, b_gate_p, kc, vc, ks, vs, kw, vw, *, tq=256, tk=256):
    B, L, _ = proj.shape
    G, dh = C_KV_HEADS, C_HEAD_DIM
    tq = min(tq, L)
    tk = min(tk, L)
    assert tk % tq == 0 and L % tk == 0 and L // C_SEL_BLOCK <= LANES
    R = C_HPG * tq
    n_cmp = kc.shape[2]
    kc_a = _aug_keys(kc, np.arange(n_cmp) * C_CMP_STRIDE + (C_CMP_BLOCK - 1), False)
    ks_a = _aug_keys(ks, np.arange(L), True)
    kw_a = _aug_keys(kw, np.arange(L), False)
    lead = np.zeros((C_WINDOW, LANES), np.float32)
    lead[:, dh:dh + 2 * SLOPE_PIECES:2] = NEG_INF
    kw_a = jnp.concatenate([jnp.broadcast_to(jnp.asarray(lead, bf16), (B, G, C_WINDOW, LANES)), kw_a], axis=2)
    gate_blk = (C_HEADS * dh + 6 * G * dh) // LANES
    seq = lambda n, w: pl.BlockSpec((None, None, n, w), lambda b, g, i, sp: (b, g, 0, 0))
    tr = lambda a: jnp.swapaxes(a, 2, 3)

    def with_ones(a):
        n = a.shape[2]
        tail = np.zeros((VAL_ROWS - dh, n), np.float32)
        tail[0] = 1.0
        return jnp.concatenate([tr(a), jnp.broadcast_to(jnp.asarray(tail, bf16), (B, G, VAL_ROWS - dh, n))], axis=2)

    return pl.pallas_call(
        functools.partial(_nsa_body, tk=tk),
        out_shape=jax.ShapeDtypeStruct((B, L, C_HEADS * dh), bf16),
        grid_spec=pltpu.PrefetchScalarGridSpec(
            num_scalar_prefetch=1,
            grid=(B, G, L // tq),
            in_specs=[
                pl.BlockSpec((None, tq, C_HPG * dh), lambda b, g, i, sp: (b, i, g)),
                pl.BlockSpec((None, tq, LANES), lambda b, g, i, sp: (b, i, gate_blk)),
                pl.BlockSpec((1, LANES), lambda b, g, i, sp: (0, 0)),
                seq(n_cmp, LANES), seq(dh, n_cmp), seq(L, 2 * LANES), seq(VAL_ROWS, L), seq(L + C_WINDOW, LANES),
                seq(VAL_ROWS, L + C_WINDOW),
            ],
            out_specs=pl.BlockSpec((None, tq, C_HPG * dh), lambda b, g, i, sp: (b, i, g)),
            scratch_shapes=[pltpu.VMEM((1, R), f32), pltpu.VMEM((VAL_ROWS, R), f32),
                            pltpu.VMEM((LANES, tq), f32), pltpu.VMEM((tk, R), f32), pltpu.VMEM((tk, R), f32)],
        ),
        compiler_params=_params("parallel", "parallel", "parallel"),
        name="nsa_attention_prompt",
    )(jnp.asarray(SLOPES_C_PIECES), proj, proj, b_gate_p, kc_a, tr(vc), ks_a, with_ones(vs), kw_a,
      jnp.pad(with_ones(vw), ((0, 0), (0, 0), (0, 0), (C_WINDOW, 0))))


def _slot_rows(ref, off, n, w):
    rows = jnp.concatenate([ref[:, off + s * w:off + (s + 1) * w] for s in range(n)]
                           + ([jnp.zeros((8 - n, w), f32)] if n < 8 else []), axis=0)
    return jnp.concatenate([rows, jnp.zeros((8, LANES - w), f32)], axis=-1) if w < LANES else rows


def _dil_sample_body(sl_ref, qkv_ref, c0_ref, c1_ref, c2_ref, o_ref):
    dh = A_HEAD_DIM
    scale = dh ** -0.5
    sect = A_GROUPS * A_OUT
    scores, news, v_cols = [], [], []
    for grp, c_ref in enumerate((c0_ref, c1_ref, c2_ref)):
        W = c_ref.shape[-1]
        d = A_DILATIONS[grp]
        q8 = _slot_rows(qkv_ref, grp * A_OUT, A_SLOTS, dh)
        kn8 = _slot_rows(qkv_ref, sect + grp * A_OUT, A_SLOTS, dh)
        vn8 = _slot_rows(qkv_ref, 2 * sect + grp * A_OUT, A_SLOTS, dh)
        q_cols = jnp.transpose(q8)[:dh]
        v_cols.append(jnp.transpose(vn8)[:dh])
        news.append(jnp.sum(q8 * kn8, axis=-1, keepdims=True) * scale)
        pos = lax.broadcasted_iota(jnp.int32, (1, W), 1)
        rows = [jnp.sum(c_ref[0, s] * q_cols[:, s:s + 1], axis=0, keepdims=True) for s in range(A_SLOTS)]
        slope = jnp.concatenate([jnp.full((1, 1), sl_ref[grp * A_SLOTS + s], f32) for s in range(A_SLOTS)], axis=0)
        sc = jnp.concatenate(rows, axis=0) * scale - slope * (W - pos).astype(f32)
        scores.append(jnp.where((pos & (d - 1)) == 0, sc, NEG_INF))
    m = functools.reduce(jnp.maximum, [jnp.max(s, axis=-1, keepdims=True) for s in scores] + news)
    den = jnp.zeros((A_SLOTS, 1), f32)
    acc = [jnp.zeros((dh, 1), f32) for _ in range(A_SLOTS)]
    for grp, c_ref in enumerate((c0_ref, c1_ref, c2_ref)):
        e = jnp.exp(scores[grp] - m)
        en = jnp.exp(news[grp] - m)
        den = den + jnp.sum(e, axis=-1, keepdims=True) + en
        for s in range(A_SLOTS):
            acc[s] = acc[s] + jnp.sum(c_ref[1, s] * e[s:s + 1, :], axis=-1, keepdims=True) \
                + en[s:s + 1, :] * v_cols[grp][:, s:s + 1]
    o_ref[...] = jnp.concatenate([acc[s] / den[s:s + 1, :] for s in range(A_SLOTS)], axis=-1)


def dil_sample(qkv, caches_t, layer):
    Bs = qkv.shape[0]
    for grp, c in enumerate(caches_t):
        assert c.shape[-1] == A_WINDOWS[grp]
    c_spec = lambda c: pl.BlockSpec((None, None) + c.shape[2:], lambda b, sl: (layer, b, 0, 0, 0, 0))
    return pl.pallas_call(
        _dil_sample_body,
        out_shape=jax.ShapeDtypeStruct((Bs, A_HEAD_DIM, A_SLOTS), f32),
        grid_spec=pltpu.PrefetchScalarGridSpec(
            num_scalar_prefetch=1,
            grid=(Bs,),
            in_specs=[pl.BlockSpec((None, 1, qkv.shape[-1]), lambda b, sl: (b, 0, 0))]
            + [c_spec(c) for c in caches_t],
            out_specs=pl.BlockSpec((None, A_HEAD_DIM, A_SLOTS), lambda b, sl: (b, 0, 0)),
        ),
        compiler_params=_params("parallel"),
        name="dil_sample",
    )(jnp.asarray(SLOPES_A.reshape(-1)), qkv, *caches_t)


def _xattn_sample_body(q_ref, c_ref, o_ref):
    dh = q_ref.shape[-1] // MEM_HEADS
    outs = []
    for h in range(MEM_HEADS):
        qh = q_ref[:, h * dh:(h + 1) * dh]
        s = jnp.sum(c_ref[:, 0, h, :] * qh, axis=-1, keepdims=True) * dh ** -0.5
        e = jnp.exp(s - jnp.max(s, axis=0, keepdims=True))
        outs.append(jnp.sum(e * c_ref[:, 1, h, :], axis=0, keepdims=True) / jnp.sum(e, axis=0, keepdims=True))
    o_ref[...] = jnp.concatenate(outs, axis=-1)


def xattn_sample(q, cache, layer):
    Bs, _, D = q.shape
    return pl.pallas_call(
        _xattn_sample_body,
        out_shape=jax.ShapeDtypeStruct((Bs, 1, D), f32),
        grid=(Bs,),
        in_specs=[pl.BlockSpec((None, 1, D), lambda b: (b, 0, 0)),
                  pl.BlockSpec((None, None) + cache.shape[2:], lambda b: (layer, b, 0, 0, 0, 0))],
        out_specs=pl.BlockSpec((None, 1, D), lambda b: (b, 0, 0)),
        compiler_params=_params("parallel"),
        name="xattn_sample",
    )(q, cache)


def _gla_step_body(q_ref, k_ref, v_ref, r_ref, gl_ref, w2_ref, bg_ref, gh_ref, s0_ref, y_ref, s_ref):
    z = _hdot(gl_ref[...], w2_ref[...]) + bg_ref[...]
    a = jnp.exp(_log_sigmoid(z) / B_GATE_TAU)
    pad = jnp.zeros((5, B_DK), f32)
    for h in range(B_HEADS):
        ks_ = slice(h * B_DK, (h + 1) * B_DK)
        vs_ = slice(h * B_DV, (h + 1) * B_DV)
        rows = jnp.concatenate([a[:, ks_], k_ref[:, ks_], q_ref[:, ks_] * B_DK ** -0.5, pad], axis=0)
        cols = jnp.transpose(rows)
        S = cols[:, 0:1] * s0_ref[h] + cols[:, 1:2] * v_ref[:, vs_]
        s_ref[h] = S
        o = jnp.sum(cols[:, 2:3] * S, axis=0, keepdims=True)
        rh = r_ref[:, vs_]
        y_ref[:, vs_] = (_rms(o, gh_ref[:, vs_]) * (rh * jax.nn.sigmoid(rh))).astype(y_ref.dtype)


def gla_step(proj, w_gate2p, b_gate, g_head, state, layer):
    Bs = proj.shape[0]
    HK, HV = B_HEADS * B_DK, B_HEADS * B_DV
    return pl.pallas_call(
        _gla_step_body,
        out_shape=[jax.ShapeDtypeStruct((Bs, 1, HV), bf16),
                   jax.ShapeDtypeStruct((Bs, B_HEADS, B_DK, B_DV), f32)],
        grid=(Bs,),
        in_specs=[
            pl.BlockSpec((None, 1, HK), lambda b: (b, 0, 0)),
            pl.BlockSpec((None, 1, HK), lambda b: (b, 0, 1)),
            pl.BlockSpec((None, 1, HV), lambda b: (b, 0, 1)),
            pl.BlockSpec((None, 1, HV), lambda b: (b, 0, 2)),
            pl.BlockSpec((None, 1, LANES), lambda b: (b, 0, (2 * HK + 2 * HV) // LANES)),
            pl.BlockSpec((LANES, HK), lambda b: (0, 0)),
            pl.BlockSpec((1, HK), lambda b: (0, 0)),
            pl.BlockSpec((1, HV), lambda b: (0, 0)),
            pl.BlockSpec((None, None, B_HEADS, B_DK, B_DV), lambda b: (layer, b, 0, 0, 0)),
        ],
        out_specs=[pl.BlockSpec((None, 1, HV), lambda b: (b, 0, 0)),
                   pl.BlockSpec((None, B_HEADS, B_DK, B_DV), lambda b: (b, 0, 0, 0))],
        compiler_params=_params("parallel"),
        name="gla_step",
    )(proj, proj, proj, proj, proj, w_gate2p, b_gate.reshape(1, HK), g_head.reshape(1, HV), state)


PAGES_PER_STEP = 8


def _cmp_sample_body(pt_ref, *refs):
    pages = refs[:PAGES_PER_STEP]
    (pk_ref, pv_ref, wk1_ref, wk2_ref, wv1_ref, wv2_ref, wkp_ref, wvp_ref,
     ok_ref, ov_ref, x_ref) = refs[PAGES_PER_STEP:]
    i = pl.program_id(1)
    H = C_CMP_HIDDEN
    n_tiles = x_ref.shape[0]
    per_kind = n_tiles // 2
    per_page = PAGE_SIZE // C_CMP_STRIDE
    out_row = lax.broadcasted_iota(jnp.int32, (PAGE_SIZE, PAGE_SIZE), 0)
    in_row = lax.broadcasted_iota(jnp.int32, (PAGE_SIZE, PAGE_SIZE), 1)
    shift = per_page.bit_length() - 1
    perm = (in_row == (out_row & (per_page - 1)) * C_CMP_STRIDE + (out_row >> shift)).astype(bf16)
    for k in range(PAGES_PER_STEP):
        c0 = pl.multiple_of((i * PAGES_PER_STEP + k) * per_page, per_page)
        for kind in range(2):
            for tile in range(per_kind):
                xt = pages[k][kind, 2 * tile:2 * tile + 2].reshape(LANES, PAGE_SIZE).astype(bf16)
                xs = _dot_t(perm, xt)
                for p in range(C_CMP_STRIDE):
                    x_ref[kind * per_kind + tile, p, pl.ds(c0, per_page), :] = xs[p * per_page:(p + 1) * per_page]

    @pl.when(i == pl.num_programs(1) - 1)
    def _():
        n = x_ref.shape[2]
        for kind, (p_ref, w1_ref, w2_ref, wp_ref, o_ref) in enumerate(
                ((pk_ref, wk1_ref, wk2_ref, wkp_ref, ok_ref), (pv_ref, wv1_ref, wv2_ref, wvp_ref, ov_ref))):
            posb = _cmp_pos_bias(p_ref, w1_ref)
            outs = []
            for tile in range(per_kind):
                ct = kind * per_kind + tile
                z2 = jnp.zeros((n, 4 * H), f32)
                for pp in range(C_CMP_STRIDE // 2):
                    xp = jnp.concatenate([x_ref[ct, 2 * pp], x_ref[ct, 2 * pp + 1]], axis=-1).astype(bf16)
                    z2 = z2 + jnp.dot(xp, wp_ref[pp], preferred_element_type=f32)
                outs += [_cmp_mlp_tail(z, posb, w2_ref) for z in (z2[:, :2 * H], z2[:, 2 * H:])]
            o_ref[...] = jnp.concatenate(outs, axis=-1)


def _pair_block_diag(w1):
    dh = C_HEAD_DIM
    w = w1.reshape(C_CMP_STRIDE, dh, w1.shape[1])
    z = jnp.zeros_like(w)
    bd = jnp.concatenate([jnp.concatenate([w, z], axis=-1), jnp.concatenate([z, w], axis=-1)], axis=1)
    return bd.reshape(C_CMP_STRIDE // 2, 4 * dh, 2 * w1.shape[1])


def nsa_compress_sample(page_table, cache_t, layer, pos_k, pos_v, wk1, wk2, wv1, wv2):
    Bs, n_pages = page_table.shape
    G, dh = C_KV_HEADS, C_HEAD_DIM
    assert n_pages % PAGES_PER_STEP == 0 and cache_t.shape[2:] == (4, G, dh, PAGE_SIZE) and 2 * dh == LANES
    n_rows = n_pages * PAGE_SIZE
    n = n_rows // C_CMP_STRIDE
    page = lambda k: pl.BlockSpec((None, None, 2, G, dh, PAGE_SIZE),
                                  lambda b, i, pt, k=k: (layer, pt[b, i * PAGES_PER_STEP + k], 0, 0, 0, 0))
    consts = [pos_k, pos_v, wk1, wk2, wv1, wv2, _pair_block_diag(wk1), _pair_block_diag(wv1)]
    full = lambda a: pl.BlockSpec(a.shape, lambda b, i, pt: (0,) * a.ndim)
    o_spec = pl.BlockSpec((None, n, G * dh), lambda b, i, pt: (b, 0, 0))
    return pl.pallas_call(
        _cmp_sample_body,
        out_shape=[jax.ShapeDtypeStruct((Bs, n, G * dh), f32)] * 2,
        grid_spec=pltpu.PrefetchScalarGridSpec(
            num_scalar_prefetch=1,
            grid=(Bs, n_pages // PAGES_PER_STEP),
            in_specs=[page(k) for k in range(PAGES_PER_STEP)] + [full(a) for a in consts],
            out_specs=[o_spec, o_spec],
            scratch_shapes=[pltpu.VMEM((2 * G * dh // LANES, C_CMP_STRIDE, n, LANES), f32)],
        ),
        compiler_params=_params("parallel", "arbitrary"),
        name="nsa_compress_sample",
    )(page_table, *([cache_t] * PAGES_PER_STEP), *consts)


def _cmp_topk_sample_body(sl_ref, q_ref, kc_ref, vc_ref, oc_ref, idx_ref, *, t):
    G, dh = C_KV_HEADS, C_HEAD_DIM
    n = kc_ref.shape[0]
    n_slc = t // C_SEL_BLOCK + 1
    NS = -(-n_slc // LANES) * LANES
    cpos = lax.broadcasted_iota(jnp.int32, (1, n), 1) * C_CMP_STRIDE + (C_CMP_BLOCK - 1)
    mask = cpos <= t
    c_start = lax.broadcasted_iota(jnp.int32, (n, NS), 0) * C_CMP_STRIDE
    jb = lax.broadcasted_iota(jnp.int32, (n, NS), 1)
    cover = jnp.where(c_start < (jb + 1) * C_SEL_BLOCK,
                      jnp.where(c_start + C_CMP_BLOCK > jb * C_SEL_BLOCK, 1.0, 0.0), 0.0).astype(bf16)
    blk = lax.broadcasted_iota(jnp.int32, (1, NS), 1)
    cur = t // C_SEL_BLOCK
    forced = (blk == 0) | (blk == cur) | (blk == cur - 1)
    lane = lax.broadcasted_iota(jnp.int32, (1, LANES), 1)
    o_parts, p_rows = [], []
    for g in range(G):
        q8 = jnp.concatenate([q_ref[:, (g * C_HPG + hh) * dh:(g * C_HPG + hh + 1) * dh] for hh in range(C_HPG)]
                             + [jnp.zeros((8 - C_HPG, dh), f32)], axis=0)
        slope = jnp.concatenate([jnp.full((1, 1), sl_ref[g * C_HPG + hh], f32) for hh in range(C_HPG)]
                                + [jnp.zeros((8 - C_HPG, 1), f32)], axis=0)
        s = _dot_t(q8.astype(bf16), kc_ref[:, g * dh:(g + 1) * dh].astype(bf16)) * dh ** -0.5
        s = s - slope * (float(t) - cpos.astype(f32))
        s = jnp.where(mask, s, NEG_INF)
        m = jnp.max(s, axis=-1, keepdims=True)
        e = jnp.where(mask, jnp.exp(s - m), 0.0)
        p = e / jnp.maximum(jnp.sum(e, axis=-1, keepdims=True), TINY)
        o = jnp.dot(p.astype(bf16), vc_ref[:, g * dh:(g + 1) * dh].astype(bf16), preferred_element_type=f32)
        o_parts += [o[hh:hh + 1] for hh in range(C_HPG)]
        p_rows.append(jnp.sum(p[0:C_HPG], axis=0, keepdims=True))
    oc_ref[...] = jnp.concatenate(o_parts, axis=-1)
    rest = jnp.concatenate(p_rows + [jnp.zeros((8 - G, n), f32)], axis=0)
    imp = jnp.zeros((8, NS), f32)
    for _ in range(N_PIECES):
        piece = rest.astype(bf16)
        imp = imp + jnp.dot(piece, cover, preferred_element_type=f32)
        rest = rest - piece.astype(f32)
    row = lax.broadcasted_iota(jnp.int32, (8, 1), 0)
    score = jnp.where((blk * C_SEL_BLOCK <= t) & (row < G), jnp.where(forced, FORCE, imp), NEG_INF)
    rank = jnp.zeros((8, NS), f32)
    for k in range(1, NS):
        other = pltpu.roll(score, k, 1)
        ahead = jnp.where(blk >= k, jnp.where(other >= score, 1.0, 0.0), jnp.where(other > score, 1.0, 0.0))
        rank = rank + ahead
    blk_f = blk.astype(f32)
    idx = jnp.zeros((8, LANES), f32)
    for r in range(C_TOPK):
        pick = jnp.sum(jnp.where(rank == float(r), blk_f, 0.0), axis=-1, keepdims=True)
        idx = jnp.where(lane == r, pick, idx)
    idx_ref[...] = idx.astype(jnp.int32)


def nsa_cmp_topk_sample(proj, kc, vc, t):
    Bs = proj.shape[0]
    n, W = kc.shape[1:]
    HD = C_HEADS * C_HEAD_DIM
    return pl.pallas_call(
        functools.partial(_cmp_topk_sample_body, t=t),
        out_shape=[jax.ShapeDtypeStruct((Bs, 1, HD), f32), jax.ShapeDtypeStruct((Bs, 8, LANES), jnp.int32)],
        grid_spec=pltpu.PrefetchScalarGridSpec(
            num_scalar_prefetch=1,
            grid=(Bs,),
            in_specs=[pl.BlockSpec((None, 1, HD), lambda b, sl: (b, 0, 0)),
                      pl.BlockSpec((None, n, W), lambda b, sl: (b, 0, 0)),
                      pl.BlockSpec((None, n, W), lambda b, sl: (b, 0, 0))],
            out_specs=[pl.BlockSpec((None, 1, HD), lambda b, sl: (b, 0, 0)),
                       pl.BlockSpec((None, 8, LANES), lambda b, sl: (b, 0, 0))],
        ),
        compiler_params=_params("parallel"),
        name="nsa_cmp_topk_sample",
    )(jnp.asarray(SLOPES_C), proj, kc, vc)


def _sel_win_sample_body(pt_ref, ix_ref, sl_ref, *refs, t):
    blocks = refs[:C_TOPK]
    (win_ref, q_ref, ksn_ref, vsn_ref, kwn_ref, vwn_ref, gate_ref, bg_ref, oc_ref, o_ref) = refs[C_TOPK:]
    b = pl.program_id(0)
    g = pl.program_id(1)
    G, dh = C_KV_HEADS, C_HEAD_DIM
    scale = dh ** -0.5
    n_past = t // C_SEL_BLOCK
    per_page = PAGE_SIZE // C_SEL_BLOCK
    q8 = jnp.concatenate([q_ref[:, hh * dh:(hh + 1) * dh] for hh in range(C_HPG)]
                         + [jnp.zeros((8 - C_HPG, dh), f32)], axis=0)
    q8b = q8.astype(bf16)
    slope = jnp.concatenate([jnp.full((1, 1), sl_ref[g * C_HPG + hh], f32) for hh in range(C_HPG)]
                            + [jnp.zeros((8 - C_HPG, 1), f32)], axis=0)

    def own(ref):
        return functools.reduce(jnp.add, [jnp.where(g == gg, ref[:, gg * dh:(gg + 1) * dh], 0.0) for gg in range(G)])

    def softmax_pv(scores, values_t, s_new, v_new):
        m = functools.reduce(jnp.maximum, [jnp.max(s, axis=-1, keepdims=True) for s in scores] + [s_new])
        e_new = jnp.exp(s_new - m)
        den = e_new
        acc = e_new * v_new
        for s, vt in zip(scores, values_t):
            e = jnp.exp(s - m)
            den = den + jnp.sum(e, axis=-1, keepdims=True)
            acc = acc + _dot_t(e.astype(bf16), vt.astype(bf16))
        return acc / den

    lane = lax.broadcasted_iota(jnp.int32, (1, PAGE_SIZE), 1)
    scores, values_t = [], []
    for s_i in range(C_TOPK):
        j = ix_ref[b, g * C_TOPK + s_i]
        jc = jnp.minimum(j, n_past - 1)
        kpos = (jc // per_page) * PAGE_SIZE + lane
        sc = jnp.dot(q8b, blocks[s_i][0].astype(bf16), preferred_element_type=f32) * scale \
            - slope * (t - kpos).astype(f32)
        ok = ((lane >> C_SEL_SHIFT) == (jc % per_page)) & (j < n_past)
        scores.append(jnp.where(ok, sc, NEG_INF))
        values_t.append(blocks[s_i][1])
    s_new = jnp.sum(q8 * own(ksn_ref), axis=-1, keepdims=True) * scale
    o_s = softmax_pv(scores, values_t, s_new, own(vsn_ref))

    nw = win_ref.shape[-1]
    wdist = (nw - lax.broadcasted_iota(jnp.int32, (1, nw), 1)).astype(f32)
    sw = jnp.dot(q8b, win_ref[0].astype(bf16), preferred_element_type=f32) * scale - slope * wdist
    s_new = jnp.sum(q8 * own(kwn_ref), axis=-1, keepdims=True) * scale
    o_w = softmax_pv([sw], [win_ref[1]], s_new, own(vwn_ref))

    gs = jax.nn.sigmoid(gate_ref[...] + bg_ref[...])
    glane = lax.broadcasted_iota(jnp.int32, (1, LANES), 1)
    outs = []
    for hh in range(C_HPG):
        gate = [jnp.sum(jnp.where(glane == (g * C_HPG + hh) * 3 + r, gs, 0.0), axis=-1, keepdims=True)
                for r in range(3)]
        outs.append(gate[0] * oc_ref[:, hh * dh:(hh + 1) * dh] + gate[1] * o_s[hh:hh + 1] + gate[2] * o_w[hh:hh + 1])
    o_ref[...] = jnp.concatenate(outs, axis=-1).astype(o_ref.dtype)


def nsa_sel_win_sample(page_table, sel_idx, cache_t, win_t, layer, proj, b_gate_p, o_c, t):
    Bs = proj.shape[0]
    G, dh = C_KV_HEADS, C_HEAD_DIM
    GD = G * dh
    assert win_t.shape[-1] == C_WINDOW and t % C_SEL_BLOCK == 0
    n_past = t // C_SEL_BLOCK
    per_page = PAGE_SIZE // C_SEL_BLOCK

    def blk_spec(s_i):
        def imap(b, g, pt, ix, sl):
            j = jnp.minimum(ix[b, g * C_TOPK + s_i], n_past - 1)
            return (layer, pt[b, j // per_page], 1, g, 0, 0)
        return pl.BlockSpec((None, None, 2, None, dh, PAGE_SIZE), imap)

    row = lambda w, blk: pl.BlockSpec((None, 1, w), lambda b, g, pt, ix, sl, blk=blk: (b, 0, blk))
    base = C_HEADS * dh // GD
    gate_blk = (C_HEADS * dh + 6 * GD) // LANES
    return pl.pallas_call(
        functools.partial(_sel_win_sample_body, t=t),
        out_shape=jax.ShapeDtypeStruct((Bs, 1, C_HEADS * dh), bf16),
        grid_spec=pltpu.PrefetchScalarGridSpec(
            num_scalar_prefetch=3,
            grid=(Bs, G),
            in_specs=[blk_spec(s_i) for s_i in range(C_TOPK)] + [
                pl.BlockSpec((None, None, 2, None, dh, C_WINDOW), lambda b, g, pt, ix, sl: (layer, b, 0, g, 0, 0)),
                pl.BlockSpec((None, 1, C_HPG * dh), lambda b, g, pt, ix, sl: (b, 0, g)),
                row(GD, base + 2), row(GD, base + 3), row(GD, base + 4), row(GD, base + 5),
                row(LANES, gate_blk),
                pl.BlockSpec((1, LANES), lambda b, g, pt, ix, sl: (0, 0)),
                pl.BlockSpec((None, 1, C_HPG * dh), lambda b, g, pt, ix, sl: (b, 0, g)),
            ],
            out_specs=pl.BlockSpec((None, 1, C_HPG * dh), lambda b, g, pt, ix, sl: (b, 0, g)),
        ),
        compiler_params=_params("parallel", "parallel"),
        name="nsa_sel_win_sample",
    )(page_table, sel_idx, jnp.asarray(SLOPES_C), *([cache_t] * C_TOPK), win_t, proj, proj, proj, proj, proj,
      proj, b_gate_p, o_c)


def _norm_body(x_ref, g_ref, o_ref):
    o_ref[...] = _rms(x_ref[...], g_ref[...])


def rmsnorm_rows(x, g, *, tm=1024):
    M, D = x.shape
    tm = min(tm, M)
    return pl.pallas_call(
        _norm_body,
        out_shape=jax.ShapeDtypeStruct((M, D), f32),
        grid=(M // tm,),
        in_specs=[pl.BlockSpec((tm, D), lambda i: (i, 0)), pl.BlockSpec((1, D), lambda i: (0, 0))],
        out_specs=pl.BlockSpec((tm, D), lambda i: (i, 0)),
        compiler_params=_params("parallel"),
        name="rmsnorm",
    )(x, g.reshape(1, D))


def _position_minor(cache):
    return jnp.transpose(cache, (0, 1, 3, 4, 5, 2))


def kernel(x_prompt, x_sample, cache_dil_w128, cache_dil_w512, cache_dil_w2048, state_gla, cache_nsa_win,
           cache_nsa_kv, cache_mem_kv, page_table, mem_prompt, g_mix, g_cross, g_mem, g_ffn, g_final,
           w_a_qkv, w_a_o, w_b_in, w_b_gate2, b_b_gate, g_b_head, w_b_o, w_c_in, b_c_gate, c_pos_k, c_pos_v,
           w_c_k1, w_c_k2, w_c_v1, w_c_v2, w_c_o, w_x_q, w_x_kv, w_x_o, w_ffn_in, w_ffn_out):
    Bp, L, D = x_prompt.shape
    Bs = x_sample.shape[0]
    depth = g_mix.shape[0]
    n_pages = page_table.shape[1]
    t_s = n_pages * PAGE_SIZE
    G, dh = C_KV_HEADS, C_HEAD_DIM
    GD = G * dh
    HK, HV = B_HEADS * B_DK, B_HEADS * B_DV
    cast = lambda a: a.astype(bf16)

    wa_qkv, wa_o = cast(w_a_qkv), cast(w_a_o)
    nb = 2 * HK + HV
    wb_in = cast(jnp.concatenate(
        [w_b_in[..., :nb], w_b_in[..., nb + B_GATE_RANK:], w_b_in[..., nb:nb + B_GATE_RANK],
         jnp.zeros(w_b_in.shape[:2] + (LANES - B_GATE_RANK,), f32)], axis=-1))
    wb_gate2 = jnp.pad(w_b_gate2, ((0, 0), (0, LANES - B_GATE_RANK), (0, 0)))
    wb_o = cast(w_b_o)
    nc = -(-w_c_in.shape[-1] // LANES) * LANES
    wc_in = cast(jnp.pad(w_c_in, ((0, 0), (0, 0), (0, nc - w_c_in.shape[-1]))))
    bc_gate = jnp.pad(b_c_gate, ((0, 0), (0, LANES - b_c_gate.shape[-1])))
    half = C_CMP_STRIDE * dh
    two_chunk = lambda w: cast(jnp.concatenate([w[:, :half], w[:, half:]], axis=-1))
    wc_k1, wc_v1 = two_chunk(w_c_k1), two_chunk(w_c_v1)
    wc_k2, wc_v2, wc_o = cast(w_c_k2), cast(w_c_v2), cast(w_c_o)
    pos_k = c_pos_k.reshape(-1, 2, half)
    pos_v = c_pos_v.reshape(-1, 2, half)
    wx_q, wx_kv, wx_o = cast(w_x_q), cast(w_x_kv), cast(w_x_o)
    wf_in, wf_out = cast(w_ffn_in), cast(w_ffn_out)

    Mp = Bp * L
    x = x_prompt.reshape(Mp, D)
    mem2d = mem_prompt.reshape(-1, D)
    dil_p = [[] for _ in range(A_GROUPS)]
    gla_p, rows_p, win_p, mem_p = [], [], [], []
    for l in range(depth):
        kind, j = l % N_MIXERS, l // N_MIXERS
        if kind == 0:
            qkv = mm(x, (wa_qkv, j), g=g_mix[l]).reshape(Bp, L, -1)
            outs = [dilated_group(qkv, grp) for grp in range(A_GROUPS)]
            x = dilated_merge_out([o.reshape(Mp, A_OUT) for o, _ in outs],
                                  [s.reshape(Mp, A_OUT) for _, s in outs], (wa_o, j), x)
            sect = A_GROUPS * A_OUT
            for grp in range(A_GROUPS):
                w = min(A_WINDOWS[grp], L)
                tail = lambda t: qkv[:, L - w:, t * sect + grp * A_OUT:t * sect + (grp + 1) * A_OUT].reshape(
                    Bp, w, A_SLOTS, A_HEAD_DIM)
                dil_p[grp].append(jnp.stack([tail(1), tail(2)], axis=2))
        elif kind == 1:
            proj = mm(x, (wb_in, j), g=g_mix[l]).reshape(Bp, L, -1)
            y, S = gla_prompt(proj, wb_gate2[j], b_b_gate[j], g_b_head[j].reshape(-1),
                              jnp.zeros((Bp, B_HEADS, B_DK, B_DV), f32))
            gla_p.append(S)
            x = mm(y.reshape(Mp, HV), (wb_o, j), res=x)
        else:
            proj = mm(x, (wc_in, j), g=g_mix[l]).reshape(Bp, L, -1)
            q_w = C_HEADS * dh
            rows = proj[..., q_w:q_w + 6 * GD].reshape(Bp, L, 6, G, dh)
            chunks = lambda a: a.reshape(Bp, L // C_CMP_STRIDE, C_CMP_STRIDE, G, dh).transpose(
                0, 3, 1, 2, 4).reshape(Bp, G, L // C_CMP_STRIDE, half)
            kc, vc = nsa_compress(chunks(rows[:, :, 0]), chunks(rows[:, :, 1]), pos_k[j], pos_v[j],
                                  wc_k1[j], wc_k2[j], wc_v1[j], wc_v2[j])
            seqs = [cast(rows[:, :, r].transpose(0, 2, 1, 3)) for r in range(2, 6)]
            o = nsa_attention_prompt(proj, bc_gate[j:j + 1], kc, vc, *seqs)
            x = mm(o.reshape(Mp, q_w), (wc_o, j), res=x)
            rows_p.append(rows[:, :, :4])
            win_p.append(rows[:, L - min(C_WINDOW, L):, 4:])
        mem_kv = mm(mem2d, (wx_kv, l), g=g_mem[l]).reshape(Bp, -1, 2 * D)
        mem_p.append(mem_kv.reshape(Bp, -1, 2, MEM_HEADS, D // MEM_HEADS))
        x = xattn_prompt(x.reshape(Bp, L, D), g_cross[l], (wx_q, l), cast(mem_kv), (wx_o, l)).reshape(Mp, D)
        x = ffn(x, g_ffn[l], (wf_in, l), (wf_out, l), g_final=g_final if l == depth - 1 else None)
    y_prompt = x.reshape(Bp, L, D)

    x = x_sample.reshape(Bs, D)
    dil_t = [_position_minor(c) for c in (cache_dil_w128, cache_dil_w512, cache_dil_w2048)]
    nsa_t = _position_minor(cache_nsa_kv)
    win_t = _position_minor(cache_nsa_win)
    dil_s = [[] for _ in range(A_GROUPS)]
    gla_s, rows_s, win_s = [], [], []
    for l in range(depth):
        kind, j = l % N_MIXERS, l // N_MIXERS
        if kind == 0:
            qkv = mm(x, (wa_qkv, j), g=g_mix[l]).reshape(Bs, 1, -1)
            o = dil_sample(qkv, dil_t, j)
            x = mm(o.transpose(0, 2, 1).reshape(Bs, A_OUT), (wa_o, j), res=x)
            kv = qkv.reshape(Bs, 1, 3, A_GROUPS, A_SLOTS, A_HEAD_DIM)
            for grp in range(A_GROUPS):
                dil_s[grp].append(jnp.stack([kv[:, :, 1, grp], kv[:, :, 2, grp]], axis=2))
        elif kind == 1:
            proj = mm(x, (wb_in, j), g=g_mix[l]).reshape(Bs, 1, -1)
            y, S = gla_step(proj, wb_gate2[j], b_b_gate[j], g_b_head[j].reshape(-1), state_gla, j)
            gla_s.append(S)
            x = mm(y.reshape(Bs, HV), (wb_o, j), res=x)
        else:
            proj = mm(x, (wc_in, j), g=g_mix[l]).reshape(Bs, 1, -1)
            q_w = C_HEADS * dh
            kc, vc = nsa_compress_sample(page_table, nsa_t, j, pos_k[j], pos_v[j],
                                         wc_k1[j], wc_k2[j], wc_v1[j], wc_v2[j])
            o_c, idx = nsa_cmp_topk_sample(proj, kc, vc, t_s)
            sel_idx = idx[:, :G, :C_TOPK].reshape(Bs, G * C_TOPK)
            o = nsa_sel_win_sample(page_table, sel_idx, nsa_t, win_t, j, proj, bc_gate[j:j + 1], o_c, t_s)
            x = mm(o.reshape(Bs, q_w), (wc_o, j), res=x)
            rows = proj[..., q_w:q_w + 6 * GD].reshape(Bs, 1, 6, G, dh)
            rows_s.append(rows[:, :, :4])
            win_s.append(rows[:, :, 4:])
        q = mm(x, (wx_q, l), g=g_cross[l]).reshape(Bs, 1, D)
        o = xattn_sample(q, cache_mem_kv, l)
        x = mm(o.reshape(Bs, D), (wx_o, l), res=x)
        x = ffn(x, g_ffn[l], (wf_in, l), (wf_out, l), g_final=g_final if l == depth - 1 else None)
    y_sample = x.reshape(Bs, 1, D)

    st = jnp.stack
    return (y_prompt, y_sample, st(dil_p[0]), st(dil_s[0]), st(dil_p[1]), st(dil_s[1]), st(dil_p[2]), st(dil_s[2]),
            st(gla_p), st(gla_s), st(win_p), st(win_s), st(rows_p), st(rows_s), st(mem_p))
```
